```python
import math
import jax, jax.numpy as jnp
from jax import lax
import numpy as np

D_MODEL = 1024
BATCH = 2
SEQ = 8192
DEPTH = 2

HEAD_DIM = 64
BLOCK = 128
EPS = 1e-6
NEG = -1e30
A_HEADS = 8
A_KV_HEADS = 2
IDX_HEADS = 8
IDX_DIM = 64
TOPK_MAX = 256
LRU_WIDTH = 512
LRU_BLOCKS = 8
CONV_WIDTH = 4
LRU_C = 8.0
C_HEADS = 8
Q_LORA = 256
KV_LORA = 128
QK_NOPE = 64
QK_ROPE = 32
V_DIM = 64
ROPE_THETA = 10000.0
D_HEADS = 8
D_KV_HEADS = 2
WINDOW = 128
N_BRANCH = 4
BRANCH_WIDTH = 512
N_ALIBI = A_HEADS + D_HEADS
D_FF = 2816
IN_SPLITS = (A_HEADS * HEAD_DIM, A_KV_HEADS * HEAD_DIM, A_KV_HEADS * HEAD_DIM,
             IDX_HEADS * IDX_DIM, IDX_DIM, IDX_HEADS,
             LRU_WIDTH, LRU_WIDTH,
             Q_LORA, KV_LORA + QK_ROPE,
             D_HEADS * HEAD_DIM, D_KV_HEADS * HEAD_DIM, D_KV_HEADS * HEAD_DIM,
             N_BRANCH * D_MODEL)
IN_WIDTH = sum(IN_SPLITS)

kernel_name = "hybrid_gated_dsa_rglru_mla_swa_macaron"


def rmsnorm(x, g):
    xf = x.astype(jnp.float32)
    y = xf * lax.rsqrt(jnp.mean(xf * xf, axis=-1, keepdims=True) + EPS)
    return (y * g.astype(jnp.float32)).astype(x.dtype)


def alibi_slopes():
    i = jnp.arange(1, N_ALIBI + 1, dtype=jnp.float32)
    return 2.0 ** (-8.0 * i / N_ALIBI)


def rope(x, pos):
    half = QK_ROPE // 2
    inv = ROPE_THETA ** (-jnp.arange(half, dtype=jnp.float32) / half)
    ang = pos.astype(jnp.float32)[:, None] * inv[None, :]
    cos, sin = jnp.cos(ang)[:, None, :], jnp.sin(ang)[:, None, :]
    xf = x.astype(jnp.float32)
    x1, x2 = xf[..., :half], xf[..., half:]
    return jnp.concatenate([x1 * cos - x2 * sin, x2 * cos + x1 * sin], axis=-1).astype(x.dtype)


def to_blocks(t, nb):
    return jnp.moveaxis(t.reshape(t.shape[0], nb, BLOCK, *t.shape[2:]), 1, 0)


def from_blocks(t):
    t = jnp.moveaxis(t, 0, 1)
    return t.reshape(t.shape[0], -1, t.shape[-1])


gather_rows = jax.vmap(lambda rows, idx: rows[idx])


def swiglu(h, w_in, w_out):
    gate, up = jnp.split(h @ w_in, 2, axis=-1)
    return (jax.nn.silu(gate) * up) @ w_out


def dsa_attention(q, k, v, qi, ki, wi, slopes):
    B, S, H, Dh = q.shape
    G = k.shape[2]
    R = H // G
    nb = S // BLOCK
    topk = min(TOPK_MAX, S // 4)
    scale = Dh ** -0.5
    key_pos = jnp.arange(S)
    slopes_gr = slopes.reshape(G, R).astype(jnp.float32)
    ki32 = ki.astype(jnp.float32)

    def one_block(args):
        qb, qib, wib, start = args
        qpos = start + jnp.arange(BLOCK)
        causal = key_pos[None, :] <= qpos[:, None]
        dots = jnp.einsum('bthd,bsd->bths', qib.astype(jnp.float32), ki32)
        iscore = jnp.einsum('bth,bths->bts', wib.astype(jnp.float32), jax.nn.relu(dots))
        iscore = jnp.where(causal[None], iscore, -jnp.inf)
        _, sel = lax.top_k(iscore, topk)
        ksel = gather_rows(k, sel)
        vsel = gather_rows(v, sel)
        dist = qpos[None, :, None] - sel
        valid = dist >= 0
        s = jnp.einsum('btgrd,btkgd->btgrk', qb.reshape(B, BLOCK, G, R, Dh), ksel).astype(jnp.float32) * scale
        s = s - slopes_gr[None, None, :, :, None] * dist.astype(jnp.float32)[:, :, None, None, :]
        s = jnp.where(valid[:, :, None, None, :], s, NEG)
        p = jax.nn.softmax(s, axis=-1).astype(v.dtype)
        o = jnp.einsum('btgrk,btkgd->btgrd', p, vsel)
        return o.reshape(B, BLOCK, H * Dh)

    starts = jnp.arange(nb, dtype=jnp.int32) * BLOCK
    out = lax.map(one_block, (to_blocks(q, nb), to_blocks(qi, nb), to_blocks(wi, nb), starts))
    return from_blocks(out)


def rglru_branch(xr, xg, conv_w, conv_b, wa, ba, wx, bx, lam):
    B, S, W = xr.shape
    xc = lax.conv_general_dilated(xr, conv_w.astype(xr.dtype).reshape(CONV_WIDTH, 1, W),
                                  window_strides=(1,), padding=[(CONV_WIDTH - 1, 0)],
                                  dimension_numbers=('NWC', 'WIO', 'NWC'),
                                  feature_group_count=W) + conv_b
    xb = xc.reshape(B, S, LRU_BLOCKS, W // LRU_BLOCKS)
    r = jax.nn.sigmoid((jnp.einsum('bsni,nij->bsnj', xb, wa).reshape(B, S, W) + ba).astype(jnp.float32))
    i = jax.nn.sigmoid((jnp.einsum('bsni,nij->bsnj', xb, wx).reshape(B, S, W) + bx).astype(jnp.float32))
    log_a = -LRU_C * r * jax.nn.softplus(-lam.astype(jnp.float32))
    a = jnp.exp(log_a)
    b = jnp.sqrt(-jnp.expm1(2.0 * log_a)) * (i * xc.astype(jnp.float32))

    def combine(left, right):
        a_l, b_l = left
        a_r, b_r = right
        return a_l * a_r, a_r * b_l + b_r

    _, h = lax.associative_scan(combine, (a, b), axis=1)
    return h.astype(xr.dtype) * jax.nn.gelu(xg)


def causal_block_attention(q, k, v, scale):
    B, S, H, _ = q.shape
    nb = S // BLOCK
    key_pos = jnp.arange(S)

    def one_block(args):
        qb, start = args
        qpos = start + jnp.arange(BLOCK)
        s = jnp.einsum('bthd,bshd->bhts', qb, k).astype(jnp.float32) * scale
        s = jnp.where((key_pos[None, :] <= qpos[:, None])[None, None], s, NEG)
        p = jax.nn.softmax(s, axis=-1).astype(v.dtype)
        return jnp.einsum('bhts,bshd->bthd', p, v).reshape(B, BLOCK, -1)

    starts = jnp.arange(nb, dtype=jnp.int32) * BLOCK
    return from_blocks(lax.map(one_block, (to_blocks(q, nb), starts)))


def mla_attention(cq, ckv_full, g_q, g_kv, w_uq, w_ukv):
    B, S, _ = cq.shape
    q = (rmsnorm(cq, g_q) @ w_uq).reshape(B, S, C_HEADS, QK_NOPE + QK_ROPE)
    c_kv, k_rope = ckv_full[..., :KV_LORA], ckv_full[..., KV_LORA:]
    kv = (rmsnorm(c_kv, g_kv) @ w_ukv).reshape(B, S, C_HEADS, QK_NOPE + V_DIM)
    k_nope, v = kv[..., :QK_NOPE], kv[..., QK_NOPE:]
    pos = jnp.arange(S)
    q = jnp.concatenate([q[..., :QK_NOPE], rope(q[..., QK_NOPE:], pos)], axis=-1)
    k_rope = jnp.broadcast_to(rope(k_rope[:, :, None, :], pos), (B, S, C_HEADS, QK_ROPE))
    k = jnp.concatenate([k_nope, k_rope], axis=-1)
    return causal_block_attention(q, k, v, (QK_NOPE + QK_ROPE) ** -0.5)


def sliding_window_attention(q, k, v, sinks, slopes):
    B, S, H, Dh = q.shape
    G = k.shape[2]
    R = H // G
    nb = S // BLOCK
    qb = q.reshape(B, nb, BLOCK, G, R, Dh)
    kb = k.reshape(B, nb, BLOCK, G, Dh)
    vb = v.reshape(B, nb, BLOCK, G, Dh)
    prev = lambda t: jnp.concatenate([jnp.zeros_like(t[:, :1]), t[:, :-1]], axis=1)
    kw = jnp.concatenate([prev(kb), kb], axis=2)
    vw = jnp.concatenate([prev(vb), vb], axis=2)
    blk = jnp.arange(nb)[:, None] * BLOCK
    qpos = blk + jnp.arange(BLOCK)[None, :]
    kpos = blk - BLOCK + jnp.arange(2 * BLOCK)[None, :]
    dist = qpos[:, :, None] - kpos[:, None, :]
    valid = (kpos[:, None, :] >= 0) & (dist >= 0) & (dist < WINDOW)
    s = jnp.einsum('bntgrd,bnkgd->bngrtk', qb, kw).astype(jnp.float32) * (Dh ** -0.5)
    s = s - slopes.reshape(G, R).astype(jnp.float32)[None, None, :, :, None, None] * dist.astype(jnp.float32)[None, :, None, None, :, :]
    s = jnp.where(valid[None, :, None, None], s, NEG)
    sink = jnp.broadcast_to(sinks.reshape(G, R).astype(jnp.float32)[None, None, :, :, None, None], s.shape[:-1] + (1,))
    p = jax.nn.softmax(jnp.concatenate([s, sink], axis=-1), axis=-1)[..., :-1].astype(v.dtype)
    o = jnp.einsum('bngrtk,bnkgd->bntgrd', p, vw)
    return o.reshape(B, S, H * Dh)


def token_mixer(h, w_in, conv_w, conv_b, lru_wa, lru_ba, lru_wx, lru_bx, lru_lambda,
                mla_g_q, mla_g_kv, mla_w_uq, mla_w_ukv, swa_sinks, w_branch, w_out):
    B, S, _ = h.shape
    z = h @ w_in
    splits = np.cumsum(IN_SPLITS)[:-1].tolist()
    (a_q, a_k, a_v, i_q, i_k, i_w, r_x, r_g, c_q, c_kv, d_q, d_k, d_v, gates) = jnp.split(z, splits, axis=-1)
    slopes = alibi_slopes()
    slopes_d, slopes_a = slopes[:D_HEADS], slopes[D_HEADS:]
    y_a = dsa_attention(a_q.reshape(B, S, A_HEADS, HEAD_DIM), a_k.reshape(B, S, A_KV_HEADS, HEAD_DIM),
                        a_v.reshape(B, S, A_KV_HEADS, HEAD_DIM), i_q.reshape(B, S, IDX_HEADS, IDX_DIM),
                        i_k, i_w, slopes_a)
    y_b = rglru_branch(r_x, r_g, conv_w, conv_b, lru_wa, lru_ba, lru_wx, lru_bx, lru_lambda)
    y_c = mla_attention(c_q, c_kv, mla_g_q, mla_g_kv, mla_w_uq, mla_w_ukv)
    y_d = sliding_window_attention(d_q.reshape(B, S, D_HEADS, HEAD_DIM), d_k.reshape(B, S, D_KV_HEADS, HEAD_DIM),
                                   d_v.reshape(B, S, D_KV_HEADS, HEAD_DIM), swa_sinks, slopes_d)
    ys = jnp.stack([y_a, y_b, y_c, y_d], axis=2)
    proj = jnp.einsum('bsnw,nwd->bsnd', ys, w_branch)
    g = jax.nn.sigmoid(gates.reshape(B, S, N_BRANCH, D_MODEL))
    return jnp.sum(g * proj, axis=2) @ w_out


def sandwich(x, y_fn, g_pre, g_post, shift, scale, gate, resid_w):
    h = rmsnorm(x, g_pre) * (1.0 + scale) + shift
    return x + resid_w * gate * rmsnorm(y_fn(h), g_post)


def setup_inputs(seed: int = 0) -> dict:
    key = jax.random.key(seed)
    ks = jax.random.split(key, 24)
    f32 = jnp.float32
    nrm = lambda k, shape, fan_in: jax.random.normal(k, shape, f32) * (fan_in ** -0.5)
    bw = LRU_WIDTH // LRU_BLOCKS
    u = jax.random.uniform(ks[14], (DEPTH, LRU_WIDTH), f32, 0.9, 0.999)
    s = u ** (1.0 / LRU_C)
    lru_lambda = jnp.log(s) - jnp.log1p(-s)
    return {
        "x": jax.random.normal(ks[0], (BATCH, SEQ, D_MODEL), f32),
        "c": jax.random.normal(ks[1], (BATCH, D_MODEL), f32),
        "w_ada": 0.5 * nrm(ks[2], (DEPTH, D_MODEL, 9 * D_MODEL), D_MODEL),
        "b_ada": 0.02 * jax.random.normal(ks[3], (DEPTH, 9 * D_MODEL), f32),
        "norm_g": 1.0 + 0.05 * jax.random.normal(ks[4], (DEPTH, 6, D_MODEL), f32),
        "ffn1_w_in": nrm(ks[5], (DEPTH, D_MODEL, 2 * D_FF), D_MODEL),
        "ffn1_w_out": nrm(ks[6], (DEPTH, D_FF, D_MODEL), D_FF),
        "w_in": nrm(ks[7], (DEPTH, D_MODEL, IN_WIDTH), D_MODEL),
        "conv_w": nrm(ks[8], (DEPTH, CONV_WIDTH, LRU_WIDTH), CONV_WIDTH),
        "conv_b": 0.02 * jax.random.normal(ks[9], (DEPTH, LRU_WIDTH), f32),
        "lru_wa": nrm(ks[10], (DEPTH, LRU_BLOCKS, bw, bw), bw),
        "lru_ba": 0.02 * jax.random.normal(ks[11], (DEPTH, LRU_WIDTH), f32),
        "lru_wx": nrm(ks[12], (DEPTH, LRU_BLOCKS, bw, bw), bw),
        "lru_bx": 0.02 * jax.random.normal(ks[13], (DEPTH, LRU_WIDTH), f32),
        "lru_lambda": lru_lambda,
        "mla_g_q": 1.0 + 0.05 * jax.random.normal(ks[15], (DEPTH, Q_LORA), f32),
        "mla_g_kv": 1.0 + 0.05 * jax.random.normal(ks[16], (DEPTH, KV_LORA), f32),
        "mla_w_uq": nrm(ks[17], (DEPTH, Q_LORA, C_HEADS * (QK_NOPE + QK_ROPE)), Q_LORA),
        "mla_w_ukv": nrm(ks[18], (DEPTH, KV_LORA, C_HEADS * (QK_NOPE + V_DIM)), KV_LORA),
        "swa_sinks": 0.5 * jax.random.normal(ks[19], (DEPTH, D_HEADS), f32),
        "w_branch": nrm(ks[20], (DEPTH, N_BRANCH, BRANCH_WIDTH, D_MODEL), BRANCH_WIDTH),
        "w_out": nrm(ks[21], (DEPTH, D_MODEL, D_MODEL), D_MODEL),
        "ffn2_w_in": nrm(ks[22], (DEPTH, D_MODEL, 2 * D_FF), D_MODEL),
        "ffn2_w_out": nrm(ks[23], (DEPTH, D_FF, D_MODEL), D_FF),
    }


def reference(x, c, w_ada, b_ada, norm_g, ffn1_w_in, ffn1_w_out, w_in, conv_w, conv_b,
              lru_wa, lru_ba, lru_wx, lru_bx, lru_lambda, mla_g_q, mla_g_kv, mla_w_uq, mla_w_ukv,
              swa_sinks, w_branch, w_out, ffn2_w_in, ffn2_w_out):
    B = x.shape[0]
    for l in range(DEPTH):
        mod = (jax.nn.silu(c) @ w_ada[l] + b_ada[l]).reshape(B, 9, 1, D_MODEL)
        g = norm_g[l]
        x = sandwich(x, lambda h: swiglu(h, ffn1_w_in[l], ffn1_w_out[l]),
                     g[0], g[1], mod[:, 0], mod[:, 1], mod[:, 2], 0.5)
        x = sandwich(x, lambda h: token_mixer(h, w_in[l], conv_w[l], conv_b[l], lru_wa[l], lru_ba[l],
                                              lru_wx[l], lru_bx[l], lru_lambda[l], mla_g_q[l], mla_g_kv[l],
                                              mla_w_uq[l], mla_w_ukv[l], swa_sinks[l], w_branch[l], w_out[l]),
                     g[2], g[3], mod[:, 3], mod[:, 4], mod[:, 5], 1.0)
        x = sandwich(x, lambda h: swiglu(h, ffn2_w_in[l], ffn2_w_out[l]),
                     g[4], g[5], mod[:, 6], mod[:, 7], mod[:, 8], 0.5)
    return x
```

```python
import functools

import numpy as np
import jax
import jax.numpy as jnp
from jax import lax
from jax.experimental import pallas as pl
from jax.experimental.pallas import tpu as pltpu

F32 = jnp.float32
BF16 = jnp.bfloat16
I32 = jnp.int32

D_MODEL = 1024
HEAD_DIM = 64
BLOCK = 128
EPS = 1e-6
NEG = -1e30
A_HEADS = 8
A_KV_HEADS = 2
IDX_HEADS = 8
IDX_DIM = 64
TOPK_MAX = 256
LRU_WIDTH = 512
LRU_BLOCKS = 8
CONV_WIDTH = 4
LRU_C = 8.0
C_HEADS = 8
Q_LORA = 256
KV_LORA = 128
QK_NOPE = 64
QK_ROPE = 32
V_DIM = 64
ROPE_THETA = 10000.0
D_HEADS = 8
D_KV_HEADS = 2
WINDOW = 128
N_BRANCH = 4
BRANCH_WIDTH = 512
N_ALIBI = A_HEADS + D_HEADS
D_FF = 2816
IN_SPLITS = (A_HEADS * HEAD_DIM, A_KV_HEADS * HEAD_DIM, A_KV_HEADS * HEAD_DIM,
             IDX_HEADS * IDX_DIM, IDX_DIM, IDX_HEADS,
             LRU_WIDTH, LRU_WIDTH,
             Q_LORA, KV_LORA + QK_ROPE,
             D_HEADS * HEAD_DIM, D_KV_HEADS * HEAD_DIM, D_KV_HEADS * HEAD_DIM,
             N_BRANCH * D_MODEL)
IN_OFFSETS = tuple(int(v) for v in np.concatenate([[0], np.cumsum(IN_SPLITS)]))

LANES = 128
INT_MIN = -2 ** 31
INT_MAX = 2 ** 31 - 1
VMEM_LIMIT = 56 * 1024 * 1024

QKV_W = 512 + 256 + 128
IDX_W = 512 + 128
MLA_W = 640

FFN_TM = 1024
FFN_TF = 256
PROJ_TM = 512
DSA_TQ = 128
DSA_KC = 512
MLA_TQ = 256
MLA_KC = 512
LRU_L = 256
MERGE_TM = 512


def _alibi(i):
    return float(2.0 ** (-8.0 * i / N_ALIBI))


SLOPES_D = tuple(_alibi(i) for i in range(1, D_HEADS + 1))
SLOPES_A = tuple(_alibi(i) for i in range(D_HEADS + 1, N_ALIBI + 1))


def _cparams(sem):
    return pltpu.CompilerParams(dimension_semantics=sem, vmem_limit_bytes=VMEM_LIMIT)


def _rms(x, g):
    return x * lax.rsqrt(jnp.mean(x * x, axis=-1, keepdims=True) + EPS) * g


def _nt_dot(a, b):
    return lax.dot_general(a, b, (((1,), (1,)), ((), ())), preferred_element_type=F32)


def _expm1(y):
    u = jnp.exp(y)
    safe = (u != 1.0) & (y > -1.0)
    ratio = y / jnp.log(jnp.where(safe, u, 2.0))
    return jnp.where(u == 1.0, y, jnp.where(safe, (u - 1.0) * ratio, u - 1.0))


def _log2(n):
    l = int(n).bit_length() - 1
    assert (1 << l) == n
    return l


def _ada_kernel(c_ref, w_ref, b_ref, o_ref):
    c = c_ref[...]
    sc = (c * jax.nn.sigmoid(c)).astype(BF16)
    o_ref[0] = jnp.dot(sc, w_ref[0].astype(BF16), preferred_element_type=F32) + b_ref[0]


def _ada_call(c_pad, w_ada, b_ada):
    depth, d, n = w_ada.shape
    tn = 1152
    return pl.pallas_call(
        _ada_kernel,
        grid=(depth, n // tn),
        in_specs=[pl.BlockSpec((c_pad.shape[0], d), lambda l, j: (0, 0)),
                  pl.BlockSpec((1, d, tn), lambda l, j: (l, 0, j)),
                  pl.BlockSpec((1, 1, tn), lambda l, j: (l, 0, j))],
        out_specs=pl.BlockSpec((1, c_pad.shape[0], tn), lambda l, j: (l, 0, j)),
        out_shape=jax.ShapeDtypeStruct((depth, c_pad.shape[0], n), F32),
        compiler_params=_cparams(("arbitrary", "arbitrary")),
        name="adaln",
    )(c_pad, w_ada, b_ada.reshape(depth, 1, n))


def _ffn_kernel(x_ref, mod_ref, g_ref, wg_ref, wu_ref, wo_ref, o_ref, h_ref, acc_ref, *, sub, resid_w):
    k = pl.program_id(1)
    m = mod_ref[0]

    @pl.when(k == 0)
    def _():
        x = x_ref[...]
        xn = _rms(x, g_ref[2 * sub:2 * sub + 1])
        h_ref[...] = (xn * (1.0 + m[3 * sub + 1:3 * sub + 2]) + m[3 * sub:3 * sub + 1]).astype(BF16)
        acc_ref[...] = jnp.zeros_like(acc_ref)

    h = h_ref[...]
    gate = jnp.dot(h, wg_ref[...], preferred_element_type=F32)
    up = jnp.dot(h, wu_ref[...], preferred_element_type=F32)
    act = (gate * jax.nn.sigmoid(gate) * up).astype(BF16)
    acc_ref[...] += jnp.dot(act, wo_ref[...], preferred_element_type=F32)

    @pl.when(k == pl.num_programs(1) - 1)
    def _():
        yn = _rms(acc_ref[...], g_ref[2 * sub + 1:2 * sub + 2])
        o_ref[...] = x_ref[...] + resid_w * m[3 * sub + 2:3 * sub + 3] * yn


def _ffn_call(x2, mod_l, g_l, w_in_bf, w_out_bf, *, sub, resid_w, s_len):
    t, d = x2.shape
    f = w_out_bf.shape[0]
    tm = min(FFN_TM, s_len)
    tf = FFN_TF
    nf = f // tf
    tpb = s_len // tm
    return pl.pallas_call(
        functools.partial(_ffn_kernel, sub=sub, resid_w=resid_w),
        grid=(t // tm, nf),
        in_specs=[pl.BlockSpec((tm, d), lambda i, k: (i, 0)),
                  pl.BlockSpec((1, 9, d), lambda i, k: (i // tpb, 0, 0)),
                  pl.BlockSpec((6, d), lambda i, k: (0, 0)),
                  pl.BlockSpec((d, tf), lambda i, k: (0, k)),
                  pl.BlockSpec((d, tf), lambda i, k: (0, k + nf)),
                  pl.BlockSpec((tf, d), lambda i, k: (k, 0))],
        out_specs=pl.BlockSpec((tm, d), lambda i, k: (i, 0)),
        out_shape=jax.ShapeDtypeStruct((t, d), F32),
        scratch_shapes=[pltpu.VMEM((tm, d), BF16), pltpu.VMEM((tm, d), F32)],
        compiler_params=_cparams(("arbitrary", "arbitrary")),
        name="ffn",
    )(x2, mod_l, g_l, w_in_bf, w_in_bf, w_out_bf)


def _proj_kernel(x_ref, mod_ref, g_ref, w_ref, oa_ref, oi_ref, ow_ref, or_ref, oc_ref, od_ref):
    m = mod_ref[0]
    xn = _rms(x_ref[...], g_ref[2:3])
    h = (xn * (1.0 + m[4:5]) + m[3:4]).astype(BF16)
    off = 0
    for ref in (oa_ref, oi_ref, ow_ref, or_ref, oc_ref, od_ref):
        w = ref.shape[1]
        ref[...] = jnp.dot(h, w_ref[:, off:off + w], preferred_element_type=F32).astype(ref.dtype)
        off += w


def _proj_call(x2, mod_l, g_l, w_all, *, s_len):
    t, d = x2.shape
    tm = min(PROJ_TM, s_len)
    tpb = s_len // tm
    widths = (QKV_W, IDX_W, LANES, 2 * LRU_WIDTH, MLA_W, QKV_W)
    dtypes = (BF16, BF16, F32, F32, F32, BF16)
    assert sum(widths) == w_all.shape[1]
    return pl.pallas_call(
        _proj_kernel,
        grid=(t // tm,),
        in_specs=[pl.BlockSpec((tm, d), lambda i: (i, 0)),
                  pl.BlockSpec((1, 9, d), lambda i: (i // tpb, 0, 0)),
                  pl.BlockSpec((6, d), lambda i: (0, 0)),
                  pl.BlockSpec(w_all.shape, lambda i: (0, 0))],
        out_specs=[pl.BlockSpec((tm, w), lambda i: (i, 0)) for w in widths],
        out_shape=[jax.ShapeDtypeStruct((t, w), dt) for w, dt in zip(widths, dtypes)],
        compiler_params=_cparams(("arbitrary",)),
        name="mixer_proj",
    )(x2, mod_l, g_l, w_all)


def _head_rows(q, n_heads, tq):
    lane = lax.broadcasted_iota(I32, (tq, LANES), 1)
    rows = []
    for h in range(n_heads):
        tile = q[:, LANES * (h // 2):LANES * (h // 2 + 1)].astype(F32)
        keep = (lane >= HEAD_DIM) if (h % 2) else (lane < HEAD_DIM)
        rows.append(jnp.where(keep, tile, 0.0).astype(BF16))
    return jnp.concatenate(rows, axis=0)


def _slope_col(slopes, tq):
    return jnp.concatenate([jnp.full((tq, 1), s, F32) for s in slopes], axis=0)


def _gqa_out(acc_ref, inv_l, tq):
    lane = lax.broadcasted_iota(I32, (tq, LANES), 1)
    tiles = []
    for j in range(4):
        g = j // 2
        halves = []
        for h in (2 * j, 2 * j + 1):
            hl = h % 4
            o = acc_ref[g, hl * tq:(hl + 1) * tq, :] * inv_l[g][hl * tq:(hl + 1) * tq]
            src_hi = (g == 1)
            dst_hi = (h % 2 == 1)
            if src_hi != dst_hi:
                o = pltpu.roll(o, HEAD_DIM, axis=1)
            halves.append(o)
        tiles.append(jnp.where(lane < HEAD_DIM, halves[0], halves[1]))
    return jnp.concatenate(tiles, axis=1)


def _dsa_kernel(q_ref, k_ref, v_ref, iq_ref, ik_ref, iw_ref, o_ref,
                key_ref, tau_ref, qs_ref, m_ref, l_ref, acc_ref, *, tq, kc, topk, pos_bits):
    qb = pl.program_id(1)
    q0 = qb * tq
    nk = lax.shift_right_logical(q0 + tq + kc - 1, _log2(kc))
    rowpos = q0 + lax.broadcasted_iota(I32, (tq, kc), 0)
    lane = lax.broadcasted_iota(I32, (tq, kc), 1)
    n_lt = kc // LANES
    lane_t = lax.broadcasted_iota(I32, (tq, LANES), 1)

    iqs = _head_rows(iq_ref[...], IDX_HEADS, tq)
    iw = iw_ref[...]

    def score_body(j, carry):
        ks = pl.multiple_of(j * kc, kc)
        kic = ik_ref[pl.ds(ks, kc), :]
        acc = jnp.zeros((tq, kc), F32)
        for h in range(IDX_HEADS):
            d = _nt_dot(iqs[h * tq:(h + 1) * tq], kic)
            acc = acc + iw[:, h:h + 1] * jnp.maximum(d, 0.0)
        bits = pltpu.bitcast(acc, I32)
        key = jnp.where(bits < 0, bits ^ INT_MAX, bits)
        key_ref[j] = jnp.where(ks + lane <= rowpos, key, INT_MIN)
        return carry

    lax.fori_loop(0, nk, score_body, 0)

    def count(pred):
        def body(j, cnt):
            kj = key_ref[j]
            for c in range(n_lt):
                kp = (j * kc + c * LANES) + lane_t
                cnt = cnt + jnp.where(pred(kj[:, c * LANES:(c + 1) * LANES], kp), 1.0, 0.0)
            return cnt
        cnt = lax.fori_loop(0, nk, body, jnp.zeros((tq, LANES), F32))
        return jnp.sum(cnt, axis=1, keepdims=True)

    def count_ge(cand):
        cb = jnp.broadcast_to(cand, (tq, LANES))
        return count(lambda kv, kp: kv >= cb)

    kf = float(topk)
    tau = jnp.full((tq, 1), INT_MIN, I32)
    zero = jnp.zeros((tq, 1), I32)
    tau = jnp.where(count_ge(zero) >= kf, zero, tau)

    def bit_body(i, tau):
        cand = tau | lax.shift_left(jnp.int32(1), 30 - i)
        return jnp.where(count_ge(cand) >= kf, cand, tau)

    tau = lax.fori_loop(0, 31, bit_body, tau)

    n_ge = count_ge(tau)
    tie_f = jnp.where((n_ge > kf) & (tau > INT_MIN), 1.0, 0.0)
    any_tie = jnp.max(tie_f) > 0.0
    tau_c = jnp.maximum(tau, INT_MIN + 1)
    tau_ref[...] = tau_c

    @pl.when(any_tie)
    def _():
        tb = jnp.broadcast_to(tau_c, (tq, LANES))
        need = kf - count(lambda kv, kp: kv > tb)

        def pos_body(i, p):
            cand = p | lax.shift_left(jnp.int32(1), pos_bits - 1 - i)
            cb = jnp.broadcast_to(cand, (tq, LANES))
            below = count(lambda kv, kp: (kv == tb) & (kp < cb))
            return jnp.where(below < need, cand, p)

        p = lax.fori_loop(0, pos_bits, pos_body, jnp.zeros((tq, 1), I32))
        p = jnp.where(tie_f > 0.0, p, INT_MAX)

        def rewrite_body(j, carry):
            kj = key_ref[j]
            kp = j * kc + lane
            sel = (kj > tau_c) | ((kj == tau_c) & (kp <= p))
            key_ref[j] = jnp.where(sel, 1, -1)
            return carry

        lax.fori_loop(0, nk, rewrite_body, 0)
        tau_ref[...] = jnp.zeros((tq, 1), I32)

    q = q_ref[...]
    for g in range(A_KV_HEADS):
        qs_ref[g] = _head_rows(q[:, 4 * HEAD_DIM * g:4 * HEAD_DIM * (g + 1)], 4, tq)
    m_ref[...] = jnp.full(m_ref.shape, -jnp.inf, F32)
    l_ref[...] = jnp.zeros(l_ref.shape, F32)
    acc_ref[...] = jnp.zeros(acc_ref.shape, F32)
    scale = HEAD_DIM ** -0.5
    slope = [_slope_col(SLOPES_A[4 * g:4 * g + 4], tq) for g in range(A_KV_HEADS)]

    def attn_body(j, carry):
        ks = pl.multiple_of(j * kc, kc)
        kch = k_ref[pl.ds(ks, kc), :]
        vch = v_ref[pl.ds(ks, kc), :]
        sel = jnp.where(key_ref[j] >= tau_ref[...], 1.0, 0.0)
        sel4 = jnp.concatenate([sel] * 4, axis=0)
        dist = (rowpos - (ks + lane)).astype(F32)
        dist4 = jnp.concatenate([dist] * 4, axis=0)
        for g in range(A_KV_HEADS):
            s = _nt_dot(qs_ref[g], kch[:, g * LANES:(g + 1) * LANES]) * scale
            s = s - slope[g] * dist4
            s = jnp.where(sel4 > 0.0, s, NEG)
            m_old = m_ref[g]
            m_new = jnp.maximum(m_old, jnp.max(s, axis=1, keepdims=True))
            alpha = jnp.exp(m_old - m_new)
            p = jnp.exp(s - m_new)
            l_ref[g] = alpha * l_ref[g] + jnp.sum(p, axis=1, keepdims=True)
            acc_ref[g] = alpha * acc_ref[g] + jnp.dot(p.astype(BF16), vch, preferred_element_type=F32)
            m_ref[g] = m_new
        return carry

    lax.fori_loop(0, nk, attn_body, 0)
    inv_l = [1.0 / l_ref[g] for g in range(A_KV_HEADS)]
    o_ref[...] = _gqa_out(acc_ref, inv_l, tq).astype(o_ref.dtype)


def _dsa_call(qkv, idx, iw, *, batch, s_len):
    t = qkv.shape[0]
    tq = min(DSA_TQ, s_len)
    kc = min(DSA_KC, s_len)
    topk = min(TOPK_MAX, s_len // 4)
    assert kc >= topk
    nqb = s_len // tq
    return pl.pallas_call(
        functools.partial(_dsa_kernel, tq=tq, kc=kc, topk=topk, pos_bits=_log2(s_len)),
        grid=(batch, nqb),
        in_specs=[pl.BlockSpec((tq, 512), lambda b, i: (b * nqb + i, 0)),
                  pl.BlockSpec((s_len, 256), lambda b, i: (b, 2)),
                  pl.BlockSpec((s_len, 128), lambda b, i: (b, 6)),
                  pl.BlockSpec((tq, 512), lambda b, i: (b * nqb + i, 0)),
                  pl.BlockSpec((s_len, 128), lambda b, i: (b, 4)),
                  pl.BlockSpec((tq, LANES), lambda b, i: (b * nqb + i, 0))],
        out_specs=pl.BlockSpec((tq, 512), lambda b, i: (b * nqb + i, 0)),
        out_shape=jax.ShapeDtypeStruct((t, 512), BF16),
        scratch_shapes=[pltpu.VMEM((s_len // kc, tq, kc), I32),
                        pltpu.VMEM((tq, 1), I32),
                        pltpu.VMEM((A_KV_HEADS, 4 * tq, LANES), BF16),
                        pltpu.VMEM((A_KV_HEADS, 4 * tq, 1), F32),
                        pltpu.VMEM((A_KV_HEADS, 4 * tq, 1), F32),
                        pltpu.VMEM((A_KV_HEADS, 4 * tq, LANES), F32)],
        compiler_params=_cparams(("arbitrary", "arbitrary")),
        name="dsa",
    )(qkv, qkv, qkv, idx, idx, iw)


def _swa_kernel(q_ref, kp_ref, kc_ref, vp_ref, vc_ref, sink_ref, o_ref, acc_ref, *, tq):
    i = pl.program_id(1)
    q = q_ref[...]
    kk = jnp.concatenate([kp_ref[...], kc_ref[...]], axis=0)
    vv = jnp.concatenate([vp_ref[...], vc_ref[...]], axis=0)
    row = lax.broadcasted_iota(I32, (tq, 2 * tq), 0)
    col = lax.broadcasted_iota(I32, (tq, 2 * tq), 1)
    dist = row + tq - col
    valid = (dist >= 0) & (dist < WINDOW) & ((i * tq - tq + col) >= 0)
    valid4 = jnp.concatenate([jnp.where(valid, 1.0, 0.0)] * 4, axis=0)
    dist4 = jnp.concatenate([dist.astype(F32)] * 4, axis=0)
    scale = HEAD_DIM ** -0.5
    sinks = sink_ref[...]
    inv_l = []
    for g in range(D_KV_HEADS):
        qs = _head_rows(q[:, 4 * HEAD_DIM * g:4 * HEAD_DIM * (g + 1)], 4, tq)
        s = _nt_dot(qs, kk[:, g * LANES:(g + 1) * LANES]) * scale
        s = s - _slope_col(SLOPES_D[4 * g:4 * g + 4], tq) * dist4
        s = jnp.where(valid4 > 0.0, s, NEG)
        sink = jnp.concatenate([jnp.broadcast_to(sinks[:, 4 * g + h:4 * g + h + 1], (tq, 1)) for h in range(4)],
                               axis=0)
        m = jnp.maximum(jnp.max(s, axis=1, keepdims=True), sink)
        p = jnp.exp(s - m)
        l = jnp.sum(p, axis=1, keepdims=True) + jnp.exp(sink - m)
        acc_ref[g] = jnp.dot(p.astype(BF16), vv, preferred_element_type=F32)
        inv_l.append(1.0 / l)
    o_ref[...] = _gqa_out(acc_ref, inv_l, tq).astype(o_ref.dtype)


def _swa_call(qkv, sinks_pad, *, batch, s_len):
    t = qkv.shape[0]
    tq = BLOCK
    nqb = s_len // tq
    cur = lambda b, i: b * nqb + i
    prev = lambda b, i: b * nqb + jnp.maximum(i - 1, 0)
    return pl.pallas_call(
        functools.partial(_swa_kernel, tq=tq),
        grid=(batch, nqb),
        in_specs=[pl.BlockSpec((tq, 512), lambda b, i: (cur(b, i), 0)),
                  pl.BlockSpec((tq, 256), lambda b, i: (prev(b, i), 2)),
                  pl.BlockSpec((tq, 256), lambda b, i: (cur(b, i), 2)),
                  pl.BlockSpec((tq, 128), lambda b, i: (prev(b, i), 6)),
                  pl.BlockSpec((tq, 128), lambda b, i: (cur(b, i), 6)),
                  pl.BlockSpec((1, LANES), lambda b, i: (0, 0))],
        out_specs=pl.BlockSpec((tq, 512), lambda b, i: (cur(b, i), 0)),
        out_shape=jax.ShapeDtypeStruct((t, 512), BF16),
        scratch_shapes=[pltpu.VMEM((D_KV_HEADS, 4 * tq, LANES), F32)],
        compiler_params=_cparams(("arbitrary", "arbitrary")),
        name="swa",
    )(qkv, qkv, qkv, qkv, qkv, sinks_pad)


def _mla_prep_kernel(c_ref, cos_ref, sin_ref, gq_ref, gkv_ref, wq_ref, wqr_ref, wk_ref, wv_ref,
                     q_ref, k_ref, v_ref):
    c = c_ref[...]
    cos = cos_ref[...]
    sin = sin_ref[...]
    cos8 = jnp.concatenate([cos] * C_HEADS, axis=1)
    sin8 = jnp.concatenate([sin] * C_HEADS, axis=1)
    cqn = _rms(c[:, :Q_LORA], gq_ref[...]).astype(BF16)
    q = (jnp.dot(cqn, wq_ref[...], preferred_element_type=F32) * cos8
         + jnp.dot(cqn, wqr_ref[...], preferred_element_type=F32) * sin8)
    q_ref[...] = q.astype(BF16)
    ckvn = _rms(c[:, Q_LORA:Q_LORA + KV_LORA], gkv_ref[...]).astype(BF16)
    kr = c[:, 384:512] * cos + c[:, 512:640] * sin
    k = jnp.dot(ckvn, wk_ref[...], preferred_element_type=F32) + jnp.concatenate([kr] * C_HEADS, axis=1)
    k_ref[...] = k.astype(BF16)
    v_ref[...] = jnp.dot(ckvn, wv_ref[...], preferred_element_type=F32).astype(BF16)


def _mla_prep_call(cm, cos_t, sin_t, gq, gkv, wq, wqr, wk, wv, *, s_len):
    t = cm.shape[0]
    tm = min(PROJ_TM, s_len)
    tpb = s_len // tm
    full = lambda a: pl.BlockSpec(a.shape, lambda i: (0, 0))
    return pl.pallas_call(
        _mla_prep_kernel,
        grid=(t // tm,),
        in_specs=[pl.BlockSpec((tm, MLA_W), lambda i: (i, 0)),
                  pl.BlockSpec((tm, LANES), lambda i: (i % tpb, 0)),
                  pl.BlockSpec((tm, LANES), lambda i: (i % tpb, 0)),
                  full(gq), full(gkv), full(wq), full(wqr), full(wk), full(wv)],
        out_specs=[pl.BlockSpec((tm, 1024), lambda i: (i, 0)),
                   pl.BlockSpec((tm, 1024), lambda i: (i, 0)),
                   pl.BlockSpec((tm, 512), lambda i: (i, 0))],
        out_shape=[jax.ShapeDtypeStruct((t, 1024), BF16),
                   jax.ShapeDtypeStruct((t, 1024), BF16),
                   jax.ShapeDtypeStruct((t, 512), BF16)],
        compiler_params=_cparams(("arbitrary",)),
        name="mla_prep",
    )(cm, cos_t, sin_t, gq, gkv, wq, wqr, wk, wv)


def _mla_attn_kernel(q_ref, k_ref, v_ref, o_ref, *, tq, kc):
    qb = pl.program_id(2)
    q0 = qb * tq
    nk = lax.shift_right_logical(q0 + tq + kc - 1, _log2(kc))
    rowpos = q0 + lax.broadcasted_iota(I32, (tq, kc), 0)
    lane = lax.broadcasted_iota(I32, (tq, kc), 1)
    scale = (QK_NOPE + QK_ROPE) ** -0.5
    outs = []
    for hh in range(2):
        q = q_ref[:, hh * LANES:(hh + 1) * LANES]

        def body(j, carry):
            m_old, l_old, acc = carry
            ks = pl.multiple_of(j * kc, kc)
            s = _nt_dot(q, k_ref[pl.ds(ks, kc), hh * LANES:(hh + 1) * LANES]) * scale
            s = jnp.where(ks + lane <= rowpos, s, NEG)
            m_new = jnp.maximum(m_old, jnp.max(s, axis=1, keepdims=True))
            alpha = jnp.exp(m_old - m_new)
            p = jnp.exp(s - m_new)
            l_new = alpha * l_old + jnp.sum(p, axis=1, keepdims=True)
            acc = alpha * acc + jnp.dot(p.astype(BF16), v_ref[pl.ds(ks, kc), :], preferred_element_type=F32)
            return m_new, l_new, acc

        init = (jnp.full((tq, 1), -jnp.inf, F32), jnp.zeros((tq, 1), F32), jnp.zeros((tq, LANES), F32))
        _, l, acc = lax.fori_loop(0, nk, body, init)
        outs.append(acc * (1.0 / l))
    lane_o = lax.broadcasted_iota(I32, (tq, LANES), 1)
    o_ref[...] = jnp.where(lane_o < V_DIM, outs[0], outs[1]).astype(o_ref.dtype)


def _mla_attn_call(qm, km, vm, *, batch, s_len):
    t = qm.shape[0]
    tq = min(MLA_TQ, s_len)
    kc = min(MLA_KC, s_len)
    nqb = s_len // tq
    return pl.pallas_call(
        functools.partial(_mla_attn_kernel, tq=tq, kc=kc),
        grid=(batch, C_HEADS // 2, nqb),
        in_specs=[pl.BlockSpec((tq, 256), lambda b, h, i: (b * nqb + i, h)),
                  pl.BlockSpec((s_len, 256), lambda b, h, i: (b, h)),
                  pl.BlockSpec((s_len, 128), lambda b, h, i: (b, h))],
        out_specs=pl.BlockSpec((tq, 128), lambda b, h, i: (b * nqb + i, h)),
        out_shape=jax.ShapeDtypeStruct((t, 512), BF16),
        compiler_params=_cparams(("arbitrary", "arbitrary", "arbitrary")),
        name="mla_attn",
    )(qm, km, vm)


def _lru_kernel(xr_ref, xg_ref, prev_ref, cw_ref, cb_ref, wa_ref, ba_ref, wx_ref, bx_ref, lam_ref,
                o_ref, xe_ref, h_ref, *, ln):
    i = pl.program_id(1)
    xe_ref[0:8, :] = jnp.where(i > 0, prev_ref[...], 0.0)
    xe_ref[8:8 + ln, :] = xr_ref[...]
    cw = cw_ref[...]
    xc = cb_ref[...] + cw[0:1] * xe_ref[pl.ds(5, ln), :]
    for j in range(1, CONV_WIDTH):
        xc = xc + cw[j:j + 1] * xe_ref[pl.ds(5 + j, ln), :]
    xcb = xc.astype(BF16)
    r = jax.nn.sigmoid(jnp.dot(xcb, wa_ref[...], preferred_element_type=F32) + ba_ref[...])
    gi = jax.nn.sigmoid(jnp.dot(xcb, wx_ref[...], preferred_element_type=F32) + bx_ref[...])
    z = -lam_ref[...]
    softplus = jnp.maximum(z, 0.0) + jnp.log1p(jnp.exp(-jnp.abs(z)))
    log_a = -LRU_C * r * softplus
    a = jnp.exp(log_a)
    b = jnp.sqrt(-_expm1(2.0 * log_a)) * (gi * xc)
    row = lax.broadcasted_iota(I32, (ln, LRU_WIDTH), 0)
    d = 1
    while d < ln:
        keep = row >= d
        b = jnp.where(keep, a * pltpu.roll(b, d, axis=0) + b, b)
        a = jnp.where(keep, a * pltpu.roll(a, d, axis=0), a)
        d *= 2
    h_prev = jnp.where(i > 0, h_ref[0:1, :], 0.0)
    h = a * h_prev + b
    h_ref[0:1, :] = h[ln - 1:ln, :]
    xg = xg_ref[...]
    gelu = 0.5 * xg * (1.0 + jnp.tanh(0.7978845608028654 * (xg + 0.044715 * (xg * xg * xg))))
    o_ref[...] = (h * gelu).astype(o_ref.dtype)


def _lru_call(rm, cw, cb, wa, ba, wx, bx, lam, *, batch, s_len):
    t = rm.shape[0]
    ln = min(LRU_L, s_len)
    nt = s_len // ln
    w = LRU_WIDTH
    vec = lambda a: pl.BlockSpec(a.shape, lambda b, i: (0, 0))
    return pl.pallas_call(
        functools.partial(_lru_kernel, ln=ln),
        grid=(batch, nt),
        in_specs=[pl.BlockSpec((ln, w), lambda b, i: (b * nt + i, 0)),
                  pl.BlockSpec((ln, w), lambda b, i: (b * nt + i, 1)),
                  pl.BlockSpec((8, w), lambda b, i: (jnp.maximum((b * nt + i) * (ln // 8) - 1, 0), 0)),
                  vec(cw), vec(cb), vec(wa), vec(ba), vec(wx), vec(bx), vec(lam)],
        out_specs=pl.BlockSpec((ln, w), lambda b, i: (b * nt + i, 0)),
        out_shape=jax.ShapeDtypeStruct((t, w), BF16),
        scratch_shapes=[pltpu.VMEM((ln + 8, w), F32), pltpu.VMEM((8, w), F32)],
        compiler_params=_cparams(("arbitrary", "arbitrary")),
        name="rglru",
    )(rm, rm, rm, cw, cb, wa, ba, wx, bx, lam)


def _merge_kernel(x_ref, mod_ref, g_ref, ya_ref, yb_ref, yc_ref, yd_ref, wg_ref, wb_ref, wo_ref, o_ref):
    m = mod_ref[0]
    x = x_ref[...]
    h = (_rms(x, g_ref[2:3]) * (1.0 + m[4:5]) + m[3:4]).astype(BF16)
    merged = None
    for n, y_ref in enumerate((ya_ref, yb_ref, yc_ref, yd_ref)):
        gate = jax.nn.sigmoid(jnp.dot(h, wg_ref[:, n * D_MODEL:(n + 1) * D_MODEL], preferred_element_type=F32))
        term = gate * jnp.dot(y_ref[...], wb_ref[n], preferred_element_type=F32)
        merged = term if merged is None else merged + term
    y = jnp.dot(merged.astype(BF16), wo_ref[...], preferred_element_type=F32)
    o_ref[...] = x + m[5:6] * _rms(y, g_ref[3:4])


def _merge_call(x2, mod_l, g_l, ys, w_gate, w_branch, w_out, *, s_len):
    t, d = x2.shape
    tm = min(MERGE_TM, s_len)
    tpb = s_len // tm
    return pl.pallas_call(
        _merge_kernel,
        grid=(t // tm,),
        in_specs=[pl.BlockSpec((tm, d), lambda i: (i, 0)),
                  pl.BlockSpec((1, 9, d), lambda i: (i // tpb, 0, 0)),
                  pl.BlockSpec((6, d), lambda i: (0, 0))]
                 + [pl.BlockSpec((tm, BRANCH_WIDTH), lambda i: (i, 0))] * N_BRANCH
                 + [pl.BlockSpec(w_gate.shape, lambda i: (0, 0)),
                    pl.BlockSpec(w_branch.shape, lambda i: (0, 0, 0)),
                    pl.BlockSpec(w_out.shape, lambda i: (0, 0))],
        out_specs=pl.BlockSpec((tm, d), lambda i: (i, 0)),
        out_shape=jax.ShapeDtypeStruct((t, d), F32),
        compiler_params=_cparams(("arbitrary",)),
        name="merge",
    )(x2, mod_l, g_l, *ys, w_gate, w_branch, w_out)


def _mixer_weights(w_in_l):
    o = IN_OFFSETS
    col = lambda n: w_in_l[:, o[n]:o[n + 1]]
    z = lambda n: jnp.zeros((D_MODEL, n), w_in_l.dtype)

    def dup_heads(k):
        return jnp.concatenate([k[:, :64], k[:, :64], k[:, 64:], k[:, 64:]], axis=1)

    a_q, a_k, a_v, i_q, i_k, i_w, r_x, r_g, c_q, c_kv, d_q, d_k, d_v = (col(n) for n in range(13))
    k_rope = c_kv[:, KV_LORA:]
    half = QK_ROPE // 2
    k_rope_rot = jnp.concatenate([-k_rope[:, half:], k_rope[:, :half]], axis=1)
    groups = [a_q, dup_heads(a_k), a_v,
              i_q, i_k, i_k,
              i_w, z(LANES - IDX_HEADS),
              r_x, r_g,
              c_q, c_kv[:, :KV_LORA], z(64), k_rope, z(32), z(64), k_rope_rot, z(32),
              d_q, dup_heads(d_k), d_v]
    w_all = jnp.concatenate(groups, axis=1).astype(BF16)
    w_gate = col(13).astype(BF16)
    return w_all, w_gate


def _mla_weights(w_uq, w_ukv):
    dq = QK_NOPE + QK_ROPE
    half = QK_ROPE // 2
    zq = lambda n: jnp.zeros((Q_LORA, n), w_uq.dtype)
    wq, wqr, wk, wv = [], [], [], []
    for h in range(C_HEADS):
        nope = w_uq[:, h * dq:h * dq + QK_NOPE]
        r1 = w_uq[:, h * dq + QK_NOPE:h * dq + QK_NOPE + half]
        r2 = w_uq[:, h * dq + QK_NOPE + half:(h + 1) * dq]
        wq += [nope, r1, r2, zq(32)]
        wqr += [zq(64), -r2, r1, zq(32)]
        wk += [w_ukv[:, h * 128:h * 128 + QK_NOPE], jnp.zeros((KV_LORA, 64), w_ukv.dtype)]
        wv += [w_ukv[:, h * 128 + QK_NOPE:(h + 1) * 128]]
    cat = lambda xs: jnp.concatenate(xs, axis=1).astype(BF16)
    return cat(wq), cat(wqr), cat(wk), cat(wv)


def _rope_tables(s_len):
    half = QK_ROPE // 2
    inv = ROPE_THETA ** (-jnp.arange(half, dtype=F32) / half)
    ang = jnp.arange(s_len, dtype=F32)[:, None] * inv[None, :]
    cos, sin = jnp.cos(ang), jnp.sin(ang)
    ones = lambda n: jnp.ones((s_len, n), F32)
    zeros = lambda n: jnp.zeros((s_len, n), F32)
    cos_t = jnp.concatenate([ones(64), cos, cos, ones(32)], axis=1)
    sin_t = jnp.concatenate([zeros(64), sin, sin, zeros(32)], axis=1)
    return cos_t, sin_t


def _block_diag(w):
    n, bw, _ = w.shape
    out = jnp.zeros((n * bw, n * bw), w.dtype)
    for k in range(n):
        out = lax.dynamic_update_slice(out, w[k], (k * bw, k * bw))
    return out.astype(BF16)


def kernel(x, c, w_ada, b_ada, norm_g, ffn1_w_in, ffn1_w_out, w_in, conv_w, conv_b, lru_wa, lru_ba, lru_wx,
           lru_bx, lru_lambda, mla_g_q, mla_g_kv, mla_w_uq, mla_w_ukv, swa_sinks, w_branch, w_out, ffn2_w_in,
           ffn2_w_out):
    batch, s_len, d = x.shape
    depth = w_ada.shape[0]
    t = batch * s_len
    x2 = x.reshape(t, d)
    c_pad = jnp.zeros((8, d), F32).at[:batch].set(c)
    mod_all = _ada_call(c_pad, w_ada, b_ada)
    cos_t, sin_t = _rope_tables(s_len)
    row = lambda v: v.reshape(1, -1)
    for l in range(depth):
        mod_l = mod_all[l, :batch].reshape(batch, 9, d)
        g_l = norm_g[l]
        x2 = _ffn_call(x2, mod_l, g_l, ffn1_w_in[l].astype(BF16), ffn1_w_out[l].astype(BF16),
                       sub=0, resid_w=0.5, s_len=s_len)
        w_all, w_gate = _mixer_weights(w_in[l])
        qkv_a, idx, iw, rm, cm, qkv_d = _proj_call(x2, mod_l, g_l, w_all, s_len=s_len)
        y_a = _dsa_call(qkv_a, idx, iw, batch=batch, s_len=s_len)
        y_b = _lru_call(rm, conv_w[l], row(conv_b[l]), _block_diag(lru_wa[l]), row(lru_ba[l]),
                        _block_diag(lru_wx[l]), row(lru_bx[l]), row(lru_lambda[l]), batch=batch, s_len=s_len)
        qm, km, vm = _mla_prep_call(cm, cos_t, sin_t, row(mla_g_q[l]), row(mla_g_kv[l]),
                                    *_mla_weights(mla_w_uq[l], mla_w_ukv[l]), s_len=s_len)
        y_c = _mla_attn_call(qm, km, vm, batch=batch, s_len=s_len)
        sinks_pad = jnp.zeros((1, LANES), F32).at[0, :D_HEADS].set(swa_sinks[l])
        y_d = _swa_call(qkv_d, sinks_pad, batch=batch, s_len=s_len)
        x2 = _merge_call(x2, mod_l, g_l, (y_a, y_b, y_c, y_d), w_gate, w_branch[l].astype(BF16),
                         w_out[l].astype(BF16), s_len=s_len)
        x2 = _ffn_call(x2, mod_l, g_l, ffn2_w_in[l].astype(BF16), ffn2_w_out[l].astype(BF16),
                       sub=2, resid_w=0.5, s_len=s_len)
    return x2.reshape(batch, s_len, d)
```

```python
import functools

import numpy as np
import jax
import jax.numpy as jnp
from jax import lax
from jax.experimental import pallas as pl
from jax.experimental.pallas import tpu as pltpu

F32 = jnp.float32
BF16 = jnp.bfloat16
I32 = jnp.int32

D_MODEL = 1024
HEAD_DIM = 64
BLOCK = 128
EPS = 1e-6
NEG = -1e30
A_HEADS = 8
A_KV_HEADS = 2
IDX_HEADS = 8
IDX_DIM = 64
TOPK_MAX = 256
LRU_WIDTH = 512
LRU_BLOCKS = 8
CONV_WIDTH = 4
LRU_C = 8.0
C_HEADS = 8
Q_LORA = 256
KV_LORA = 128
QK_NOPE = 64
QK_ROPE = 32
V_DIM = 64
ROPE_THETA = 10000.0
D_HEADS = 8
D_KV_HEADS = 2
WINDOW = 128
N_BRANCH = 4
BRANCH_WIDTH = 512
N_ALIBI = A_HEADS + D_HEADS
D_FF = 2816
IN_SPLITS = (A_HEADS * HEAD_DIM, A_KV_HEADS * HEAD_DIM, A_KV_HEADS * HEAD_DIM,
             IDX_HEADS * IDX_DIM, IDX_DIM, IDX_HEADS,
             LRU_WIDTH, LRU_WIDTH,
             Q_LORA, KV_LORA + QK_ROPE,
             D_HEADS * HEAD_DIM, D_KV_HEADS * HEAD_DIM, D_KV_HEADS * HEAD_DIM,
             N_BRANCH * D_MODEL)
IN_OFFSETS = tuple(int(v) for v in np.concatenate([[0], np.cumsum(IN_SPLITS)]))

LANES = 128
INT_MIN = -2 ** 31
INT_MAX = 2 ** 31 - 1
VMEM_LIMIT = 56 * 1024 * 1024
LOG2E = 1.4426950408889634
POS_SPLIT = 64

QKV_W = 512 + 256 + 128
IDX_W = 512 + 128
MLA_W = 640

FFN_TM = 1024
FFN_TF = 256
PROJ_TM = 512
DSA_TQ = 128
DSA_KC = 512
MLA_TQ = 512
MLA_KC = 512
LRU_L = 256
MERGE_TM = 512


def _alibi(i):
    return float(2.0 ** (-8.0 * i / N_ALIBI))


SLOPES_D = tuple(_alibi(i) for i in range(1, D_HEADS + 1))
SLOPES_A = tuple(_alibi(i) for i in range(D_HEADS + 1, N_ALIBI + 1))


def _bf16_parts(x):
    parts = []
    rem = np.float32(x)
    for _ in range(3):
        p = np.float32(np.asarray(rem, np.float32).astype(jnp.bfloat16).astype(np.float32))
        parts.append(float(p))
        rem = np.float32(rem - p)
    return parts


def _cparams(sem):
    return pltpu.CompilerParams(dimension_semantics=sem, vmem_limit_bytes=VMEM_LIMIT)


def _rms(x, g):
    return x * lax.rsqrt(jnp.mean(x * x, axis=-1, keepdims=True) + EPS) * g


def _nt_dot(a, b):
    return lax.dot_general(a, b, (((1,), (1,)), ((), ())), preferred_element_type=F32)


def _expm1(y):
    u = jnp.exp(y)
    safe = (u != 1.0) & (y > -1.0)
    ratio = y / jnp.log(jnp.where(safe, u, 2.0))
    return jnp.where(u == 1.0, y, jnp.where(safe, (u - 1.0) * ratio, u - 1.0))


def _log2(n):
    l = int(n).bit_length() - 1
    assert (1 << l) == n
    return l


def _ada_kernel(c_ref, w_ref, b_ref, o_ref):
    c = c_ref[...]
    sc = (c * jax.nn.sigmoid(c)).astype(BF16)
    o_ref[0] = jnp.dot(sc, w_ref[0].astype(BF16), preferred_element_type=F32) + b_ref[0]


def _ada_call(c_pad, w_ada, b_ada):
    depth, d, n = w_ada.shape
    tn = 1152
    return pl.pallas_call(
        _ada_kernel,
        grid=(depth, n // tn),
        in_specs=[pl.BlockSpec((c_pad.shape[0], d), lambda l, j: (0, 0)),
                  pl.BlockSpec((1, d, tn), lambda l, j: (l, 0, j)),
                  pl.BlockSpec((1, 1, tn), lambda l, j: (l, 0, j))],
        out_specs=pl.BlockSpec((1, c_pad.shape[0], tn), lambda l, j: (l, 0, j)),
        out_shape=jax.ShapeDtypeStruct((depth, c_pad.shape[0], n), F32),
        compiler_params=_cparams(("arbitrary", "arbitrary")),
        name="adaln",
    )(c_pad, w_ada, b_ada.reshape(depth, 1, n))


def _ffn_kernel(x_ref, mod_ref, g_ref, wg_ref, wu_ref, wo_ref, o_ref, h_ref, acc_ref, *, sub, resid_w):
    k = pl.program_id(1)
    m = mod_ref[0]

    @pl.when(k == 0)
    def _():
        x = x_ref[...]
        xn = _rms(x, g_ref[2 * sub:2 * sub + 1])
        h_ref[...] = (xn * (1.0 + m[3 * sub + 1:3 * sub + 2]) + m[3 * sub:3 * sub + 1]).astype(BF16)
        acc_ref[...] = jnp.zeros_like(acc_ref)

    h = h_ref[...]
    gate = jnp.dot(h, wg_ref[...], preferred_element_type=F32)
    up = jnp.dot(h, wu_ref[...], preferred_element_type=F32)
    act = (gate * jax.nn.sigmoid(gate) * up).astype(BF16)
    acc_ref[...] += jnp.dot(act, wo_ref[...], preferred_element_type=F32)

    @pl.when(k == pl.num_programs(1) - 1)
    def _():
        yn = _rms(acc_ref[...], g_ref[2 * sub + 1:2 * sub + 2])
        o_ref[...] = x_ref[...] + resid_w * m[3 * sub + 2:3 * sub + 3] * yn


def _ffn_call(x2, mod_l, g_l, w_in_bf, w_out_bf, *, sub, resid_w, s_len):
    t, d = x2.shape
    f = w_out_bf.shape[0]
    tm = min(FFN_TM, s_len)
    tf = FFN_TF
    nf = f // tf
    tpb = s_len // tm
    return pl.pallas_call(
        functools.partial(_ffn_kernel, sub=sub, resid_w=resid_w),
        grid=(t // tm, nf),
        in_specs=[pl.BlockSpec((tm, d), lambda i, k: (i, 0)),
                  pl.BlockSpec((1, 9, d), lambda i, k: (i // tpb, 0, 0)),
                  pl.BlockSpec((6, d), lambda i, k: (0, 0)),
                  pl.BlockSpec((d, tf), lambda i, k: (0, k)),
                  pl.BlockSpec((d, tf), lambda i, k: (0, k + nf)),
                  pl.BlockSpec((tf, d), lambda i, k: (k, 0))],
        out_specs=pl.BlockSpec((tm, d), lambda i, k: (i, 0)),
        out_shape=jax.ShapeDtypeStruct((t, d), F32),
        scratch_shapes=[pltpu.VMEM((tm, d), BF16), pltpu.VMEM((tm, d), F32)],
        compiler_params=_cparams(("arbitrary", "arbitrary")),
        name="ffn",
    )(x2, mod_l, g_l, w_in_bf, w_in_bf, w_out_bf)


def _proj_kernel(x_ref, mod_ref, g_ref, w_ref, kpos_ref, oa_ref, oi_ref, ow_ref, or_ref, oc_ref, od_ref):
    m = mod_ref[0]
    xn = _rms(x_ref[...], g_ref[2:3])
    h = (xn * (1.0 + m[4:5]) + m[3:4]).astype(BF16)
    off = 0
    for ref in (oa_ref, oi_ref, ow_ref, or_ref, oc_ref, od_ref):
        w = ref.shape[1]
        z = jnp.dot(h, w_ref[:, off:off + w], preferred_element_type=F32)
        if ref is oa_ref or ref is od_ref:
            z = jnp.concatenate([z[:, :512] * (HEAD_DIM ** -0.5), z[:, 512:768] + kpos_ref[...], z[:, 768:]],
                                axis=1)
        ref[...] = z.astype(ref.dtype)
        off += w


def _proj_call(x2, mod_l, g_l, w_all, kpos, *, s_len):
    t, d = x2.shape
    tm = min(PROJ_TM, s_len)
    tpb = s_len // tm
    widths = (QKV_W, IDX_W, LANES, 2 * LRU_WIDTH, MLA_W, QKV_W)
    dtypes = (BF16, BF16, F32, F32, F32, BF16)
    assert sum(widths) == w_all.shape[1]
    return pl.pallas_call(
        _proj_kernel,
        grid=(t // tm,),
        in_specs=[pl.BlockSpec((tm, d), lambda i: (i, 0)),
                  pl.BlockSpec((1, 9, d), lambda i: (i // tpb, 0, 0)),
                  pl.BlockSpec((6, d), lambda i: (0, 0)),
                  pl.BlockSpec(w_all.shape, lambda i: (0, 0)),
                  pl.BlockSpec((tm, 256), lambda i: (i % tpb, 0))],
        out_specs=[pl.BlockSpec((tm, w), lambda i: (i, 0)) for w in widths],
        out_shape=[jax.ShapeDtypeStruct((t, w), dt) for w, dt in zip(widths, dtypes)],
        compiler_params=_cparams(("arbitrary",)),
        name="mixer_proj",
    )(x2, mod_l, g_l, w_all, kpos)


def _head_rows(q, n_heads, tq):
    lane = lax.broadcasted_iota(I32, (tq, LANES), 1)
    rows = []
    for h in range(n_heads):
        tile = q[:, LANES * (h // 2):LANES * (h // 2 + 1)].astype(F32)
        keep = (lane >= HEAD_DIM) if (h % 2) else (lane < HEAD_DIM)
        rows.append(jnp.where(keep, tile, 0.0).astype(BF16))
    return jnp.concatenate(rows, axis=0)


def _alibi_q_tiles(q, slopes, tq):
    lane = lax.broadcasted_iota(I32, (tq, LANES), 1)
    rows = []
    for h, slope in enumerate(slopes):
        tile = q[:, LANES * (h // 2):LANES * (h // 2 + 1)].astype(F32)
        if h % 2:
            tile = pltpu.roll(tile, HEAD_DIM, axis=1)
        for i, part in enumerate(_bf16_parts(slope)):
            tile = jnp.where(lane == HEAD_DIM + 2 * i, POS_SPLIT * part, tile)
            tile = jnp.where(lane == HEAD_DIM + 2 * i + 1, part, tile)
        tile = jnp.where(lane >= HEAD_DIM + 6, 0.0, tile)
        rows.append(tile.astype(BF16))
    return jnp.concatenate(rows, axis=0)


def _gqa_out(accs, inv_ls, tq):
    lane = lax.broadcasted_iota(I32, (tq, LANES), 1)
    tiles = []
    for j in range(4):
        g = j // 2
        halves = []
        for h in (2 * j, 2 * j + 1):
            o = accs[g][h % 4] * inv_ls[g][h % 4]
            src_hi = (g == 1)
            dst_hi = (h % 2 == 1)
            if src_hi != dst_hi:
                o = pltpu.roll(o, HEAD_DIM, axis=1)
            halves.append(o)
        tiles.append(jnp.where(lane < HEAD_DIM, halves[0], halves[1]))
    return jnp.concatenate(tiles, axis=1)


def _dsa_kernel(q_ref, k_ref, v_ref, iq_ref, ik_ref, iw_ref, o_ref,
                key_ref, tau_ref, iqs_ref, qs_ref, m_ref, l_ref, acc_ref, *, tq, kc, topk, pos_bits):
    qb = pl.program_id(1)
    q0 = qb * tq
    nk = lax.shift_right_logical(q0 + tq + kc - 1, _log2(kc))
    rowpos = q0 + lax.broadcasted_iota(I32, (tq, kc), 0)
    lane = lax.broadcasted_iota(I32, (tq, kc), 1)
    n_lt = kc // LANES
    lane_t = lax.broadcasted_iota(I32, (tq, LANES), 1)

    iqs_ref[...] = _head_rows(iq_ref[...], IDX_HEADS, tq)
    iw = iw_ref[...]

    def score_body(j, carry):
        ks = pl.multiple_of(j * kc, kc)
        d = _nt_dot(iqs_ref[...], ik_ref[pl.ds(ks, kc), :])
        acc = jnp.zeros((tq, kc), F32)
        for h in range(IDX_HEADS):
            acc = acc + iw[:, h:h + 1] * jnp.maximum(d[h * tq:(h + 1) * tq], 0.0)
        bits = pltpu.bitcast(acc, I32)
        key = jnp.where(bits < 0, bits ^ INT_MAX, bits)
        key_ref[j] = jnp.where(ks + lane <= rowpos, key, INT_MIN)
        return carry

    lax.fori_loop(0, nk, score_body, 0)

    def count(pred):
        def body(j, cnt):
            kj = key_ref[j]
            for c in range(n_lt):
                kp = (j * kc + c * LANES) + lane_t
                cnt = cnt + jnp.where(pred(kj[:, c * LANES:(c + 1) * LANES], kp), 1.0, 0.0)
            return cnt
        cnt = lax.fori_loop(0, nk, body, jnp.zeros((tq, LANES), F32))
        return jnp.sum(cnt, axis=1, keepdims=True)

    kf = float(topk)
    n_valid = (q0 + 1 + lax.broadcasted_iota(I32, (tq, 1), 0)).astype(F32)
    settled0 = n_valid <= kf

    def search_cond(state):
        i, _, _, all_settled = state
        return (i < 32) & (all_settled == 0)

    def search_body(state):
        i, tau, cnt, _ = state
        cand = tau + lax.shift_left(jnp.int32(1), 31 - i)
        cb = jnp.broadcast_to(cand, (tq, LANES))
        c = count(lambda kv, kp: kv >= cb)
        ok = c >= kf
        tau = jnp.where(ok, cand, tau)
        cnt = jnp.where(ok, c, cnt)
        settled = settled0 | (cnt == kf)
        return i + 1, tau, cnt, jnp.min(jnp.where(settled, 1, 0))

    state0 = (jnp.int32(0), jnp.full((tq, 1), INT_MIN, I32),
              jnp.broadcast_to((nk * kc).astype(F32), (tq, 1)), jnp.int32(0))
    _, tau, n_ge, _ = lax.while_loop(search_cond, search_body, state0)

    tie_f = jnp.where((n_ge > kf) & (tau > INT_MIN), 1.0, 0.0)
    any_tie = jnp.max(tie_f) > 0.0
    tau_c = jnp.maximum(tau, INT_MIN + 1)
    tau_ref[...] = tau_c

    @pl.when(any_tie)
    def _():
        tb = jnp.broadcast_to(tau_c, (tq, LANES))
        need = kf - count(lambda kv, kp: kv > tb)

        def pos_body(i, p):
            cand = p | lax.shift_left(jnp.int32(1), pos_bits - 1 - i)
            cb = jnp.broadcast_to(cand, (tq, LANES))
            below = count(lambda kv, kp: (kv == tb) & (kp < cb))
            return jnp.where(below < need, cand, p)

        p = lax.fori_loop(0, pos_bits, pos_body, jnp.zeros((tq, 1), I32))
        p = jnp.where(tie_f > 0.0, p, INT_MAX)

        def rewrite_body(j, carry):
            kj = key_ref[j]
            kp = j * kc + lane
            sel = (kj > tau_c) | ((kj == tau_c) & (kp <= p))
            key_ref[j] = jnp.where(sel, 1, -1)
            return carry

        lax.fori_loop(0, nk, rewrite_body, 0)
        tau_ref[...] = jnp.zeros((tq, 1), I32)

    q = q_ref[...]
    for g in range(A_KV_HEADS):
        qs_ref[g] = _alibi_q_tiles(q[:, 4 * HEAD_DIM * g:4 * HEAD_DIM * (g + 1)], SLOPES_A[4 * g:4 * g + 4], tq)
    m_ref[...] = jnp.full(m_ref.shape, NEG, F32)
    l_ref[...] = jnp.zeros(l_ref.shape, F32)
    acc_ref[...] = jnp.zeros(acc_ref.shape, F32)

    def scores(j):
        ks = pl.multiple_of(j * kc, kc)
        sel = key_ref[j] >= tau_ref[...]
        out = []
        for g in range(A_KV_HEADS):
            s = _nt_dot(qs_ref[g], k_ref[pl.ds(ks, kc), g * LANES:(g + 1) * LANES])
            out.append([jnp.where(sel, s[h * tq:(h + 1) * tq], NEG) for h in range(4)])
        return out

    def max_body(j, carry):
        s = scores(j)
        for g in range(A_KV_HEADS):
            for h in range(4):
                mp = m_ref[g, h]
                for c in range(n_lt):
                    mp = jnp.maximum(mp, s[g][h][:, c * LANES:(c + 1) * LANES])
                m_ref[g, h] = mp
        return carry

    lax.fori_loop(0, nk, max_body, 0)
    for g in range(A_KV_HEADS):
        for h in range(4):
            m_ref[g, h] = jnp.broadcast_to(jnp.max(m_ref[g, h], axis=1, keepdims=True), (tq, LANES))

    def pv_body(j, carry):
        ks = pl.multiple_of(j * kc, kc)
        vch = v_ref[pl.ds(ks, kc), :]
        s = scores(j)
        for g in range(A_KV_HEADS):
            ps = []
            for h in range(4):
                mb = m_ref[g, h]
                lp = l_ref[g, h]
                tiles = []
                for c in range(n_lt):
                    p = jnp.exp(s[g][h][:, c * LANES:(c + 1) * LANES] - mb)
                    lp = lp + p
                    tiles.append(p.astype(BF16))
                l_ref[g, h] = lp
                ps.append(jnp.concatenate(tiles, axis=1))
            pv = jnp.dot(jnp.concatenate(ps, axis=0), vch, preferred_element_type=F32)
            for h in range(4):
                acc_ref[g, h] += pv[h * tq:(h + 1) * tq]
        return carry

    lax.fori_loop(0, nk, pv_body, 0)
    accs = [[acc_ref[g, h] for h in range(4)] for g in range(A_KV_HEADS)]
    inv_ls = [[1.0 / jnp.sum(l_ref[g, h], axis=1, keepdims=True) for h in range(4)] for g in range(A_KV_HEADS)]
    o_ref[...] = _gqa_out(accs, inv_ls, tq).astype(o_ref.dtype)


def _dsa_call(qkv, idx, iw, *, batch, s_len):
    t = qkv.shape[0]
    tq = min(DSA_TQ, s_len)
    kc = min(DSA_KC, s_len)
    topk = min(TOPK_MAX, s_len // 4)
    assert kc >= topk and kc % tq == 0
    nqb = s_len // tq
    return pl.pallas_call(
        functools.partial(_dsa_kernel, tq=tq, kc=kc, topk=topk, pos_bits=_log2(s_len)),
        grid=(batch, nqb),
        in_specs=[pl.BlockSpec((tq, 512), lambda b, i: (b * nqb + i, 0)),
                  pl.BlockSpec((s_len, 256), lambda b, i: (b, 2)),
                  pl.BlockSpec((s_len, 128), lambda b, i: (b, 6)),
                  pl.BlockSpec((tq, 512), lambda b, i: (b * nqb + i, 0)),
                  pl.BlockSpec((s_len, 128), lambda b, i: (b, 4)),
                  pl.BlockSpec((tq, LANES), lambda b, i: (b * nqb + i, 0))],
        out_specs=pl.BlockSpec((tq, 512), lambda b, i: (b * nqb + i, 0)),
        out_shape=jax.ShapeDtypeStruct((t, 512), BF16),
        scratch_shapes=[pltpu.VMEM((s_len // kc, tq, kc), I32),
                        pltpu.VMEM((tq, 1), I32),
                        pltpu.VMEM((IDX_HEADS * tq, LANES), BF16),
                        pltpu.VMEM((A_KV_HEADS, 4 * tq, LANES), BF16),
                        pltpu.VMEM((A_KV_HEADS, 4, tq, LANES), F32),
                        pltpu.VMEM((A_KV_HEADS, 4, tq, LANES), F32),
                        pltpu.VMEM((A_KV_HEADS, 4, tq, LANES), F32)],
        compiler_params=_cparams(("arbitrary", "arbitrary")),
        name="dsa",
    )(qkv, qkv, qkv, idx, idx, iw)


def _swa_kernel(q_ref, kp_ref, kc_ref, vp_ref, vc_ref, sink_ref, o_ref, *, tq):
    i = pl.program_id(1)
    q = q_ref[...]
    kk = jnp.concatenate([kp_ref[...], kc_ref[...]], axis=0)
    vv = jnp.concatenate([vp_ref[...], vc_ref[...]], axis=0)
    row = lax.broadcasted_iota(I32, (tq, 2 * tq), 0)
    col = lax.broadcasted_iota(I32, (tq, 2 * tq), 1)
    dist = row + tq - col
    valid = (dist >= 0) & (dist < WINDOW) & ((i * tq - tq + col) >= 0)
    sinks = sink_ref[...]
    accs, inv_ls = [], []
    for g in range(D_KV_HEADS):
        qs = _alibi_q_tiles(q[:, 4 * HEAD_DIM * g:4 * HEAD_DIM * (g + 1)], SLOPES_D[4 * g:4 * g + 4], tq)
        s = _nt_dot(qs, kk[:, g * LANES:(g + 1) * LANES])
        ps, ils = [], []
        for h in range(4):
            sh = jnp.where(valid, s[h * tq:(h + 1) * tq], NEG)
            sink = sinks[:, 4 * g + h:4 * g + h + 1] + SLOPES_D[4 * g + h] * (
                i * tq + lax.broadcasted_iota(I32, (tq, 1), 0)).astype(F32)
            m = jnp.maximum(jnp.max(sh, axis=1, keepdims=True), sink)
            p = jnp.exp(sh - m)
            ils.append(1.0 / (jnp.sum(p, axis=1, keepdims=True) + jnp.exp(sink - m)))
            ps.append(p.astype(BF16))
        pv = jnp.dot(jnp.concatenate(ps, axis=0), vv, preferred_element_type=F32)
        accs.append([pv[h * tq:(h + 1) * tq] for h in range(4)])
        inv_ls.append(ils)
    o_ref[...] = _gqa_out(accs, inv_ls, tq).astype(o_ref.dtype)


def _swa_call(qkv, sinks_pad, *, batch, s_len):
    t = qkv.shape[0]
    tq = BLOCK
    nqb = s_len // tq
    cur = lambda b, i: b * nqb + i
    prev = lambda b, i: b * nqb + jnp.maximum(i - 1, 0)
    return pl.pallas_call(
        functools.partial(_swa_kernel, tq=tq),
        grid=(batch, nqb),
        in_specs=[pl.BlockSpec((tq, 512), lambda b, i: (cur(b, i), 0)),
                  pl.BlockSpec((tq, 256), lambda b, i: (prev(b, i), 2)),
                  pl.BlockSpec((tq, 256), lambda b, i: (cur(b, i), 2)),
                  pl.BlockSpec((tq, 128), lambda b, i: (prev(b, i), 6)),
                  pl.BlockSpec((tq, 128), lambda b, i: (cur(b, i), 6)),
                  pl.BlockSpec((1, LANES), lambda b, i: (0, 0))],
        out_specs=pl.BlockSpec((tq, 512), lambda b, i: (cur(b, i), 0)),
        out_shape=jax.ShapeDtypeStruct((t, 512), BF16),
        compiler_params=_cparams(("arbitrary", "arbitrary")),
        name="swa",
    )(qkv, qkv, qkv, qkv, qkv, sinks_pad)


def _mla_prep_kernel(c_ref, cos_ref, sin_ref, gq_ref, gkv_ref, wq_ref, wqr_ref, wk_ref, wv_ref,
                     q_ref, k_ref, v_ref):
    c = c_ref[...]
    cos = cos_ref[...]
    sin = sin_ref[...]
    cos8 = jnp.concatenate([cos] * C_HEADS, axis=1)
    sin8 = jnp.concatenate([sin] * C_HEADS, axis=1)
    cqn = _rms(c[:, :Q_LORA], gq_ref[...]).astype(BF16)
    q = (jnp.dot(cqn, wq_ref[...], preferred_element_type=F32) * cos8
         + jnp.dot(cqn, wqr_ref[...], preferred_element_type=F32) * sin8)
    q_ref[...] = q.astype(BF16)
    ckvn = _rms(c[:, Q_LORA:Q_LORA + KV_LORA], gkv_ref[...]).astype(BF16)
    kr = c[:, 384:512] * cos + c[:, 512:640] * sin
    k = jnp.dot(ckvn, wk_ref[...], preferred_element_type=F32) + jnp.concatenate([kr] * C_HEADS, axis=1)
    k_ref[...] = k.astype(BF16)
    v_ref[...] = jnp.dot(ckvn, wv_ref[...], preferred_element_type=F32).astype(BF16)


def _mla_prep_call(cm, cos_t, sin_t, gq, gkv, wq, wqr, wk, wv, *, s_len):
    t = cm.shape[0]
    tm = min(PROJ_TM, s_len)
    tpb = s_len // tm
    full = lambda a: pl.BlockSpec(a.shape, lambda i: (0, 0))
    return pl.pallas_call(
        _mla_prep_kernel,
        grid=(t // tm,),
        in_specs=[pl.BlockSpec((tm, MLA_W), lambda i: (i, 0)),
                  pl.BlockSpec((tm, LANES), lambda i: (i % tpb, 0)),
                  pl.BlockSpec((tm, LANES), lambda i: (i % tpb, 0)),
                  full(gq), full(gkv), full(wq), full(wqr), full(wk), full(wv)],
        out_specs=[pl.BlockSpec((tm, 1024), lambda i: (i, 0)),
                   pl.BlockSpec((tm, 1024), lambda i: (i, 0)),
                   pl.BlockSpec((tm, 512), lambda i: (i, 0))],
        out_shape=[jax.ShapeDtypeStruct((t, 1024), BF16),
                   jax.ShapeDtypeStruct((t, 1024), BF16),
                   jax.ShapeDtypeStruct((t, 512), BF16)],
        compiler_params=_cparams(("arbitrary",)),
        name="mla_prep",
    )(cm, cos_t, sin_t, gq, gkv, wq, wqr, wk, wv)


def _mla_attn_kernel(q_ref, k_ref, v_ref, o_ref, m_ref, l_ref, acc_ref, *, tq, kc):
    qb = pl.program_id(2)
    q0 = qb * tq
    n_full = lax.shift_right_logical(q0, _log2(kc))
    rowpos = q0 + lax.broadcasted_iota(I32, (tq, kc), 0)
    lane = lax.broadcasted_iota(I32, (tq, kc), 1)
    c = ((QK_NOPE + QK_ROPE) ** -0.5) * LOG2E
    n_lt = kc // LANES
    m_ref[...] = jnp.full(m_ref.shape, NEG, F32)
    l_ref[...] = jnp.zeros(l_ref.shape, F32)
    acc_ref[...] = jnp.zeros(acc_ref.shape, F32)

    def scores(j, hh, masked):
        ks = pl.multiple_of(j * kc, kc)
        s = _nt_dot(q_ref[:, hh * LANES:(hh + 1) * LANES], k_ref[pl.ds(ks, kc), hh * LANES:(hh + 1) * LANES])
        if masked:
            s = jnp.where(ks + lane <= rowpos, s, NEG)
        return s

    def max_step(j, masked):
        for hh in range(2):
            s = scores(j, hh, masked)
            mp = m_ref[hh]
            for t in range(n_lt):
                mp = jnp.maximum(mp, s[:, t * LANES:(t + 1) * LANES])
            m_ref[hh] = mp

    def max_body(j, carry):
        max_step(j, False)
        return carry

    lax.fori_loop(0, n_full, max_body, 0)
    max_step(n_full, True)
    for hh in range(2):
        m_ref[hh] = jnp.broadcast_to(jnp.max(m_ref[hh], axis=1, keepdims=True), (tq, LANES))

    def pv_step(j, masked):
        ks = pl.multiple_of(j * kc, kc)
        vch = v_ref[pl.ds(ks, kc), :]
        for hh in range(2):
            s = scores(j, hh, masked)
            mb = m_ref[hh]
            lp = l_ref[hh]
            tiles = []
            for t in range(n_lt):
                p = jnp.exp2((s[:, t * LANES:(t + 1) * LANES] - mb) * c)
                lp = lp + p
                tiles.append(p.astype(BF16))
            l_ref[hh] = lp
            acc_ref[hh] += jnp.dot(jnp.concatenate(tiles, axis=1), vch, preferred_element_type=F32)

    def pv_body(j, carry):
        pv_step(j, False)
        return carry

    lax.fori_loop(0, n_full, pv_body, 0)
    pv_step(n_full, True)
    outs = [acc_ref[hh] * (1.0 / jnp.sum(l_ref[hh], axis=1, keepdims=True)) for hh in range(2)]
    lane_o = lax.broadcasted_iota(I32, (tq, LANES), 1)
    o_ref[...] = jnp.where(lane_o < V_DIM, outs[0], outs[1]).astype(o_ref.dtype)


def _mla_attn_call(qm, km, vm, *, batch, s_len):
    t = qm.shape[0]
    tq = min(MLA_TQ, s_len)
    kc = min(MLA_KC, s_len)
    assert kc % tq == 0
    nqb = s_len // tq
    return pl.pallas_call(
        functools.partial(_mla_attn_kernel, tq=tq, kc=kc),
        grid=(batch, C_HEADS // 2, nqb),
        in_specs=[pl.BlockSpec((tq, 256), lambda b, h, i: (b * nqb + i, h)),
                  pl.BlockSpec((s_len, 256), lambda b, h, i: (b, h)),
                  pl.BlockSpec((s_len, 128), lambda b, h, i: (b, h))],
        out_specs=pl.BlockSpec((tq, 128), lambda b, h, i: (b * nqb + i, h)),
        out_shape=jax.ShapeDtypeStruct((t, 512), BF16),
        scratch_shapes=[pltpu.VMEM((2, tq, LANES), F32)] * 3,
        compiler_params=_cparams(("arbitrary", "arbitrary", "arbitrary")),
        name="mla_attn",
    )(qm, km, vm)


def _lru_kernel(xr_ref, xg_ref, prev_ref, cw_ref, cb_ref, wa_ref, ba_ref, wx_ref, bx_ref, lam_ref,
                o_ref, xe_ref, h_ref, *, ln):
    i = pl.program_id(1)
    xe_ref[0:8, :] = jnp.where(i > 0, prev_ref[...], 0.0)
    xe_ref[8:8 + ln, :] = xr_ref[...]
    cw = cw_ref[...]
    xc = cb_ref[...] + cw[0:1] * xe_ref[pl.ds(5, ln), :]
    for j in range(1, CONV_WIDTH):
        xc = xc + cw[j:j + 1] * xe_ref[pl.ds(5 + j, ln), :]
    xcb = xc.astype(BF16)
    r = jax.nn.sigmoid(jnp.dot(xcb, wa_ref[...], preferred_element_type=F32) + ba_ref[...])
    gi = jax.nn.sigmoid(jnp.dot(xcb, wx_ref[...], preferred_element_type=F32) + bx_ref[...])
    z = -lam_ref[...]
    softplus = jnp.maximum(z, 0.0) + jnp.log1p(jnp.exp(-jnp.abs(z)))
    log_a = -LRU_C * r * softplus
    a = jnp.exp(log_a)
    b = jnp.sqrt(-_expm1(2.0 * log_a)) * (gi * xc)
    row = lax.broadcasted_iota(I32, (ln, LRU_WIDTH), 0)
    d = 1
    while d < ln:
        keep = row >= d
        b = jnp.where(keep, a * pltpu.roll(b, d, axis=0) + b, b)
        a = jnp.where(keep, a * pltpu.roll(a, d, axis=0), a)
        d *= 2
    h_prev = jnp.where(i > 0, h_ref[0:1, :], 0.0)
    h = a * h_prev + b
    h_ref[0:1, :] = h[ln - 1:ln, :]
    xg = xg_ref[...]
    gelu = 0.5 * xg * (1.0 + jnp.tanh(0.7978845608028654 * (xg + 0.044715 * (xg * xg * xg))))
    o_ref[...] = (h * gelu).astype(o_ref.dtype)


def _lru_call(rm, cw, cb, wa, ba, wx, bx, lam, *, batch, s_len):
    t = rm.shape[0]
    ln = min(LRU_L, s_len)
    nt = s_len // ln
    w = LRU_WIDTH
    vec = lambda a: pl.BlockSpec(a.shape, lambda b, i: (0, 0))
    return pl.pallas_call(
        functools.partial(_lru_kernel, ln=ln),
        grid=(batch, nt),
        in_specs=[pl.BlockSpec((ln, w), lambda b, i: (b * nt + i, 0)),
                  pl.BlockSpec((ln, w), lambda b, i: (b * nt + i, 1)),
                  pl.BlockSpec((8, w), lambda b, i: (jnp.maximum((b * nt + i) * (ln // 8) - 1, 0), 0)),
                  vec(cw), vec(cb), vec(wa), vec(ba), vec(wx), vec(bx), vec(lam)],
        out_specs=pl.BlockSpec((ln, w), lambda b, i: (b * nt + i, 0)),
        out_shape=jax.ShapeDtypeStruct((t, w), BF16),
        scratch_shapes=[pltpu.VMEM((ln + 8, w), F32), pltpu.VMEM((8, w), F32)],
        compiler_params=_cparams(("arbitrary", "arbitrary")),
        name="rglru",
    )(rm, rm, rm, cw, cb, wa, ba, wx, bx, lam)


def _merge_kernel(x_ref, mod_ref, g_ref, ya_ref, yb_ref, yc_ref, yd_ref, wg_ref, wb_ref, wo_ref, o_ref):
    m = mod_ref[0]
    x = x_ref[...]
    h = (_rms(x, g_ref[2:3]) * (1.0 + m[4:5]) + m[3:4]).astype(BF16)
    merged = None
    for n, y_ref in enumerate((ya_ref, yb_ref, yc_ref, yd_ref)):
        gate = jax.nn.sigmoid(jnp.dot(h, wg_ref[:, n * D_MODEL:(n + 1) * D_MODEL], preferred_element_type=F32))
        term = gate * jnp.dot(y_ref[...], wb_ref[n], preferred_element_type=F32)
        merged = term if merged is None else merged + term
    y = jnp.dot(merged.astype(BF16), wo_ref[...], preferred_element_type=F32)
    o_ref[...] = x + m[5:6] * _rms(y, g_ref[3:4])


def _merge_call(x2, mod_l, g_l, ys, w_gate, w_branch, w_out, *, s_len):
    t, d = x2.shape
    tm = min(MERGE_TM, s_len)
    tpb = s_len // tm
    return pl.pallas_call(
        _merge_kernel,
        grid=(t // tm,),
        in_specs=[pl.BlockSpec((tm, d), lambda i: (i, 0)),
                  pl.BlockSpec((1, 9, d), lambda i: (i // tpb, 0, 0)),
                  pl.BlockSpec((6, d), lambda i: (0, 0))]
                 + [pl.BlockSpec((tm, BRANCH_WIDTH), lambda i: (i, 0))] * N_BRANCH
                 + [pl.BlockSpec(w_gate.shape, lambda i: (0, 0)),
                    pl.BlockSpec(w_branch.shape, lambda i: (0, 0, 0)),
                    pl.BlockSpec(w_out.shape, lambda i: (0, 0))],
        out_specs=pl.BlockSpec((tm, d), lambda i: (i, 0)),
        out_shape=jax.ShapeDtypeStruct((t, d), F32),
        compiler_params=_cparams(("arbitrary",)),
        name="merge",
    )(x2, mod_l, g_l, *ys, w_gate, w_branch, w_out)


def _mixer_weights(w_in_l):
    o = IN_OFFSETS
    col = lambda n: w_in_l[:, o[n]:o[n + 1]]
    z = lambda n: jnp.zeros((D_MODEL, n), w_in_l.dtype)

    def pad_heads(k):
        return jnp.concatenate([k[:, :64], z(64), k[:, 64:], z(64)], axis=1)

    a_q, a_k, a_v, i_q, i_k, i_w, r_x, r_g, c_q, c_kv, d_q, d_k, d_v = (col(n) for n in range(13))
    k_rope = c_kv[:, KV_LORA:]
    half = QK_ROPE // 2
    k_rope_rot = jnp.concatenate([-k_rope[:, half:], k_rope[:, :half]], axis=1)
    groups = [a_q, pad_heads(a_k), a_v,
              i_q, i_k, i_k,
              i_w, z(LANES - IDX_HEADS),
              r_x, r_g,
              c_q, c_kv[:, :KV_LORA], z(64), k_rope, z(32), z(64), k_rope_rot, z(32),
              d_q, pad_heads(d_k), d_v]
    w_all = jnp.concatenate(groups, axis=1).astype(BF16)
    w_gate = col(13).astype(BF16)
    return w_all, w_gate


def _kpos_table(s_len):
    pos = np.arange(s_len)
    tile = np.zeros((s_len, LANES), np.float32)
    for i in range(3):
        tile[:, HEAD_DIM + 2 * i] = pos // POS_SPLIT
        tile[:, HEAD_DIM + 2 * i + 1] = pos % POS_SPLIT
    return jnp.asarray(np.concatenate([tile, tile], axis=1))


def _mla_weights(w_uq, w_ukv):
    dq = QK_NOPE + QK_ROPE
    half = QK_ROPE // 2
    zq = lambda n: jnp.zeros((Q_LORA, n), w_uq.dtype)
    wq, wqr, wk, wv = [], [], [], []
    for h in range(C_HEADS):
        nope = w_uq[:, h * dq:h * dq + QK_NOPE]
        r1 = w_uq[:, h * dq + QK_NOPE:h * dq + QK_NOPE + half]
        r2 = w_uq[:, h * dq + QK_NOPE + half:(h + 1) * dq]
        wq += [nope, r1, r2, zq(32)]
        wqr += [zq(64), -r2, r1, zq(32)]
        wk += [w_ukv[:, h * 128:h * 128 + QK_NOPE], jnp.zeros((KV_LORA, 64), w_ukv.dtype)]
        wv += [w_ukv[:, h * 128 + QK_NOPE:(h + 1) * 128]]
    cat = lambda xs: jnp.concatenate(xs, axis=1).astype(BF16)
    return cat(wq), cat(wqr), cat(wk), cat(wv)


def _rope_tables(s_len):
    half = QK_ROPE // 2
    inv = ROPE_THETA ** (-jnp.arange(half, dtype=F32) / half)
    ang = jnp.arange(s_len, dtype=F32)[:, None] * inv[None, :]
    cos, sin = jnp.cos(ang), jnp.sin(ang)
    ones = lambda n: jnp.ones((s_len, n), F32)
    zeros = lambda n: jnp.zeros((s_len, n), F32)
    cos_t = jnp.concatenate([ones(64), cos, cos, ones(32)], axis=1)
    sin_t = jnp.concatenate([zeros(64), sin, sin, zeros(32)], axis=1)
    return cos_t, sin_t


def _block_diag(w):
    n, bw, _ = w.shape
    out = jnp.zeros((n * bw, n * bw), w.dtype)
    for k in range(n):
        out = lax.dynamic_update_slice(out, w[k], (k * bw, k * bw))
    return out.astype(BF16)


def kernel(x, c, w_ada, b_ada, norm_g, ffn1_w_in, ffn1_w_out, w_in, conv_w, conv_b, lru_wa, lru_ba, lru_wx,
           lru_bx, lru_lambda, mla_g_q, mla_g_kv, mla_w_uq, mla_w_ukv, swa_sinks, w_branch, w_out, ffn2_w_in,
           ffn2_w_out):
    batch, s_len, d = x.shape
    depth = w_ada.shape[0]
    t = batch * s_len
    assert s_len // POS_SPLIT <= 256
    x2 = x.reshape(t, d)
    c_pad = jnp.zeros((8, d), F32).at[:batch].set(c)
    mod_all = _ada_call(c_pad, w_ada, b_ada)
    cos_t, sin_t = _rope_tables(s_len)
    kpos = _kpos_table(s_len)
    row = lambda v: v.reshape(1, -1)
    for l in range(depth):
        mod_l = mod_all[l, :batch].reshape(batch, 9, d)
        g_l = norm_g[l]
        x2 = _ffn_call(x2, mod_l, g_l, ffn1_w_in[l].astype(BF16), ffn1_w_out[l].astype(BF16),
                       sub=0, resid_w=0.5, s_len=s_len)
        w_all, w_gate = _mixer_weights(w_in[l])
        qkv_a, idx, iw, rm, cm, qkv_d = _proj_call(x2, mod_l, g_l, w_all, kpos, s_len=s_len)
        y_a = _dsa_call(qkv_a, idx, iw, batch=batch, s_len=s_len)
        y_b = _lru_call(rm, conv_w[l], row(conv_b[l]), _block_diag(lru_wa[l]), row(lru_ba[l]),
                        _block_diag(lru_wx[l]), row(lru_bx[l]), row(lru_lambda[l]), batch=batch, s_len=s_len)
        qm, km, vm = _mla_prep_call(cm, cos_t, sin_t, row(mla_g_q[l]), row(mla_g_kv[l]),
                                    *_mla_weights(mla_w_uq[l], mla_w_ukv[l]), s_len=s_len)
        y_c = _mla_attn_call(qm, km, vm, batch=batch, s_len=s_len)
        sinks_pad = jnp.zeros((1, LANES), F32).at[0, :D_HEADS].set(swa_sinks[l])
        y_d = _swa_call(qkv_d, sinks_pad, batch=batch, s_len=s_len)
        x2 = _merge_call(x2, mod_l, g_l, (y_a, y_b, y_c, y_d), w_gate, w_branch[l].astype(BF16),
                         w_out[l].astype(BF16), s_len=s_len)
        x2 = _ffn_call(x2, mod_l, g_l, ffn2_w_in[l].astype(BF16), ffn2_w_out[l].astype(BF16),
                       sub=2, resid_w=0.5, s_len=s_len)
    return x2.reshape(batch, s_len, d)
```

```python
import functools

import numpy as np
import jax
import jax.numpy as jnp
from jax import lax
from jax.experimental import pallas as pl
from jax.experimental.pallas import tpu as pltpu

F32 = jnp.float32
BF16 = jnp.bfloat16
I32 = jnp.int32

D_MODEL = 1024
HEAD_DIM = 64
BLOCK = 128
EPS = 1e-6
NEG = -1e30
A_HEADS = 8
A_KV_HEADS = 2
IDX_HEADS = 8
IDX_DIM = 64
TOPK_MAX = 256
LRU_WIDTH = 512
LRU_BLOCKS = 8
CONV_WIDTH = 4
LRU_C = 8.0
C_HEADS = 8
Q_LORA = 256
KV_LORA = 128
QK_NOPE = 64
QK_ROPE = 32
V_DIM = 64
ROPE_THETA = 10000.0
D_HEADS = 8
D_KV_HEADS = 2
WINDOW = 128
N_BRANCH = 4
BRANCH_WIDTH = 512
N_ALIBI = A_HEADS + D_HEADS
D_FF = 2816
IN_SPLITS = (A_HEADS * HEAD_DIM, A_KV_HEADS * HEAD_DIM, A_KV_HEADS * HEAD_DIM,
             IDX_HEADS * IDX_DIM, IDX_DIM, IDX_HEADS,
             LRU_WIDTH, LRU_WIDTH,
             Q_LORA, KV_LORA + QK_ROPE,
             D_HEADS * HEAD_DIM, D_KV_HEADS * HEAD_DIM, D_KV_HEADS * HEAD_DIM,
             N_BRANCH * D_MODEL)
IN_OFFSETS = tuple(int(v) for v in np.concatenate([[0], np.cumsum(IN_SPLITS)]))

LANES = 128
INT_MIN = -2 ** 31
INT_MAX = 2 ** 31 - 1
VMEM_LIMIT = 56 * 1024 * 1024
DSA_VMEM_LIMIT = 62 * 1024 * 1024
LOG2E = 1.4426950408889634
POS_SPLIT = 64
SEARCH_GROUP = 4

QKV_W = 512 + 256 + 128
IDX_W = 512 + 128
MLA_W = 640

FFN_TM = 1024
FFN_TF = 256
PROJ_TM = 512
DSA_TQ = 128
DSA_KC = 512
MLA_TQ = 512
MLA_KC = 512
LRU_L = 256
MERGE_TM = 512


def _alibi(i):
    return float(2.0 ** (-8.0 * i / N_ALIBI))


SLOPES_D = tuple(_alibi(i) for i in range(1, D_HEADS + 1))
SLOPES_A = tuple(_alibi(i) for i in range(D_HEADS + 1, N_ALIBI + 1))


def _bf16_parts(x):
    parts = []
    rem = np.float32(x)
    for _ in range(3):
        p = np.float32(np.asarray(rem, np.float32).astype(jnp.bfloat16).astype(np.float32))
        parts.append(float(p))
        rem = np.float32(rem - p)
    return parts


def _cparams(sem):
    return pltpu.CompilerParams(dimension_semantics=sem, vmem_limit_bytes=VMEM_LIMIT)


def _rms(x, g):
    return x * lax.rsqrt(jnp.mean(x * x, axis=-1, keepdims=True) + EPS) * g


def _nt_dot(a, b):
    return lax.dot_general(a, b, (((1,), (1,)), ((), ())), preferred_element_type=F32)


def _expm1(y):
    u = jnp.exp(y)
    safe = (u != 1.0) & (y > -1.0)
    ratio = y / jnp.log(jnp.where(safe, u, 2.0))
    return jnp.where(u == 1.0, y, jnp.where(safe, (u - 1.0) * ratio, u - 1.0))


def _log2(n):
    l = int(n).bit_length() - 1
    assert (1 << l) == n
    return l


def _ada_kernel(c_ref, w_ref, b_ref, o_ref):
    c = c_ref[...]
    sc = (c * jax.nn.sigmoid(c)).astype(BF16)
    o_ref[0] = jnp.dot(sc, w_ref[0].astype(BF16), preferred_element_type=F32) + b_ref[0]


def _ada_call(c_pad, w_ada, b_ada):
    depth, d, n = w_ada.shape
    tn = 1152
    return pl.pallas_call(
        _ada_kernel,
        grid=(depth, n // tn),
        in_specs=[pl.BlockSpec((c_pad.shape[0], d), lambda l, j: (0, 0)),
                  pl.BlockSpec((1, d, tn), lambda l, j: (l, 0, j)),
                  pl.BlockSpec((1, 1, tn), lambda l, j: (l, 0, j))],
        out_specs=pl.BlockSpec((1, c_pad.shape[0], tn), lambda l, j: (l, 0, j)),
        out_shape=jax.ShapeDtypeStruct((depth, c_pad.shape[0], n), F32),
        compiler_params=_cparams(("arbitrary", "arbitrary")),
        name="adaln",
    )(c_pad, w_ada, b_ada.reshape(depth, 1, n))


def _ffn_kernel(x_ref, mod_ref, g_ref, wg_ref, wu_ref, wo_ref, o_ref, h_ref, acc_ref, *, sub, resid_w):
    k = pl.program_id(1)
    m = mod_ref[0]

    @pl.when(k == 0)
    def _():
        x = x_ref[...]
        xn = _rms(x, g_ref[2 * sub:2 * sub + 1])
        h_ref[...] = (xn * (1.0 + m[3 * sub + 1:3 * sub + 2]) + m[3 * sub:3 * sub + 1]).astype(BF16)
        acc_ref[...] = jnp.zeros_like(acc_ref)

    h = h_ref[...]
    gate = jnp.dot(h, wg_ref[...], preferred_element_type=F32)
    up = jnp.dot(h, wu_ref[...], preferred_element_type=F32)
    act = (gate * jax.nn.sigmoid(gate) * up).astype(BF16)
    acc_ref[...] += jnp.dot(act, wo_ref[...], preferred_element_type=F32)

    @pl.when(k == pl.num_programs(1) - 1)
    def _():
        yn = _rms(acc_ref[...], g_ref[2 * sub + 1:2 * sub + 2])
        o_ref[...] = x_ref[...] + resid_w * m[3 * sub + 2:3 * sub + 3] * yn


def _ffn_call(x2, mod_l, g_l, w_in_bf, w_out_bf, *, sub, resid_w, s_len):
    t, d = x2.shape
    f = w_out_bf.shape[0]
    tm = min(FFN_TM, s_len)
    tf = FFN_TF
    nf = f // tf
    tpb = s_len // tm
    return pl.pallas_call(
        functools.partial(_ffn_kernel, sub=sub, resid_w=resid_w),
        grid=(t // tm, nf),
        in_specs=[pl.BlockSpec((tm, d), lambda i, k: (i, 0)),
                  pl.BlockSpec((1, 9, d), lambda i, k: (i // tpb, 0, 0)),
                  pl.BlockSpec((6, d), lambda i, k: (0, 0)),
                  pl.BlockSpec((d, tf), lambda i, k: (0, k)),
                  pl.BlockSpec((d, tf), lambda i, k: (0, k + nf)),
                  pl.BlockSpec((tf, d), lambda i, k: (k, 0))],
        out_specs=pl.BlockSpec((tm, d), lambda i, k: (i, 0)),
        out_shape=jax.ShapeDtypeStruct((t, d), F32),
        scratch_shapes=[pltpu.VMEM((tm, d), BF16), pltpu.VMEM((tm, d), F32)],
        compiler_params=_cparams(("arbitrary", "arbitrary")),
        name="ffn",
    )(x2, mod_l, g_l, w_in_bf, w_in_bf, w_out_bf)


def _proj_kernel(x_ref, mod_ref, g_ref, w_ref, kpos_ref, oa_ref, oi_ref, ow_ref, or_ref, oc_ref, od_ref):
    m = mod_ref[0]
    xn = _rms(x_ref[...], g_ref[2:3])
    h = (xn * (1.0 + m[4:5]) + m[3:4]).astype(BF16)
    off = 0
    for ref in (oa_ref, oi_ref, ow_ref, or_ref, oc_ref, od_ref):
        w = ref.shape[1]
        z = jnp.dot(h, w_ref[:, off:off + w], preferred_element_type=F32)
        if ref is oa_ref or ref is od_ref:
            z = jnp.concatenate([z[:, :512] * (HEAD_DIM ** -0.5), z[:, 512:768] + kpos_ref[...], z[:, 768:]],
                                axis=1)
        ref[...] = z.astype(ref.dtype)
        off += w


def _proj_call(x2, mod_l, g_l, w_all, kpos, *, s_len):
    t, d = x2.shape
    tm = min(PROJ_TM, s_len)
    tpb = s_len // tm
    widths = (QKV_W, IDX_W, LANES, 2 * LRU_WIDTH, MLA_W, QKV_W)
    dtypes = (BF16, BF16, F32, F32, F32, BF16)
    assert sum(widths) == w_all.shape[1]
    return pl.pallas_call(
        _proj_kernel,
        grid=(t // tm,),
        in_specs=[pl.BlockSpec((tm, d), lambda i: (i, 0)),
                  pl.BlockSpec((1, 9, d), lambda i: (i // tpb, 0, 0)),
                  pl.BlockSpec((6, d), lambda i: (0, 0)),
                  pl.BlockSpec(w_all.shape, lambda i: (0, 0)),
                  pl.BlockSpec((tm, 256), lambda i: (i % tpb, 0))],
        out_specs=[pl.BlockSpec((tm, w), lambda i: (i, 0)) for w in widths],
        out_shape=[jax.ShapeDtypeStruct((t, w), dt) for w, dt in zip(widths, dtypes)],
        compiler_params=_cparams(("arbitrary",)),
        name="mixer_proj",
    )(x2, mod_l, g_l, w_all, kpos)


def _head_rows(q, n_heads, tq):
    lane = lax.broadcasted_iota(I32, (tq, LANES), 1)
    rows = []
    for h in range(n_heads):
        tile = q[:, LANES * (h // 2):LANES * (h // 2 + 1)].astype(F32)
        keep = (lane >= HEAD_DIM) if (h % 2) else (lane < HEAD_DIM)
        rows.append(jnp.where(keep, tile, 0.0).astype(BF16))
    return jnp.concatenate(rows, axis=0)


def _alibi_q_tiles(q, slopes, tq):
    lane = lax.broadcasted_iota(I32, (tq, LANES), 1)
    rows = []
    for h, slope in enumerate(slopes):
        tile = q[:, LANES * (h // 2):LANES * (h // 2 + 1)].astype(F32)
        if h % 2:
            tile = pltpu.roll(tile, HEAD_DIM, axis=1)
        for i, part in enumerate(_bf16_parts(slope)):
            tile = jnp.where(lane == HEAD_DIM + 2 * i, POS_SPLIT * part, tile)
            tile = jnp.where(lane == HEAD_DIM + 2 * i + 1, part, tile)
        tile = jnp.where(lane >= HEAD_DIM + 6, 0.0, tile)
        rows.append(tile.astype(BF16))
    return jnp.concatenate(rows, axis=0)


def _gqa_out(accs, inv_ls, tq):
    lane = lax.broadcasted_iota(I32, (tq, LANES), 1)
    tiles = []
    for j in range(4):
        g = j // 2
        halves = []
        for h in (2 * j, 2 * j + 1):
            o = accs[g][h % 4] * inv_ls[g][h % 4]
            src_hi = (g == 1)
            dst_hi = (h % 2 == 1)
            if src_hi != dst_hi:
                o = pltpu.roll(o, HEAD_DIM, axis=1)
            halves.append(o)
        tiles.append(jnp.where(lane < HEAD_DIM, halves[0], halves[1]))
    return jnp.concatenate(tiles, axis=1)


def _dsa_kernel(q_ref, k_ref, v_ref, iq_ref, ik_ref, iw_ref, o_ref,
                key_ref, hi_ref, s_ref, tau_ref, iqs_ref, qs_ref, m_ref, l_ref, acc_ref, *, tq, kc, topk, pos_bits):
    qb = pl.program_id(1)
    q0 = qb * tq
    nk = lax.shift_right_logical(q0 + tq + kc - 1, _log2(kc))
    rowpos = q0 + lax.broadcasted_iota(I32, (tq, kc), 0)
    lane = lax.broadcasted_iota(I32, (tq, kc), 1)
    n_lt = kc // LANES
    lane_t = lax.broadcasted_iota(I32, (tq, LANES), 1)

    iqs_ref[...] = _head_rows(iq_ref[...], IDX_HEADS, tq)
    iw = iw_ref[...]

    def score_body(j, carry):
        ks = pl.multiple_of(j * kc, kc)
        d = _nt_dot(iqs_ref[...], ik_ref[pl.ds(ks, kc), :])
        acc = jnp.zeros((tq, kc), F32)
        for h in range(IDX_HEADS):
            acc = acc + iw[:, h:h + 1] * jnp.maximum(d[h * tq:(h + 1) * tq], 0.0)
        bits = pltpu.bitcast(acc, I32)
        key = jnp.where(bits < 0, bits ^ INT_MAX, bits)
        key = jnp.where(ks + lane <= rowpos, key, INT_MIN)
        key_ref[j] = key
        hi_ref[j] = lax.shift_right_arithmetic(key, 16).astype(jnp.int16)
        return carry

    lax.fori_loop(0, nk, score_body, 0)

    def count(pred):
        def body(j, cnt):
            kj = key_ref[j]
            for c in range(n_lt):
                kp = (j * kc + c * LANES) + lane_t
                cnt = cnt + jnp.where(pred(kj[:, c * LANES:(c + 1) * LANES], kp), 1.0, 0.0)
            return cnt
        cnt = lax.fori_loop(0, nk, body, jnp.zeros((tq, LANES), F32))
        return jnp.sum(cnt, axis=1, keepdims=True)

    kf = float(topk)
    n_valid = (q0 + 1 + lax.broadcasted_iota(I32, (tq, 1), 0)).astype(F32)
    settled0 = n_valid <= kf

    def count_hi(cand):
        cb = jnp.broadcast_to(lax.shift_right_arithmetic(cand, 16), (tq, LANES)).astype(jnp.int16)

        def body(j, cnt):
            hj = hi_ref[j]
            for c in range(n_lt):
                cnt = cnt + jnp.where(hj[:, c * LANES:(c + 1) * LANES] >= cb, jnp.int16(1), jnp.int16(0))
            return cnt
        cnt = lax.fori_loop(0, nk, body, jnp.zeros((tq, LANES), jnp.int16))
        return jnp.sum(cnt.astype(F32), axis=1, keepdims=True)

    def count_full(cand):
        cb = jnp.broadcast_to(cand, (tq, LANES))
        return count(lambda kv, kp: kv >= cb)

    def search(state, count_fn, stop):
        def cond(state):
            i, _, _, all_settled = state
            return (i < stop) & (all_settled == 0)

        def body(state):
            i, tau, cnt, _ = state
            for b in range(SEARCH_GROUP):
                cand = tau + lax.shift_left(jnp.int32(1), 31 - b - i)
                c = count_fn(cand)
                ok = c >= kf
                tau = jnp.where(ok, cand, tau)
                cnt = jnp.where(ok, c, cnt)
            settled = settled0 | (cnt == kf)
            return i + SEARCH_GROUP, tau, cnt, jnp.min(jnp.where(settled, 1, 0))

        return lax.while_loop(cond, body, state)

    state = (jnp.int32(0), jnp.full((tq, 1), INT_MIN, I32),
             jnp.broadcast_to((nk * kc).astype(F32), (tq, 1)), jnp.int32(0))
    state = search(state, count_hi, 16)
    _, tau, n_ge, _ = search(state, count_full, 32)

    tie_f = jnp.where((n_ge > kf) & (tau > INT_MIN), 1.0, 0.0)
    any_tie = jnp.max(tie_f) > 0.0
    tau_c = jnp.maximum(tau, INT_MIN + 1)
    tau_ref[...] = tau_c

    @pl.when(any_tie)
    def _():
        tb = jnp.broadcast_to(tau_c, (tq, LANES))
        need = kf - count(lambda kv, kp: kv > tb)

        def pos_body(i, p):
            cand = p | lax.shift_left(jnp.int32(1), pos_bits - 1 - i)
            cb = jnp.broadcast_to(cand, (tq, LANES))
            below = count(lambda kv, kp: (kv == tb) & (kp < cb))
            return jnp.where(below < need, cand, p)

        p = lax.fori_loop(0, pos_bits, pos_body, jnp.zeros((tq, 1), I32))
        p = jnp.where(tie_f > 0.0, p, INT_MAX)

        def rewrite_body(j, carry):
            kj = key_ref[j]
            kp = j * kc + lane
            sel = (kj > tau_c) | ((kj == tau_c) & (kp <= p))
            key_ref[j] = jnp.where(sel, 1, -1)
            return carry

        lax.fori_loop(0, nk, rewrite_body, 0)
        tau_ref[...] = jnp.zeros((tq, 1), I32)

    q = q_ref[...]
    for g in range(A_KV_HEADS):
        qs_ref[g] = _alibi_q_tiles(q[:, 4 * HEAD_DIM * g:4 * HEAD_DIM * (g + 1)], SLOPES_A[4 * g:4 * g + 4], tq)
    m_ref[...] = jnp.full(m_ref.shape, NEG, F32)
    l_ref[...] = jnp.zeros(l_ref.shape, F32)
    acc_ref[...] = jnp.zeros(acc_ref.shape, F32)

    def max_body(j, carry):
        ks = pl.multiple_of(j * kc, kc)
        sel = key_ref[j] >= tau_ref[...]
        for g in range(A_KV_HEADS):
            s = _nt_dot(qs_ref[g], k_ref[pl.ds(ks, kc), g * LANES:(g + 1) * LANES])
            for h in range(4):
                sh = jnp.where(sel, s[h * tq:(h + 1) * tq] * LOG2E, NEG)
                s_ref[g, h, j] = sh
                mp = m_ref[g, h]
                for c in range(n_lt):
                    mp = jnp.maximum(mp, sh[:, c * LANES:(c + 1) * LANES])
                m_ref[g, h] = mp
        return carry

    lax.fori_loop(0, nk, max_body, 0)
    for g in range(A_KV_HEADS):
        for h in range(4):
            m_ref[g, h] = jnp.broadcast_to(jnp.max(m_ref[g, h], axis=1, keepdims=True), (tq, LANES))

    def pv_body(j, carry):
        ks = pl.multiple_of(j * kc, kc)
        vch = v_ref[pl.ds(ks, kc), :]
        for g in range(A_KV_HEADS):
            ps = []
            for h in range(4):
                mb = m_ref[g, h]
                lp = l_ref[g, h]
                tiles = []
                for c in range(n_lt):
                    p = jnp.exp2(s_ref[g, h, j, :, c * LANES:(c + 1) * LANES] - mb)
                    lp = lp + p
                    tiles.append(p.astype(BF16))
                l_ref[g, h] = lp
                ps.append(jnp.concatenate(tiles, axis=1))
            pv = jnp.dot(jnp.concatenate(ps, axis=0), vch, preferred_element_type=F32)
            for h in range(4):
                acc_ref[g, h] += pv[h * tq:(h + 1) * tq]
        return carry

    lax.fori_loop(0, nk, pv_body, 0)
    accs = [[acc_ref[g, h] for h in range(4)] for g in range(A_KV_HEADS)]
    inv_ls = [[1.0 / jnp.sum(l_ref[g, h], axis=1, keepdims=True) for h in range(4)] for g in range(A_KV_HEADS)]
    o_ref[...] = _gqa_out(accs, inv_ls, tq).astype(o_ref.dtype)


def _dsa_call(qkv, idx, iw, *, batch, s_len):
    t = qkv.shape[0]
    tq = min(DSA_TQ, s_len)
    kc = min(DSA_KC, s_len)
    topk = min(TOPK_MAX, s_len // 4)
    assert kc >= topk and kc % tq == 0
    nqb = s_len // tq
    once = pl.Buffered(1)
    return pl.pallas_call(
        functools.partial(_dsa_kernel, tq=tq, kc=kc, topk=topk, pos_bits=_log2(s_len)),
        grid=(batch, nqb),
        in_specs=[pl.BlockSpec((tq, 512), lambda b, i: (b * nqb + i, 0)),
                  pl.BlockSpec((s_len, 256), lambda b, i: (b, 2), pipeline_mode=once),
                  pl.BlockSpec((s_len, 128), lambda b, i: (b, 6), pipeline_mode=once),
                  pl.BlockSpec((tq, 512), lambda b, i: (b * nqb + i, 0)),
                  pl.BlockSpec((s_len, 128), lambda b, i: (b, 4), pipeline_mode=once),
                  pl.BlockSpec((tq, LANES), lambda b, i: (b * nqb + i, 0))],
        out_specs=pl.BlockSpec((tq, 512), lambda b, i: (b * nqb + i, 0)),
        out_shape=jax.ShapeDtypeStruct((t, 512), BF16),
        scratch_shapes=[pltpu.VMEM((s_len // kc, tq, kc), I32),
                        pltpu.VMEM((s_len // kc, tq, kc), jnp.int16),
                        pltpu.VMEM((A_KV_HEADS, 4, s_len // kc, tq, kc), F32),
                        pltpu.VMEM((tq, 1), I32),
                        pltpu.VMEM((IDX_HEADS * tq, LANES), BF16),
                        pltpu.VMEM((A_KV_HEADS, 4 * tq, LANES), BF16),
                        pltpu.VMEM((A_KV_HEADS, 4, tq, LANES), F32),
                        pltpu.VMEM((A_KV_HEADS, 4, tq, LANES), F32),
                        pltpu.VMEM((A_KV_HEADS, 4, tq, LANES), F32)],
        compiler_params=pltpu.CompilerParams(dimension_semantics=("arbitrary", "arbitrary"),
                                             vmem_limit_bytes=DSA_VMEM_LIMIT),
        name="dsa",
    )(qkv, qkv, qkv, idx, idx, iw)


def _swa_kernel(q_ref, kp_ref, kc_ref, vp_ref, vc_ref, sink_ref, o_ref, *, tq):
    i = pl.program_id(1)
    q = q_ref[...]
    kk = jnp.concatenate([kp_ref[...], kc_ref[...]], axis=0)
    vv = jnp.concatenate([vp_ref[...], vc_ref[...]], axis=0)
    row = lax.broadcasted_iota(I32, (tq, 2 * tq), 0)
    col = lax.broadcasted_iota(I32, (tq, 2 * tq), 1)
    dist = row + tq - col
    valid = (dist >= 0) & (dist < WINDOW) & ((i * tq - tq + col) >= 0)
    sinks = sink_ref[...]
    accs, inv_ls = [], []
    for g in range(D_KV_HEADS):
        qs = _alibi_q_tiles(q[:, 4 * HEAD_DIM * g:4 * HEAD_DIM * (g + 1)], SLOPES_D[4 * g:4 * g + 4], tq)
        s = _nt_dot(qs, kk[:, g * LANES:(g + 1) * LANES])
        ps, ils = [], []
        for h in range(4):
            sh = jnp.where(valid, s[h * tq:(h + 1) * tq], NEG)
            sink = sinks[:, 4 * g + h:4 * g + h + 1] + SLOPES_D[4 * g + h] * (
                i * tq + lax.broadcasted_iota(I32, (tq, 1), 0)).astype(F32)
            m = jnp.maximum(jnp.max(sh, axis=1, keepdims=True), sink)
            p = jnp.exp(sh - m)
            ils.append(1.0 / (jnp.sum(p, axis=1, keepdims=True) + jnp.exp(sink - m)))
            ps.append(p.astype(BF16))
        pv = jnp.dot(jnp.concatenate(ps, axis=0), vv, preferred_element_type=F32)
        accs.append([pv[h * tq:(h + 1) * tq] for h in range(4)])
        inv_ls.append(ils)
    o_ref[...] = _gqa_out(accs, inv_ls, tq).astype(o_ref.dtype)


def _swa_call(qkv, sinks_pad, *, batch, s_len):
    t = qkv.shape[0]
    tq = BLOCK
    nqb = s_len // tq
    cur = lambda b, i: b * nqb + i
    prev = lambda b, i: b * nqb + jnp.maximum(i - 1, 0)
    return pl.pallas_call(
        functools.partial(_swa_kernel, tq=tq),
        grid=(batch, nqb),
        in_specs=[pl.BlockSpec((tq, 512), lambda b, i: (cur(b, i), 0)),
                  pl.BlockSpec((tq, 256), lambda b, i: (prev(b, i), 2)),
                  pl.BlockSpec((tq, 256), lambda b, i: (cur(b, i), 2)),
                  pl.BlockSpec((tq, 128), lambda b, i: (prev(b, i), 6)),
                  pl.BlockSpec((tq, 128), lambda b, i: (cur(b, i), 6)),
                  pl.BlockSpec((1, LANES), lambda b, i: (0, 0))],
        out_specs=pl.BlockSpec((tq, 512), lambda b, i: (cur(b, i), 0)),
        out_shape=jax.ShapeDtypeStruct((t, 512), BF16),
        compiler_params=_cparams(("arbitrary", "arbitrary")),
        name="swa",
    )(qkv, qkv, qkv, qkv, qkv, sinks_pad)


def _mla_prep_kernel(c_ref, cos_ref, sin_ref, gq_ref, gkv_ref, wq_ref, wqr_ref, wk_ref, wv_ref,
                     q_ref, k_ref, v_ref):
    c = c_ref[...]
    cos = cos_ref[...]
    sin = sin_ref[...]
    cos8 = jnp.concatenate([cos] * C_HEADS, axis=1)
    sin8 = jnp.concatenate([sin] * C_HEADS, axis=1)
    cqn = _rms(c[:, :Q_LORA], gq_ref[...]).astype(BF16)
    q = (jnp.dot(cqn, wq_ref[...], preferred_element_type=F32) * cos8
         + jnp.dot(cqn, wqr_ref[...], preferred_element_type=F32) * sin8)
    q_ref[...] = q.astype(BF16)
    ckvn = _rms(c[:, Q_LORA:Q_LORA + KV_LORA], gkv_ref[...]).astype(BF16)
    kr = c[:, 384:512] * cos + c[:, 512:640] * sin
    k = jnp.dot(ckvn, wk_ref[...], preferred_element_type=F32) + jnp.concatenate([kr] * C_HEADS, axis=1)
    k_ref[...] = k.astype(BF16)
    v_ref[...] = jnp.dot(ckvn, wv_ref[...], preferred_element_type=F32).astype(BF16)


def _mla_prep_call(cm, cos_t, sin_t, gq, gkv, wq, wqr, wk, wv, *, s_len):
    t = cm.shape[0]
    tm = min(PROJ_TM, s_len)
    tpb = s_len // tm
    full = lambda a: pl.BlockSpec(a.shape, lambda i: (0, 0))
    return pl.pallas_call(
        _mla_prep_kernel,
        grid=(t // tm,),
        in_specs=[pl.BlockSpec((tm, MLA_W), lambda i: (i, 0)),
                  pl.BlockSpec((tm, LANES), lambda i: (i % tpb, 0)),
                  pl.BlockSpec((tm, LANES), lambda i: (i % tpb, 0)),
                  full(gq), full(gkv), full(wq), full(wqr), full(wk), full(wv)],
        out_specs=[pl.BlockSpec((tm, 1024), lambda i: (i, 0)),
                   pl.BlockSpec((tm, 1024), lambda i: (i, 0)),
                   pl.BlockSpec((tm, 512), lambda i: (i, 0))],
        out_shape=[jax.ShapeDtypeStruct((t, 1024), BF16),
                   jax.ShapeDtypeStruct((t, 1024), BF16),
                   jax.ShapeDtypeStruct((t, 512), BF16)],
        compiler_params=_cparams(("arbitrary",)),
        name="mla_prep",
    )(cm, cos_t, sin_t, gq, gkv, wq, wqr, wk, wv)


def _mla_attn_kernel(q_ref, k_ref, v_ref, o_ref, m_ref, l_ref, acc_ref, *, tq, kc):
    qb = pl.program_id(2)
    q0 = qb * tq
    n_full = lax.shift_right_logical(q0, _log2(kc))
    rowpos = q0 + lax.broadcasted_iota(I32, (tq, kc), 0)
    lane = lax.broadcasted_iota(I32, (tq, kc), 1)
    c = ((QK_NOPE + QK_ROPE) ** -0.5) * LOG2E
    n_lt = kc // LANES
    m_ref[...] = jnp.full(m_ref.shape, NEG, F32)
    l_ref[...] = jnp.zeros(l_ref.shape, F32)
    acc_ref[...] = jnp.zeros(acc_ref.shape, F32)

    def scores(j, hh, masked):
        ks = pl.multiple_of(j * kc, kc)
        s = _nt_dot(q_ref[:, hh * LANES:(hh + 1) * LANES], k_ref[pl.ds(ks, kc), hh * LANES:(hh + 1) * LANES])
        if masked:
            s = jnp.where(ks + lane <= rowpos, s, NEG)
        return s

    def max_step(j, masked):
        for hh in range(2):
            s = scores(j, hh, masked)
            mp = m_ref[hh]
            for t in range(n_lt):
                mp = jnp.maximum(mp, s[:, t * LANES:(t + 1) * LANES])
            m_ref[hh] = mp

    def max_body(j, carry):
        max_step(j, False)
        return carry

    lax.fori_loop(0, n_full, max_body, 0)
    max_step(n_full, True)
    for hh in range(2):
        m_ref[hh] = jnp.broadcast_to(jnp.max(m_ref[hh], axis=1, keepdims=True), (tq, LANES))

    def pv_step(j, masked):
        ks = pl.multiple_of(j * kc, kc)
        vch = v_ref[pl.ds(ks, kc), :]
        for hh in range(2):
            s = scores(j, hh, masked)
            mb = m_ref[hh]
            lp = l_ref[hh]
            tiles = []
            for t in range(n_lt):
                p = jnp.exp2((s[:, t * LANES:(t + 1) * LANES] - mb) * c)
                lp = lp + p
                tiles.append(p.astype(BF16))
            l_ref[hh] = lp
            acc_ref[hh] += jnp.dot(jnp.concatenate(tiles, axis=1), vch, preferred_element_type=F32)

    def pv_body(j, carry):
        pv_step(j, False)
        return carry

    lax.fori_loop(0, n_full, pv_body, 0)
    pv_step(n_full, True)
    outs = [acc_ref[hh] * (1.0 / jnp.sum(l_ref[hh], axis=1, keepdims=True)) for hh in range(2)]
    lane_o = lax.broadcasted_iota(I32, (tq, LANES), 1)
    o_ref[...] = jnp.where(lane_o < V_DIM, outs[0], outs[1]).astype(o_ref.dtype)


def _mla_attn_call(qm, km, vm, *, batch, s_len):
    t = qm.shape[0]
    tq = min(MLA_TQ, s_len)
    kc = min(MLA_KC, s_len)
    assert kc % tq == 0
    nqb = s_len // tq
    return pl.pallas_call(
        functools.partial(_mla_attn_kernel, tq=tq, kc=kc),
        grid=(batch, C_HEADS // 2, nqb),
        in_specs=[pl.BlockSpec((tq, 256), lambda b, h, i: (b * nqb + i, h)),
                  pl.BlockSpec((s_len, 256), lambda b, h, i: (b, h)),
                  pl.BlockSpec((s_len, 128), lambda b, h, i: (b, h))],
        out_specs=pl.BlockSpec((tq, 128), lambda b, h, i: (b * nqb + i, h)),
        out_shape=jax.ShapeDtypeStruct((t, 512), BF16),
        scratch_shapes=[pltpu.VMEM((2, tq, LANES), F32)] * 3,
        compiler_params=_cparams(("arbitrary", "arbitrary", "arbitrary")),
        name="mla_attn",
    )(qm, km, vm)


def _lru_kernel(xr_ref, xg_ref, prev_ref, cw_ref, cb_ref, wa_ref, ba_ref, wx_ref, bx_ref, lam_ref,
                o_ref, xe_ref, h_ref, *, ln):
    i = pl.program_id(1)
    xe_ref[0:8, :] = jnp.where(i > 0, prev_ref[...], 0.0)
    xe_ref[8:8 + ln, :] = xr_ref[...]
    cw = cw_ref[...]
    xc = cb_ref[...] + cw[0:1] * xe_ref[pl.ds(5, ln), :]
    for j in range(1, CONV_WIDTH):
        xc = xc + cw[j:j + 1] * xe_ref[pl.ds(5 + j, ln), :]
    xcb = xc.astype(BF16)
    r = jax.nn.sigmoid(jnp.dot(xcb, wa_ref[...], preferred_element_type=F32) + ba_ref[...])
    gi = jax.nn.sigmoid(jnp.dot(xcb, wx_ref[...], preferred_element_type=F32) + bx_ref[...])
    z = -lam_ref[...]
    softplus = jnp.maximum(z, 0.0) + jnp.log1p(jnp.exp(-jnp.abs(z)))
    log_a = -LRU_C * r * softplus
    a = jnp.exp(log_a)
    b = jnp.sqrt(-_expm1(2.0 * log_a)) * (gi * xc)
    row = lax.broadcasted_iota(I32, (ln, LRU_WIDTH), 0)
    d = 1
    while d < ln:
        keep = row >= d
        b = jnp.where(keep, a * pltpu.roll(b, d, axis=0) + b, b)
        a = jnp.where(keep, a * pltpu.roll(a, d, axis=0), a)
        d *= 2
    h_prev = jnp.where(i > 0, h_ref[0:1, :], 0.0)
    h = a * h_prev + b
    h_ref[0:1, :] = h[ln - 1:ln, :]
    xg = xg_ref[...]
    gelu = 0.5 * xg * (1.0 + jnp.tanh(0.7978845608028654 * (xg + 0.044715 * (xg * xg * xg))))
    o_ref[...] = (h * gelu).astype(o_ref.dtype)


def _lru_call(rm, cw, cb, wa, ba, wx, bx, lam, *, batch, s_len):
    t = rm.shape[0]
    ln = min(LRU_L, s_len)
    nt = s_len // ln
    w = LRU_WIDTH
    vec = lambda a: pl.BlockSpec(a.shape, lambda b, i: (0, 0))
    return pl.pallas_call(
        functools.partial(_lru_kernel, ln=ln),
        grid=(batch, nt),
        in_specs=[pl.BlockSpec((ln, w), lambda b, i: (b * nt + i, 0)),
                  pl.BlockSpec((ln, w), lambda b, i: (b * nt + i, 1)),
                  pl.BlockSpec((8, w), lambda b, i: (jnp.maximum((b * nt + i) * (ln // 8) - 1, 0), 0)),
                  vec(cw), vec(cb), vec(wa), vec(ba), vec(wx), vec(bx), vec(lam)],
        out_specs=pl.BlockSpec((ln, w), lambda b, i: (b * nt + i, 0)),
        out_shape=jax.ShapeDtypeStruct((t, w), BF16),
        scratch_shapes=[pltpu.VMEM((ln + 8, w), F32), pltpu.VMEM((8, w), F32)],
        compiler_params=_cparams(("arbitrary", "arbitrary")),
        name="rglru",
    )(rm, rm, rm, cw, cb, wa, ba, wx, bx, lam)


def _merge_kernel(x_ref, mod_ref, g_ref, ya_ref, yb_ref, yc_ref, yd_ref, wg_ref, wb_ref, wo_ref, o_ref):
    m = mod_ref[0]
    x = x_ref[...]
    h = (_rms(x, g_ref[2:3]) * (1.0 + m[4:5]) + m[3:4]).astype(BF16)
    merged = None
    for n, y_ref in enumerate((ya_ref, yb_ref, yc_ref, yd_ref)):
        gate = jax.nn.sigmoid(jnp.dot(h, wg_ref[:, n * D_MODEL:(n + 1) * D_MODEL], preferred_element_type=F32))
        term = gate * jnp.dot(y_ref[...], wb_ref[n], preferred_element_type=F32)
        merged = term if merged is None else merged + term
    y = jnp.dot(merged.astype(BF16), wo_ref[...], preferred_element_type=F32)
    o_ref[...] = x + m[5:6] * _rms(y, g_ref[3:4])


def _merge_call(x2, mod_l, g_l, ys, w_gate, w_branch, w_out, *, s_len):
    t, d = x2.shape
    tm = min(MERGE_TM, s_len)
    tpb = s_len // tm
    return pl.pallas_call(
        _merge_kernel,
        grid=(t // tm,),
        in_specs=[pl.BlockSpec((tm, d), lambda i: (i, 0)),
                  pl.BlockSpec((1, 9, d), lambda i: (i // tpb, 0, 0)),
                  pl.BlockSpec((6, d), lambda i: (0, 0))]
                 + [pl.BlockSpec((tm, BRANCH_WIDTH), lambda i: (i, 0))] * N_BRANCH
                 + [pl.BlockSpec(w_gate.shape, lambda i: (0, 0)),
                    pl.BlockSpec(w_branch.shape, lambda i: (0, 0, 0)),
                    pl.BlockSpec(w_out.shape, lambda i: (0, 0))],
        out_specs=pl.BlockSpec((tm, d), lambda i: (i, 0)),
        out_shape=jax.ShapeDtypeStruct((t, d), F32),
        compiler_params=_cparams(("arbitrary",)),
        name="merge",
    )(x2, mod_l, g_l, *ys, w_gate, w_branch, w_out)


def _mixer_weights(w_in_l):
    o = IN_OFFSETS
    col = lambda n: w_in_l[:, o[n]:o[n + 1]]
    z = lambda n: jnp.zeros((D_MODEL, n), w_in_l.dtype)

    def pad_heads(k):
        return jnp.concatenate([k[:, :64], z(64), k[:, 64:], z(64)], axis=1)

    a_q, a_k, a_v, i_q, i_k, i_w, r_x, r_g, c_q, c_kv, d_q, d_k, d_v = (col(n) for n in range(13))
    k_rope = c_kv[:, KV_LORA:]
    half = QK_ROPE // 2
    k_rope_rot = jnp.concatenate([-k_rope[:, half:], k_rope[:, :half]], axis=1)
    groups = [a_q, pad_heads(a_k), a_v,
              i_q, i_k, i_k,
              i_w, z(LANES - IDX_HEADS),
              r_x, r_g,
              c_q, c_kv[:, :KV_LORA], z(64), k_rope, z(32), z(64), k_rope_rot, z(32),
              d_q, pad_heads(d_k), d_v]
    w_all = jnp.concatenate(groups, axis=1).astype(BF16)
    w_gate = col(13).astype(BF16)
    return w_all, w_gate


def _kpos_table(s_len):
    pos = np.arange(s_len)
    tile = np.zeros((s_len, LANES), np.float32)
    for i in range(3):
        tile[:, HEAD_DIM + 2 * i] = pos // POS_SPLIT
        tile[:, HEAD_DIM + 2 * i + 1] = pos % POS_SPLIT
    return jnp.asarray(np.concatenate([tile, tile], axis=1))


def _mla_weights(w_uq, w_ukv):
    dq = QK_NOPE + QK_ROPE
    half = QK_ROPE // 2
    zq = lambda n: jnp.zeros((Q_LORA, n), w_uq.dtype)
    wq, wqr, wk, wv = [], [], [], []
    for h in range(C_HEADS):
        nope = w_uq[:, h * dq:h * dq + QK_NOPE]
        r1 = w_uq[:, h * dq + QK_NOPE:h * dq + QK_NOPE + half]
        r2 = w_uq[:, h * dq + QK_NOPE + half:(h + 1) * dq]
        wq += [nope, r1, r2, zq(32)]
        wqr += [zq(64), -r2, r1, zq(32)]
        wk += [w_ukv[:, h * 128:h * 128 + QK_NOPE], jnp.zeros((KV_LORA, 64), w_ukv.dtype)]
        wv += [w_ukv[:, h * 128 + QK_NOPE:(h + 1) * 128]]
    cat = lambda xs: jnp.concatenate(xs, axis=1).astype(BF16)
    return cat(wq), cat(wqr), cat(wk), cat(wv)


def _rope_tables(s_len):
    half = QK_ROPE // 2
    inv = ROPE_THETA ** (-jnp.arange(half, dtype=F32) / half)
    ang = jnp.arange(s_len, dtype=F32)[:, None] * inv[None, :]
    cos, sin = jnp.cos(ang), jnp.sin(ang)
    ones = lambda n: jnp.ones((s_len, n), F32)
    zeros = lambda n: jnp.zeros((s_len, n), F32)
    cos_t = jnp.concatenate([ones(64), cos, cos, ones(32)], axis=1)
    sin_t = jnp.concatenate([zeros(64), sin, sin, zeros(32)], axis=1)
    return cos_t, sin_t


def _block_diag(w):
    n, bw, _ = w.shape
    out = jnp.zeros((n * bw, n * bw), w.dtype)
    for k in range(n):
        out = lax.dynamic_update_slice(out, w[k], (k * bw, k * bw))
    return out.astype(BF16)


def kernel(x, c, w_ada, b_ada, norm_g, ffn1_w_in, ffn1_w_out, w_in, conv_w, conv_b, lru_wa, lru_ba, lru_wx,
           lru_bx, lru_lambda, mla_g_q, mla_g_kv, mla_w_uq, mla_w_ukv, swa_sinks, w_branch, w_out, ffn2_w_in,
           ffn2_w_out):
    batch, s_len, d = x.shape
    depth = w_ada.shape[0]
    t = batch * s_len
    assert s_len // POS_SPLIT <= 256
    x2 = x.reshape(t, d)
    c_pad = jnp.zeros((8, d), F32).at[:batch].set(c)
    mod_all = _ada_call(c_pad, w_ada, b_ada)
    cos_t, sin_t = _rope_tables(s_len)
    kpos = _kpos_table(s_len)
    row = lambda v: v.reshape(1, -1)
    for l in range(depth):
        mod_l = mod_all[l, :batch].reshape(batch, 9, d)
        g_l = norm_g[l]
        x2 = _ffn_call(x2, mod_l, g_l, ffn1_w_in[l].astype(BF16), ffn1_w_out[l].astype(BF16),
                       sub=0, resid_w=0.5, s_len=s_len)
        w_all, w_gate = _mixer_weights(w_in[l])
        qkv_a, idx, iw, rm, cm, qkv_d = _proj_call(x2, mod_l, g_l, w_all, kpos, s_len=s_len)
        y_a = _dsa_call(qkv_a, idx, iw, batch=batch, s_len=s_len)
        y_b = _lru_call(rm, conv_w[l], row(conv_b[l]), _block_diag(lru_wa[l]), row(lru_ba[l]),
                        _block_diag(lru_wx[l]), row(lru_bx[l]), row(lru_lambda[l]), batch=batch, s_len=s_len)
        qm, km, vm = _mla_prep_call(cm, cos_t, sin_t, row(mla_g_q[l]), row(mla_g_kv[l]),
                                    *_mla_weights(mla_w_uq[l], mla_w_ukv[l]), s_len=s_len)
        y_c = _mla_attn_call(qm, km, vm, batch=batch, s_len=s_len)
        sinks_pad = jnp.zeros((1, LANES), F32).at[0, :D_HEADS].set(swa_sinks[l])
        y_d = _swa_call(qkv_d, sinks_pad, batch=batch, s_len=s_len)
        x2 = _merge_call(x2, mod_l, g_l, (y_a, y_b, y_c, y_d), w_gate, w_branch[l].astype(BF16),
                         w_out[l].astype(BF16), s_len=s_len)
        x2 = _ffn_call(x2, mod_l, g_l, ffn2_w_in[l].astype(BF16), ffn2_w_out[l].astype(BF16),
                       sub=2, resid_w=0.5, s_len=s_len)
    return x2.reshape(batch, s_len, d)
```

```python
import functools

import numpy as np
import jax
import jax.numpy as jnp
from jax import lax
from jax.experimental import pallas as pl
from jax.experimental.pallas import tpu as pltpu

F32 = jnp.float32
BF16 = jnp.bfloat16
I32 = jnp.int32

D_MODEL = 1024
HEAD_DIM = 64
BLOCK = 128
EPS = 1e-6
NEG = -1e30
A_HEADS = 8
A_KV_HEADS = 2
IDX_HEADS = 8
IDX_DIM = 64
TOPK_MAX = 256
LRU_WIDTH = 512
LRU_BLOCKS = 8
CONV_WIDTH = 4
LRU_C = 8.0
C_HEADS = 8
Q_LORA = 256
KV_LORA = 128
QK_NOPE = 64
QK_ROPE = 32
V_DIM = 64
ROPE_THETA = 10000.0
D_HEADS = 8
D_KV_HEADS = 2
WINDOW = 128
N_BRANCH = 4
BRANCH_WIDTH = 512
N_ALIBI = A_HEADS + D_HEADS
D_FF = 2816
IN_SPLITS = (A_HEADS * HEAD_DIM, A_KV_HEADS * HEAD_DIM, A_KV_HEADS * HEAD_DIM,
             IDX_HEADS * IDX_DIM, IDX_DIM, IDX_HEADS,
             LRU_WIDTH, LRU_WIDTH,
             Q_LORA, KV_LORA + QK_ROPE,
             D_HEADS * HEAD_DIM, D_KV_HEADS * HEAD_DIM, D_KV_HEADS * HEAD_DIM,
             N_BRANCH * D_MODEL)
IN_OFFSETS = tuple(int(v) for v in np.concatenate([[0], np.cumsum(IN_SPLITS)]))

LANES = 128
INT_MIN = -2 ** 31
INT_MAX = 2 ** 31 - 1
VMEM_LIMIT = 56 * 1024 * 1024
DSA_VMEM_LIMIT = 62 * 1024 * 1024
LOG2E = 1.4426950408889634
POS_SPLIT = 64
SEARCH_GROUP = 4

QKV_W = 512 + 256 + 128
IDX_W = 512 + 128
MLA_W = 640

FFN_TM = 1024
FFN_TF = 256
PROJ_TM = 512
DSA_TQ = 128
DSA_KC = 512
MLA_TQ = 512
MLA_KC = 512
SWA_BLOCKS = 4
LRU_L = 256
MERGE_TM = 512


def _alibi(i):
    return float(2.0 ** (-8.0 * i / N_ALIBI))


SLOPES_D = tuple(_alibi(i) for i in range(1, D_HEADS + 1))
SLOPES_A = tuple(_alibi(i) for i in range(D_HEADS + 1, N_ALIBI + 1))


def _bf16_parts(x):
    parts = []
    rem = np.float32(x)
    for _ in range(3):
        p = np.float32(np.asarray(rem, np.float32).astype(jnp.bfloat16).astype(np.float32))
        parts.append(float(p))
        rem = np.float32(rem - p)
    return parts


def _cparams(sem):
    return pltpu.CompilerParams(dimension_semantics=sem, vmem_limit_bytes=VMEM_LIMIT)


def _rms(x, g):
    return x * lax.rsqrt(jnp.mean(x * x, axis=-1, keepdims=True) + EPS) * g


def _nt_dot(a, b):
    return lax.dot_general(a, b, (((1,), (1,)), ((), ())), preferred_element_type=F32)


def _expm1(y):
    u = jnp.exp(y)
    safe = (u != 1.0) & (y > -1.0)
    ratio = y / jnp.log(jnp.where(safe, u, 2.0))
    return jnp.where(u == 1.0, y, jnp.where(safe, (u - 1.0) * ratio, u - 1.0))


def _log2(n):
    l = int(n).bit_length() - 1
    assert (1 << l) == n
    return l


def _ada_kernel(c_ref, w_ref, b_ref, o_ref):
    c = c_ref[...]
    sc = (c * jax.nn.sigmoid(c)).astype(BF16)
    o_ref[0] = jnp.dot(sc, w_ref[0].astype(BF16), preferred_element_type=F32) + b_ref[0]


def _ada_call(c_pad, w_ada, b_ada):
    depth, d, n = w_ada.shape
    tn = 1152
    return pl.pallas_call(
        _ada_kernel,
        grid=(depth, n // tn),
        in_specs=[pl.BlockSpec((c_pad.shape[0], d), lambda l, j: (0, 0)),
                  pl.BlockSpec((1, d, tn), lambda l, j: (l, 0, j)),
                  pl.BlockSpec((1, 1, tn), lambda l, j: (l, 0, j))],
        out_specs=pl.BlockSpec((1, c_pad.shape[0], tn), lambda l, j: (l, 0, j)),
        out_shape=jax.ShapeDtypeStruct((depth, c_pad.shape[0], n), F32),
        compiler_params=_cparams(("arbitrary", "arbitrary")),
        name="adaln",
    )(c_pad, w_ada, b_ada.reshape(depth, 1, n))


def _ffn_kernel(x_ref, mod_ref, g_ref, wg_ref, wu_ref, wo_ref, o_ref, h_ref, acc_ref, *, sub, resid_w):
    k = pl.program_id(1)
    m = mod_ref[0]

    @pl.when(k == 0)
    def _():
        x = x_ref[...]
        xn = _rms(x, g_ref[2 * sub:2 * sub + 1])
        h_ref[...] = (xn * (1.0 + m[3 * sub + 1:3 * sub + 2]) + m[3 * sub:3 * sub + 1]).astype(BF16)
        acc_ref[...] = jnp.zeros_like(acc_ref)

    h = h_ref[...]
    gate = jnp.dot(h, wg_ref[...], preferred_element_type=F32)
    up = jnp.dot(h, wu_ref[...], preferred_element_type=F32)
    act = (gate * jax.nn.sigmoid(gate) * up).astype(BF16)
    acc_ref[...] += jnp.dot(act, wo_ref[...], preferred_element_type=F32)

    @pl.when(k == pl.num_programs(1) - 1)
    def _():
        yn = _rms(acc_ref[...], g_ref[2 * sub + 1:2 * sub + 2])
        o_ref[...] = x_ref[...] + resid_w * m[3 * sub + 2:3 * sub + 3] * yn


def _ffn_call(x2, mod_l, g_l, w_in_bf, w_out_bf, *, sub, resid_w, s_len):
    t, d = x2.shape
    f = w_out_bf.shape[0]
    tm = min(FFN_TM, s_len)
    tf = FFN_TF
    nf = f // tf
    tpb = s_len // tm
    return pl.pallas_call(
        functools.partial(_ffn_kernel, sub=sub, resid_w=resid_w),
        grid=(t // tm, nf),
        in_specs=[pl.BlockSpec((tm, d), lambda i, k: (i, 0)),
                  pl.BlockSpec((1, 9, d), lambda i, k: (i // tpb, 0, 0)),
                  pl.BlockSpec((6, d), lambda i, k: (0, 0)),
                  pl.BlockSpec((d, tf), lambda i, k: (0, k)),
                  pl.BlockSpec((d, tf), lambda i, k: (0, k + nf)),
                  pl.BlockSpec((tf, d), lambda i, k: (k, 0))],
        out_specs=pl.BlockSpec((tm, d), lambda i, k: (i, 0)),
        out_shape=jax.ShapeDtypeStruct((t, d), F32),
        scratch_shapes=[pltpu.VMEM((tm, d), BF16), pltpu.VMEM((tm, d), F32)],
        compiler_params=_cparams(("arbitrary", "arbitrary")),
        name="ffn",
    )(x2, mod_l, g_l, w_in_bf, w_in_bf, w_out_bf)


def _proj_kernel(x_ref, mod_ref, g_ref, w_ref, kpos_ref, oa_ref, oi_ref, ow_ref, or_ref, oc_ref, od_ref):
    m = mod_ref[0]
    xn = _rms(x_ref[...], g_ref[2:3])
    h = (xn * (1.0 + m[4:5]) + m[3:4]).astype(BF16)
    off = 0
    for ref in (oa_ref, oi_ref, ow_ref, or_ref, oc_ref, od_ref):
        w = ref.shape[1]
        z = jnp.dot(h, w_ref[:, off:off + w], preferred_element_type=F32)
        if ref is oa_ref or ref is od_ref:
            z = jnp.concatenate([z[:, :512] * (HEAD_DIM ** -0.5), z[:, 512:768] + kpos_ref[...], z[:, 768:]],
                                axis=1)
        ref[...] = z.astype(ref.dtype)
        off += w


def _proj_call(x2, mod_l, g_l, w_all, kpos, *, s_len):
    t, d = x2.shape
    tm = min(PROJ_TM, s_len)
    tpb = s_len // tm
    widths = (QKV_W, IDX_W, LANES, 2 * LRU_WIDTH, MLA_W, QKV_W)
    dtypes = (BF16, BF16, F32, F32, F32, BF16)
    assert sum(widths) == w_all.shape[1]
    return pl.pallas_call(
        _proj_kernel,
        grid=(t // tm,),
        in_specs=[pl.BlockSpec((tm, d), lambda i: (i, 0)),
                  pl.BlockSpec((1, 9, d), lambda i: (i // tpb, 0, 0)),
                  pl.BlockSpec((6, d), lambda i: (0, 0)),
                  pl.BlockSpec(w_all.shape, lambda i: (0, 0)),
                  pl.BlockSpec((tm, 256), lambda i: (i % tpb, 0))],
        out_specs=[pl.BlockSpec((tm, w), lambda i: (i, 0)) for w in widths],
        out_shape=[jax.ShapeDtypeStruct((t, w), dt) for w, dt in zip(widths, dtypes)],
        compiler_params=_cparams(("arbitrary",)),
        name="mixer_proj",
    )(x2, mod_l, g_l, w_all, kpos)


def _head_rows(q, n_heads, tq):
    lane = lax.broadcasted_iota(I32, (tq, LANES), 1)
    rows = []
    for h in range(n_heads):
        tile = q[:, LANES * (h // 2):LANES * (h // 2 + 1)].astype(F32)
        keep = (lane >= HEAD_DIM) if (h % 2) else (lane < HEAD_DIM)
        rows.append(jnp.where(keep, tile, 0.0).astype(BF16))
    return jnp.concatenate(rows, axis=0)


def _alibi_q_tiles(q, slopes, tq):
    lane = lax.broadcasted_iota(I32, (tq, LANES), 1)
    rows = []
    for h, slope in enumerate(slopes):
        tile = q[:, LANES * (h // 2):LANES * (h // 2 + 1)].astype(F32)
        if h % 2:
            tile = pltpu.roll(tile, HEAD_DIM, axis=1)
        for i, part in enumerate(_bf16_parts(slope)):
            tile = jnp.where(lane == HEAD_DIM + 2 * i, POS_SPLIT * part, tile)
            tile = jnp.where(lane == HEAD_DIM + 2 * i + 1, part, tile)
        tile = jnp.where(lane >= HEAD_DIM + 6, 0.0, tile)
        rows.append(tile.astype(BF16))
    return jnp.concatenate(rows, axis=0)


def _gqa_out(accs, inv_ls, tq):
    lane = lax.broadcasted_iota(I32, (tq, LANES), 1)
    tiles = []
    for j in range(4):
        g = j // 2
        halves = []
        for h in (2 * j, 2 * j + 1):
            o = accs[g][h % 4] * inv_ls[g][h % 4]
            src_hi = (g == 1)
            dst_hi = (h % 2 == 1)
            if src_hi != dst_hi:
                o = pltpu.roll(o, HEAD_DIM, axis=1)
            halves.append(o)
        tiles.append(jnp.where(lane < HEAD_DIM, halves[0], halves[1]))
    return jnp.concatenate(tiles, axis=1)


def _dsa_kernel(q_ref, k_ref, v_ref, iq_ref, ik_ref, iw_ref, o_ref,
                key_ref, hi_ref, s_ref, tau_ref, iqs_ref, qs_ref, m_ref, l_ref, acc_ref, *, tq, kc, topk, pos_bits):
    qb = pl.program_id(1)
    q0 = qb * tq
    nk = lax.shift_right_logical(q0 + tq + kc - 1, _log2(kc))
    rowpos = q0 + lax.broadcasted_iota(I32, (tq, kc), 0)
    lane = lax.broadcasted_iota(I32, (tq, kc), 1)
    n_lt = kc // LANES
    lane_t = lax.broadcasted_iota(I32, (tq, LANES), 1)

    iqs_ref[...] = _head_rows(iq_ref[...], IDX_HEADS, tq)
    iw = iw_ref[...]

    def score_body(j, carry):
        ks = pl.multiple_of(j * kc, kc)
        d = _nt_dot(iqs_ref[...], ik_ref[pl.ds(ks, kc), :])
        acc = jnp.zeros((tq, kc), F32)
        for h in range(IDX_HEADS):
            acc = acc + iw[:, h:h + 1] * jnp.maximum(d[h * tq:(h + 1) * tq], 0.0)
        bits = pltpu.bitcast(acc, I32)
        key = jnp.where(bits < 0, bits ^ INT_MAX, bits)
        key = jnp.where(ks + lane <= rowpos, key, INT_MIN)
        key_ref[j] = key
        hi_ref[j] = lax.shift_right_arithmetic(key, 16).astype(jnp.int16)
        return carry

    lax.fori_loop(0, nk, score_body, 0)

    def count(pred):
        def body(j, cnt):
            kj = key_ref[j]
            for c in range(n_lt):
                kp = (j * kc + c * LANES) + lane_t
                cnt = cnt + jnp.where(pred(kj[:, c * LANES:(c + 1) * LANES], kp), 1.0, 0.0)
            return cnt
        cnt = lax.fori_loop(0, nk, body, jnp.zeros((tq, LANES), F32))
        return jnp.sum(cnt, axis=1, keepdims=True)

    kf = float(topk)
    n_valid = (q0 + 1 + lax.broadcasted_iota(I32, (tq, 1), 0)).astype(F32)
    settled0 = n_valid <= kf

    def count_hi(cand):
        cb = jnp.broadcast_to(lax.shift_right_arithmetic(cand, 16), (tq, LANES)).astype(jnp.int16)

        def body(j, cnt):
            hj = hi_ref[j]
            for c in range(n_lt):
                cnt = cnt + jnp.where(hj[:, c * LANES:(c + 1) * LANES] >= cb, jnp.int16(1), jnp.int16(0))
            return cnt
        cnt = lax.fori_loop(0, nk, body, jnp.zeros((tq, LANES), jnp.int16))
        return jnp.sum(cnt.astype(F32), axis=1, keepdims=True)

    def count_full(cand):
        cb = jnp.broadcast_to(cand, (tq, LANES))
        return count(lambda kv, kp: kv >= cb)

    def search(state, count_fn, stop):
        def cond(state):
            i, _, _, all_settled = state
            return (i < stop) & (all_settled == 0)

        def body(state):
            i, tau, cnt, _ = state
            for b in range(SEARCH_GROUP):
                cand = tau + lax.shift_left(jnp.int32(1), 31 - b - i)
                c = count_fn(cand)
                ok = c >= kf
                tau = jnp.where(ok, cand, tau)
                cnt = jnp.where(ok, c, cnt)
            settled = settled0 | (cnt == kf)
            return i + SEARCH_GROUP, tau, cnt, jnp.min(jnp.where(settled, 1, 0))

        return lax.while_loop(cond, body, state)

    state = (jnp.int32(0), jnp.full((tq, 1), INT_MIN, I32),
             jnp.broadcast_to((nk * kc).astype(F32), (tq, 1)), jnp.int32(0))
    state = search(state, count_hi, 16)
    _, tau, n_ge, _ = search(state, count_full, 32)

    tie_f = jnp.where((n_ge > kf) & (tau > INT_MIN), 1.0, 0.0)
    any_tie = jnp.max(tie_f) > 0.0
    tau_c = jnp.maximum(tau, INT_MIN + 1)
    tau_ref[...] = tau_c

    @pl.when(any_tie)
    def _():
        tb = jnp.broadcast_to(tau_c, (tq, LANES))
        need = kf - count(lambda kv, kp: kv > tb)

        def pos_body(i, p):
            cand = p | lax.shift_left(jnp.int32(1), pos_bits - 1 - i)
            cb = jnp.broadcast_to(cand, (tq, LANES))
            below = count(lambda kv, kp: (kv == tb) & (kp < cb))
            return jnp.where(below < need, cand, p)

        p = lax.fori_loop(0, pos_bits, pos_body, jnp.zeros((tq, 1), I32))
        p = jnp.where(tie_f > 0.0, p, INT_MAX)

        def rewrite_body(j, carry):
            kj = key_ref[j]
            kp = j * kc + lane
            sel = (kj > tau_c) | ((kj == tau_c) & (kp <= p))
            key_ref[j] = jnp.where(sel, 1, -1)
            return carry

        lax.fori_loop(0, nk, rewrite_body, 0)
        tau_ref[...] = jnp.zeros((tq, 1), I32)

    q = q_ref[...]
    for g in range(A_KV_HEADS):
        qs_ref[g] = _alibi_q_tiles(q[:, 4 * HEAD_DIM * g:4 * HEAD_DIM * (g + 1)], SLOPES_A[4 * g:4 * g + 4], tq)
    m_ref[...] = jnp.full(m_ref.shape, NEG, F32)
    l_ref[...] = jnp.zeros(l_ref.shape, F32)
    acc_ref[...] = jnp.zeros(acc_ref.shape, F32)

    def max_body(j, carry):
        ks = pl.multiple_of(j * kc, kc)
        sel = key_ref[j] >= tau_ref[...]
        for g in range(A_KV_HEADS):
            s = _nt_dot(qs_ref[g], k_ref[pl.ds(ks, kc), g * LANES:(g + 1) * LANES])
            for h in range(4):
                sh = jnp.where(sel, s[h * tq:(h + 1) * tq] * LOG2E, NEG)
                s_ref[g, h, j] = sh
                mp = m_ref[g, h]
                for c in range(n_lt):
                    mp = jnp.maximum(mp, sh[:, c * LANES:(c + 1) * LANES])
                m_ref[g, h] = mp
        return carry

    lax.fori_loop(0, nk, max_body, 0)
    for g in range(A_KV_HEADS):
        for h in range(4):
            m_ref[g, h] = jnp.broadcast_to(jnp.max(m_ref[g, h], axis=1, keepdims=True), (tq, LANES))

    def pv_body(j, carry):
        ks = pl.multiple_of(j * kc, kc)
        vch = v_ref[pl.ds(ks, kc), :]
        for g in range(A_KV_HEADS):
            ps = []
            for h in range(4):
                mb = m_ref[g, h]
                lp = l_ref[g, h]
                tiles = []
                for c in range(n_lt):
                    p = jnp.exp2(s_ref[g, h, j, :, c * LANES:(c + 1) * LANES] - mb)
                    lp = lp + p
                    tiles.append(p.astype(BF16))
                l_ref[g, h] = lp
                ps.append(jnp.concatenate(tiles, axis=1))
            pv = jnp.dot(jnp.concatenate(ps, axis=0), vch, preferred_element_type=F32)
            for h in range(4):
                acc_ref[g, h] += pv[h * tq:(h + 1) * tq]
        return carry

    lax.fori_loop(0, nk, pv_body, 0)
    accs = [[acc_ref[g, h] for h in range(4)] for g in range(A_KV_HEADS)]
    inv_ls = [[1.0 / jnp.sum(l_ref[g, h], axis=1, keepdims=True) for h in range(4)] for g in range(A_KV_HEADS)]
    o_ref[...] = _gqa_out(accs, inv_ls, tq).astype(o_ref.dtype)


def _dsa_call(qkv, idx, iw, *, batch, s_len):
    t = qkv.shape[0]
    tq = min(DSA_TQ, s_len)
    kc = min(DSA_KC, s_len)
    topk = min(TOPK_MAX, s_len // 4)
    assert kc >= topk and kc % tq == 0
    nqb = s_len // tq
    once = pl.Buffered(1)
    return pl.pallas_call(
        functools.partial(_dsa_kernel, tq=tq, kc=kc, topk=topk, pos_bits=_log2(s_len)),
        grid=(batch, nqb),
        in_specs=[pl.BlockSpec((tq, 512), lambda b, i: (b * nqb + i, 0)),
                  pl.BlockSpec((s_len, 256), lambda b, i: (b, 2), pipeline_mode=once),
                  pl.BlockSpec((s_len, 128), lambda b, i: (b, 6), pipeline_mode=once),
                  pl.BlockSpec((tq, 512), lambda b, i: (b * nqb + i, 0)),
                  pl.BlockSpec((s_len, 128), lambda b, i: (b, 4), pipeline_mode=once),
                  pl.BlockSpec((tq, LANES), lambda b, i: (b * nqb + i, 0))],
        out_specs=pl.BlockSpec((tq, 512), lambda b, i: (b * nqb + i, 0)),
        out_shape=jax.ShapeDtypeStruct((t, 512), BF16),
        scratch_shapes=[pltpu.VMEM((s_len // kc, tq, kc), I32),
                        pltpu.VMEM((s_len // kc, tq, kc), jnp.int16),
                        pltpu.VMEM((A_KV_HEADS, 4, s_len // kc, tq, kc), F32),
                        pltpu.VMEM((tq, 1), I32),
                        pltpu.VMEM((IDX_HEADS * tq, LANES), BF16),
                        pltpu.VMEM((A_KV_HEADS, 4 * tq, LANES), BF16),
                        pltpu.VMEM((A_KV_HEADS, 4, tq, LANES), F32),
                        pltpu.VMEM((A_KV_HEADS, 4, tq, LANES), F32),
                        pltpu.VMEM((A_KV_HEADS, 4, tq, LANES), F32)],
        compiler_params=pltpu.CompilerParams(dimension_semantics=("arbitrary", "arbitrary"),
                                             vmem_limit_bytes=DSA_VMEM_LIMIT),
        name="dsa",
    )(qkv, qkv, qkv, idx, idx, iw)


def _swa_kernel(q_ref, kp_ref, kc_ref, vp_ref, vc_ref, sink_ref, o_ref, *, tq, nblk):
    i = pl.program_id(1)
    row = lax.broadcasted_iota(I32, (tq, 2 * tq), 0)
    col = lax.broadcasted_iota(I32, (tq, 2 * tq), 1)
    dist = row + tq - col
    in_band = (dist >= 0) & (dist < WINDOW)
    sinks = sink_ref[...]
    for n in range(nblk):
        blk = i * nblk + n
        rows = slice(n * tq, (n + 1) * tq)
        q = q_ref[rows, :]
        k_prev = kp_ref[...] if n == 0 else kc_ref[(n - 1) * tq:n * tq, :]
        v_prev = vp_ref[...] if n == 0 else vc_ref[(n - 1) * tq:n * tq, :]
        kk = jnp.concatenate([k_prev, kc_ref[rows, :]], axis=0)
        vv = jnp.concatenate([v_prev, vc_ref[rows, :]], axis=0)
        valid = in_band & ((blk * tq - tq + col) >= 0)
        qpos = (blk * tq + lax.broadcasted_iota(I32, (tq, 1), 0)).astype(F32)
        accs, inv_ls = [], []
        for g in range(D_KV_HEADS):
            qs = _alibi_q_tiles(q[:, 4 * HEAD_DIM * g:4 * HEAD_DIM * (g + 1)], SLOPES_D[4 * g:4 * g + 4], tq)
            s = _nt_dot(qs, kk[:, g * LANES:(g + 1) * LANES])
            ps, ils = [], []
            for h in range(4):
                sh = jnp.where(valid, s[h * tq:(h + 1) * tq], NEG)
                sink = sinks[:, 4 * g + h:4 * g + h + 1] + SLOPES_D[4 * g + h] * qpos
                m = jnp.maximum(jnp.max(sh, axis=1, keepdims=True), sink)
                p = jnp.exp(sh - m)
                ils.append(1.0 / (jnp.sum(p, axis=1, keepdims=True) + jnp.exp(sink - m)))
                ps.append(p.astype(BF16))
            pv = jnp.dot(jnp.concatenate(ps, axis=0), vv, preferred_element_type=F32)
            accs.append([pv[h * tq:(h + 1) * tq] for h in range(4)])
            inv_ls.append(ils)
        o_ref[rows, :] = _gqa_out(accs, inv_ls, tq).astype(o_ref.dtype)


def _swa_call(qkv, sinks_pad, *, batch, s_len):
    t = qkv.shape[0]
    tq = BLOCK
    nblk = min(SWA_BLOCKS, s_len // tq)
    ts = nblk * tq
    nst = s_len // ts
    cur = lambda b, i: b * nst + i
    prev = lambda b, i: jnp.maximum((b * nst + i) * nblk - 1, 0)
    return pl.pallas_call(
        functools.partial(_swa_kernel, tq=tq, nblk=nblk),
        grid=(batch, nst),
        in_specs=[pl.BlockSpec((ts, 512), lambda b, i: (cur(b, i), 0)),
                  pl.BlockSpec((tq, 256), lambda b, i: (prev(b, i), 2)),
                  pl.BlockSpec((ts, 256), lambda b, i: (cur(b, i), 2)),
                  pl.BlockSpec((tq, 128), lambda b, i: (prev(b, i), 6)),
                  pl.BlockSpec((ts, 128), lambda b, i: (cur(b, i), 6)),
                  pl.BlockSpec((1, LANES), lambda b, i: (0, 0))],
        out_specs=pl.BlockSpec((ts, 512), lambda b, i: (cur(b, i), 0)),
        out_shape=jax.ShapeDtypeStruct((t, 512), BF16),
        compiler_params=_cparams(("arbitrary", "arbitrary")),
        name="swa",
    )(qkv, qkv, qkv, qkv, qkv, sinks_pad)


def _mla_prep_kernel(c_ref, cos_ref, sin_ref, gq_ref, gkv_ref, wq_ref, wqr_ref, wk_ref, wv_ref,
                     q_ref, k_ref, v_ref):
    c = c_ref[...]
    cos = cos_ref[...]
    sin = sin_ref[...]
    cos8 = jnp.concatenate([cos] * C_HEADS, axis=1)
    sin8 = jnp.concatenate([sin] * C_HEADS, axis=1)
    cqn = _rms(c[:, :Q_LORA], gq_ref[...]).astype(BF16)
    q = (jnp.dot(cqn, wq_ref[...], preferred_element_type=F32) * cos8
         + jnp.dot(cqn, wqr_ref[...], preferred_element_type=F32) * sin8)
    q_ref[...] = q.astype(BF16)
    ckvn = _rms(c[:, Q_LORA:Q_LORA + KV_LORA], gkv_ref[...]).astype(BF16)
    kr = c[:, 384:512] * cos + c[:, 512:640] * sin
    k = jnp.dot(ckvn, wk_ref[...], preferred_element_type=F32) + jnp.concatenate([kr] * C_HEADS, axis=1)
    k_ref[...] = k.astype(BF16)
    v_ref[...] = jnp.dot(ckvn, wv_ref[...], preferred_element_type=F32).astype(BF16)


def _mla_prep_call(cm, cos_t, sin_t, gq, gkv, wq, wqr, wk, wv, *, s_len):
    t = cm.shape[0]
    tm = min(PROJ_TM, s_len)
    tpb = s_len // tm
    full = lambda a: pl.BlockSpec(a.shape, lambda i: (0, 0))
    return pl.pallas_call(
        _mla_prep_kernel,
        grid=(t // tm,),
        in_specs=[pl.BlockSpec((tm, MLA_W), lambda i: (i, 0)),
                  pl.BlockSpec((tm, LANES), lambda i: (i % tpb, 0)),
                  pl.BlockSpec((tm, LANES), lambda i: (i % tpb, 0)),
                  full(gq), full(gkv), full(wq), full(wqr), full(wk), full(wv)],
        out_specs=[pl.BlockSpec((tm, 1024), lambda i: (i, 0)),
                   pl.BlockSpec((tm, 1024), lambda i: (i, 0)),
                   pl.BlockSpec((tm, 512), lambda i: (i, 0))],
        out_shape=[jax.ShapeDtypeStruct((t, 1024), BF16),
                   jax.ShapeDtypeStruct((t, 1024), BF16),
                   jax.ShapeDtypeStruct((t, 512), BF16)],
        compiler_params=_cparams(("arbitrary",)),
        name="mla_prep",
    )(cm, cos_t, sin_t, gq, gkv, wq, wqr, wk, wv)


def _mla_attn_kernel(q_ref, k_ref, v_ref, o_ref, s_ref, m_ref, l_ref, acc_ref, *, tq, kc):
    qb = pl.program_id(2)
    q0 = qb * tq
    n_full = lax.shift_right_logical(q0, _log2(kc))
    rowpos = q0 + lax.broadcasted_iota(I32, (tq, kc), 0)
    lane = lax.broadcasted_iota(I32, (tq, kc), 1)
    c = ((QK_NOPE + QK_ROPE) ** -0.5) * LOG2E
    n_lt = kc // LANES
    m_ref[...] = jnp.full(m_ref.shape, NEG, F32)
    l_ref[...] = jnp.zeros(l_ref.shape, F32)
    acc_ref[...] = jnp.zeros(acc_ref.shape, F32)

    def max_step(j, masked):
        ks = pl.multiple_of(j * kc, kc)
        for hh in range(2):
            s = _nt_dot(q_ref[:, hh * LANES:(hh + 1) * LANES],
                        k_ref[pl.ds(ks, kc), hh * LANES:(hh + 1) * LANES]) * c
            if masked:
                s = jnp.where(ks + lane <= rowpos, s, NEG)
            s_ref[hh, j] = s
            mp = m_ref[hh]
            for t in range(n_lt):
                mp = jnp.maximum(mp, s[:, t * LANES:(t + 1) * LANES])
            m_ref[hh] = mp

    def max_body(j, carry):
        max_step(j, False)
        return carry

    lax.fori_loop(0, n_full, max_body, 0)
    max_step(n_full, True)
    for hh in range(2):
        m_ref[hh] = jnp.broadcast_to(jnp.max(m_ref[hh], axis=1, keepdims=True), (tq, LANES))

    def pv_body(j, carry):
        ks = pl.multiple_of(j * kc, kc)
        vch = v_ref[pl.ds(ks, kc), :]
        for hh in range(2):
            mb = m_ref[hh]
            lp = l_ref[hh]
            tiles = []
            for t in range(n_lt):
                p = jnp.exp2(s_ref[hh, j, :, t * LANES:(t + 1) * LANES] - mb)
                lp = lp + p
                tiles.append(p.astype(BF16))
            l_ref[hh] = lp
            acc_ref[hh] += jnp.dot(jnp.concatenate(tiles, axis=1), vch, preferred_element_type=F32)
        return carry

    lax.fori_loop(0, n_full + 1, pv_body, 0)
    outs = [acc_ref[hh] * (1.0 / jnp.sum(l_ref[hh], axis=1, keepdims=True)) for hh in range(2)]
    lane_o = lax.broadcasted_iota(I32, (tq, LANES), 1)
    o_ref[...] = jnp.where(lane_o < V_DIM, outs[0], outs[1]).astype(o_ref.dtype)


def _mla_attn_call(qm, km, vm, *, batch, s_len):
    t = qm.shape[0]
    tq = min(MLA_TQ, s_len)
    kc = min(MLA_KC, s_len)
    assert kc % tq == 0
    nqb = s_len // tq
    once = pl.Buffered(1)
    return pl.pallas_call(
        functools.partial(_mla_attn_kernel, tq=tq, kc=kc),
        grid=(batch, C_HEADS // 2, nqb),
        in_specs=[pl.BlockSpec((tq, 256), lambda b, h, i: (b * nqb + i, h)),
                  pl.BlockSpec((s_len, 256), lambda b, h, i: (b, h), pipeline_mode=once),
                  pl.BlockSpec((s_len, 128), lambda b, h, i: (b, h), pipeline_mode=once)],
        out_specs=pl.BlockSpec((tq, 128), lambda b, h, i: (b * nqb + i, h)),
        out_shape=jax.ShapeDtypeStruct((t, 512), BF16),
        scratch_shapes=[pltpu.VMEM((2, s_len // kc, tq, kc), F32)] + [pltpu.VMEM((2, tq, LANES), F32)] * 3,
        compiler_params=_cparams(("arbitrary", "arbitrary", "arbitrary")),
        name="mla_attn",
    )(qm, km, vm)


def _lru_kernel(xr_ref, xg_ref, prev_ref, cw_ref, cb_ref, wa_ref, ba_ref, wx_ref, bx_ref, lam_ref,
                o_ref, xe_ref, h_ref, *, ln):
    i = pl.program_id(1)
    xe_ref[0:8, :] = jnp.where(i > 0, prev_ref[...], 0.0)
    xe_ref[8:8 + ln, :] = xr_ref[...]
    cw = cw_ref[...]
    xc = cb_ref[...] + cw[0:1] * xe_ref[pl.ds(5, ln), :]
    for j in range(1, CONV_WIDTH):
        xc = xc + cw[j:j + 1] * xe_ref[pl.ds(5 + j, ln), :]
    xcb = xc.astype(BF16)
    r = jax.nn.sigmoid(jnp.dot(xcb, wa_ref[...], preferred_element_type=F32) + ba_ref[...])
    gi = jax.nn.sigmoid(jnp.dot(xcb, wx_ref[...], preferred_element_type=F32) + bx_ref[...])
    z = -lam_ref[...]
    softplus = jnp.maximum(z, 0.0) + jnp.log1p(jnp.exp(-jnp.abs(z)))
    log_a = -LRU_C * r * softplus
    a = jnp.exp(log_a)
    b = jnp.sqrt(-_expm1(2.0 * log_a)) * (gi * xc)
    row = lax.broadcasted_iota(I32, (ln, LRU_WIDTH), 0)
    d = 1
    while d < ln:
        keep = row >= d
        b = jnp.where(keep, a * pltpu.roll(b, d, axis=0) + b, b)
        a = jnp.where(keep, a * pltpu.roll(a, d, axis=0), a)
        d *= 2
    h_prev = jnp.where(i > 0, h_ref[0:1, :], 0.0)
    h = a * h_prev + b
    h_ref[0:1, :] = h[ln - 1:ln, :]
    xg = xg_ref[...]
    gelu = 0.5 * xg * (1.0 + jnp.tanh(0.7978845608028654 * (xg + 0.044715 * (xg * xg * xg))))
    o_ref[...] = (h * gelu).astype(o_ref.dtype)


def _lru_call(rm, cw, cb, wa, ba, wx, bx, lam, *, batch, s_len):
    t = rm.shape[0]
    ln = min(LRU_L, s_len)
    nt = s_len // ln
    w = LRU_WIDTH
    vec = lambda a: pl.BlockSpec(a.shape, lambda b, i: (0, 0))
    return pl.pallas_call(
        functools.partial(_lru_kernel, ln=ln),
        grid=(batch, nt),
        in_specs=[pl.BlockSpec((ln, w), lambda b, i: (b * nt + i, 0)),
                  pl.BlockSpec((ln, w), lambda b, i: (b * nt + i, 1)),
                  pl.BlockSpec((8, w), lambda b, i: (jnp.maximum((b * nt + i) * (ln // 8) - 1, 0), 0)),
                  vec(cw), vec(cb), vec(wa), vec(ba), vec(wx), vec(bx), vec(lam)],
        out_specs=pl.BlockSpec((ln, w), lambda b, i: (b * nt + i, 0)),
        out_shape=jax.ShapeDtypeStruct((t, w), BF16),
        scratch_shapes=[pltpu.VMEM((ln + 8, w), F32), pltpu.VMEM((8, w), F32)],
        compiler_params=_cparams(("arbitrary", "arbitrary")),
        name="rglru",
    )(rm, rm, rm, cw, cb, wa, ba, wx, bx, lam)


def _merge_kernel(x_ref, mod_ref, g_ref, ya_ref, yb_ref, yc_ref, yd_ref, wg_ref, wb_ref, wo_ref, o_ref):
    m = mod_ref[0]
    x = x_ref[...]
    h = (_rms(x, g_ref[2:3]) * (1.0 + m[4:5]) + m[3:4]).astype(BF16)
    merged = None
    for n, y_ref in enumerate((ya_ref, yb_ref, yc_ref, yd_ref)):
        gate = jax.nn.sigmoid(jnp.dot(h, wg_ref[:, n * D_MODEL:(n + 1) * D_MODEL], preferred_element_type=F32))
        term = gate * jnp.dot(y_ref[...], wb_ref[n], preferred_element_type=F32)
        merged = term if merged is None else merged + term
    y = jnp.dot(merged.astype(BF16), wo_ref[...], preferred_element_type=F32)
    o_ref[...] = x + m[5:6] * _rms(y, g_ref[3:4])


def _merge_call(x2, mod_l, g_l, ys, w_gate, w_branch, w_out, *, s_len):
    t, d = x2.shape
    tm = min(MERGE_TM, s_len)
    tpb = s_len // tm
    return pl.pallas_call(
        _merge_kernel,
        grid=(t // tm,),
        in_specs=[pl.BlockSpec((tm, d), lambda i: (i, 0)),
                  pl.BlockSpec((1, 9, d), lambda i: (i // tpb, 0, 0)),
                  pl.BlockSpec((6, d), lambda i: (0, 0))]
                 + [pl.BlockSpec((tm, BRANCH_WIDTH), lambda i: (i, 0))] * N_BRANCH
                 + [pl.BlockSpec(w_gate.shape, lambda i: (0, 0)),
                    pl.BlockSpec(w_branch.shape, lambda i: (0, 0, 0)),
                    pl.BlockSpec(w_out.shape, lambda i: (0, 0))],
        out_specs=pl.BlockSpec((tm, d), lambda i: (i, 0)),
        out_shape=jax.ShapeDtypeStruct((t, d), F32),
        compiler_params=_cparams(("arbitrary",)),
        name="merge",
    )(x2, mod_l, g_l, *ys, w_gate, w_branch, w_out)


def _mixer_weights(w_in_l):
    o = IN_OFFSETS
    col = lambda n: w_in_l[:, o[n]:o[n + 1]]
    z = lambda n: jnp.zeros((D_MODEL, n), w_in_l.dtype)

    def pad_heads(k):
        return jnp.concatenate([k[:, :64], z(64), k[:, 64:], z(64)], axis=1)

    a_q, a_k, a_v, i_q, i_k, i_w, r_x, r_g, c_q, c_kv, d_q, d_k, d_v = (col(n) for n in range(13))
    k_rope = c_kv[:, KV_LORA:]
    half = QK_ROPE // 2
    k_rope_rot = jnp.concatenate([-k_rope[:, half:], k_rope[:, :half]], axis=1)
    groups = [a_q, pad_heads(a_k), a_v,
              i_q, i_k, i_k,
              i_w, z(LANES - IDX_HEADS),
              r_x, r_g,
              c_q, c_kv[:, :KV_LORA], z(64), k_rope, z(32), z(64), k_rope_rot, z(32),
              d_q, pad_heads(d_k), d_v]
    w_all = jnp.concatenate(groups, axis=1).astype(BF16)
    w_gate = col(13).astype(BF16)
    return w_all, w_gate


def _kpos_table(s_len):
    pos = np.arange(s_len)
    tile = np.zeros((s_len, LANES), np.float32)
    for i in range(3):
        tile[:, HEAD_DIM + 2 * i] = pos // POS_SPLIT
        tile[:, HEAD_DIM + 2 * i + 1] = pos % POS_SPLIT
    return jnp.asarray(np.concatenate([tile, tile], axis=1))


def _mla_weights(w_uq, w_ukv):
    dq = QK_NOPE + QK_ROPE
    half = QK_ROPE // 2
    zq = lambda n: jnp.zeros((Q_LORA, n), w_uq.dtype)
    wq, wqr, wk, wv = [], [], [], []
    for h in range(C_HEADS):
        nope = w_uq[:, h * dq:h * dq + QK_NOPE]
        r1 = w_uq[:, h * dq + QK_NOPE:h * dq + QK_NOPE + half]
        r2 = w_uq[:, h * dq + QK_NOPE + half:(h + 1) * dq]
        wq += [nope, r1, r2, zq(32)]
        wqr += [zq(64), -r2, r1, zq(32)]
        wk += [w_ukv[:, h * 128:h * 128 + QK_NOPE], jnp.zeros((KV_LORA, 64), w_ukv.dtype)]
        wv += [w_ukv[:, h * 128 + QK_NOPE:(h + 1) * 128]]
    cat = lambda xs: jnp.concatenate(xs, axis=1).astype(BF16)
    return cat(wq), cat(wqr), cat(wk), cat(wv)


def _rope_tables(s_len):
    half = QK_ROPE // 2
    inv = ROPE_THETA ** (-jnp.arange(half, dtype=F32) / half)
    ang = jnp.arange(s_len, dtype=F32)[:, None] * inv[None, :]
    cos, sin = jnp.cos(ang), jnp.sin(ang)
    ones = lambda n: jnp.ones((s_len, n), F32)
    zeros = lambda n: jnp.zeros((s_len, n), F32)
    cos_t = jnp.concatenate([ones(64), cos, cos, ones(32)], axis=1)
    sin_t = jnp.concatenate([zeros(64), sin, sin, zeros(32)], axis=1)
    return cos_t, sin_t


def _block_diag(w):
    n, bw, _ = w.shape
    out = jnp.zeros((n * bw, n * bw), w.dtype)
    for k in range(n):
        out = lax.dynamic_update_slice(out, w[k], (k * bw, k * bw))
    return out.astype(BF16)


def kernel(x, c, w_ada, b_ada, norm_g, ffn1_w_in, ffn1_w_out, w_in, conv_w, conv_b, lru_wa, lru_ba, lru_wx,
           lru_bx, lru_lambda, mla_g_q, mla_g_kv, mla_w_uq, mla_w_ukv, swa_sinks, w_branch, w_out, ffn2_w_in,
           ffn2_w_out):
    batch, s_len, d = x.shape
    depth = w_ada.shape[0]
    t = batch * s_len
    assert s_len // POS_SPLIT <= 256
    x2 = x.reshape(t, d)
    c_pad = jnp.zeros((8, d), F32).at[:batch].set(c)
    mod_all = _ada_call(c_pad, w_ada, b_ada)
    cos_t, sin_t = _rope_tables(s_len)
    kpos = _kpos_table(s_len)
    row = lambda v: v.reshape(1, -1)
    for l in range(depth):
        mod_l = mod_all[l, :batch].reshape(batch, 9, d)
        g_l = norm_g[l]
        x2 = _ffn_call(x2, mod_l, g_l, ffn1_w_in[l].astype(BF16), ffn1_w_out[l].astype(BF16),
                       sub=0, resid_w=0.5, s_len=s_len)
        w_all, w_gate = _mixer_weights(w_in[l])
        qkv_a, idx, iw, rm, cm, qkv_d = _proj_call(x2, mod_l, g_l, w_all, kpos, s_len=s_len)
        y_a = _dsa_call(qkv_a, idx, iw, batch=batch, s_len=s_len)
        y_b = _lru_call(rm, conv_w[l], row(conv_b[l]), _block_diag(lru_wa[l]), row(lru_ba[l]),
                        _block_diag(lru_wx[l]), row(lru_bx[l]), row(lru_lambda[l]), batch=batch, s_len=s_len)
        qm, km, vm = _mla_prep_call(cm, cos_t, sin_t, row(mla_g_q[l]), row(mla_g_kv[l]),
                                    *_mla_weights(mla_w_uq[l], mla_w_ukv[l]), s_len=s_len)
        y_c = _mla_attn_call(qm, km, vm, batch=batch, s_len=s_len)
        sinks_pad = jnp.zeros((1, LANES), F32).at[0, :D_HEADS].set(swa_sinks[l])
        y_d = _swa_call(qkv_d, sinks_pad, batch=batch, s_len=s_len)
        x2 = _merge_call(x2, mod_l, g_l, (y_a, y_b, y_c, y_d), w_gate, w_branch[l].astype(BF16),
                         w_out[l].astype(BF16), s_len=s_len)
        x2 = _ffn_call(x2, mod_l, g_l, ffn2_w_in[l].astype(BF16), ffn2_w_out[l].astype(BF16),
                       sub=2, resid_w=0.5, s_len=s_len)
    return x2.reshape(batch, s_len, d)
```

```python
import functools

import numpy as np
import jax
import jax.numpy as jnp
from jax import lax
from jax.experimental import pallas as pl
from jax.experimental.pallas import tpu as pltpu

F32 = jnp.float32
BF16 = jnp.bfloat16
I32 = jnp.int32

D_MODEL = 1024
HEAD_DIM = 64
BLOCK = 128
EPS = 1e-6
NEG = -1e30
A_HEADS = 8
A_KV_HEADS = 2
IDX_HEADS = 8
IDX_DIM = 64
TOPK_MAX = 256
LRU_WIDTH = 512
LRU_BLOCKS = 8
CONV_WIDTH = 4
LRU_C = 8.0
C_HEADS = 8
Q_LORA = 256
KV_LORA = 128
QK_NOPE = 64
QK_ROPE = 32
V_DIM = 64
ROPE_THETA = 10000.0
D_HEADS = 8
D_KV_HEADS = 2
WINDOW = 128
N_BRANCH = 4
BRANCH_WIDTH = 512
N_ALIBI = A_HEADS + D_HEADS
D_FF = 2816
IN_SPLITS = (A_HEADS * HEAD_DIM, A_KV_HEADS * HEAD_DIM, A_KV_HEADS * HEAD_DIM,
             IDX_HEADS * IDX_DIM, IDX_DIM, IDX_HEADS,
             LRU_WIDTH, LRU_WIDTH,
             Q_LORA, KV_LORA + QK_ROPE,
             D_HEADS * HEAD_DIM, D_KV_HEADS * HEAD_DIM, D_KV_HEADS * HEAD_DIM,
             N_BRANCH * D_MODEL)
IN_OFFSETS = tuple(int(v) for v in np.concatenate([[0], np.cumsum(IN_SPLITS)]))

LANES = 128
INT_MIN = -2 ** 31
INT_MAX = 2 ** 31 - 1
VMEM_LIMIT = 56 * 1024 * 1024
DSA_VMEM_LIMIT = 62 * 1024 * 1024
LOG2E = 1.4426950408889634
POS_SPLIT = 64
SEARCH_GROUP = 4
CNT_ROWS = 64

QKV_W = 512 + 256 + 128
IDX_W = 512 + 128
MLA_W = 640

FFN_TM = 1024
FFN_TF = 256
PROJ_TM = 512
DSA_TQ = 128
DSA_KC = 512
MLA_TQ = 512
MLA_KC = 512
SWA_BLOCKS = 4
LRU_L = 256
MERGE_TM = 512


def _alibi(i):
    return float(2.0 ** (-8.0 * i / N_ALIBI))


SLOPES_D = tuple(_alibi(i) for i in range(1, D_HEADS + 1))
SLOPES_A = tuple(_alibi(i) for i in range(D_HEADS + 1, N_ALIBI + 1))


def _bf16_parts(x):
    parts = []
    rem = np.float32(x)
    for _ in range(3):
        p = np.float32(np.asarray(rem, np.float32).astype(jnp.bfloat16).astype(np.float32))
        parts.append(float(p))
        rem = np.float32(rem - p)
    return parts


def _cparams(sem):
    return pltpu.CompilerParams(dimension_semantics=sem, vmem_limit_bytes=VMEM_LIMIT)


def _rms(x, g):
    return x * lax.rsqrt(jnp.mean(x * x, axis=-1, keepdims=True) + EPS) * g


def _nt_dot(a, b):
    return lax.dot_general(a, b, (((1,), (1,)), ((), ())), preferred_element_type=F32)


def _expm1(y):
    u = jnp.exp(y)
    safe = (u != 1.0) & (y > -1.0)
    ratio = y / jnp.log(jnp.where(safe, u, 2.0))
    return jnp.where(u == 1.0, y, jnp.where(safe, (u - 1.0) * ratio, u - 1.0))


def _log2(n):
    l = int(n).bit_length() - 1
    assert (1 << l) == n
    return l


def _ada_kernel(c_ref, w_ref, b_ref, o_ref):
    c = c_ref[...]
    sc = (c * jax.nn.sigmoid(c)).astype(BF16)
    o_ref[0] = jnp.dot(sc, w_ref[0].astype(BF16), preferred_element_type=F32) + b_ref[0]


def _ada_call(c_pad, w_ada, b_ada):
    depth, d, n = w_ada.shape
    tn = 1152
    return pl.pallas_call(
        _ada_kernel,
        grid=(depth, n // tn),
        in_specs=[pl.BlockSpec((c_pad.shape[0], d), lambda l, j: (0, 0)),
                  pl.BlockSpec((1, d, tn), lambda l, j: (l, 0, j)),
                  pl.BlockSpec((1, 1, tn), lambda l, j: (l, 0, j))],
        out_specs=pl.BlockSpec((1, c_pad.shape[0], tn), lambda l, j: (l, 0, j)),
        out_shape=jax.ShapeDtypeStruct((depth, c_pad.shape[0], n), F32),
        compiler_params=_cparams(("arbitrary", "arbitrary")),
        name="adaln",
    )(c_pad, w_ada, b_ada.reshape(depth, 1, n))


def _ffn_kernel(x_ref, mod_ref, g_ref, wg_ref, wu_ref, wo_ref, o_ref, h_ref, acc_ref, *, sub, resid_w):
    k = pl.program_id(1)
    m = mod_ref[0]

    @pl.when(k == 0)
    def _():
        x = x_ref[...]
        xn = _rms(x, g_ref[2 * sub:2 * sub + 1])
        h_ref[...] = (xn * (1.0 + m[3 * sub + 1:3 * sub + 2]) + m[3 * sub:3 * sub + 1]).astype(BF16)
        acc_ref[...] = jnp.zeros_like(acc_ref)

    h = h_ref[...]
    gate = jnp.dot(h, wg_ref[...], preferred_element_type=F32)
    up = jnp.dot(h, wu_ref[...], preferred_element_type=F32)
    act = (gate * jax.nn.sigmoid(gate) * up).astype(BF16)
    acc_ref[...] += jnp.dot(act, wo_ref[...], preferred_element_type=F32)

    @pl.when(k == pl.num_programs(1) - 1)
    def _():
        yn = _rms(acc_ref[...], g_ref[2 * sub + 1:2 * sub + 2])
        o_ref[...] = x_ref[...] + resid_w * m[3 * sub + 2:3 * sub + 3] * yn


def _ffn_call(x2, mod_l, g_l, w_in_bf, w_out_bf, *, sub, resid_w, s_len):
    t, d = x2.shape
    f = w_out_bf.shape[0]
    tm = min(FFN_TM, s_len)
    tf = FFN_TF
    nf = f // tf
    tpb = s_len // tm
    return pl.pallas_call(
        functools.partial(_ffn_kernel, sub=sub, resid_w=resid_w),
        grid=(t // tm, nf),
        in_specs=[pl.BlockSpec((tm, d), lambda i, k: (i, 0)),
                  pl.BlockSpec((1, 9, d), lambda i, k: (i // tpb, 0, 0)),
                  pl.BlockSpec((6, d), lambda i, k: (0, 0)),
                  pl.BlockSpec((d, tf), lambda i, k: (0, k)),
                  pl.BlockSpec((d, tf), lambda i, k: (0, k + nf)),
                  pl.BlockSpec((tf, d), lambda i, k: (k, 0))],
        out_specs=pl.BlockSpec((tm, d), lambda i, k: (i, 0)),
        out_shape=jax.ShapeDtypeStruct((t, d), F32),
        scratch_shapes=[pltpu.VMEM((tm, d), BF16), pltpu.VMEM((tm, d), F32)],
        compiler_params=_cparams(("arbitrary", "arbitrary")),
        name="ffn",
    )(x2, mod_l, g_l, w_in_bf, w_in_bf, w_out_bf)


def _proj_kernel(x_ref, mod_ref, g_ref, w_ref, kpos_ref, oa_ref, oi_ref, ow_ref, or_ref, oc_ref, od_ref):
    m = mod_ref[0]
    xn = _rms(x_ref[...], g_ref[2:3])
    h = (xn * (1.0 + m[4:5]) + m[3:4]).astype(BF16)
    off = 0
    for ref in (oa_ref, oi_ref, ow_ref, or_ref, oc_ref, od_ref):
        w = ref.shape[1]
        z = jnp.dot(h, w_ref[:, off:off + w], preferred_element_type=F32)
        if ref is oa_ref or ref is od_ref:
            z = jnp.concatenate([z[:, :512] * (HEAD_DIM ** -0.5), z[:, 512:768] + kpos_ref[...], z[:, 768:]],
                                axis=1)
        ref[...] = z.astype(ref.dtype)
        off += w


def _proj_call(x2, mod_l, g_l, w_all, kpos, *, s_len):
    t, d = x2.shape
    tm = min(PROJ_TM, s_len)
    tpb = s_len // tm
    widths = (QKV_W, IDX_W, LANES, 2 * LRU_WIDTH, MLA_W, QKV_W)
    dtypes = (BF16, BF16, F32, F32, F32, BF16)
    assert sum(widths) == w_all.shape[1]
    return pl.pallas_call(
        _proj_kernel,
        grid=(t // tm,),
        in_specs=[pl.BlockSpec((tm, d), lambda i: (i, 0)),
                  pl.BlockSpec((1, 9, d), lambda i: (i // tpb, 0, 0)),
                  pl.BlockSpec((6, d), lambda i: (0, 0)),
                  pl.BlockSpec(w_all.shape, lambda i: (0, 0)),
                  pl.BlockSpec((tm, 256), lambda i: (i % tpb, 0))],
        out_specs=[pl.BlockSpec((tm, w), lambda i: (i, 0)) for w in widths],
        out_shape=[jax.ShapeDtypeStruct((t, w), dt) for w, dt in zip(widths, dtypes)],
        compiler_params=_cparams(("arbitrary",)),
        name="mixer_proj",
    )(x2, mod_l, g_l, w_all, kpos)


def _head_rows(q, n_heads, tq):
    lane = lax.broadcasted_iota(I32, (tq, LANES), 1)
    rows = []
    for h in range(n_heads):
        tile = q[:, LANES * (h // 2):LANES * (h // 2 + 1)].astype(F32)
        keep = (lane >= HEAD_DIM) if (h % 2) else (lane < HEAD_DIM)
        rows.append(jnp.where(keep, tile, 0.0).astype(BF16))
    return jnp.concatenate(rows, axis=0)


def _alibi_q_tiles(q, slopes, tq):
    lane = lax.broadcasted_iota(I32, (tq, LANES), 1)
    rows = []
    for h, slope in enumerate(slopes):
        tile = q[:, LANES * (h // 2):LANES * (h // 2 + 1)].astype(F32)
        if h % 2:
            tile = pltpu.roll(tile, HEAD_DIM, axis=1)
        for i, part in enumerate(_bf16_parts(slope)):
            tile = jnp.where(lane == HEAD_DIM + 2 * i, POS_SPLIT * part, tile)
            tile = jnp.where(lane == HEAD_DIM + 2 * i + 1, part, tile)
        tile = jnp.where(lane >= HEAD_DIM + 6, 0.0, tile)
        rows.append(tile.astype(BF16))
    return jnp.concatenate(rows, axis=0)


def _gqa_out(accs, inv_ls, tq):
    lane = lax.broadcasted_iota(I32, (tq, LANES), 1)
    tiles = []
    for j in range(4):
        g = j // 2
        halves = []
        for h in (2 * j, 2 * j + 1):
            o = accs[g][h % 4] * inv_ls[g][h % 4]
            src_hi = (g == 1)
            dst_hi = (h % 2 == 1)
            if src_hi != dst_hi:
                o = pltpu.roll(o, HEAD_DIM, axis=1)
            halves.append(o)
        tiles.append(jnp.where(lane < HEAD_DIM, halves[0], halves[1]))
    return jnp.concatenate(tiles, axis=1)


def _dsa_kernel(q_ref, k_ref, v_ref, iq_ref, ik_ref, iw_ref, o_ref,
                key_ref, hi_ref, s_ref, tau_ref, iqs_ref, qs_ref, m_ref, l_ref, acc_ref, *, tq, kc, topk, pos_bits):
    qb = pl.program_id(1)
    q0 = qb * tq
    nk = lax.shift_right_logical(q0 + tq + kc - 1, _log2(kc))
    n_lt = kc // LANES
    n_rt = kc // CNT_ROWS
    qpos_l = q0 + lax.broadcasted_iota(I32, (kc, tq), 1)
    krow = lax.broadcasted_iota(I32, (kc, tq), 0)
    krow_t = lax.broadcasted_iota(I32, (CNT_ROWS, tq), 0)

    iqs_ref[...] = _head_rows(iq_ref[...], IDX_HEADS, tq)
    iw_t = iw_ref[...].T

    def score_body(j, carry):
        ks = pl.multiple_of(j * kc, kc)
        d = _nt_dot(ik_ref[pl.ds(ks, kc), :], iqs_ref[...])
        acc = jnp.zeros((kc, tq), F32)
        for h in range(IDX_HEADS):
            acc = acc + iw_t[h:h + 1, :] * jnp.maximum(d[:, h * tq:(h + 1) * tq], 0.0)
        bits = pltpu.bitcast(acc, I32)
        key = jnp.where(bits < 0, bits ^ INT_MAX, bits)
        key = jnp.where(ks + krow <= qpos_l, key, INT_MIN)
        key_ref[j] = key
        hi_ref[j] = lax.shift_right_arithmetic(key, 16).astype(jnp.int16)
        return carry

    lax.fori_loop(0, nk, score_body, 0)

    def count(pred):
        def body(j, cnt):
            for r in range(n_rt):
                kp = (j * kc + r * CNT_ROWS) + krow_t
                cnt = cnt + jnp.where(pred(key_ref[j, r * CNT_ROWS:(r + 1) * CNT_ROWS, :], kp), 1.0, 0.0)
            return cnt
        cnt = lax.fori_loop(0, nk, body, jnp.zeros((CNT_ROWS, tq), F32))
        return jnp.sum(cnt, axis=0, keepdims=True)

    kf = float(topk)
    n_valid = (q0 + 1 + lax.broadcasted_iota(I32, (1, tq), 1)).astype(F32)
    settled0 = n_valid <= kf

    def count_hi(cand):
        cb = jnp.broadcast_to(lax.shift_right_arithmetic(cand, 16), (CNT_ROWS, tq)).astype(jnp.int16)

        def body(j, cnt):
            for r in range(n_rt):
                hj = hi_ref[j, r * CNT_ROWS:(r + 1) * CNT_ROWS, :]
                cnt = cnt + jnp.where(hj >= cb, jnp.int16(1), jnp.int16(0))
            return cnt
        cnt = lax.fori_loop(0, nk, body, jnp.zeros((CNT_ROWS, tq), jnp.int16))
        return jnp.sum(cnt.astype(F32), axis=0, keepdims=True)

    def count_full(cand):
        cb = jnp.broadcast_to(cand, (CNT_ROWS, tq))
        return count(lambda kv, kp: kv >= cb)

    def bit_step(i, tau, cnt, count_fn):
        cand = tau + lax.shift_left(jnp.int32(1), 31 - i)
        c = count_fn(cand)
        ok = c >= kf
        return jnp.where(ok, cand, tau), jnp.where(ok, c, cnt)

    tau, n_ge = lax.fori_loop(0, 16, lambda i, st: bit_step(i, st[0], st[1], count_hi),
                              (jnp.full((1, tq), INT_MIN, I32), jnp.broadcast_to((nk * kc).astype(F32), (1, tq))))

    def all_settled(cnt):
        return jnp.min(jnp.where(settled0 | (cnt == kf), 1, 0))

    def low_cond(state):
        i, _, _, done = state
        return (i < 32) & (done == 0)

    def low_body(state):
        i, tau, cnt, _ = state
        for b in range(SEARCH_GROUP):
            tau, cnt = bit_step(i + b, tau, cnt, count_full)
        return i + SEARCH_GROUP, tau, cnt, all_settled(cnt)

    _, tau, n_ge, _ = lax.while_loop(low_cond, low_body, (jnp.int32(16), tau, n_ge, all_settled(n_ge)))

    tie_f = jnp.where((n_ge > kf) & (tau > INT_MIN), 1.0, 0.0)
    any_tie = jnp.max(tie_f) > 0.0
    tau_c = jnp.maximum(tau, INT_MIN + 1)
    tau_ref[...] = jnp.broadcast_to(tau_c, tau_ref.shape)

    @pl.when(any_tie)
    def _():
        tb = jnp.broadcast_to(tau_c, (CNT_ROWS, tq))
        need = kf - count(lambda kv, kp: kv > tb)

        def pos_body(i, p):
            cand = p | lax.shift_left(jnp.int32(1), pos_bits - 1 - i)
            cb = jnp.broadcast_to(cand, (CNT_ROWS, tq))
            below = count(lambda kv, kp: (kv == tb) & (kp < cb))
            return jnp.where(below < need, cand, p)

        p = lax.fori_loop(0, pos_bits, pos_body, jnp.zeros((1, tq), I32))
        p = jnp.where(tie_f > 0.0, p, INT_MAX)

        def rewrite_body(j, carry):
            kj = key_ref[j]
            kp = j * kc + krow
            sel = (kj > tau_c) | ((kj == tau_c) & (kp <= p))
            key_ref[j] = jnp.where(sel, 1, -1)
            return carry

        lax.fori_loop(0, nk, rewrite_body, 0)
        tau_ref[...] = jnp.zeros(tau_ref.shape, I32)

    q = q_ref[...]
    for g in range(A_KV_HEADS):
        qs_ref[g] = _alibi_q_tiles(q[:, 4 * HEAD_DIM * g:4 * HEAD_DIM * (g + 1)], SLOPES_A[4 * g:4 * g + 4], tq)
    m_ref[...] = jnp.full(m_ref.shape, NEG, F32)
    l_ref[...] = jnp.zeros(l_ref.shape, F32)
    acc_ref[...] = jnp.zeros(acc_ref.shape, F32)

    def max_body(j, carry):
        ks = pl.multiple_of(j * kc, kc)
        sel = jnp.where(key_ref[j] >= tau_ref[0:1, :], 1.0, 0.0).T > 0.5
        for g in range(A_KV_HEADS):
            s = _nt_dot(qs_ref[g], k_ref[pl.ds(ks, kc), g * LANES:(g + 1) * LANES])
            for h in range(4):
                sh = jnp.where(sel, s[h * tq:(h + 1) * tq] * LOG2E, NEG)
                s_ref[g, h, j] = sh
                mp = m_ref[g, h]
                for c in range(n_lt):
                    mp = jnp.maximum(mp, sh[:, c * LANES:(c + 1) * LANES])
                m_ref[g, h] = mp
        return carry

    lax.fori_loop(0, nk, max_body, 0)
    for g in range(A_KV_HEADS):
        for h in range(4):
            m_ref[g, h] = jnp.broadcast_to(jnp.max(m_ref[g, h], axis=1, keepdims=True), (tq, LANES))

    def pv_body(j, carry):
        ks = pl.multiple_of(j * kc, kc)
        vch = v_ref[pl.ds(ks, kc), :]
        for g in range(A_KV_HEADS):
            ps = []
            for h in range(4):
                mb = m_ref[g, h]
                lp = l_ref[g, h]
                tiles = []
                for c in range(n_lt):
                    p = jnp.exp2(s_ref[g, h, j, :, c * LANES:(c + 1) * LANES] - mb)
                    lp = lp + p
                    tiles.append(p.astype(BF16))
                l_ref[g, h] = lp
                ps.append(jnp.concatenate(tiles, axis=1))
            pv = jnp.dot(jnp.concatenate(ps, axis=0), vch, preferred_element_type=F32)
            for h in range(4):
                acc_ref[g, h] += pv[h * tq:(h + 1) * tq]
        return carry

    lax.fori_loop(0, nk, pv_body, 0)
    accs = [[acc_ref[g, h] for h in range(4)] for g in range(A_KV_HEADS)]
    inv_ls = [[1.0 / jnp.sum(l_ref[g, h], axis=1, keepdims=True) for h in range(4)] for g in range(A_KV_HEADS)]
    o_ref[...] = _gqa_out(accs, inv_ls, tq).astype(o_ref.dtype)


def _dsa_call(qkv, idx, iw, *, batch, s_len):
    t = qkv.shape[0]
    tq = min(DSA_TQ, s_len)
    kc = min(DSA_KC, s_len)
    topk = min(TOPK_MAX, s_len // 4)
    assert kc >= topk and kc % tq == 0
    nqb = s_len // tq
    once = pl.Buffered(1)
    return pl.pallas_call(
        functools.partial(_dsa_kernel, tq=tq, kc=kc, topk=topk, pos_bits=_log2(s_len)),
        grid=(batch, nqb),
        in_specs=[pl.BlockSpec((tq, 512), lambda b, i: (b * nqb + i, 0)),
                  pl.BlockSpec((s_len, 256), lambda b, i: (b, 2), pipeline_mode=once),
                  pl.BlockSpec((s_len, 128), lambda b, i: (b, 6), pipeline_mode=once),
                  pl.BlockSpec((tq, 512), lambda b, i: (b * nqb + i, 0)),
                  pl.BlockSpec((s_len, 128), lambda b, i: (b, 4), pipeline_mode=once),
                  pl.BlockSpec((tq, LANES), lambda b, i: (b * nqb + i, 0))],
        out_specs=pl.BlockSpec((tq, 512), lambda b, i: (b * nqb + i, 0)),
        out_shape=jax.ShapeDtypeStruct((t, 512), BF16),
        scratch_shapes=[pltpu.VMEM((s_len // kc, kc, tq), I32),
                        pltpu.VMEM((s_len // kc, kc, tq), jnp.int16),
                        pltpu.VMEM((A_KV_HEADS, 4, s_len // kc, tq, kc), F32),
                        pltpu.VMEM((8, tq), I32),
                        pltpu.VMEM((IDX_HEADS * tq, LANES), BF16),
                        pltpu.VMEM((A_KV_HEADS, 4 * tq, LANES), BF16),
                        pltpu.VMEM((A_KV_HEADS, 4, tq, LANES), F32),
                        pltpu.VMEM((A_KV_HEADS, 4, tq, LANES), F32),
                        pltpu.VMEM((A_KV_HEADS, 4, tq, LANES), F32)],
        compiler_params=pltpu.CompilerParams(dimension_semantics=("arbitrary", "arbitrary"),
                                             vmem_limit_bytes=DSA_VMEM_LIMIT),
        name="dsa",
    )(qkv, qkv, qkv, idx, idx, iw)


def _swa_kernel(q_ref, kp_ref, kc_ref, vp_ref, vc_ref, sink_ref, o_ref, *, tq, nblk):
    i = pl.program_id(1)
    row = lax.broadcasted_iota(I32, (tq, 2 * tq), 0)
    col = lax.broadcasted_iota(I32, (tq, 2 * tq), 1)
    dist = row + tq - col
    in_band = (dist >= 0) & (dist < WINDOW)
    sinks = sink_ref[...]
    for n in range(nblk):
        blk = i * nblk + n
        rows = slice(n * tq, (n + 1) * tq)
        q = q_ref[rows, :]
        k_prev = kp_ref[...] if n == 0 else kc_ref[(n - 1) * tq:n * tq, :]
        v_prev = vp_ref[...] if n == 0 else vc_ref[(n - 1) * tq:n * tq, :]
        kk = jnp.concatenate([k_prev, kc_ref[rows, :]], axis=0)
        vv = jnp.concatenate([v_prev, vc_ref[rows, :]], axis=0)
        valid = in_band & ((blk * tq - tq + col) >= 0)
        qpos = (blk * tq + lax.broadcasted_iota(I32, (tq, 1), 0)).astype(F32)
        accs, inv_ls = [], []
        for g in range(D_KV_HEADS):
            qs = _alibi_q_tiles(q[:, 4 * HEAD_DIM * g:4 * HEAD_DIM * (g + 1)], SLOPES_D[4 * g:4 * g + 4], tq)
            s = _nt_dot(qs, kk[:, g * LANES:(g + 1) * LANES])
            ps, ils = [], []
            for h in range(4):
                sh = jnp.where(valid, s[h * tq:(h + 1) * tq], NEG)
                sink = sinks[:, 4 * g + h:4 * g + h + 1] + SLOPES_D[4 * g + h] * qpos
                m = jnp.maximum(jnp.max(sh, axis=1, keepdims=True), sink)
                p = jnp.exp(sh - m)
                ils.append(1.0 / (jnp.sum(p, axis=1, keepdims=True) + jnp.exp(sink - m)))
                ps.append(p.astype(BF16))
            pv = jnp.dot(jnp.concatenate(ps, axis=0), vv, preferred_element_type=F32)
            accs.append([pv[h * tq:(h + 1) * tq] for h in range(4)])
            inv_ls.append(ils)
        o_ref[rows, :] = _gqa_out(accs, inv_ls, tq).astype(o_ref.dtype)


def _swa_call(qkv, sinks_pad, *, batch, s_len):
    t = qkv.shape[0]
    tq = BLOCK
    nblk = min(SWA_BLOCKS, s_len // tq)
    ts = nblk * tq
    nst = s_len // ts
    cur = lambda b, i: b * nst + i
    prev = lambda b, i: jnp.maximum((b * nst + i) * nblk - 1, 0)
    return pl.pallas_call(
        functools.partial(_swa_kernel, tq=tq, nblk=nblk),
        grid=(batch, nst),
        in_specs=[pl.BlockSpec((ts, 512), lambda b, i: (cur(b, i), 0)),
                  pl.BlockSpec((tq, 256), lambda b, i: (prev(b, i), 2)),
                  pl.BlockSpec((ts, 256), lambda b, i: (cur(b, i), 2)),
                  pl.BlockSpec((tq, 128), lambda b, i: (prev(b, i), 6)),
                  pl.BlockSpec((ts, 128), lambda b, i: (cur(b, i), 6)),
                  pl.BlockSpec((1, LANES), lambda b, i: (0, 0))],
        out_specs=pl.BlockSpec((ts, 512), lambda b, i: (cur(b, i), 0)),
        out_shape=jax.ShapeDtypeStruct((t, 512), BF16),
        compiler_params=_cparams(("arbitrary", "arbitrary")),
        name="swa",
    )(qkv, qkv, qkv, qkv, qkv, sinks_pad)


def _mla_prep_kernel(c_ref, cos_ref, sin_ref, gq_ref, gkv_ref, wq_ref, wqr_ref, wk_ref, wv_ref,
                     q_ref, k_ref, v_ref):
    c = c_ref[...]
    cos = cos_ref[...]
    sin = sin_ref[...]
    cos8 = jnp.concatenate([cos] * C_HEADS, axis=1)
    sin8 = jnp.concatenate([sin] * C_HEADS, axis=1)
    cqn = _rms(c[:, :Q_LORA], gq_ref[...]).astype(BF16)
    q = (jnp.dot(cqn, wq_ref[...], preferred_element_type=F32) * cos8
         + jnp.dot(cqn, wqr_ref[...], preferred_element_type=F32) * sin8)
    q_ref[...] = q.astype(BF16)
    ckvn = _rms(c[:, Q_LORA:Q_LORA + KV_LORA], gkv_ref[...]).astype(BF16)
    kr = c[:, 384:512] * cos + c[:, 512:640] * sin
    k = jnp.dot(ckvn, wk_ref[...], preferred_element_type=F32) + jnp.concatenate([kr] * C_HEADS, axis=1)
    k_ref[...] = k.astype(BF16)
    v_ref[...] = jnp.dot(ckvn, wv_ref[...], preferred_element_type=F32).astype(BF16)


def _mla_prep_call(cm, cos_t, sin_t, gq, gkv, wq, wqr, wk, wv, *, s_len):
    t = cm.shape[0]
    tm = min(PROJ_TM, s_len)
    tpb = s_len // tm
    full = lambda a: pl.BlockSpec(a.shape, lambda i: (0, 0))
    return pl.pallas_call(
        _mla_prep_kernel,
        grid=(t // tm,),
        in_specs=[pl.BlockSpec((tm, MLA_W), lambda i: (i, 0)),
                  pl.BlockSpec((tm, LANES), lambda i: (i % tpb, 0)),
                  pl.BlockSpec((tm, LANES), lambda i: (i % tpb, 0)),
                  full(gq), full(gkv), full(wq), full(wqr), full(wk), full(wv)],
        out_specs=[pl.BlockSpec((tm, 1024), lambda i: (i, 0)),
                   pl.BlockSpec((tm, 1024), lambda i: (i, 0)),
                   pl.BlockSpec((tm, 512), lambda i: (i, 0))],
        out_shape=[jax.ShapeDtypeStruct((t, 1024), BF16),
                   jax.ShapeDtypeStruct((t, 1024), BF16),
                   jax.ShapeDtypeStruct((t, 512), BF16)],
        compiler_params=_cparams(("arbitrary",)),
        name="mla_prep",
    )(cm, cos_t, sin_t, gq, gkv, wq, wqr, wk, wv)


def _mla_attn_kernel(q_ref, k_ref, v_ref, o_ref, s_ref, m_ref, l_ref, acc_ref, *, tq, kc):
    qb = pl.program_id(2)
    q0 = qb * tq
    n_full = lax.shift_right_logical(q0, _log2(kc))
    rowpos = q0 + lax.broadcasted_iota(I32, (tq, kc), 0)
    lane = lax.broadcasted_iota(I32, (tq, kc), 1)
    c = ((QK_NOPE + QK_ROPE) ** -0.5) * LOG2E
    n_lt = kc // LANES
    m_ref[...] = jnp.full(m_ref.shape, NEG, F32)
    l_ref[...] = jnp.zeros(l_ref.shape, F32)
    acc_ref[...] = jnp.zeros(acc_ref.shape, F32)

    def max_step(j, masked):
        ks = pl.multiple_of(j * kc, kc)
        for hh in range(2):
            s = _nt_dot(q_ref[:, hh * LANES:(hh + 1) * LANES],
                        k_ref[pl.ds(ks, kc), hh * LANES:(hh + 1) * LANES]) * c
            if masked:
                s = jnp.where(ks + lane <= rowpos, s, NEG)
            s_ref[hh, j] = s
            mp = m_ref[hh]
            for t in range(n_lt):
                mp = jnp.maximum(mp, s[:, t * LANES:(t + 1) * LANES])
            m_ref[hh] = mp

    def max_body(j, carry):
        max_step(j, False)
        return carry

    lax.fori_loop(0, n_full, max_body, 0)
    max_step(n_full, True)
    for hh in range(2):
        m_ref[hh] = jnp.broadcast_to(jnp.max(m_ref[hh], axis=1, keepdims=True), (tq, LANES))

    def pv_body(j, carry):
        ks = pl.multiple_of(j * kc, kc)
        vch = v_ref[pl.ds(ks, kc), :]
        for hh in range(2):
            mb = m_ref[hh]
            lp = l_ref[hh]
            tiles = []
            for t in range(n_lt):
                p = jnp.exp2(s_ref[hh, j, :, t * LANES:(t + 1) * LANES] - mb)
                lp = lp + p
                tiles.append(p.astype(BF16))
            l_ref[hh] = lp
            acc_ref[hh] += jnp.dot(jnp.concatenate(tiles, axis=1), vch, preferred_element_type=F32)
        return carry

    lax.fori_loop(0, n_full + 1, pv_body, 0)
    outs = [acc_ref[hh] * (1.0 / jnp.sum(l_ref[hh], axis=1, keepdims=True)) for hh in range(2)]
    lane_o = lax.broadcasted_iota(I32, (tq, LANES), 1)
    o_ref[...] = jnp.where(lane_o < V_DIM, outs[0], outs[1]).astype(o_ref.dtype)


def _mla_attn_call(qm, km, vm, *, batch, s_len):
    t = qm.shape[0]
    tq = min(MLA_TQ, s_len)
    kc = min(MLA_KC, s_len)
    assert kc % tq == 0
    nqb = s_len // tq
    once = pl.Buffered(1)
    return pl.pallas_call(
        functools.partial(_mla_attn_kernel, tq=tq, kc=kc),
        grid=(batch, C_HEADS // 2, nqb),
        in_specs=[pl.BlockSpec((tq, 256), lambda b, h, i: (b * nqb + i, h)),
                  pl.BlockSpec((s_len, 256), lambda b, h, i: (b, h), pipeline_mode=once),
                  pl.BlockSpec((s_len, 128), lambda b, h, i: (b, h), pipeline_mode=once)],
        out_specs=pl.BlockSpec((tq, 128), lambda b, h, i: (b * nqb + i, h)),
        out_shape=jax.ShapeDtypeStruct((t, 512), BF16),
        scratch_shapes=[pltpu.VMEM((2, s_len // kc, tq, kc), F32)] + [pltpu.VMEM((2, tq, LANES), F32)] * 3,
        compiler_params=_cparams(("arbitrary", "arbitrary", "arbitrary")),
        name="mla_attn",
    )(qm, km, vm)


def _lru_kernel(xr_ref, xg_ref, prev_ref, cw_ref, cb_ref, wa_ref, ba_ref, wx_ref, bx_ref, lam_ref,
                o_ref, xe_ref, h_ref, *, ln):
    i = pl.program_id(1)
    xe_ref[0:8, :] = jnp.where(i > 0, prev_ref[...], 0.0)
    xe_ref[8:8 + ln, :] = xr_ref[...]
    cw = cw_ref[...]
    xc = cb_ref[...] + cw[0:1] * xe_ref[pl.ds(5, ln), :]
    for j in range(1, CONV_WIDTH):
        xc = xc + cw[j:j + 1] * xe_ref[pl.ds(5 + j, ln), :]
    xcb = xc.astype(BF16)
    r = jax.nn.sigmoid(jnp.dot(xcb, wa_ref[...], preferred_element_type=F32) + ba_ref[...])
    gi = jax.nn.sigmoid(jnp.dot(xcb, wx_ref[...], preferred_element_type=F32) + bx_ref[...])
    z = -lam_ref[...]
    softplus = jnp.maximum(z, 0.0) + jnp.log1p(jnp.exp(-jnp.abs(z)))
    log_a = -LRU_C * r * softplus
    a = jnp.exp(log_a)
    b = jnp.sqrt(-_expm1(2.0 * log_a)) * (gi * xc)
    row = lax.broadcasted_iota(I32, (ln, LRU_WIDTH), 0)
    d = 1
    while d < ln:
        keep = row >= d
        b = jnp.where(keep, a * pltpu.roll(b, d, axis=0) + b, b)
        a = jnp.where(keep, a * pltpu.roll(a, d, axis=0), a)
        d *= 2
    h_prev = jnp.where(i > 0, h_ref[0:1, :], 0.0)
    h = a * h_prev + b
    h_ref[0:1, :] = h[ln - 1:ln, :]
    xg = xg_ref[...]
    gelu = 0.5 * xg * (1.0 + jnp.tanh(0.7978845608028654 * (xg + 0.044715 * (xg * xg * xg))))
    o_ref[...] = (h * gelu).astype(o_ref.dtype)


def _lru_call(rm, cw, cb, wa, ba, wx, bx, lam, *, batch, s_len):
    t = rm.shape[0]
    ln = min(LRU_L, s_len)
    nt = s_len // ln
    w = LRU_WIDTH
    vec = lambda a: pl.BlockSpec(a.shape, lambda b, i: (0, 0))
    return pl.pallas_call(
        functools.partial(_lru_kernel, ln=ln),
        grid=(batch, nt),
        in_specs=[pl.BlockSpec((ln, w), lambda b, i: (b * nt + i, 0)),
                  pl.BlockSpec((ln, w), lambda b, i: (b * nt + i, 1)),
                  pl.BlockSpec((8, w), lambda b, i: (jnp.maximum((b * nt + i) * (ln // 8) - 1, 0), 0)),
                  vec(cw), vec(cb), vec(wa), vec(ba), vec(wx), vec(bx), vec(lam)],
        out_specs=pl.BlockSpec((ln, w), lambda b, i: (b * nt + i, 0)),
        out_shape=jax.ShapeDtypeStruct((t, w), BF16),
        scratch_shapes=[pltpu.VMEM((ln + 8, w), F32), pltpu.VMEM((8, w), F32)],
        compiler_params=_cparams(("arbitrary", "arbitrary")),
        name="rglru",
    )(rm, rm, rm, cw, cb, wa, ba, wx, bx, lam)


def _merge_kernel(x_ref, mod_ref, g_ref, ya_ref, yb_ref, yc_ref, yd_ref, wg_ref, wb_ref, wo_ref, o_ref):
    m = mod_ref[0]
    x = x_ref[...]
    h = (_rms(x, g_ref[2:3]) * (1.0 + m[4:5]) + m[3:4]).astype(BF16)
    merged = None
    for n, y_ref in enumerate((ya_ref, yb_ref, yc_ref, yd_ref)):
        gate = jax.nn.sigmoid(jnp.dot(h, wg_ref[:, n * D_MODEL:(n + 1) * D_MODEL], preferred_element_type=F32))
        term = gate * jnp.dot(y_ref[...], wb_ref[n], preferred_element_type=F32)
        merged = term if merged is None else merged + term
    y = jnp.dot(merged.astype(BF16), wo_ref[...], preferred_element_type=F32)
    o_ref[...] = x + m[5:6] * _rms(y, g_ref[3:4])


def _merge_call(x2, mod_l, g_l, ys, w_gate, w_branch, w_out, *, s_len):
    t, d = x2.shape
    tm = min(MERGE_TM, s_len)
    tpb = s_len // tm
    return pl.pallas_call(
        _merge_kernel,
        grid=(t // tm,),
        in_specs=[pl.BlockSpec((tm, d), lambda i: (i, 0)),
                  pl.BlockSpec((1, 9, d), lambda i: (i // tpb, 0, 0)),
                  pl.BlockSpec((6, d), lambda i: (0, 0))]
                 + [pl.BlockSpec((tm, BRANCH_WIDTH), lambda i: (i, 0))] * N_BRANCH
                 + [pl.BlockSpec(w_gate.shape, lambda i: (0, 0)),
                    pl.BlockSpec(w_branch.shape, lambda i: (0, 0, 0)),
                    pl.BlockSpec(w_out.shape, lambda i: (0, 0))],
        out_specs=pl.BlockSpec((tm, d), lambda i: (i, 0)),
        out_shape=jax.ShapeDtypeStruct((t, d), F32),
        compiler_params=_cparams(("arbitrary",)),
        name="merge",
    )(x2, mod_l, g_l, *ys, w_gate, w_branch, w_out)


def _mixer_weights(w_in_l):
    o = IN_OFFSETS
    col = lambda n: w_in_l[:, o[n]:o[n + 1]]
    z = lambda n: jnp.zeros((D_MODEL, n), w_in_l.dtype)

    def pad_heads(k):
        return jnp.concatenate([k[:, :64], z(64), k[:, 64:], z(64)], axis=1)

    a_q, a_k, a_v, i_q, i_k, i_w, r_x, r_g, c_q, c_kv, d_q, d_k, d_v = (col(n) for n in range(13))
    k_rope = c_kv[:, KV_LORA:]
    half = QK_ROPE // 2
    k_rope_rot = jnp.concatenate([-k_rope[:, half:], k_rope[:, :half]], axis=1)
    groups = [a_q, pad_heads(a_k), a_v,
              i_q, i_k, i_k,
              i_w, z(LANES - IDX_HEADS),
              r_x, r_g,
              c_q, c_kv[:, :KV_LORA], z(64), k_rope, z(32), z(64), k_rope_rot, z(32),
              d_q, pad_heads(d_k), d_v]
    w_all = jnp.concatenate(groups, axis=1).astype(BF16)
    w_gate = col(13).astype(BF16)
    return w_all, w_gate


def _kpos_table(s_len):
    pos = np.arange(s_len)
    tile = np.zeros((s_len, LANES), np.float32)
    for i in range(3):
        tile[:, HEAD_DIM + 2 * i] = pos // POS_SPLIT
        tile[:, HEAD_DIM + 2 * i + 1] = pos % POS_SPLIT
    return jnp.asarray(np.concatenate([tile, tile], axis=1))


def _mla_weights(w_uq, w_ukv):
    dq = QK_NOPE + QK_ROPE
    half = QK_ROPE // 2
    zq = lambda n: jnp.zeros((Q_LORA, n), w_uq.dtype)
    wq, wqr, wk, wv = [], [], [], []
    for h in range(C_HEADS):
        nope = w_uq[:, h * dq:h * dq + QK_NOPE]
        r1 = w_uq[:, h * dq + QK_NOPE:h * dq + QK_NOPE + half]
        r2 = w_uq[:, h * dq + QK_NOPE + half:(h + 1) * dq]
        wq += [nope, r1, r2, zq(32)]
        wqr += [zq(64), -r2, r1, zq(32)]
        wk += [w_ukv[:, h * 128:h * 128 + QK_NOPE], jnp.zeros((KV_LORA, 64), w_ukv.dtype)]
        wv += [w_ukv[:, h * 128 + QK_NOPE:(h + 1) * 128]]
    cat = lambda xs: jnp.concatenate(xs, axis=1).astype(BF16)
    return cat(wq), cat(wqr), cat(wk), cat(wv)


def _rope_tables(s_len):
    half = QK_ROPE // 2
    inv = ROPE_THETA ** (-jnp.arange(half, dtype=F32) / half)
    ang = jnp.arange(s_len, dtype=F32)[:, None] * inv[None, :]
    cos, sin = jnp.cos(ang), jnp.sin(ang)
    ones = lambda n: jnp.ones((s_len, n), F32)
    zeros = lambda n: jnp.zeros((s_len, n), F32)
    cos_t = jnp.concatenate([ones(64), cos, cos, ones(32)], axis=1)
    sin_t = jnp.concatenate([zeros(64), sin, sin, zeros(32)], axis=1)
    return cos_t, sin_t


def _block_diag(w):
    n, bw, _ = w.shape
    out = jnp.zeros((n * bw, n * bw), w.dtype)
    for k in range(n):
        out = lax.dynamic_update_slice(out, w[k], (k * bw, k * bw))
    return out.astype(BF16)


def kernel(x, c, w_ada, b_ada, norm_g, ffn1_w_in, ffn1_w_out, w_in, conv_w, conv_b, lru_wa, lru_ba, lru_wx,
           lru_bx, lru_lambda, mla_g_q, mla_g_kv, mla_w_uq, mla_w_ukv, swa_sinks, w_branch, w_out, ffn2_w_in,
           ffn2_w_out):
    batch, s_len, d = x.shape
    depth = w_ada.shape[0]
    t = batch * s_len
    assert s_len // POS_SPLIT <= 256
    x2 = x.reshape(t, d)
    c_pad = jnp.zeros((8, d), F32).at[:batch].set(c)
    mod_all = _ada_call(c_pad, w_ada, b_ada)
    cos_t, sin_t = _rope_tables(s_len)
    kpos = _kpos_table(s_len)
    row = lambda v: v.reshape(1, -1)
    for l in range(depth):
        mod_l = mod_all[l, :batch].reshape(batch, 9, d)
        g_l = norm_g[l]
        x2 = _ffn_call(x2, mod_l, g_l, ffn1_w_in[l].astype(BF16), ffn1_w_out[l].astype(BF16),
                       sub=0, resid_w=0.5, s_len=s_len)
        w_all, w_gate = _mixer_weights(w_in[l])
        qkv_a, idx, iw, rm, cm, qkv_d = _proj_call(x2, mod_l, g_l, w_all, kpos, s_len=s_len)
        y_a = _dsa_call(qkv_a, idx, iw, batch=batch, s_len=s_len)
        y_b = _lru_call(rm, conv_w[l], row(conv_b[l]), _block_diag(lru_wa[l]), row(lru_ba[l]),
                        _block_diag(lru_wx[l]), row(lru_bx[l]), row(lru_lambda[l]), batch=batch, s_len=s_len)
        qm, km, vm = _mla_prep_call(cm, cos_t, sin_t, row(mla_g_q[l]), row(mla_g_kv[l]),
                                    *_mla_weights(mla_w_uq[l], mla_w_ukv[l]), s_len=s_len)
        y_c = _mla_attn_call(qm, km, vm, batch=batch, s_len=s_len)
        sinks_pad = jnp.zeros((1, LANES), F32).at[0, :D_HEADS].set(swa_sinks[l])
        y_d = _swa_call(qkv_d, sinks_pad, batch=batch, s_len=s_len)
        x2 = _merge_call(x2, mod_l, g_l, (y_a, y_b, y_c, y_d), w_gate, w_branch[l].astype(BF16),
                         w_out[l].astype(BF16), s_len=s_len)
        x2 = _ffn_call(x2, mod_l, g_l, ffn2_w_in[l].astype(BF16), ffn2_w_out[l].astype(BF16),
                       sub=2, resid_w=0.5, s_len=s_len)
    return x2.reshape(batch, s_len, d)
```

```python
import functools

import numpy as np
import jax
import jax.numpy as jnp
from jax import lax
from jax.experimental import pallas as pl
from jax.experimental.pallas import tpu as pltpu

F32 = jnp.float32
BF16 = jnp.bfloat16
I32 = jnp.int32

D_MODEL = 1024
HEAD_DIM = 64
BLOCK = 128
EPS = 1e-6
NEG = -1e30
A_HEADS = 8
A_KV_HEADS = 2
IDX_HEADS = 8
IDX_DIM = 64
TOPK_MAX = 256
LRU_WIDTH = 512
LRU_BLOCKS = 8
CONV_WIDTH = 4
LRU_C = 8.0
C_HEADS = 8
Q_LORA = 256
KV_LORA = 128
QK_NOPE = 64
QK_ROPE = 32
V_DIM = 64
ROPE_THETA = 10000.0
D_HEADS = 8
D_KV_HEADS = 2
WINDOW = 128
N_BRANCH = 4
BRANCH_WIDTH = 512
N_ALIBI = A_HEADS + D_HEADS
D_FF = 2816
IN_SPLITS = (A_HEADS * HEAD_DIM, A_KV_HEADS * HEAD_DIM, A_KV_HEADS * HEAD_DIM,
             IDX_HEADS * IDX_DIM, IDX_DIM, IDX_HEADS,
             LRU_WIDTH, LRU_WIDTH,
             Q_LORA, KV_LORA + QK_ROPE,
             D_HEADS * HEAD_DIM, D_KV_HEADS * HEAD_DIM, D_KV_HEADS * HEAD_DIM,
             N_BRANCH * D_MODEL)
IN_OFFSETS = tuple(int(v) for v in np.concatenate([[0], np.cumsum(IN_SPLITS)]))

LANES = 128
INT_MIN = -2 ** 31
INT_MAX = 2 ** 31 - 1
VMEM_LIMIT = 56 * 1024 * 1024
DSA_VMEM_LIMIT = 62 * 1024 * 1024
LOG2E = 1.4426950408889634
POS_SPLIT = 64
SEARCH_GROUP = 4
CNT_ROWS = 64

QKV_W = 512 + 256 + 128
IDX_W = 512 + 128
MLA_W = 640

FFN_TM = 1024
FFN_TF = 256
PROJ_TM = 512
DSA_TQ = 128
DSA_KC = 512
MLA_TQ = 512
MLA_KC = 512
SWA_BLOCKS = 4
LRU_L = 256
MERGE_TM = 512


def _alibi(i):
    return float(2.0 ** (-8.0 * i / N_ALIBI))


SLOPES_D = tuple(_alibi(i) for i in range(1, D_HEADS + 1))
SLOPES_A = tuple(_alibi(i) for i in range(D_HEADS + 1, N_ALIBI + 1))


def _bf16_parts(x):
    parts = []
    rem = np.float32(x)
    for _ in range(3):
        p = np.float32(np.asarray(rem, np.float32).astype(jnp.bfloat16).astype(np.float32))
        parts.append(float(p))
        rem = np.float32(rem - p)
    return parts


def _cparams(sem):
    return pltpu.CompilerParams(dimension_semantics=sem, vmem_limit_bytes=VMEM_LIMIT)


def _rms(x, g):
    return x * lax.rsqrt(jnp.mean(x * x, axis=-1, keepdims=True) + EPS) * g


def _nt_dot(a, b):
    return lax.dot_general(a, b, (((1,), (1,)), ((), ())), preferred_element_type=F32)


def _expm1(y):
    u = jnp.exp(y)
    safe = (u != 1.0) & (y > -1.0)
    ratio = y / jnp.log(jnp.where(safe, u, 2.0))
    return jnp.where(u == 1.0, y, jnp.where(safe, (u - 1.0) * ratio, u - 1.0))


def _loop_pairs(n, body):
    def pair(i, carry):
        body(2 * i, carry)
        body(2 * i + 1, carry)
        return carry

    lax.fori_loop(0, lax.shift_right_logical(n, 1), pair, 0)

    @pl.when((n & 1) == 1)
    def _():
        body(n - 1, 0)


def _log2(n):
    l = int(n).bit_length() - 1
    assert (1 << l) == n
    return l


def _ada_kernel(c_ref, w_ref, b_ref, o_ref):
    c = c_ref[...]
    sc = (c * jax.nn.sigmoid(c)).astype(BF16)
    o_ref[0] = jnp.dot(sc, w_ref[0].astype(BF16), preferred_element_type=F32) + b_ref[0]


def _ada_call(c_pad, w_ada, b_ada):
    depth, d, n = w_ada.shape
    tn = 1152
    return pl.pallas_call(
        _ada_kernel,
        grid=(depth, n // tn),
        in_specs=[pl.BlockSpec((c_pad.shape[0], d), lambda l, j: (0, 0)),
                  pl.BlockSpec((1, d, tn), lambda l, j: (l, 0, j)),
                  pl.BlockSpec((1, 1, tn), lambda l, j: (l, 0, j))],
        out_specs=pl.BlockSpec((1, c_pad.shape[0], tn), lambda l, j: (l, 0, j)),
        out_shape=jax.ShapeDtypeStruct((depth, c_pad.shape[0], n), F32),
        compiler_params=_cparams(("arbitrary", "arbitrary")),
        name="adaln",
    )(c_pad, w_ada, b_ada.reshape(depth, 1, n))


def _ffn_kernel(x_ref, mod_ref, g_ref, wg_ref, wu_ref, wo_ref, o_ref, h_ref, acc_ref, *, sub, resid_w):
    k = pl.program_id(1)
    m = mod_ref[0]

    @pl.when(k == 0)
    def _():
        x = x_ref[...]
        xn = _rms(x, g_ref[2 * sub:2 * sub + 1])
        h_ref[...] = (xn * (1.0 + m[3 * sub + 1:3 * sub + 2]) + m[3 * sub:3 * sub + 1]).astype(BF16)
        acc_ref[...] = jnp.zeros_like(acc_ref)

    h = h_ref[...]
    gate = jnp.dot(h, wg_ref[...], preferred_element_type=F32)
    up = jnp.dot(h, wu_ref[...], preferred_element_type=F32)
    act = (gate * jax.nn.sigmoid(gate) * up).astype(BF16)
    acc_ref[...] += jnp.dot(act, wo_ref[...], preferred_element_type=F32)

    @pl.when(k == pl.num_programs(1) - 1)
    def _():
        yn = _rms(acc_ref[...], g_ref[2 * sub + 1:2 * sub + 2])
        o_ref[...] = x_ref[...] + resid_w * m[3 * sub + 2:3 * sub + 3] * yn


def _ffn_call(x2, mod_l, g_l, w_in_bf, w_out_bf, *, sub, resid_w, s_len):
    t, d = x2.shape
    f = w_out_bf.shape[0]
    tm = min(FFN_TM, s_len)
    tf = FFN_TF
    nf = f // tf
    tpb = s_len // tm
    return pl.pallas_call(
        functools.partial(_ffn_kernel, sub=sub, resid_w=resid_w),
        grid=(t // tm, nf),
        in_specs=[pl.BlockSpec((tm, d), lambda i, k: (i, 0)),
                  pl.BlockSpec((1, 9, d), lambda i, k: (i // tpb, 0, 0)),
                  pl.BlockSpec((6, d), lambda i, k: (0, 0)),
                  pl.BlockSpec((d, tf), lambda i, k: (0, k)),
                  pl.BlockSpec((d, tf), lambda i, k: (0, k + nf)),
                  pl.BlockSpec((tf, d), lambda i, k: (k, 0))],
        out_specs=pl.BlockSpec((tm, d), lambda i, k: (i, 0)),
        out_shape=jax.ShapeDtypeStruct((t, d), F32),
        scratch_shapes=[pltpu.VMEM((tm, d), BF16), pltpu.VMEM((tm, d), F32)],
        compiler_params=_cparams(("arbitrary", "arbitrary")),
        name="ffn",
    )(x2, mod_l, g_l, w_in_bf, w_in_bf, w_out_bf)


def _proj_kernel(x_ref, mod_ref, g_ref, w_ref, kpos_ref, oa_ref, oi_ref, ow_ref, or_ref, oc_ref, od_ref):
    m = mod_ref[0]
    xn = _rms(x_ref[...], g_ref[2:3])
    h = (xn * (1.0 + m[4:5]) + m[3:4]).astype(BF16)
    off = 0
    for ref in (oa_ref, oi_ref, ow_ref, or_ref, oc_ref, od_ref):
        w = ref.shape[1]
        z = jnp.dot(h, w_ref[:, off:off + w], preferred_element_type=F32)
        if ref is oa_ref or ref is od_ref:
            z = jnp.concatenate([z[:, :512] * (HEAD_DIM ** -0.5), z[:, 512:768] + kpos_ref[...], z[:, 768:]],
                                axis=1)
        ref[...] = z.astype(ref.dtype)
        off += w


def _proj_call(x2, mod_l, g_l, w_all, kpos, *, s_len):
    t, d = x2.shape
    tm = min(PROJ_TM, s_len)
    tpb = s_len // tm
    widths = (QKV_W, IDX_W, LANES, 2 * LRU_WIDTH, MLA_W, QKV_W)
    dtypes = (BF16, BF16, F32, F32, F32, BF16)
    assert sum(widths) == w_all.shape[1]
    return pl.pallas_call(
        _proj_kernel,
        grid=(t // tm,),
        in_specs=[pl.BlockSpec((tm, d), lambda i: (i, 0)),
                  pl.BlockSpec((1, 9, d), lambda i: (i // tpb, 0, 0)),
                  pl.BlockSpec((6, d), lambda i: (0, 0)),
                  pl.BlockSpec(w_all.shape, lambda i: (0, 0)),
                  pl.BlockSpec((tm, 256), lambda i: (i % tpb, 0))],
        out_specs=[pl.BlockSpec((tm, w), lambda i: (i, 0)) for w in widths],
        out_shape=[jax.ShapeDtypeStruct((t, w), dt) for w, dt in zip(widths, dtypes)],
        compiler_params=_cparams(("arbitrary",)),
        name="mixer_proj",
    )(x2, mod_l, g_l, w_all, kpos)


def _head_rows(q, n_heads, tq):
    lane = lax.broadcasted_iota(I32, (tq, LANES), 1)
    rows = []
    for h in range(n_heads):
        tile = q[:, LANES * (h // 2):LANES * (h // 2 + 1)].astype(F32)
        keep = (lane >= HEAD_DIM) if (h % 2) else (lane < HEAD_DIM)
        rows.append(jnp.where(keep, tile, 0.0).astype(BF16))
    return jnp.concatenate(rows, axis=0)


def _alibi_q_tiles(q, slopes, tq):
    lane = lax.broadcasted_iota(I32, (tq, LANES), 1)
    rows = []
    for h, slope in enumerate(slopes):
        tile = q[:, LANES * (h // 2):LANES * (h // 2 + 1)].astype(F32)
        if h % 2:
            tile = pltpu.roll(tile, HEAD_DIM, axis=1)
        for i, part in enumerate(_bf16_parts(slope)):
            tile = jnp.where(lane == HEAD_DIM + 2 * i, POS_SPLIT * part, tile)
            tile = jnp.where(lane == HEAD_DIM + 2 * i + 1, part, tile)
        tile = jnp.where(lane >= HEAD_DIM + 6, 0.0, tile)
        rows.append(tile.astype(BF16))
    return jnp.concatenate(rows, axis=0)


def _gqa_out(accs, inv_ls, tq):
    lane = lax.broadcasted_iota(I32, (tq, LANES), 1)
    tiles = []
    for j in range(4):
        g = j // 2
        halves = []
        for h in (2 * j, 2 * j + 1):
            o = accs[g][h % 4] * inv_ls[g][h % 4]
            src_hi = (g == 1)
            dst_hi = (h % 2 == 1)
            if src_hi != dst_hi:
                o = pltpu.roll(o, HEAD_DIM, axis=1)
            halves.append(o)
        tiles.append(jnp.where(lane < HEAD_DIM, halves[0], halves[1]))
    return jnp.concatenate(tiles, axis=1)


def _dsa_kernel(q_ref, k_ref, v_ref, iq_ref, ik_ref, iw_ref, o_ref,
                key_ref, hi_ref, s_ref, tau_ref, iqs_ref, qs_ref, m_ref, l_ref, acc_ref, *, tq, kc, topk, pos_bits):
    qb = pl.program_id(1)
    q0 = qb * tq
    nk = lax.shift_right_logical(q0 + tq + kc - 1, _log2(kc))
    n_lt = kc // LANES
    n_rt = kc // CNT_ROWS
    qpos_l = q0 + lax.broadcasted_iota(I32, (kc, tq), 1)
    krow = lax.broadcasted_iota(I32, (kc, tq), 0)
    krow_t = lax.broadcasted_iota(I32, (CNT_ROWS, tq), 0)

    iqs_ref[...] = _head_rows(iq_ref[...], IDX_HEADS, tq)
    iw_t = iw_ref[...].T

    def score_body(j, carry):
        ks = pl.multiple_of(j * kc, kc)
        d = _nt_dot(ik_ref[pl.ds(ks, kc), :], iqs_ref[...])
        acc = jnp.zeros((kc, tq), F32)
        for h in range(IDX_HEADS):
            acc = acc + iw_t[h:h + 1, :] * jnp.maximum(d[:, h * tq:(h + 1) * tq], 0.0)
        bits = pltpu.bitcast(acc, I32)
        key = jnp.where(bits < 0, bits ^ INT_MAX, bits)
        key = jnp.where(ks + krow <= qpos_l, key, INT_MIN)
        key_ref[j] = key
        hi_ref[j] = lax.shift_right_arithmetic(key, 16).astype(jnp.int16)
        return carry

    _loop_pairs(nk, score_body)

    def count(pred):
        def body(j, cnt):
            for r in range(n_rt):
                kp = (j * kc + r * CNT_ROWS) + krow_t
                cnt = cnt + jnp.where(pred(key_ref[j, r * CNT_ROWS:(r + 1) * CNT_ROWS, :], kp), 1.0, 0.0)
            return cnt
        cnt = lax.fori_loop(0, nk, body, jnp.zeros((CNT_ROWS, tq), F32))
        return jnp.sum(cnt, axis=0, keepdims=True)

    kf = float(topk)
    n_valid = (q0 + 1 + lax.broadcasted_iota(I32, (1, tq), 1)).astype(F32)
    settled0 = n_valid <= kf

    def count_hi(cand):
        cb = jnp.broadcast_to(lax.shift_right_arithmetic(cand, 16), (CNT_ROWS, tq)).astype(jnp.int16)

        def body(j, cnt):
            for r in range(n_rt):
                hj = hi_ref[j, r * CNT_ROWS:(r + 1) * CNT_ROWS, :]
                cnt = cnt + jnp.where(hj >= cb, jnp.int16(1), jnp.int16(0))
            return cnt
        cnt = lax.fori_loop(0, nk, body, jnp.zeros((CNT_ROWS, tq), jnp.int16))
        return jnp.sum(cnt.astype(F32), axis=0, keepdims=True)

    def count_full(cand):
        cb = jnp.broadcast_to(cand, (CNT_ROWS, tq))
        return count(lambda kv, kp: kv >= cb)

    def bit_step(i, tau, cnt, count_fn):
        cand = tau + lax.shift_left(jnp.int32(1), 31 - i)
        c = count_fn(cand)
        ok = c >= kf
        return jnp.where(ok, cand, tau), jnp.where(ok, c, cnt)

    tau, n_ge = lax.fori_loop(0, 16, lambda i, st: bit_step(i, st[0], st[1], count_hi),
                              (jnp.full((1, tq), INT_MIN, I32), jnp.broadcast_to((nk * kc).astype(F32), (1, tq))))

    def all_settled(cnt):
        return jnp.min(jnp.where(settled0 | (cnt == kf), 1, 0))

    def low_cond(state):
        i, _, _, done = state
        return (i < 32) & (done == 0)

    def low_body(state):
        i, tau, cnt, _ = state
        for b in range(SEARCH_GROUP):
            tau, cnt = bit_step(i + b, tau, cnt, count_full)
        return i + SEARCH_GROUP, tau, cnt, all_settled(cnt)

    _, tau, n_ge, _ = lax.while_loop(low_cond, low_body, (jnp.int32(16), tau, n_ge, all_settled(n_ge)))

    tie_f = jnp.where((n_ge > kf) & (tau > INT_MIN), 1.0, 0.0)
    any_tie = jnp.max(tie_f) > 0.0
    tau_c = jnp.maximum(tau, INT_MIN + 1)
    tau_ref[...] = jnp.broadcast_to(tau_c, tau_ref.shape)

    @pl.when(any_tie)
    def _():
        tb = jnp.broadcast_to(tau_c, (CNT_ROWS, tq))
        need = kf - count(lambda kv, kp: kv > tb)

        def pos_body(i, p):
            cand = p | lax.shift_left(jnp.int32(1), pos_bits - 1 - i)
            cb = jnp.broadcast_to(cand, (CNT_ROWS, tq))
            below = count(lambda kv, kp: (kv == tb) & (kp < cb))
            return jnp.where(below < need, cand, p)

        p = lax.fori_loop(0, pos_bits, pos_body, jnp.zeros((1, tq), I32))
        p = jnp.where(tie_f > 0.0, p, INT_MAX)

        def rewrite_body(j, carry):
            kj = key_ref[j]
            kp = j * kc + krow
            sel = (kj > tau_c) | ((kj == tau_c) & (kp <= p))
            key_ref[j] = jnp.where(sel, 1, -1)
            return carry

        lax.fori_loop(0, nk, rewrite_body, 0)
        tau_ref[...] = jnp.zeros(tau_ref.shape, I32)

    q = q_ref[...]
    for g in range(A_KV_HEADS):
        qs_ref[g] = _alibi_q_tiles(q[:, 4 * HEAD_DIM * g:4 * HEAD_DIM * (g + 1)], SLOPES_A[4 * g:4 * g + 4], tq)
    m_ref[...] = jnp.full(m_ref.shape, NEG, F32)
    l_ref[...] = jnp.zeros(l_ref.shape, F32)
    acc_ref[...] = jnp.zeros(acc_ref.shape, F32)

    def max_body(j, carry):
        ks = pl.multiple_of(j * kc, kc)
        sel = jnp.where(key_ref[j] >= tau_ref[0:1, :], 1.0, 0.0).T > 0.5
        for g in range(A_KV_HEADS):
            s = _nt_dot(qs_ref[g], k_ref[pl.ds(ks, kc), g * LANES:(g + 1) * LANES])
            for h in range(4):
                sh = jnp.where(sel, s[h * tq:(h + 1) * tq] * LOG2E, NEG)
                s_ref[g, h, j] = sh
                mp = m_ref[g, h]
                for c in range(n_lt):
                    mp = jnp.maximum(mp, sh[:, c * LANES:(c + 1) * LANES])
                m_ref[g, h] = mp
        return carry

    _loop_pairs(nk, max_body)
    for g in range(A_KV_HEADS):
        for h in range(4):
            m_ref[g, h] = jnp.broadcast_to(jnp.max(m_ref[g, h], axis=1, keepdims=True), (tq, LANES))

    def pv_body(j, carry):
        ks = pl.multiple_of(j * kc, kc)
        vch = v_ref[pl.ds(ks, kc), :]
        for g in range(A_KV_HEADS):
            ps = []
            for h in range(4):
                mb = m_ref[g, h]
                lp = l_ref[g, h]
                tiles = []
                for c in range(n_lt):
                    p = jnp.exp2(s_ref[g, h, j, :, c * LANES:(c + 1) * LANES] - mb)
                    lp = lp + p
                    tiles.append(p.astype(BF16))
                l_ref[g, h] = lp
                ps.append(jnp.concatenate(tiles, axis=1))
            pv = jnp.dot(jnp.concatenate(ps, axis=0), vch, preferred_element_type=F32)
            for h in range(4):
                acc_ref[g, h] += pv[h * tq:(h + 1) * tq]
        return carry

    _loop_pairs(nk, pv_body)
    accs = [[acc_ref[g, h] for h in range(4)] for g in range(A_KV_HEADS)]
    inv_ls = [[1.0 / jnp.sum(l_ref[g, h], axis=1, keepdims=True) for h in range(4)] for g in range(A_KV_HEADS)]
    o_ref[...] = _gqa_out(accs, inv_ls, tq).astype(o_ref.dtype)


def _dsa_call(qkv, idx, iw, *, batch, s_len):
    t = qkv.shape[0]
    tq = min(DSA_TQ, s_len)
    kc = min(DSA_KC, s_len)
    topk = min(TOPK_MAX, s_len // 4)
    assert kc >= topk and kc % tq == 0
    nqb = s_len // tq
    once = pl.Buffered(1)
    return pl.pallas_call(
        functools.partial(_dsa_kernel, tq=tq, kc=kc, topk=topk, pos_bits=_log2(s_len)),
        grid=(batch, nqb),
        in_specs=[pl.BlockSpec((tq, 512), lambda b, i: (b * nqb + i, 0)),
                  pl.BlockSpec((s_len, 256), lambda b, i: (b, 2), pipeline_mode=once),
                  pl.BlockSpec((s_len, 128), lambda b, i: (b, 6), pipeline_mode=once),
                  pl.BlockSpec((tq, 512), lambda b, i: (b * nqb + i, 0)),
                  pl.BlockSpec((s_len, 128), lambda b, i: (b, 4), pipeline_mode=once),
                  pl.BlockSpec((tq, LANES), lambda b, i: (b * nqb + i, 0))],
        out_specs=pl.BlockSpec((tq, 512), lambda b, i: (b * nqb + i, 0)),
        out_shape=jax.ShapeDtypeStruct((t, 512), BF16),
        scratch_shapes=[pltpu.VMEM((s_len // kc, kc, tq), I32),
                        pltpu.VMEM((s_len // kc, kc, tq), jnp.int16),
                        pltpu.VMEM((A_KV_HEADS, 4, s_len // kc, tq, kc), F32),
                        pltpu.VMEM((8, tq), I32),
                        pltpu.VMEM((IDX_HEADS * tq, LANES), BF16),
                        pltpu.VMEM((A_KV_HEADS, 4 * tq, LANES), BF16),
                        pltpu.VMEM((A_KV_HEADS, 4, tq, LANES), F32),
                        pltpu.VMEM((A_KV_HEADS, 4, tq, LANES), F32),
                        pltpu.VMEM((A_KV_HEADS, 4, tq, LANES), F32)],
        compiler_params=pltpu.CompilerParams(dimension_semantics=("arbitrary", "arbitrary"),
                                             vmem_limit_bytes=DSA_VMEM_LIMIT),
        name="dsa",
    )(qkv, qkv, qkv, idx, idx, iw)


def _swa_kernel(q_ref, kp_ref, kc_ref, vp_ref, vc_ref, sink_ref, o_ref, *, tq, nblk):
    i = pl.program_id(1)
    row = lax.broadcasted_iota(I32, (tq, 2 * tq), 0)
    col = lax.broadcasted_iota(I32, (tq, 2 * tq), 1)
    dist = row + tq - col
    in_band = (dist >= 0) & (dist < WINDOW)
    sinks = sink_ref[...]
    for n in range(nblk):
        blk = i * nblk + n
        rows = slice(n * tq, (n + 1) * tq)
        q = q_ref[rows, :]
        k_prev = kp_ref[...] if n == 0 else kc_ref[(n - 1) * tq:n * tq, :]
        v_prev = vp_ref[...] if n == 0 else vc_ref[(n - 1) * tq:n * tq, :]
        kk = jnp.concatenate([k_prev, kc_ref[rows, :]], axis=0)
        vv = jnp.concatenate([v_prev, vc_ref[rows, :]], axis=0)
        valid = in_band & ((blk * tq - tq + col) >= 0)
        qpos = (blk * tq + lax.broadcasted_iota(I32, (tq, 1), 0)).astype(F32)
        accs, inv_ls = [], []
        for g in range(D_KV_HEADS):
            qs = _alibi_q_tiles(q[:, 4 * HEAD_DIM * g:4 * HEAD_DIM * (g + 1)], SLOPES_D[4 * g:4 * g + 4], tq)
            s = _nt_dot(qs, kk[:, g * LANES:(g + 1) * LANES])
            ps, ils = [], []
            for h in range(4):
                sh = jnp.where(valid, s[h * tq:(h + 1) * tq], NEG)
                sink = sinks[:, 4 * g + h:4 * g + h + 1] + SLOPES_D[4 * g + h] * qpos
                m = jnp.maximum(jnp.max(sh, axis=1, keepdims=True), sink)
                p = jnp.exp(sh - m)
                ils.append(1.0 / (jnp.sum(p, axis=1, keepdims=True) + jnp.exp(sink - m)))
                ps.append(p.astype(BF16))
            pv = jnp.dot(jnp.concatenate(ps, axis=0), vv, preferred_element_type=F32)
            accs.append([pv[h * tq:(h + 1) * tq] for h in range(4)])
            inv_ls.append(ils)
        o_ref[rows, :] = _gqa_out(accs, inv_ls, tq).astype(o_ref.dtype)


def _swa_call(qkv, sinks_pad, *, batch, s_len):
    t = qkv.shape[0]
    tq = BLOCK
    nblk = min(SWA_BLOCKS, s_len // tq)
    ts = nblk * tq
    nst = s_len // ts
    cur = lambda b, i: b * nst + i
    prev = lambda b, i: jnp.maximum((b * nst + i) * nblk - 1, 0)
    return pl.pallas_call(
        functools.partial(_swa_kernel, tq=tq, nblk=nblk),
        grid=(batch, nst),
        in_specs=[pl.BlockSpec((ts, 512), lambda b, i: (cur(b, i), 0)),
                  pl.BlockSpec((tq, 256), lambda b, i: (prev(b, i), 2)),
                  pl.BlockSpec((ts, 256), lambda b, i: (cur(b, i), 2)),
                  pl.BlockSpec((tq, 128), lambda b, i: (prev(b, i), 6)),
                  pl.BlockSpec((ts, 128), lambda b, i: (cur(b, i), 6)),
                  pl.BlockSpec((1, LANES), lambda b, i: (0, 0))],
        out_specs=pl.BlockSpec((ts, 512), lambda b, i: (cur(b, i), 0)),
        out_shape=jax.ShapeDtypeStruct((t, 512), BF16),
        compiler_params=_cparams(("arbitrary", "arbitrary")),
        name="swa",
    )(qkv, qkv, qkv, qkv, qkv, sinks_pad)


def _mla_prep_kernel(c_ref, cos_ref, sin_ref, gq_ref, gkv_ref, wq_ref, wqr_ref, wk_ref, wv_ref,
                     q_ref, k_ref, v_ref):
    c = c_ref[...]
    cos = cos_ref[...]
    sin = sin_ref[...]
    cos8 = jnp.concatenate([cos] * C_HEADS, axis=1)
    sin8 = jnp.concatenate([sin] * C_HEADS, axis=1)
    cqn = _rms(c[:, :Q_LORA], gq_ref[...]).astype(BF16)
    q = (jnp.dot(cqn, wq_ref[...], preferred_element_type=F32) * cos8
         + jnp.dot(cqn, wqr_ref[...], preferred_element_type=F32) * sin8)
    q_ref[...] = q.astype(BF16)
    ckvn = _rms(c[:, Q_LORA:Q_LORA + KV_LORA], gkv_ref[...]).astype(BF16)
    kr = c[:, 384:512] * cos + c[:, 512:640] * sin
    k = jnp.dot(ckvn, wk_ref[...], preferred_element_type=F32) + jnp.concatenate([kr] * C_HEADS, axis=1)
    k_ref[...] = k.astype(BF16)
    v_ref[...] = jnp.dot(ckvn, wv_ref[...], preferred_element_type=F32).astype(BF16)


def _mla_prep_call(cm, cos_t, sin_t, gq, gkv, wq, wqr, wk, wv, *, s_len):
    t = cm.shape[0]
    tm = min(PROJ_TM, s_len)
    tpb = s_len // tm
    full = lambda a: pl.BlockSpec(a.shape, lambda i: (0, 0))
    return pl.pallas_call(
        _mla_prep_kernel,
        grid=(t // tm,),
        in_specs=[pl.BlockSpec((tm, MLA_W), lambda i: (i, 0)),
                  pl.BlockSpec((tm, LANES), lambda i: (i % tpb, 0)),
                  pl.BlockSpec((tm, LANES), lambda i: (i % tpb, 0)),
                  full(gq), full(gkv), full(wq), full(wqr), full(wk), full(wv)],
        out_specs=[pl.BlockSpec((tm, 1024), lambda i: (i, 0)),
                   pl.BlockSpec((tm, 1024), lambda i: (i, 0)),
                   pl.BlockSpec((tm, 512), lambda i: (i, 0))],
        out_shape=[jax.ShapeDtypeStruct((t, 1024), BF16),
                   jax.ShapeDtypeStruct((t, 1024), BF16),
                   jax.ShapeDtypeStruct((t, 512), BF16)],
        compiler_params=_cparams(("arbitrary",)),
        name="mla_prep",
    )(cm, cos_t, sin_t, gq, gkv, wq, wqr, wk, wv)


def _mla_attn_kernel(q_ref, k_ref, v_ref, o_ref, s_ref, m_ref, l_ref, acc_ref, *, tq, kc):
    qb = pl.program_id(2)
    q0 = qb * tq
    n_full = lax.shift_right_logical(q0, _log2(kc))
    rowpos = q0 + lax.broadcasted_iota(I32, (tq, kc), 0)
    lane = lax.broadcasted_iota(I32, (tq, kc), 1)
    c = ((QK_NOPE + QK_ROPE) ** -0.5) * LOG2E
    n_lt = kc // LANES
    m_ref[...] = jnp.full(m_ref.shape, NEG, F32)
    l_ref[...] = jnp.zeros(l_ref.shape, F32)
    acc_ref[...] = jnp.zeros(acc_ref.shape, F32)

    def max_step(j, masked):
        ks = pl.multiple_of(j * kc, kc)
        for hh in range(2):
            s = _nt_dot(q_ref[:, hh * LANES:(hh + 1) * LANES],
                        k_ref[pl.ds(ks, kc), hh * LANES:(hh + 1) * LANES]) * c
            if masked:
                s = jnp.where(ks + lane <= rowpos, s, NEG)
            s_ref[hh, j] = s
            mp = m_ref[hh]
            for t in range(n_lt):
                mp = jnp.maximum(mp, s[:, t * LANES:(t + 1) * LANES])
            m_ref[hh] = mp

    def max_body(j, carry):
        max_step(j, False)
        return carry

    _loop_pairs(n_full, max_body)
    max_step(n_full, True)
    for hh in range(2):
        m_ref[hh] = jnp.broadcast_to(jnp.max(m_ref[hh], axis=1, keepdims=True), (tq, LANES))

    def pv_body(j, carry):
        ks = pl.multiple_of(j * kc, kc)
        vch = v_ref[pl.ds(ks, kc), :]
        for hh in range(2):
            mb = m_ref[hh]
            lp = l_ref[hh]
            tiles = []
            for t in range(n_lt):
                p = jnp.exp2(s_ref[hh, j, :, t * LANES:(t + 1) * LANES] - mb)
                lp = lp + p
                tiles.append(p.astype(BF16))
            l_ref[hh] = lp
            acc_ref[hh] += jnp.dot(jnp.concatenate(tiles, axis=1), vch, preferred_element_type=F32)
        return carry

    _loop_pairs(n_full + 1, pv_body)
    outs = [acc_ref[hh] * (1.0 / jnp.sum(l_ref[hh], axis=1, keepdims=True)) for hh in range(2)]
    lane_o = lax.broadcasted_iota(I32, (tq, LANES), 1)
    o_ref[...] = jnp.where(lane_o < V_DIM, outs[0], outs[1]).astype(o_ref.dtype)


def _mla_attn_call(qm, km, vm, *, batch, s_len):
    t = qm.shape[0]
    tq = min(MLA_TQ, s_len)
    kc = min(MLA_KC, s_len)
    assert kc % tq == 0
    nqb = s_len // tq
    once = pl.Buffered(1)
    return pl.pallas_call(
        functools.partial(_mla_attn_kernel, tq=tq, kc=kc),
        grid=(batch, C_HEADS // 2, nqb),
        in_specs=[pl.BlockSpec((tq, 256), lambda b, h, i: (b * nqb + i, h)),
                  pl.BlockSpec((s_len, 256), lambda b, h, i: (b, h), pipeline_mode=once),
                  pl.BlockSpec((s_len, 128), lambda b, h, i: (b, h), pipeline_mode=once)],
        out_specs=pl.BlockSpec((tq, 128), lambda b, h, i: (b * nqb + i, h)),
        out_shape=jax.ShapeDtypeStruct((t, 512), BF16),
        scratch_shapes=[pltpu.VMEM((2, s_len // kc, tq, kc), F32)] + [pltpu.VMEM((2, tq, LANES), F32)] * 3,
        compiler_params=_cparams(("arbitrary", "arbitrary", "arbitrary")),
        name="mla_attn",
    )(qm, km, vm)


def _lru_kernel(xr_ref, xg_ref, prev_ref, cw_ref, cb_ref, wa_ref, ba_ref, wx_ref, bx_ref, lam_ref,
                o_ref, xe_ref, h_ref, *, ln):
    i = pl.program_id(1)
    xe_ref[0:8, :] = jnp.where(i > 0, prev_ref[...], 0.0)
    xe_ref[8:8 + ln, :] = xr_ref[...]
    cw = cw_ref[...]
    xc = cb_ref[...] + cw[0:1] * xe_ref[pl.ds(5, ln), :]
    for j in range(1, CONV_WIDTH):
        xc = xc + cw[j:j + 1] * xe_ref[pl.ds(5 + j, ln), :]
    xcb = xc.astype(BF16)
    r = jax.nn.sigmoid(jnp.dot(xcb, wa_ref[...], preferred_element_type=F32) + ba_ref[...])
    gi = jax.nn.sigmoid(jnp.dot(xcb, wx_ref[...], preferred_element_type=F32) + bx_ref[...])
    z = -lam_ref[...]
    softplus = jnp.maximum(z, 0.0) + jnp.log1p(jnp.exp(-jnp.abs(z)))
    log_a = -LRU_C * r * softplus
    a = jnp.exp(log_a)
    b = jnp.sqrt(-_expm1(2.0 * log_a)) * (gi * xc)
    row = lax.broadcasted_iota(I32, (ln, LRU_WIDTH), 0)
    d = 1
    while d < ln:
        keep = row >= d
        b = jnp.where(keep, a * pltpu.roll(b, d, axis=0) + b, b)
        a = jnp.where(keep, a * pltpu.roll(a, d, axis=0), a)
        d *= 2
    h_prev = jnp.where(i > 0, h_ref[0:1, :], 0.0)
    h = a * h_prev + b
    h_ref[0:1, :] = h[ln - 1:ln, :]
    xg = xg_ref[...]
    gelu = 0.5 * xg * (1.0 + jnp.tanh(0.7978845608028654 * (xg + 0.044715 * (xg * xg * xg))))
    o_ref[...] = (h * gelu).astype(o_ref.dtype)


def _lru_call(rm, cw, cb, wa, ba, wx, bx, lam, *, batch, s_len):
    t = rm.shape[0]
    ln = min(LRU_L, s_len)
    nt = s_len // ln
    w = LRU_WIDTH
    vec = lambda a: pl.BlockSpec(a.shape, lambda b, i: (0, 0))
    return pl.pallas_call(
        functools.partial(_lru_kernel, ln=ln),
        grid=(batch, nt),
        in_specs=[pl.BlockSpec((ln, w), lambda b, i: (b * nt + i, 0)),
                  pl.BlockSpec((ln, w), lambda b, i: (b * nt + i, 1)),
                  pl.BlockSpec((8, w), lambda b, i: (jnp.maximum((b * nt + i) * (ln // 8) - 1, 0), 0)),
                  vec(cw), vec(cb), vec(wa), vec(ba), vec(wx), vec(bx), vec(lam)],
        out_specs=pl.BlockSpec((ln, w), lambda b, i: (b * nt + i, 0)),
        out_shape=jax.ShapeDtypeStruct((t, w), BF16),
        scratch_shapes=[pltpu.VMEM((ln + 8, w), F32), pltpu.VMEM((8, w), F32)],
        compiler_params=_cparams(("arbitrary", "arbitrary")),
        name="rglru",
    )(rm, rm, rm, cw, cb, wa, ba, wx, bx, lam)


def _merge_kernel(x_ref, mod_ref, g_ref, ya_ref, yb_ref, yc_ref, yd_ref, wg_ref, wb_ref, wo_ref, o_ref):
    m = mod_ref[0]
    x = x_ref[...]
    h = (_rms(x, g_ref[2:3]) * (1.0 + m[4:5]) + m[3:4]).astype(BF16)
    merged = None
    for n, y_ref in enumerate((ya_ref, yb_ref, yc_ref, yd_ref)):
        gate = jax.nn.sigmoid(jnp.dot(h, wg_ref[:, n * D_MODEL:(n + 1) * D_MODEL], preferred_element_type=F32))
        term = gate * jnp.dot(y_ref[...], wb_ref[n], preferred_element_type=F32)
        merged = term if merged is None else merged + term
    y = jnp.dot(merged.astype(BF16), wo_ref[...], preferred_element_type=F32)
    o_ref[...] = x + m[5:6] * _rms(y, g_ref[3:4])


def _merge_call(x2, mod_l, g_l, ys, w_gate, w_branch, w_out, *, s_len):
    t, d = x2.shape
    tm = min(MERGE_TM, s_len)
    tpb = s_len // tm
    return pl.pallas_call(
        _merge_kernel,
        grid=(t // tm,),
        in_specs=[pl.BlockSpec((tm, d), lambda i: (i, 0)),
                  pl.BlockSpec((1, 9, d), lambda i: (i // tpb, 0, 0)),
                  pl.BlockSpec((6, d), lambda i: (0, 0))]
                 + [pl.BlockSpec((tm, BRANCH_WIDTH), lambda i: (i, 0))] * N_BRANCH
                 + [pl.BlockSpec(w_gate.shape, lambda i: (0, 0)),
                    pl.BlockSpec(w_branch.shape, lambda i: (0, 0, 0)),
                    pl.BlockSpec(w_out.shape, lambda i: (0, 0))],
        out_specs=pl.BlockSpec((tm, d), lambda i: (i, 0)),
        out_shape=jax.ShapeDtypeStruct((t, d), F32),
        compiler_params=_cparams(("arbitrary",)),
        name="merge",
    )(x2, mod_l, g_l, *ys, w_gate, w_branch, w_out)


def _mixer_weights(w_in_l):
    o = IN_OFFSETS
    col = lambda n: w_in_l[:, o[n]:o[n + 1]]
    z = lambda n: jnp.zeros((D_MODEL, n), w_in_l.dtype)

    def pad_heads(k):
        return jnp.concatenate([k[:, :64], z(64), k[:, 64:], z(64)], axis=1)

    a_q, a_k, a_v, i_q, i_k, i_w, r_x, r_g, c_q, c_kv, d_q, d_k, d_v = (col(n) for n in range(13))
    k_rope = c_kv[:, KV_LORA:]
    half = QK_ROPE // 2
    k_rope_rot = jnp.concatenate([-k_rope[:, half:], k_rope[:, :half]], axis=1)
    groups = [a_q, pad_heads(a_k), a_v,
              i_q, i_k, i_k,
              i_w, z(LANES - IDX_HEADS),
              r_x, r_g,
              c_q, c_kv[:, :KV_LORA], z(64), k_rope, z(32), z(64), k_rope_rot, z(32),
              d_q, pad_heads(d_k), d_v]
    w_all = jnp.concatenate(groups, axis=1).astype(BF16)
    w_gate = col(13).astype(BF16)
    return w_all, w_gate


def _kpos_table(s_len):
    pos = np.arange(s_len)
    tile = np.zeros((s_len, LANES), np.float32)
    for i in range(3):
        tile[:, HEAD_DIM + 2 * i] = pos // POS_SPLIT
        tile[:, HEAD_DIM + 2 * i + 1] = pos % POS_SPLIT
    return jnp.asarray(np.concatenate([tile, tile], axis=1))


def _mla_weights(w_uq, w_ukv):
    dq = QK_NOPE + QK_ROPE
    half = QK_ROPE // 2
    zq = lambda n: jnp.zeros((Q_LORA, n), w_uq.dtype)
    wq, wqr, wk, wv = [], [], [], []
    for h in range(C_HEADS):
        nope = w_uq[:, h * dq:h * dq + QK_NOPE]
        r1 = w_uq[:, h * dq + QK_NOPE:h * dq + QK_NOPE + half]
        r2 = w_uq[:, h * dq + QK_NOPE + half:(h + 1) * dq]
        wq += [nope, r1, r2, zq(32)]
        wqr += [zq(64), -r2, r1, zq(32)]
        wk += [w_ukv[:, h * 128:h * 128 + QK_NOPE], jnp.zeros((KV_LORA, 64), w_ukv.dtype)]
        wv += [w_ukv[:, h * 128 + QK_NOPE:(h + 1) * 128]]
    cat = lambda xs: jnp.concatenate(xs, axis=1).astype(BF16)
    return cat(wq), cat(wqr), cat(wk), cat(wv)


def _rope_tables(s_len):
    half = QK_ROPE // 2
    inv = ROPE_THETA ** (-jnp.arange(half, dtype=F32) / half)
    ang = jnp.arange(s_len, dtype=F32)[:, None] * inv[None, :]
    cos, sin = jnp.cos(ang), jnp.sin(ang)
    ones = lambda n: jnp.ones((s_len, n), F32)
    zeros = lambda n: jnp.zeros((s_len, n), F32)
    cos_t = jnp.concatenate([ones(64), cos, cos, ones(32)], axis=1)
    sin_t = jnp.concatenate([zeros(64), sin, sin, zeros(32)], axis=1)
    return cos_t, sin_t


def _block_diag(w):
    n, bw, _ = w.shape
    out = jnp.zeros((n * bw, n * bw), w.dtype)
    for k in range(n):
        out = lax.dynamic_update_slice(out, w[k], (k * bw, k * bw))
    return out.astype(BF16)


def kernel(x, c, w_ada, b_ada, norm_g, ffn1_w_in, ffn1_w_out, w_in, conv_w, conv_b, lru_wa, lru_ba, lru_wx,
           lru_bx, lru_lambda, mla_g_q, mla_g_kv, mla_w_uq, mla_w_ukv, swa_sinks, w_branch, w_out, ffn2_w_in,
           ffn2_w_out):
    batch, s_len, d = x.shape
    depth = w_ada.shape[0]
    t = batch * s_len
    assert s_len // POS_SPLIT <= 256
    x2 = x.reshape(t, d)
    c_pad = jnp.zeros((8, d), F32).at[:batch].set(c)
    mod_all = _ada_call(c_pad, w_ada, b_ada)
    cos_t, sin_t = _rope_tables(s_len)
    kpos = _kpos_table(s_len)
    row = lambda v: v.reshape(1, -1)
    for l in range(depth):
        mod_l = mod_all[l, :batch].reshape(batch, 9, d)
        g_l = norm_g[l]
        x2 = _ffn_call(x2, mod_l, g_l, ffn1_w_in[l].astype(BF16), ffn1_w_out[l].astype(BF16),
                       sub=0, resid_w=0.5, s_len=s_len)
        w_all, w_gate = _mixer_weights(w_in[l])
        qkv_a, idx, iw, rm, cm, qkv_d = _proj_call(x2, mod_l, g_l, w_all, kpos, s_len=s_len)
        y_a = _dsa_call(qkv_a, idx, iw, batch=batch, s_len=s_len)
        y_b = _lru_call(rm, conv_w[l], row(conv_b[l]), _block_diag(lru_wa[l]), row(lru_ba[l]),
                        _block_diag(lru_wx[l]), row(lru_bx[l]), row(lru_lambda[l]), batch=batch, s_len=s_len)
        qm, km, vm = _mla_prep_call(cm, cos_t, sin_t, row(mla_g_q[l]), row(mla_g_kv[l]),
                                    *_mla_weights(mla_w_uq[l], mla_w_ukv[l]), s_len=s_len)
        y_c = _mla_attn_call(qm, km, vm, batch=batch, s_len=s_len)
        sinks_pad = jnp.zeros((1, LANES), F32).at[0, :D_HEADS].set(swa_sinks[l])
        y_d = _swa_call(qkv_d, sinks_pad, batch=batch, s_len=s_len)
        x2 = _merge_call(x2, mod_l, g_l, (y_a, y_b, y_c, y_d), w_gate, w_branch[l].astype(BF16),
                         w_out[l].astype(BF16), s_len=s_len)
        x2 = _ffn_call(x2, mod_l, g_l, ffn2_w_in[l].astype(BF16), ffn2_w_out[l].astype(BF16),
                       sub=2, resid_w=0.5, s_len=s_len)
    return x2.reshape(batch, s_len, d)
```

```python
import functools

import numpy as np
import jax
import jax.numpy as jnp
from jax import lax
from jax.experimental import pallas as pl
from jax.experimental.pallas import tpu as pltpu

F32 = jnp.float32
BF16 = jnp.bfloat16
I32 = jnp.int32

D_MODEL = 1024
HEAD_DIM = 64
BLOCK = 128
EPS = 1e-6
NEG = -1e30
A_HEADS = 8
A_KV_HEADS = 2
IDX_HEADS = 8
IDX_DIM = 64
TOPK_MAX = 256
LRU_WIDTH = 512
LRU_BLOCKS = 8
CONV_WIDTH = 4
LRU_C = 8.0
C_HEADS = 8
Q_LORA = 256
KV_LORA = 128
QK_NOPE = 64
QK_ROPE = 32
V_DIM = 64
ROPE_THETA = 10000.0
D_HEADS = 8
D_KV_HEADS = 2
WINDOW = 128
N_BRANCH = 4
BRANCH_WIDTH = 512
N_ALIBI = A_HEADS + D_HEADS
D_FF = 2816
IN_SPLITS = (A_HEADS * HEAD_DIM, A_KV_HEADS * HEAD_DIM, A_KV_HEADS * HEAD_DIM,
             IDX_HEADS * IDX_DIM, IDX_DIM, IDX_HEADS,
             LRU_WIDTH, LRU_WIDTH,
             Q_LORA, KV_LORA + QK_ROPE,
             D_HEADS * HEAD_DIM, D_KV_HEADS * HEAD_DIM, D_KV_HEADS * HEAD_DIM,
             N_BRANCH * D_MODEL)
IN_OFFSETS = tuple(int(v) for v in np.concatenate([[0], np.cumsum(IN_SPLITS)]))

LANES = 128
INT_MIN = -2 ** 31
INT_MAX = 2 ** 31 - 1
VMEM_LIMIT = 56 * 1024 * 1024
DSA_VMEM_LIMIT = 62 * 1024 * 1024
LOG2E = 1.4426950408889634
POS_SPLIT = 64
SEARCH_GROUP = 4
LOOP_GROUP = 4
CNT_ROWS = 64

QKV_W = 512 + 256 + 128
IDX_W = 512 + 128
MLA_W = 640

FFN_TM = 1024
FFN_TF = 256
PROJ_TM = 512
DSA_TQ = 128
DSA_KC = 512
MLA_TQ = 512
MLA_KC = 512
SWA_BLOCKS = 4
LRU_L = 256
MERGE_TM = 512


def _alibi(i):
    return float(2.0 ** (-8.0 * i / N_ALIBI))


SLOPES_D = tuple(_alibi(i) for i in range(1, D_HEADS + 1))
SLOPES_A = tuple(_alibi(i) for i in range(D_HEADS + 1, N_ALIBI + 1))


def _bf16_parts(x):
    parts = []
    rem = np.float32(x)
    for _ in range(3):
        p = np.float32(np.asarray(rem, np.float32).astype(jnp.bfloat16).astype(np.float32))
        parts.append(float(p))
        rem = np.float32(rem - p)
    return parts


def _cparams(sem):
    return pltpu.CompilerParams(dimension_semantics=sem, vmem_limit_bytes=VMEM_LIMIT)


def _rms(x, g):
    return x * lax.rsqrt(jnp.mean(x * x, axis=-1, keepdims=True) + EPS) * g


def _nt_dot(a, b):
    return lax.dot_general(a, b, (((1,), (1,)), ((), ())), preferred_element_type=F32)


def _expm1(y):
    u = jnp.exp(y)
    safe = (u != 1.0) & (y > -1.0)
    ratio = y / jnp.log(jnp.where(safe, u, 2.0))
    return jnp.where(u == 1.0, y, jnp.where(safe, (u - 1.0) * ratio, u - 1.0))


def _loop_pairs(n, body):
    def group(i, carry):
        for u in range(LOOP_GROUP):
            body(LOOP_GROUP * i + u, carry)
        return carry

    n_groups = lax.shift_right_logical(n, _log2(LOOP_GROUP))
    lax.fori_loop(0, n_groups, group, 0)
    lax.fori_loop(n_groups * LOOP_GROUP, n, body, 0)


def _log2(n):
    l = int(n).bit_length() - 1
    assert (1 << l) == n
    return l


def _ada_kernel(c_ref, w_ref, b_ref, o_ref):
    c = c_ref[...]
    sc = (c * jax.nn.sigmoid(c)).astype(BF16)
    o_ref[0] = jnp.dot(sc, w_ref[0].astype(BF16), preferred_element_type=F32) + b_ref[0]


def _ada_call(c_pad, w_ada, b_ada):
    depth, d, n = w_ada.shape
    tn = 1152
    return pl.pallas_call(
        _ada_kernel,
        grid=(depth, n // tn),
        in_specs=[pl.BlockSpec((c_pad.shape[0], d), lambda l, j: (0, 0)),
                  pl.BlockSpec((1, d, tn), lambda l, j: (l, 0, j)),
                  pl.BlockSpec((1, 1, tn), lambda l, j: (l, 0, j))],
        out_specs=pl.BlockSpec((1, c_pad.shape[0], tn), lambda l, j: (l, 0, j)),
        out_shape=jax.ShapeDtypeStruct((depth, c_pad.shape[0], n), F32),
        compiler_params=_cparams(("arbitrary", "arbitrary")),
        name="adaln",
    )(c_pad, w_ada, b_ada.reshape(depth, 1, n))


def _ffn_kernel(x_ref, mod_ref, g_ref, wg_ref, wu_ref, wo_ref, o_ref, h_ref, acc_ref, *, sub, resid_w):
    k = pl.program_id(1)
    m = mod_ref[0]

    @pl.when(k == 0)
    def _():
        x = x_ref[...]
        xn = _rms(x, g_ref[2 * sub:2 * sub + 1])
        h_ref[...] = (xn * (1.0 + m[3 * sub + 1:3 * sub + 2]) + m[3 * sub:3 * sub + 1]).astype(BF16)
        acc_ref[...] = jnp.zeros_like(acc_ref)

    h = h_ref[...]
    gate = jnp.dot(h, wg_ref[...], preferred_element_type=F32)
    up = jnp.dot(h, wu_ref[...], preferred_element_type=F32)
    act = (gate * jax.nn.sigmoid(gate) * up).astype(BF16)
    acc_ref[...] += jnp.dot(act, wo_ref[...], preferred_element_type=F32)

    @pl.when(k == pl.num_programs(1) - 1)
    def _():
        yn = _rms(acc_ref[...], g_ref[2 * sub + 1:2 * sub + 2])
        o_ref[...] = x_ref[...] + resid_w * m[3 * sub + 2:3 * sub + 3] * yn


def _ffn_call(x2, mod_l, g_l, w_in_bf, w_out_bf, *, sub, resid_w, s_len):
    t, d = x2.shape
    f = w_out_bf.shape[0]
    tm = min(FFN_TM, s_len)
    tf = FFN_TF
    nf = f // tf
    tpb = s_len // tm
    return pl.pallas_call(
        functools.partial(_ffn_kernel, sub=sub, resid_w=resid_w),
        grid=(t // tm, nf),
        in_specs=[pl.BlockSpec((tm, d), lambda i, k: (i, 0)),
                  pl.BlockSpec((1, 9, d), lambda i, k: (i // tpb, 0, 0)),
                  pl.BlockSpec((6, d), lambda i, k: (0, 0)),
                  pl.BlockSpec((d, tf), lambda i, k: (0, k)),
                  pl.BlockSpec((d, tf), lambda i, k: (0, k + nf)),
                  pl.BlockSpec((tf, d), lambda i, k: (k, 0))],
        out_specs=pl.BlockSpec((tm, d), lambda i, k: (i, 0)),
        out_shape=jax.ShapeDtypeStruct((t, d), F32),
        scratch_shapes=[pltpu.VMEM((tm, d), BF16), pltpu.VMEM((tm, d), F32)],
        compiler_params=_cparams(("arbitrary", "arbitrary")),
        name="ffn",
    )(x2, mod_l, g_l, w_in_bf, w_in_bf, w_out_bf)


def _proj_kernel(x_ref, mod_ref, g_ref, w_ref, kpos_ref, oa_ref, oi_ref, ow_ref, or_ref, oc_ref, od_ref):
    m = mod_ref[0]
    xn = _rms(x_ref[...], g_ref[2:3])
    h = (xn * (1.0 + m[4:5]) + m[3:4]).astype(BF16)
    off = 0
    for ref in (oa_ref, oi_ref, ow_ref, or_ref, oc_ref, od_ref):
        w = ref.shape[1]
        z = jnp.dot(h, w_ref[:, off:off + w], preferred_element_type=F32)
        if ref is oa_ref or ref is od_ref:
            qscale = (HEAD_DIM ** -0.5) * (LOG2E if ref is oa_ref else 1.0)
            z = jnp.concatenate([z[:, :512] * qscale, z[:, 512:768] + kpos_ref[...], z[:, 768:]], axis=1)
        ref[...] = z.astype(ref.dtype)
        off += w


def _proj_call(x2, mod_l, g_l, w_all, kpos, *, s_len):
    t, d = x2.shape
    tm = min(PROJ_TM, s_len)
    tpb = s_len // tm
    widths = (QKV_W, IDX_W, LANES, 2 * LRU_WIDTH, MLA_W, QKV_W)
    dtypes = (BF16, BF16, F32, F32, F32, BF16)
    assert sum(widths) == w_all.shape[1]
    return pl.pallas_call(
        _proj_kernel,
        grid=(t // tm,),
        in_specs=[pl.BlockSpec((tm, d), lambda i: (i, 0)),
                  pl.BlockSpec((1, 9, d), lambda i: (i // tpb, 0, 0)),
                  pl.BlockSpec((6, d), lambda i: (0, 0)),
                  pl.BlockSpec(w_all.shape, lambda i: (0, 0)),
                  pl.BlockSpec((tm, 256), lambda i: (i % tpb, 0))],
        out_specs=[pl.BlockSpec((tm, w), lambda i: (i, 0)) for w in widths],
        out_shape=[jax.ShapeDtypeStruct((t, w), dt) for w, dt in zip(widths, dtypes)],
        compiler_params=_cparams(("arbitrary",)),
        name="mixer_proj",
    )(x2, mod_l, g_l, w_all, kpos)


def _head_rows(q, n_heads, tq):
    lane = lax.broadcasted_iota(I32, (tq, LANES), 1)
    rows = []
    for h in range(n_heads):
        tile = q[:, LANES * (h // 2):LANES * (h // 2 + 1)].astype(F32)
        keep = (lane >= HEAD_DIM) if (h % 2) else (lane < HEAD_DIM)
        rows.append(jnp.where(keep, tile, 0.0).astype(BF16))
    return jnp.concatenate(rows, axis=0)


def _alibi_lanes(slopes, tq):
    lane = lax.broadcasted_iota(I32, (tq, LANES), 1)
    out = []
    for slope in slopes:
        tile = jnp.zeros((tq, LANES), F32)
        for i, part in enumerate(_bf16_parts(slope)):
            tile = jnp.where(lane == HEAD_DIM + 2 * i, POS_SPLIT * part, tile)
            tile = jnp.where(lane == HEAD_DIM + 2 * i + 1, part, tile)
        out.append(tile)
    return out


def _alibi_q_tiles(q, alibi_lanes, tq):
    lane = lax.broadcasted_iota(I32, (tq, LANES), 1)
    rows = []
    for h, al in enumerate(alibi_lanes):
        tile = q[:, LANES * (h // 2):LANES * (h // 2 + 1)].astype(F32)
        if h % 2:
            tile = pltpu.roll(tile, HEAD_DIM, axis=1)
        rows.append(jnp.where(lane < HEAD_DIM, tile, al).astype(BF16))
    return jnp.concatenate(rows, axis=0)


def _gqa_out(accs, inv_ls, tq):
    lane = lax.broadcasted_iota(I32, (tq, LANES), 1)
    tiles = []
    for j in range(4):
        g = j // 2
        halves = []
        for h in (2 * j, 2 * j + 1):
            o = accs[g][h % 4] * inv_ls[g][h % 4]
            src_hi = (g == 1)
            dst_hi = (h % 2 == 1)
            if src_hi != dst_hi:
                o = pltpu.roll(o, HEAD_DIM, axis=1)
            halves.append(o)
        tiles.append(jnp.where(lane < HEAD_DIM, halves[0], halves[1]))
    return jnp.concatenate(tiles, axis=1)


def _dsa_kernel(q_ref, k_ref, v_ref, iq_ref, ik_ref, iw_ref, o_ref,
                key_ref, hi_ref, s_ref, tau_ref, iqs_ref, qs_ref, m_ref, l_ref, acc_ref, *, tq, kc, topk, pos_bits):
    qb = pl.program_id(1)
    q0 = qb * tq
    nk = lax.shift_right_logical(q0 + tq + kc - 1, _log2(kc))
    n_lt = kc // LANES
    n_rt = kc // CNT_ROWS
    qpos_l = q0 + lax.broadcasted_iota(I32, (kc, tq), 1)
    krow = lax.broadcasted_iota(I32, (kc, tq), 0)
    krow_t = lax.broadcasted_iota(I32, (CNT_ROWS, tq), 0)

    iqs_ref[...] = _head_rows(iq_ref[...], IDX_HEADS, tq)
    iw_t = iw_ref[...].T

    def score_body(j, carry):
        ks = pl.multiple_of(j * kc, kc)
        d = _nt_dot(ik_ref[pl.ds(ks, kc), :], iqs_ref[...])
        acc = jnp.zeros((kc, tq), F32)
        for h in range(IDX_HEADS):
            acc = acc + iw_t[h:h + 1, :] * jnp.maximum(d[:, h * tq:(h + 1) * tq], 0.0)
        bits = pltpu.bitcast(acc, I32)
        key = jnp.where(bits < 0, bits ^ INT_MAX, bits)
        key = jnp.where(ks + krow <= qpos_l, key, INT_MIN)
        key_ref[j] = key
        hi_ref[j] = lax.shift_right_arithmetic(key, 16).astype(jnp.int16)
        return carry

    _loop_pairs(nk, score_body)

    def count(pred):
        def body(j, cnt):
            for r in range(n_rt):
                kp = (j * kc + r * CNT_ROWS) + krow_t
                cnt = cnt + jnp.where(pred(key_ref[j, r * CNT_ROWS:(r + 1) * CNT_ROWS, :], kp), 1.0, 0.0)
            return cnt
        cnt = lax.fori_loop(0, nk, body, jnp.zeros((CNT_ROWS, tq), F32))
        return jnp.sum(cnt, axis=0, keepdims=True)

    kf = float(topk)
    n_valid = (q0 + 1 + lax.broadcasted_iota(I32, (1, tq), 1)).astype(F32)
    settled0 = n_valid <= kf

    def count_hi(cand):
        cb = jnp.broadcast_to(lax.shift_right_arithmetic(cand, 16), (CNT_ROWS, tq)).astype(jnp.int16)

        def body(j, cnt):
            for r in range(n_rt):
                hj = hi_ref[j, r * CNT_ROWS:(r + 1) * CNT_ROWS, :]
                cnt = cnt + jnp.where(hj >= cb, jnp.int16(1), jnp.int16(0))
            return cnt
        cnt = lax.fori_loop(0, nk, body, jnp.zeros((CNT_ROWS, tq), jnp.int16))
        return jnp.sum(cnt.astype(F32), axis=0, keepdims=True)

    def count_full(cand):
        cb = jnp.broadcast_to(cand, (CNT_ROWS, tq))
        return count(lambda kv, kp: kv >= cb)

    def bit_step(i, tau, cnt, count_fn):
        cand = tau + lax.shift_left(jnp.int32(1), 31 - i)
        c = count_fn(cand)
        ok = c >= kf
        return jnp.where(ok, cand, tau), jnp.where(ok, c, cnt)

    tau, n_ge = lax.fori_loop(0, 16, lambda i, st: bit_step(i, st[0], st[1], count_hi),
                              (jnp.full((1, tq), INT_MIN, I32), jnp.broadcast_to((nk * kc).astype(F32), (1, tq))))

    def all_settled(cnt):
        return jnp.min(jnp.where(settled0 | (cnt == kf), 1, 0))

    def low_cond(state):
        i, _, _, done = state
        return (i < 32) & (done == 0)

    def low_body(state):
        i, tau, cnt, _ = state
        for b in range(SEARCH_GROUP):
            tau, cnt = bit_step(i + b, tau, cnt, count_full)
        return i + SEARCH_GROUP, tau, cnt, all_settled(cnt)

    _, tau, n_ge, _ = lax.while_loop(low_cond, low_body, (jnp.int32(16), tau, n_ge, all_settled(n_ge)))

    tie_f = jnp.where((n_ge > kf) & (tau > INT_MIN), 1.0, 0.0)
    any_tie = jnp.max(tie_f) > 0.0
    tau_c = jnp.maximum(tau, INT_MIN + 1)
    tau_ref[...] = jnp.broadcast_to(tau_c, tau_ref.shape)

    @pl.when(any_tie)
    def _():
        tb = jnp.broadcast_to(tau_c, (CNT_ROWS, tq))
        need = kf - count(lambda kv, kp: kv > tb)

        def pos_body(i, p):
            cand = p | lax.shift_left(jnp.int32(1), pos_bits - 1 - i)
            cb = jnp.broadcast_to(cand, (CNT_ROWS, tq))
            below = count(lambda kv, kp: (kv == tb) & (kp < cb))
            return jnp.where(below < need, cand, p)

        p = lax.fori_loop(0, pos_bits, pos_body, jnp.zeros((1, tq), I32))
        p = jnp.where(tie_f > 0.0, p, INT_MAX)

        def rewrite_body(j, carry):
            kj = key_ref[j]
            kp = j * kc + krow
            sel = (kj > tau_c) | ((kj == tau_c) & (kp <= p))
            key_ref[j] = jnp.where(sel, 1, -1)
            return carry

        lax.fori_loop(0, nk, rewrite_body, 0)
        tau_ref[...] = jnp.zeros(tau_ref.shape, I32)

    q = q_ref[...]
    for g in range(A_KV_HEADS):
        qs_ref[g] = _alibi_q_tiles(q[:, 4 * HEAD_DIM * g:4 * HEAD_DIM * (g + 1)],
                                   _alibi_lanes([s * LOG2E for s in SLOPES_A[4 * g:4 * g + 4]], tq), tq)
    m_ref[...] = jnp.full(m_ref.shape, NEG, F32)
    l_ref[...] = jnp.zeros(l_ref.shape, F32)
    acc_ref[...] = jnp.zeros(acc_ref.shape, F32)

    def max_body(j, carry):
        ks = pl.multiple_of(j * kc, kc)
        sel = jnp.where(key_ref[j] >= tau_ref[0:1, :], 1.0, 0.0).T > 0.5
        for g in range(A_KV_HEADS):
            s = _nt_dot(qs_ref[g], k_ref[pl.ds(ks, kc), g * LANES:(g + 1) * LANES])
            for h in range(4):
                sh = jnp.where(sel, s[h * tq:(h + 1) * tq], NEG)
                s_ref[g, h, j] = sh
                mp = m_ref[g, h]
                for c in range(n_lt):
                    mp = jnp.maximum(mp, sh[:, c * LANES:(c + 1) * LANES])
                m_ref[g, h] = mp
        return carry

    _loop_pairs(nk, max_body)
    for g in range(A_KV_HEADS):
        for h in range(4):
            m_ref[g, h] = jnp.broadcast_to(jnp.max(m_ref[g, h], axis=1, keepdims=True), (tq, LANES))

    def pv_body(j, carry):
        ks = pl.multiple_of(j * kc, kc)
        vch = v_ref[pl.ds(ks, kc), :]
        for g in range(A_KV_HEADS):
            ps = []
            for h in range(4):
                mb = m_ref[g, h]
                lp = l_ref[g, h]
                tiles = []
                for c in range(n_lt):
                    p = jnp.exp2(s_ref[g, h, j, :, c * LANES:(c + 1) * LANES] - mb)
                    lp = lp + p
                    tiles.append(p.astype(BF16))
                l_ref[g, h] = lp
                ps.append(jnp.concatenate(tiles, axis=1))
            pv = jnp.dot(jnp.concatenate(ps, axis=0), vch, preferred_element_type=F32)
            for h in range(4):
                acc_ref[g, h] += pv[h * tq:(h + 1) * tq]
        return carry

    _loop_pairs(nk, pv_body)
    accs = [[acc_ref[g, h] for h in range(4)] for g in range(A_KV_HEADS)]
    inv_ls = [[1.0 / jnp.sum(l_ref[g, h], axis=1, keepdims=True) for h in range(4)] for g in range(A_KV_HEADS)]
    o_ref[...] = _gqa_out(accs, inv_ls, tq).astype(o_ref.dtype)


def _dsa_call(qkv, idx, iw, *, batch, s_len):
    t = qkv.shape[0]
    tq = min(DSA_TQ, s_len)
    kc = min(DSA_KC, s_len)
    topk = min(TOPK_MAX, s_len // 4)
    assert kc >= topk and kc % tq == 0
    nqb = s_len // tq
    once = pl.Buffered(1)
    return pl.pallas_call(
        functools.partial(_dsa_kernel, tq=tq, kc=kc, topk=topk, pos_bits=_log2(s_len)),
        grid=(batch, nqb),
        in_specs=[pl.BlockSpec((tq, 512), lambda b, i: (b * nqb + i, 0)),
                  pl.BlockSpec((s_len, 256), lambda b, i: (b, 2), pipeline_mode=once),
                  pl.BlockSpec((s_len, 128), lambda b, i: (b, 6), pipeline_mode=once),
                  pl.BlockSpec((tq, 512), lambda b, i: (b * nqb + i, 0)),
                  pl.BlockSpec((s_len, 128), lambda b, i: (b, 4), pipeline_mode=once),
                  pl.BlockSpec((tq, LANES), lambda b, i: (b * nqb + i, 0))],
        out_specs=pl.BlockSpec((tq, 512), lambda b, i: (b * nqb + i, 0)),
        out_shape=jax.ShapeDtypeStruct((t, 512), BF16),
        scratch_shapes=[pltpu.VMEM((s_len // kc, kc, tq), I32),
                        pltpu.VMEM((s_len // kc, kc, tq), jnp.int16),
                        pltpu.VMEM((A_KV_HEADS, 4, s_len // kc, tq, kc), F32),
                        pltpu.VMEM((8, tq), I32),
                        pltpu.VMEM((IDX_HEADS * tq, LANES), BF16),
                        pltpu.VMEM((A_KV_HEADS, 4 * tq, LANES), BF16),
                        pltpu.VMEM((A_KV_HEADS, 4, tq, LANES), F32),
                        pltpu.VMEM((A_KV_HEADS, 4, tq, LANES), F32),
                        pltpu.VMEM((A_KV_HEADS, 4, tq, LANES), F32)],
        compiler_params=pltpu.CompilerParams(dimension_semantics=("arbitrary", "arbitrary"),
                                             vmem_limit_bytes=DSA_VMEM_LIMIT),
        name="dsa",
    )(qkv, qkv, qkv, idx, idx, iw)


def _swa_kernel(q_ref, kp_ref, kc_ref, vp_ref, vc_ref, sink_ref, o_ref, *, tq, nblk):
    i = pl.program_id(1)
    krow = lax.broadcasted_iota(I32, (2 * tq, tq), 0)
    qcol = lax.broadcasted_iota(I32, (2 * tq, tq), 1)
    dist = qcol + tq - krow
    in_band = (dist >= 0) & (dist < WINDOW)
    sinks = sink_ref[...]
    sink_vec = [jnp.broadcast_to(sinks[:, h:h + 1], (1, tq)) for h in range(D_HEADS)]
    alibi = _alibi_lanes(SLOPES_D, tq)
    for n in range(nblk):
        blk = i * nblk + n
        rows = slice(n * tq, (n + 1) * tq)
        q = q_ref[rows, :]
        k_prev = kp_ref[...] if n == 0 else kc_ref[(n - 1) * tq:n * tq, :]
        v_prev = vp_ref[...] if n == 0 else vc_ref[(n - 1) * tq:n * tq, :]
        kk = jnp.concatenate([k_prev, kc_ref[rows, :]], axis=0)
        vv_t = jnp.concatenate([v_prev, vc_ref[rows, :]], axis=0).T
        valid = in_band & ((blk * tq - tq + krow) >= 0)
        qpos = (blk * tq + lax.broadcasted_iota(I32, (1, tq), 1)).astype(F32)
        tiles = []
        for g in range(D_KV_HEADS):
            qs = _alibi_q_tiles(q[:, 4 * HEAD_DIM * g:4 * HEAD_DIM * (g + 1)], alibi[4 * g:4 * g + 4], tq)
            s_t = _nt_dot(kk[:, g * LANES:(g + 1) * LANES], qs)
            ps, ils = [], []
            for h in range(4):
                sh = jnp.where(valid, s_t[:, h * tq:(h + 1) * tq], NEG)
                sink = sink_vec[4 * g + h] + SLOPES_D[4 * g + h] * qpos
                m = jnp.maximum(jnp.max(sh, axis=0, keepdims=True), sink)
                p = jnp.exp(sh - m)
                ils.append(1.0 / (jnp.sum(p, axis=0, keepdims=True) + jnp.exp(sink - m)))
                ps.append(p.astype(BF16))
            o_t = jnp.dot(vv_t, jnp.concatenate(ps, axis=1), preferred_element_type=F32)
            o_t = o_t[g * HEAD_DIM:(g + 1) * HEAD_DIM, :] * jnp.concatenate(ils, axis=1)
            for j in range(2):
                pair = jnp.concatenate([o_t[:, (2 * j) * tq:(2 * j + 1) * tq],
                                        o_t[:, (2 * j + 1) * tq:(2 * j + 2) * tq]], axis=0)
                tiles.append(pair.T)
        o_ref[rows, :] = jnp.concatenate(tiles, axis=1).astype(o_ref.dtype)


def _swa_call(qkv, sinks_pad, *, batch, s_len):
    t = qkv.shape[0]
    tq = BLOCK
    nblk = min(SWA_BLOCKS, s_len // tq)
    ts = nblk * tq
    nst = s_len // ts
    cur = lambda b, i: b * nst + i
    prev = lambda b, i: jnp.maximum((b * nst + i) * nblk - 1, 0)
    return pl.pallas_call(
        functools.partial(_swa_kernel, tq=tq, nblk=nblk),
        grid=(batch, nst),
        in_specs=[pl.BlockSpec((ts, 512), lambda b, i: (cur(b, i), 0)),
                  pl.BlockSpec((tq, 256), lambda b, i: (prev(b, i), 2)),
                  pl.BlockSpec((ts, 256), lambda b, i: (cur(b, i), 2)),
                  pl.BlockSpec((tq, 128), lambda b, i: (prev(b, i), 6)),
                  pl.BlockSpec((ts, 128), lambda b, i: (cur(b, i), 6)),
                  pl.BlockSpec((1, LANES), lambda b, i: (0, 0))],
        out_specs=pl.BlockSpec((ts, 512), lambda b, i: (cur(b, i), 0)),
        out_shape=jax.ShapeDtypeStruct((t, 512), BF16),
        compiler_params=_cparams(("arbitrary", "arbitrary")),
        name="swa",
    )(qkv, qkv, qkv, qkv, qkv, sinks_pad)


def _mla_prep_kernel(c_ref, cos_ref, sin_ref, gq_ref, gkv_ref, wq_ref, wqr_ref, wk_ref, wv_ref,
                     q_ref, k_ref, v_ref):
    c = c_ref[...]
    cos = cos_ref[...]
    sin = sin_ref[...]
    cos8 = jnp.concatenate([cos] * C_HEADS, axis=1)
    sin8 = jnp.concatenate([sin] * C_HEADS, axis=1)
    cqn = _rms(c[:, :Q_LORA], gq_ref[...]).astype(BF16)
    q = (jnp.dot(cqn, wq_ref[...], preferred_element_type=F32) * cos8
         + jnp.dot(cqn, wqr_ref[...], preferred_element_type=F32) * sin8)
    q_ref[...] = q.astype(BF16)
    ckvn = _rms(c[:, Q_LORA:Q_LORA + KV_LORA], gkv_ref[...]).astype(BF16)
    kr = c[:, 384:512] * cos + c[:, 512:640] * sin
    k = jnp.dot(ckvn, wk_ref[...], preferred_element_type=F32) + jnp.concatenate([kr] * C_HEADS, axis=1)
    k_ref[...] = k.astype(BF16)
    v_ref[...] = jnp.dot(ckvn, wv_ref[...], preferred_element_type=F32).astype(BF16)


def _mla_prep_call(cm, cos_t, sin_t, gq, gkv, wq, wqr, wk, wv, *, s_len):
    t = cm.shape[0]
    tm = min(PROJ_TM, s_len)
    tpb = s_len // tm
    full = lambda a: pl.BlockSpec(a.shape, lambda i: (0, 0))
    return pl.pallas_call(
        _mla_prep_kernel,
        grid=(t // tm,),
        in_specs=[pl.BlockSpec((tm, MLA_W), lambda i: (i, 0)),
                  pl.BlockSpec((tm, LANES), lambda i: (i % tpb, 0)),
                  pl.BlockSpec((tm, LANES), lambda i: (i % tpb, 0)),
                  full(gq), full(gkv), full(wq), full(wqr), full(wk), full(wv)],
        out_specs=[pl.BlockSpec((tm, 1024), lambda i: (i, 0)),
                   pl.BlockSpec((tm, 1024), lambda i: (i, 0)),
                   pl.BlockSpec((tm, 512), lambda i: (i, 0))],
        out_shape=[jax.ShapeDtypeStruct((t, 1024), BF16),
                   jax.ShapeDtypeStruct((t, 1024), BF16),
                   jax.ShapeDtypeStruct((t, 512), BF16)],
        compiler_params=_cparams(("arbitrary",)),
        name="mla_prep",
    )(cm, cos_t, sin_t, gq, gkv, wq, wqr, wk, wv)


def _mla_attn_kernel(q_ref, k_ref, v_ref, o_ref, s_ref, m_ref, l_ref, acc_ref, *, tq, kc):
    qb = pl.program_id(2)
    q0 = qb * tq
    n_full = lax.shift_right_logical(q0, _log2(kc))
    rowpos = q0 + lax.broadcasted_iota(I32, (tq, kc), 0)
    lane = lax.broadcasted_iota(I32, (tq, kc), 1)
    c = ((QK_NOPE + QK_ROPE) ** -0.5) * LOG2E
    n_lt = kc // LANES
    m_ref[...] = jnp.full(m_ref.shape, NEG, F32)
    l_ref[...] = jnp.zeros(l_ref.shape, F32)
    acc_ref[...] = jnp.zeros(acc_ref.shape, F32)

    def max_step(j, masked):
        ks = pl.multiple_of(j * kc, kc)
        for hh in range(2):
            s = _nt_dot(q_ref[:, hh * LANES:(hh + 1) * LANES],
                        k_ref[pl.ds(ks, kc), hh * LANES:(hh + 1) * LANES]) * c
            if masked:
                s = jnp.where(ks + lane <= rowpos, s, NEG)
            s_ref[hh, j] = s
            mp = m_ref[hh]
            for t in range(n_lt):
                mp = jnp.maximum(mp, s[:, t * LANES:(t + 1) * LANES])
            m_ref[hh] = mp

    def max_body(j, carry):
        max_step(j, False)
        return carry

    _loop_pairs(n_full, max_body)
    max_step(n_full, True)
    for hh in range(2):
        m_ref[hh] = jnp.broadcast_to(jnp.max(m_ref[hh], axis=1, keepdims=True), (tq, LANES))

    def pv_body(j, carry):
        ks = pl.multiple_of(j * kc, kc)
        vch = v_ref[pl.ds(ks, kc), :]
        for hh in range(2):
            mb = m_ref[hh]
            lp = l_ref[hh]
            tiles = []
            for t in range(n_lt):
                p = jnp.exp2(s_ref[hh, j, :, t * LANES:(t + 1) * LANES] - mb)
                lp = lp + p
                tiles.append(p.astype(BF16))
            l_ref[hh] = lp
            acc_ref[hh] += jnp.dot(jnp.concatenate(tiles, axis=1), vch, preferred_element_type=F32)
        return carry

    _loop_pairs(n_full + 1, pv_body)
    outs = [acc_ref[hh] * (1.0 / jnp.sum(l_ref[hh], axis=1, keepdims=True)) for hh in range(2)]
    lane_o = lax.broadcasted_iota(I32, (tq, LANES), 1)
    o_ref[...] = jnp.where(lane_o < V_DIM, outs[0], outs[1]).astype(o_ref.dtype)


def _mla_attn_call(qm, km, vm, *, batch, s_len):
    t = qm.shape[0]
    tq = min(MLA_TQ, s_len)
    kc = min(MLA_KC, s_len)
    assert kc % tq == 0
    nqb = s_len // tq
    once = pl.Buffered(1)
    return pl.pallas_call(
        functools.partial(_mla_attn_kernel, tq=tq, kc=kc),
        grid=(batch, C_HEADS // 2, nqb),
        in_specs=[pl.BlockSpec((tq, 256), lambda b, h, i: (b * nqb + i, h)),
                  pl.BlockSpec((s_len, 256), lambda b, h, i: (b, h), pipeline_mode=once),
                  pl.BlockSpec((s_len, 128), lambda b, h, i: (b, h), pipeline_mode=once)],
        out_specs=pl.BlockSpec((tq, 128), lambda b, h, i: (b * nqb + i, h)),
        out_shape=jax.ShapeDtypeStruct((t, 512), BF16),
        scratch_shapes=[pltpu.VMEM((2, s_len // kc, tq, kc), F32)] + [pltpu.VMEM((2, tq, LANES), F32)] * 3,
        compiler_params=_cparams(("arbitrary", "arbitrary", "arbitrary")),
        name="mla_attn",
    )(qm, km, vm)


def _lru_kernel(xr_ref, xg_ref, prev_ref, cw_ref, cb_ref, wa_ref, ba_ref, wx_ref, bx_ref, lam_ref,
                o_ref, xe_ref, h_ref, *, ln):
    i = pl.program_id(1)
    xe_ref[0:8, :] = jnp.where(i > 0, prev_ref[...], 0.0)
    xe_ref[8:8 + ln, :] = xr_ref[...]
    cw = cw_ref[...]
    xc = cb_ref[...] + cw[0:1] * xe_ref[pl.ds(5, ln), :]
    for j in range(1, CONV_WIDTH):
        xc = xc + cw[j:j + 1] * xe_ref[pl.ds(5 + j, ln), :]
    xcb = xc.astype(BF16)
    r = jax.nn.sigmoid(jnp.dot(xcb, wa_ref[...], preferred_element_type=F32) + ba_ref[...])
    gi = jax.nn.sigmoid(jnp.dot(xcb, wx_ref[...], preferred_element_type=F32) + bx_ref[...])
    z = -lam_ref[...]
    softplus = jnp.maximum(z, 0.0) + jnp.log1p(jnp.exp(-jnp.abs(z)))
    log_a = -LRU_C * r * softplus
    a = jnp.exp(log_a)
    b = jnp.sqrt(-_expm1(2.0 * log_a)) * (gi * xc)
    row = lax.broadcasted_iota(I32, (ln, LRU_WIDTH), 0)
    d = 1
    while d < ln:
        keep = row >= d
        b = jnp.where(keep, a * pltpu.roll(b, d, axis=0) + b, b)
        a = jnp.where(keep, a * pltpu.roll(a, d, axis=0), a)
        d *= 2
    h_prev = jnp.where(i > 0, h_ref[0:1, :], 0.0)
    h = a * h_prev + b
    h_ref[0:1, :] = h[ln - 1:ln, :]
    xg = xg_ref[...]
    gelu = 0.5 * xg * (1.0 + jnp.tanh(0.7978845608028654 * (xg + 0.044715 * (xg * xg * xg))))
    o_ref[...] = (h * gelu).astype(o_ref.dtype)


def _lru_call(rm, cw, cb, wa, ba, wx, bx, lam, *, batch, s_len):
    t = rm.shape[0]
    ln = min(LRU_L, s_len)
    nt = s_len // ln
    w = LRU_WIDTH
    vec = lambda a: pl.BlockSpec(a.shape, lambda b, i: (0, 0))
    return pl.pallas_call(
        functools.partial(_lru_kernel, ln=ln),
        grid=(batch, nt),
        in_specs=[pl.BlockSpec((ln, w), lambda b, i: (b * nt + i, 0)),
                  pl.BlockSpec((ln, w), lambda b, i: (b * nt + i, 1)),
                  pl.BlockSpec((8, w), lambda b, i: (jnp.maximum((b * nt + i) * (ln // 8) - 1, 0), 0)),
                  vec(cw), vec(cb), vec(wa), vec(ba), vec(wx), vec(bx), vec(lam)],
        out_specs=pl.BlockSpec((ln, w), lambda b, i: (b * nt + i, 0)),
        out_shape=jax.ShapeDtypeStruct((t, w), BF16),
        scratch_shapes=[pltpu.VMEM((ln + 8, w), F32), pltpu.VMEM((8, w), F32)],
        compiler_params=_cparams(("arbitrary", "arbitrary")),
        name="rglru",
    )(rm, rm, rm, cw, cb, wa, ba, wx, bx, lam)


def _merge_kernel(x_ref, mod_ref, g_ref, ya_ref, yb_ref, yc_ref, yd_ref, wg_ref, wb_ref, wo_ref, o_ref):
    m = mod_ref[0]
    x = x_ref[...]
    h = (_rms(x, g_ref[2:3]) * (1.0 + m[4:5]) + m[3:4]).astype(BF16)
    merged = None
    for n, y_ref in enumerate((ya_ref, yb_ref, yc_ref, yd_ref)):
        gate = jax.nn.sigmoid(jnp.dot(h, wg_ref[:, n * D_MODEL:(n + 1) * D_MODEL], preferred_element_type=F32))
        term = gate * jnp.dot(y_ref[...], wb_ref[n], preferred_element_type=F32)
        merged = term if merged is None else merged + term
    y = jnp.dot(merged.astype(BF16), wo_ref[...], preferred_element_type=F32)
    o_ref[...] = x + m[5:6] * _rms(y, g_ref[3:4])


def _merge_call(x2, mod_l, g_l, ys, w_gate, w_branch, w_out, *, s_len):
    t, d = x2.shape
    tm = min(MERGE_TM, s_len)
    tpb = s_len // tm
    return pl.pallas_call(
        _merge_kernel,
        grid=(t // tm,),
        in_specs=[pl.BlockSpec((tm, d), lambda i: (i, 0)),
                  pl.BlockSpec((1, 9, d), lambda i: (i // tpb, 0, 0)),
                  pl.BlockSpec((6, d), lambda i: (0, 0))]
                 + [pl.BlockSpec((tm, BRANCH_WIDTH), lambda i: (i, 0))] * N_BRANCH
                 + [pl.BlockSpec(w_gate.shape, lambda i: (0, 0)),
                    pl.BlockSpec(w_branch.shape, lambda i: (0, 0, 0)),
                    pl.BlockSpec(w_out.shape, lambda i: (0, 0))],
        out_specs=pl.BlockSpec((tm, d), lambda i: (i, 0)),
        out_shape=jax.ShapeDtypeStruct((t, d), F32),
        compiler_params=_cparams(("arbitrary",)),
        name="merge",
    )(x2, mod_l, g_l, *ys, w_gate, w_branch, w_out)


def _mixer_weights(w_in_l):
    o = IN_OFFSETS
    col = lambda n: w_in_l[:, o[n]:o[n + 1]]
    z = lambda n: jnp.zeros((D_MODEL, n), w_in_l.dtype)

    def pad_heads(k):
        return jnp.concatenate([k[:, :64], z(64), k[:, 64:], z(64)], axis=1)

    a_q, a_k, a_v, i_q, i_k, i_w, r_x, r_g, c_q, c_kv, d_q, d_k, d_v = (col(n) for n in range(13))
    k_rope = c_kv[:, KV_LORA:]
    half = QK_ROPE // 2
    k_rope_rot = jnp.concatenate([-k_rope[:, half:], k_rope[:, :half]], axis=1)
    groups = [a_q, pad_heads(a_k), a_v,
              i_q, i_k, i_k,
              i_w, z(LANES - IDX_HEADS),
              r_x, r_g,
              c_q, c_kv[:, :KV_LORA], z(64), k_rope, z(32), z(64), k_rope_rot, z(32),
              d_q, pad_heads(d_k), d_v]
    w_all = jnp.concatenate(groups, axis=1).astype(BF16)
    w_gate = col(13).astype(BF16)
    return w_all, w_gate


def _kpos_table(s_len):
    pos = np.arange(s_len)
    tile = np.zeros((s_len, LANES), np.float32)
    for i in range(3):
        tile[:, HEAD_DIM + 2 * i] = pos // POS_SPLIT
        tile[:, HEAD_DIM + 2 * i + 1] = pos % POS_SPLIT
    return jnp.asarray(np.concatenate([tile, tile], axis=1))


def _mla_weights(w_uq, w_ukv):
    dq = QK_NOPE + QK_ROPE
    half = QK_ROPE // 2
    zq = lambda n: jnp.zeros((Q_LORA, n), w_uq.dtype)
    wq, wqr, wk, wv = [], [], [], []
    for h in range(C_HEADS):
        nope = w_uq[:, h * dq:h * dq + QK_NOPE]
        r1 = w_uq[:, h * dq + QK_NOPE:h * dq + QK_NOPE + half]
        r2 = w_uq[:, h * dq + QK_NOPE + half:(h + 1) * dq]
        wq += [nope, r1, r2, zq(32)]
        wqr += [zq(64), -r2, r1, zq(32)]
        wk += [w_ukv[:, h * 128:h * 128 + QK_NOPE], jnp.zeros((KV_LORA, 64), w_ukv.dtype)]
        wv += [w_ukv[:, h * 128 + QK_NOPE:(h + 1) * 128]]
    cat = lambda xs: jnp.concatenate(xs, axis=1).astype(BF16)
    return cat(wq), cat(wqr), cat(wk), cat(wv)


def _rope_tables(s_len):
    half = QK_ROPE // 2
    inv = ROPE_THETA ** (-jnp.arange(half, dtype=F32) / half)
    ang = jnp.arange(s_len, dtype=F32)[:, None] * inv[None, :]
    cos, sin = jnp.cos(ang), jnp.sin(ang)
    ones = lambda n: jnp.ones((s_len, n), F32)
    zeros = lambda n: jnp.zeros((s_len, n), F32)
    cos_t = jnp.concatenate([ones(64), cos, cos, ones(32)], axis=1)
    sin_t = jnp.concatenate([zeros(64), sin, sin, zeros(32)], axis=1)
    return cos_t, sin_t


def _block_diag(w):
    n, bw, _ = w.shape
    out = jnp.zeros((n * bw, n * bw), w.dtype)
    for k in range(n):
        out = lax.dynamic_update_slice(out, w[k], (k * bw, k * bw))
    return out.astype(BF16)


def kernel(x, c, w_ada, b_ada, norm_g, ffn1_w_in, ffn1_w_out, w_in, conv_w, conv_b, lru_wa, lru_ba, lru_wx,
           lru_bx, lru_lambda, mla_g_q, mla_g_kv, mla_w_uq, mla_w_ukv, swa_sinks, w_branch, w_out, ffn2_w_in,
           ffn2_w_out):
    batch, s_len, d = x.shape
    depth = w_ada.shape[0]
    t = batch * s_len
    assert s_len // POS_SPLIT <= 256
    x2 = x.reshape(t, d)
    c_pad = jnp.zeros((8, d), F32).at[:batch].set(c)
    mod_all = _ada_call(c_pad, w_ada, b_ada)
    cos_t, sin_t = _rope_tables(s_len)
    kpos = _kpos_table(s_len)
    row = lambda v: v.reshape(1, -1)
    for l in range(depth):
        mod_l = mod_all[l, :batch].reshape(batch, 9, d)
        g_l = norm_g[l]
        x2 = _ffn_call(x2, mod_l, g_l, ffn1_w_in[l].astype(BF16), ffn1_w_out[l].astype(BF16),
                       sub=0, resid_w=0.5, s_len=s_len)
        w_all, w_gate = _mixer_weights(w_in[l])
        qkv_a, idx, iw, rm, cm, qkv_d = _proj_call(x2, mod_l, g_l, w_all, kpos, s_len=s_len)
        y_a = _dsa_call(qkv_a, idx, iw, batch=batch, s_len=s_len)
        y_b = _lru_call(rm, conv_w[l], row(conv_b[l]), _block_diag(lru_wa[l]), row(lru_ba[l]),
                        _block_diag(lru_wx[l]), row(lru_bx[l]), row(lru_lambda[l]), batch=batch, s_len=s_len)
        qm, km, vm = _mla_prep_call(cm, cos_t, sin_t, row(mla_g_q[l]), row(mla_g_kv[l]),
                                    *_mla_weights(mla_w_uq[l], mla_w_ukv[l]), s_len=s_len)
        y_c = _mla_attn_call(qm, km, vm, batch=batch, s_len=s_len)
        sinks_pad = jnp.zeros((1, LANES), F32).at[0, :D_HEADS].set(swa_sinks[l])
        y_d = _swa_call(qkv_d, sinks_pad, batch=batch, s_len=s_len)
        x2 = _merge_call(x2, mod_l, g_l, (y_a, y_b, y_c, y_d), w_gate, w_branch[l].astype(BF16),
                         w_out[l].astype(BF16), s_len=s_len)
        x2 = _ffn_call(x2, mod_l, g_l, ffn2_w_in[l].astype(BF16), ffn2_w_out[l].astype(BF16),
                       sub=2, resid_w=0.5, s_len=s_len)
    return x2.reshape(batch, s_len, d)
```

```python
import functools

import numpy as np
import jax
import jax.numpy as jnp
from jax import lax
from jax.experimental import pallas as pl
from jax.experimental.pallas import tpu as pltpu

F32 = jnp.float32
BF16 = jnp.bfloat16
I32 = jnp.int32

D_MODEL = 1024
HEAD_DIM = 64
BLOCK = 128
EPS = 1e-6
NEG = -1e30
A_HEADS = 8
A_KV_HEADS = 2
IDX_HEADS = 8
IDX_DIM = 64
TOPK_MAX = 256
LRU_WIDTH = 512
LRU_BLOCKS = 8
CONV_WIDTH = 4
LRU_C = 8.0
C_HEADS = 8
Q_LORA = 256
KV_LORA = 128
QK_NOPE = 64
QK_ROPE = 32
V_DIM = 64
ROPE_THETA = 10000.0
D_HEADS = 8
D_KV_HEADS = 2
WINDOW = 128
N_BRANCH = 4
BRANCH_WIDTH = 512
N_ALIBI = A_HEADS + D_HEADS
D_FF = 2816
IN_SPLITS = (A_HEADS * HEAD_DIM, A_KV_HEADS * HEAD_DIM, A_KV_HEADS * HEAD_DIM,
             IDX_HEADS * IDX_DIM, IDX_DIM, IDX_HEADS,
             LRU_WIDTH, LRU_WIDTH,
             Q_LORA, KV_LORA + QK_ROPE,
             D_HEADS * HEAD_DIM, D_KV_HEADS * HEAD_DIM, D_KV_HEADS * HEAD_DIM,
             N_BRANCH * D_MODEL)
IN_OFFSETS = tuple(int(v) for v in np.concatenate([[0], np.cumsum(IN_SPLITS)]))

LANES = 128
INT_MIN = -2 ** 31
INT_MAX = 2 ** 31 - 1
VMEM_LIMIT = 56 * 1024 * 1024
DSA_VMEM_LIMIT = 62 * 1024 * 1024
LOG2E = 1.4426950408889634
POS_SPLIT = 64
SEARCH_GROUP = 4
LOOP_GROUP = 4
CNT_ROWS = 64

QKV_W = 512 + 256 + 128
IDX_W = 512 + 128
MLA_W = 640

FFN_TM = 1024
FFN_TF = 256
PROJ_TM = 512
DSA_TQ = 128
DSA_KC = 512
MLA_TQ = 512
MLA_KC = 512
SWA_BLOCKS = 4
LRU_L = 256
MERGE_TM = 512


def _alibi(i):
    return float(2.0 ** (-8.0 * i / N_ALIBI))


SLOPES_D = tuple(_alibi(i) for i in range(1, D_HEADS + 1))
SLOPES_A = tuple(_alibi(i) for i in range(D_HEADS + 1, N_ALIBI + 1))


def _bf16_parts(x):
    parts = []
    rem = np.float32(x)
    for _ in range(3):
        p = np.float32(np.asarray(rem, np.float32).astype(jnp.bfloat16).astype(np.float32))
        parts.append(float(p))
        rem = np.float32(rem - p)
    return parts


def _cparams(sem):
    return pltpu.CompilerParams(dimension_semantics=sem, vmem_limit_bytes=VMEM_LIMIT)


def _rms(x, g):
    return x * lax.rsqrt(jnp.mean(x * x, axis=-1, keepdims=True) + EPS) * g


def _nt_dot(a, b):
    return lax.dot_general(a, b, (((1,), (1,)), ((), ())), preferred_element_type=F32)


def _expm1(y):
    u = jnp.exp(y)
    safe = (u != 1.0) & (y > -1.0)
    ratio = y / jnp.log(jnp.where(safe, u, 2.0))
    return jnp.where(u == 1.0, y, jnp.where(safe, (u - 1.0) * ratio, u - 1.0))


def _loop_pairs(n, body):
    return _loop_groups(n, body, 0)


def _loop_groups(n, body, init):
    def group(i, carry):
        for u in range(LOOP_GROUP):
            carry = body(LOOP_GROUP * i + u, carry)
        return carry

    n_groups = lax.shift_right_logical(n, _log2(LOOP_GROUP))
    carry = lax.fori_loop(0, n_groups, group, init)
    return lax.fori_loop(n_groups * LOOP_GROUP, n, body, carry)


def _log2(n):
    l = int(n).bit_length() - 1
    assert (1 << l) == n
    return l


def _ada_kernel(c_ref, w_ref, b_ref, o_ref):
    c = c_ref[...]
    sc = (c * jax.nn.sigmoid(c)).astype(BF16)
    o_ref[0] = jnp.dot(sc, w_ref[0].astype(BF16), preferred_element_type=F32) + b_ref[0]


def _ada_call(c_pad, w_ada, b_ada):
    depth, d, n = w_ada.shape
    tn = 1152
    return pl.pallas_call(
        _ada_kernel,
        grid=(depth, n // tn),
        in_specs=[pl.BlockSpec((c_pad.shape[0], d), lambda l, j: (0, 0)),
                  pl.BlockSpec((1, d, tn), lambda l, j: (l, 0, j)),
                  pl.BlockSpec((1, 1, tn), lambda l, j: (l, 0, j))],
        out_specs=pl.BlockSpec((1, c_pad.shape[0], tn), lambda l, j: (l, 0, j)),
        out_shape=jax.ShapeDtypeStruct((depth, c_pad.shape[0], n), F32),
        compiler_params=_cparams(("arbitrary", "arbitrary")),
        name="adaln",
    )(c_pad, w_ada, b_ada.reshape(depth, 1, n))


def _ffn_kernel(x_ref, mod_ref, g_ref, wg_ref, wu_ref, wo_ref, o_ref, h_ref, acc_ref, *, sub, resid_w):
    k = pl.program_id(1)
    m = mod_ref[0]

    @pl.when(k == 0)
    def _():
        x = x_ref[...]
        xn = _rms(x, g_ref[2 * sub:2 * sub + 1])
        h_ref[...] = (xn * (1.0 + m[3 * sub + 1:3 * sub + 2]) + m[3 * sub:3 * sub + 1]).astype(BF16)
        acc_ref[...] = jnp.zeros_like(acc_ref)

    h = h_ref[...]
    gate = jnp.dot(h, wg_ref[0].astype(BF16), preferred_element_type=F32)
    up = jnp.dot(h, wu_ref[0].astype(BF16), preferred_element_type=F32)
    act = (gate * jax.nn.sigmoid(gate) * up).astype(BF16)
    acc_ref[...] += jnp.dot(act, wo_ref[0].astype(BF16), preferred_element_type=F32)

    @pl.when(k == pl.num_programs(1) - 1)
    def _():
        yn = _rms(acc_ref[...], g_ref[2 * sub + 1:2 * sub + 2])
        o_ref[...] = x_ref[...] + resid_w * m[3 * sub + 2:3 * sub + 3] * yn


def _ffn_call(x2, mod_l, g_l, w_in, w_out, layer, *, sub, resid_w, s_len):
    t, d = x2.shape
    f = w_out.shape[1]
    tm = min(FFN_TM, s_len)
    tf = FFN_TF
    nf = f // tf
    tpb = s_len // tm
    return pl.pallas_call(
        functools.partial(_ffn_kernel, sub=sub, resid_w=resid_w),
        grid=(t // tm, nf),
        in_specs=[pl.BlockSpec((tm, d), lambda i, k: (i, 0)),
                  pl.BlockSpec((1, 9, d), lambda i, k: (i // tpb, 0, 0)),
                  pl.BlockSpec((6, d), lambda i, k: (0, 0)),
                  pl.BlockSpec((1, d, tf), lambda i, k: (layer, 0, k)),
                  pl.BlockSpec((1, d, tf), lambda i, k: (layer, 0, k + nf)),
                  pl.BlockSpec((1, tf, d), lambda i, k: (layer, k, 0))],
        out_specs=pl.BlockSpec((tm, d), lambda i, k: (i, 0)),
        out_shape=jax.ShapeDtypeStruct((t, d), F32),
        scratch_shapes=[pltpu.VMEM((tm, d), BF16), pltpu.VMEM((tm, d), F32)],
        compiler_params=_cparams(("arbitrary", "arbitrary")),
        name="ffn",
    )(x2, mod_l, g_l, w_in, w_in, w_out)


def _proj_kernel(x_ref, mod_ref, g_ref, w_ref, kpos_ref, oa_ref, oi_ref, ow_ref, or_ref, oc_ref, od_ref):
    m = mod_ref[0]
    xn = _rms(x_ref[...], g_ref[2:3])
    h = (xn * (1.0 + m[4:5]) + m[3:4]).astype(BF16)
    off = 0
    for ref in (oa_ref, oi_ref, ow_ref, or_ref, oc_ref, od_ref):
        w = ref.shape[1]
        z = jnp.dot(h, w_ref[:, off:off + w], preferred_element_type=F32)
        if ref is oa_ref or ref is od_ref:
            qscale = (HEAD_DIM ** -0.5) * (LOG2E if ref is oa_ref else 1.0)
            z = jnp.concatenate([z[:, :512] * qscale, z[:, 512:768] + kpos_ref[...], z[:, 768:]], axis=1)
        ref[...] = z.astype(ref.dtype)
        off += w


def _proj_call(x2, mod_l, g_l, w_all, kpos, *, s_len):
    t, d = x2.shape
    tm = min(PROJ_TM, s_len)
    tpb = s_len // tm
    widths = (QKV_W, IDX_W, LANES, 2 * LRU_WIDTH, MLA_W, QKV_W)
    dtypes = (BF16, BF16, F32, F32, F32, BF16)
    assert sum(widths) == w_all.shape[1]
    return pl.pallas_call(
        _proj_kernel,
        grid=(t // tm,),
        in_specs=[pl.BlockSpec((tm, d), lambda i: (i, 0)),
                  pl.BlockSpec((1, 9, d), lambda i: (i // tpb, 0, 0)),
                  pl.BlockSpec((6, d), lambda i: (0, 0)),
                  pl.BlockSpec(w_all.shape, lambda i: (0, 0)),
                  pl.BlockSpec((tm, 256), lambda i: (i % tpb, 0))],
        out_specs=[pl.BlockSpec((tm, w), lambda i: (i, 0)) for w in widths],
        out_shape=[jax.ShapeDtypeStruct((t, w), dt) for w, dt in zip(widths, dtypes)],
        compiler_params=_cparams(("arbitrary",)),
        name="mixer_proj",
    )(x2, mod_l, g_l, w_all, kpos)


def _head_rows(q, n_heads, tq):
    lane = lax.broadcasted_iota(I32, (tq, LANES), 1)
    rows = []
    for h in range(n_heads):
        tile = q[:, LANES * (h // 2):LANES * (h // 2 + 1)].astype(F32)
        keep = (lane >= HEAD_DIM) if (h % 2) else (lane < HEAD_DIM)
        rows.append(jnp.where(keep, tile, 0.0).astype(BF16))
    return jnp.concatenate(rows, axis=0)


def _alibi_lanes(slopes, tq):
    lane = lax.broadcasted_iota(I32, (tq, LANES), 1)
    out = []
    for slope in slopes:
        tile = jnp.zeros((tq, LANES), F32)
        for i, part in enumerate(_bf16_parts(slope)):
            tile = jnp.where(lane == HEAD_DIM + 2 * i, POS_SPLIT * part, tile)
            tile = jnp.where(lane == HEAD_DIM + 2 * i + 1, part, tile)
        out.append(tile)
    return out


def _alibi_q_tiles(q, alibi_lanes, tq):
    lane = lax.broadcasted_iota(I32, (tq, LANES), 1)
    rows = []
    for h, al in enumerate(alibi_lanes):
        tile = q[:, LANES * (h // 2):LANES * (h // 2 + 1)].astype(F32)
        if h % 2:
            tile = pltpu.roll(tile, HEAD_DIM, axis=1)
        rows.append(jnp.where(lane < HEAD_DIM, tile, al).astype(BF16))
    return jnp.concatenate(rows, axis=0)


def _gqa_out(accs, inv_ls, tq):
    lane = lax.broadcasted_iota(I32, (tq, LANES), 1)
    tiles = []
    for j in range(4):
        g = j // 2
        halves = []
        for h in (2 * j, 2 * j + 1):
            o = accs[g][h % 4] * inv_ls[g][h % 4]
            src_hi = (g == 1)
            dst_hi = (h % 2 == 1)
            if src_hi != dst_hi:
                o = pltpu.roll(o, HEAD_DIM, axis=1)
            halves.append(o)
        tiles.append(jnp.where(lane < HEAD_DIM, halves[0], halves[1]))
    return jnp.concatenate(tiles, axis=1)


def _dsa_kernel(q_ref, k_ref, v_ref, iq_ref, ik_ref, iw_ref, o_ref,
                key_ref, hi_ref, s_ref, tau_ref, iqs_ref, qs_ref, m_ref, l_ref, acc_ref, *, tq, kc, topk, pos_bits):
    qb = pl.program_id(1)
    q0 = qb * tq
    nk = lax.shift_right_logical(q0 + tq + kc - 1, _log2(kc))
    n_lt = kc // LANES
    n_rt = kc // CNT_ROWS
    qpos_l = q0 + lax.broadcasted_iota(I32, (kc, tq), 1)
    krow = lax.broadcasted_iota(I32, (kc, tq), 0)
    krow_t = lax.broadcasted_iota(I32, (CNT_ROWS, tq), 0)

    iqs_ref[...] = _head_rows(iq_ref[...], IDX_HEADS, tq)
    iw_t = iw_ref[...].T

    def score_body(j, carry):
        ks = pl.multiple_of(j * kc, kc)
        d = _nt_dot(ik_ref[pl.ds(ks, kc), :], iqs_ref[...])
        acc = jnp.zeros((kc, tq), F32)
        for h in range(IDX_HEADS):
            acc = acc + iw_t[h:h + 1, :] * jnp.maximum(d[:, h * tq:(h + 1) * tq], 0.0)
        bits = pltpu.bitcast(acc, I32)
        key = jnp.where(bits < 0, bits ^ INT_MAX, bits)
        key = jnp.where(ks + krow <= qpos_l, key, INT_MIN)
        key_ref[j] = key
        hi_ref[j] = lax.shift_right_arithmetic(key, 16).astype(jnp.int16)
        return carry

    _loop_pairs(nk, score_body)

    def count(pred):
        def body(j, cnt):
            for r in range(n_rt):
                kp = (j * kc + r * CNT_ROWS) + krow_t
                cnt = cnt + jnp.where(pred(key_ref[j, r * CNT_ROWS:(r + 1) * CNT_ROWS, :], kp), 1.0, 0.0)
            return cnt
        cnt = _loop_groups(nk, body, jnp.zeros((CNT_ROWS, tq), F32))
        return jnp.sum(cnt, axis=0, keepdims=True)

    kf = float(topk)
    n_valid = (q0 + 1 + lax.broadcasted_iota(I32, (1, tq), 1)).astype(F32)
    settled0 = n_valid <= kf

    def count_hi(cand):
        cb = jnp.broadcast_to(lax.shift_right_arithmetic(cand, 16), (CNT_ROWS, tq)).astype(jnp.int16)

        def body(j, cnt):
            for r in range(n_rt):
                hj = hi_ref[j, r * CNT_ROWS:(r + 1) * CNT_ROWS, :]
                cnt = cnt + jnp.where(hj >= cb, jnp.int16(1), jnp.int16(0))
            return cnt
        cnt = _loop_groups(nk, body, jnp.zeros((CNT_ROWS, tq), jnp.int16))
        return jnp.sum(cnt.astype(F32), axis=0, keepdims=True)

    def count_full(cand):
        cb = jnp.broadcast_to(cand, (CNT_ROWS, tq))
        return count(lambda kv, kp: kv >= cb)

    def bit_step(i, tau, cnt, count_fn):
        cand = tau + lax.shift_left(jnp.int32(1), 31 - i)
        c = count_fn(cand)
        ok = c >= kf
        return jnp.where(ok, cand, tau), jnp.where(ok, c, cnt)

    tau, n_ge = lax.fori_loop(0, 16, lambda i, st: bit_step(i, st[0], st[1], count_hi),
                              (jnp.full((1, tq), INT_MIN, I32), jnp.broadcast_to((nk * kc).astype(F32), (1, tq))))

    def all_settled(cnt):
        return jnp.min(jnp.where(settled0 | (cnt == kf), 1, 0))

    def low_cond(state):
        i, _, _, done = state
        return (i < 32) & (done == 0)

    def low_body(state):
        i, tau, cnt, _ = state
        for b in range(SEARCH_GROUP):
            tau, cnt = bit_step(i + b, tau, cnt, count_full)
        return i + SEARCH_GROUP, tau, cnt, all_settled(cnt)

    _, tau, n_ge, _ = lax.while_loop(low_cond, low_body, (jnp.int32(16), tau, n_ge, all_settled(n_ge)))

    tie_f = jnp.where((n_ge > kf) & (tau > INT_MIN), 1.0, 0.0)
    any_tie = jnp.max(tie_f) > 0.0
    tau_c = jnp.maximum(tau, INT_MIN + 1)
    tau_ref[...] = jnp.broadcast_to(tau_c, tau_ref.shape)

    @pl.when(any_tie)
    def _():
        tb = jnp.broadcast_to(tau_c, (CNT_ROWS, tq))
        need = kf - count(lambda kv, kp: kv > tb)

        def pos_body(i, p):
            cand = p | lax.shift_left(jnp.int32(1), pos_bits - 1 - i)
            cb = jnp.broadcast_to(cand, (CNT_ROWS, tq))
            below = count(lambda kv, kp: (kv == tb) & (kp < cb))
            return jnp.where(below < need, cand, p)

        p = lax.fori_loop(0, pos_bits, pos_body, jnp.zeros((1, tq), I32))
        p = jnp.where(tie_f > 0.0, p, INT_MAX)

        def rewrite_body(j, carry):
            kj = key_ref[j]
            kp = j * kc + krow
            sel = (kj > tau_c) | ((kj == tau_c) & (kp <= p))
            key_ref[j] = jnp.where(sel, 1, -1)
            return carry

        lax.fori_loop(0, nk, rewrite_body, 0)
        tau_ref[...] = jnp.zeros(tau_ref.shape, I32)

    q = q_ref[...]
    for g in range(A_KV_HEADS):
        qs_ref[g] = _alibi_q_tiles(q[:, 4 * HEAD_DIM * g:4 * HEAD_DIM * (g + 1)],
                                   _alibi_lanes([s * LOG2E for s in SLOPES_A[4 * g:4 * g + 4]], tq), tq)
    m_ref[...] = jnp.full(m_ref.shape, NEG, F32)
    l_ref[...] = jnp.zeros(l_ref.shape, F32)
    acc_ref[...] = jnp.zeros(acc_ref.shape, F32)

    def max_body(j, carry):
        ks = pl.multiple_of(j * kc, kc)
        sel = jnp.where(key_ref[j] >= tau_ref[0:1, :], 1.0, 0.0).T > 0.5
        for g in range(A_KV_HEADS):
            s = _nt_dot(qs_ref[g], k_ref[pl.ds(ks, kc), g * LANES:(g + 1) * LANES])
            for h in range(4):
                sh = jnp.where(sel, s[h * tq:(h + 1) * tq], NEG)
                s_ref[g, h, j] = sh
                mp = m_ref[g, h]
                for c in range(n_lt):
                    mp = jnp.maximum(mp, sh[:, c * LANES:(c + 1) * LANES])
                m_ref[g, h] = mp
        return carry

    _loop_pairs(nk, max_body)
    for g in range(A_KV_HEADS):
        for h in range(4):
            m_ref[g, h] = jnp.broadcast_to(jnp.max(m_ref[g, h], axis=1, keepdims=True), (tq, LANES))

    def pv_body(j, carry):
        ks = pl.multiple_of(j * kc, kc)
        vch = v_ref[pl.ds(ks, kc), :]
        for g in range(A_KV_HEADS):
            ps = []
            for h in range(4):
                mb = m_ref[g, h]
                lp = l_ref[g, h]
                tiles = []
                for c in range(n_lt):
                    p = jnp.exp2(s_ref[g, h, j, :, c * LANES:(c + 1) * LANES] - mb)
                    lp = lp + p
                    tiles.append(p.astype(BF16))
                l_ref[g, h] = lp
                ps.append(jnp.concatenate(tiles, axis=1))
            pv = jnp.dot(jnp.concatenate(ps, axis=0), vch, preferred_element_type=F32)
            for h in range(4):
                acc_ref[g, h] += pv[h * tq:(h + 1) * tq]
        return carry

    _loop_pairs(nk, pv_body)
    accs = [[acc_ref[g, h] for h in range(4)] for g in range(A_KV_HEADS)]
    inv_ls = [[1.0 / jnp.sum(l_ref[g, h], axis=1, keepdims=True) for h in range(4)] for g in range(A_KV_HEADS)]
    o_ref[...] = _gqa_out(accs, inv_ls, tq).astype(o_ref.dtype)


def _dsa_call(qkv, idx, iw, *, batch, s_len):
    t = qkv.shape[0]
    tq = min(DSA_TQ, s_len)
    kc = min(DSA_KC, s_len)
    topk = min(TOPK_MAX, s_len // 4)
    assert kc >= topk and kc % tq == 0
    nqb = s_len // tq
    once = pl.Buffered(1)
    return pl.pallas_call(
        functools.partial(_dsa_kernel, tq=tq, kc=kc, topk=topk, pos_bits=_log2(s_len)),
        grid=(batch, nqb),
        in_specs=[pl.BlockSpec((tq, 512), lambda b, i: (b * nqb + i, 0)),
                  pl.BlockSpec((s_len, 256), lambda b, i: (b, 2), pipeline_mode=once),
                  pl.BlockSpec((s_len, 128), lambda b, i: (b, 6), pipeline_mode=once),
                  pl.BlockSpec((tq, 512), lambda b, i: (b * nqb + i, 0)),
                  pl.BlockSpec((s_len, 128), lambda b, i: (b, 4), pipeline_mode=once),
                  pl.BlockSpec((tq, LANES), lambda b, i: (b * nqb + i, 0))],
        out_specs=pl.BlockSpec((tq, 512), lambda b, i: (b * nqb + i, 0)),
        out_shape=jax.ShapeDtypeStruct((t, 512), BF16),
        scratch_shapes=[pltpu.VMEM((s_len // kc, kc, tq), I32),
                        pltpu.VMEM((s_len // kc, kc, tq), jnp.int16),
                        pltpu.VMEM((A_KV_HEADS, 4, s_len // kc, tq, kc), F32),
                        pltpu.VMEM((8, tq), I32),
                        pltpu.VMEM((IDX_HEADS * tq, LANES), BF16),
                        pltpu.VMEM((A_KV_HEADS, 4 * tq, LANES), BF16),
                        pltpu.VMEM((A_KV_HEADS, 4, tq, LANES), F32),
                        pltpu.VMEM((A_KV_HEADS, 4, tq, LANES), F32),
                        pltpu.VMEM((A_KV_HEADS, 4, tq, LANES), F32)],
        compiler_params=pltpu.CompilerParams(dimension_semantics=("arbitrary", "arbitrary"),
                                             vmem_limit_bytes=DSA_VMEM_LIMIT),
        name="dsa",
    )(qkv, qkv, qkv, idx, idx, iw)


def _swa_kernel(q_ref, kp_ref, kc_ref, vp_ref, vc_ref, sink_ref, o_ref, *, tq, nblk):
    i = pl.program_id(1)
    krow = lax.broadcasted_iota(I32, (2 * tq, tq), 0)
    qcol = lax.broadcasted_iota(I32, (2 * tq, tq), 1)
    dist = qcol + tq - krow
    in_band = (dist >= 0) & (dist < WINDOW)
    sinks = sink_ref[...]
    sink_vec = [jnp.broadcast_to(sinks[:, h:h + 1], (1, tq)) for h in range(D_HEADS)]
    alibi = _alibi_lanes(SLOPES_D, tq)
    for n in range(nblk):
        blk = i * nblk + n
        rows = slice(n * tq, (n + 1) * tq)
        q = q_ref[rows, :]
        k_prev = kp_ref[...] if n == 0 else kc_ref[(n - 1) * tq:n * tq, :]
        v_prev = vp_ref[...] if n == 0 else vc_ref[(n - 1) * tq:n * tq, :]
        kk = jnp.concatenate([k_prev, kc_ref[rows, :]], axis=0)
        vv_t = jnp.concatenate([v_prev, vc_ref[rows, :]], axis=0).T
        valid = in_band & ((blk * tq - tq + krow) >= 0)
        qpos = (blk * tq + lax.broadcasted_iota(I32, (1, tq), 1)).astype(F32)
        tiles = []
        for g in range(D_KV_HEADS):
            qs = _alibi_q_tiles(q[:, 4 * HEAD_DIM * g:4 * HEAD_DIM * (g + 1)], alibi[4 * g:4 * g + 4], tq)
            s_t = _nt_dot(kk[:, g * LANES:(g + 1) * LANES], qs)
            ps, ils = [], []
            for h in range(4):
                sh = jnp.where(valid, s_t[:, h * tq:(h + 1) * tq], NEG)
                sink = sink_vec[4 * g + h] + SLOPES_D[4 * g + h] * qpos
                m = jnp.maximum(jnp.max(sh, axis=0, keepdims=True), sink)
                p = jnp.exp(sh - m)
                ils.append(1.0 / (jnp.sum(p, axis=0, keepdims=True) + jnp.exp(sink - m)))
                ps.append(p.astype(BF16))
            o_t = jnp.dot(vv_t, jnp.concatenate(ps, axis=1), preferred_element_type=F32)
            o_t = o_t[g * HEAD_DIM:(g + 1) * HEAD_DIM, :] * jnp.concatenate(ils, axis=1)
            for j in range(2):
                pair = jnp.concatenate([o_t[:, (2 * j) * tq:(2 * j + 1) * tq],
                                        o_t[:, (2 * j + 1) * tq:(2 * j + 2) * tq]], axis=0)
                tiles.append(pair.T)
        o_ref[rows, :] = jnp.concatenate(tiles, axis=1).astype(o_ref.dtype)


def _swa_call(qkv, sinks_pad, *, batch, s_len):
    t = qkv.shape[0]
    tq = BLOCK
    nblk = min(SWA_BLOCKS, s_len // tq)
    ts = nblk * tq
    nst = s_len // ts
    cur = lambda b, i: b * nst + i
    prev = lambda b, i: jnp.maximum((b * nst + i) * nblk - 1, 0)
    return pl.pallas_call(
        functools.partial(_swa_kernel, tq=tq, nblk=nblk),
        grid=(batch, nst),
        in_specs=[pl.BlockSpec((ts, 512), lambda b, i: (cur(b, i), 0)),
                  pl.BlockSpec((tq, 256), lambda b, i: (prev(b, i), 2)),
                  pl.BlockSpec((ts, 256), lambda b, i: (cur(b, i), 2)),
                  pl.BlockSpec((tq, 128), lambda b, i: (prev(b, i), 6)),
                  pl.BlockSpec((ts, 128), lambda b, i: (cur(b, i), 6)),
                  pl.BlockSpec((1, LANES), lambda b, i: (0, 0))],
        out_specs=pl.BlockSpec((ts, 512), lambda b, i: (cur(b, i), 0)),
        out_shape=jax.ShapeDtypeStruct((t, 512), BF16),
        compiler_params=_cparams(("arbitrary", "arbitrary")),
        name="swa",
    )(qkv, qkv, qkv, qkv, qkv, sinks_pad)


def _mla_prep_kernel(c_ref, cos_ref, sin_ref, gq_ref, gkv_ref, wq_ref, wqr_ref, wk_ref, wv_ref,
                     q_ref, k_ref, v_ref):
    c = c_ref[...]
    cos = cos_ref[...]
    sin = sin_ref[...]
    cos8 = jnp.concatenate([cos] * C_HEADS, axis=1)
    sin8 = jnp.concatenate([sin] * C_HEADS, axis=1)
    cqn = _rms(c[:, :Q_LORA], gq_ref[...]).astype(BF16)
    q = (jnp.dot(cqn, wq_ref[...], preferred_element_type=F32) * cos8
         + jnp.dot(cqn, wqr_ref[...], preferred_element_type=F32) * sin8)
    q_ref[...] = q.astype(BF16)
    ckvn = _rms(c[:, Q_LORA:Q_LORA + KV_LORA], gkv_ref[...]).astype(BF16)
    kr = c[:, 384:512] * cos + c[:, 512:640] * sin
    k = jnp.dot(ckvn, wk_ref[...], preferred_element_type=F32) + jnp.concatenate([kr] * C_HEADS, axis=1)
    k_ref[...] = k.astype(BF16)
    v_ref[...] = jnp.dot(ckvn, wv_ref[...], preferred_element_type=F32).astype(BF16)


def _mla_prep_call(cm, cos_t, sin_t, gq, gkv, wq, wqr, wk, wv, *, s_len):
    t = cm.shape[0]
    tm = min(PROJ_TM, s_len)
    tpb = s_len // tm
    full = lambda a: pl.BlockSpec(a.shape, lambda i: (0, 0))
    return pl.pallas_call(
        _mla_prep_kernel,
        grid=(t // tm,),
        in_specs=[pl.BlockSpec((tm, MLA_W), lambda i: (i, 0)),
                  pl.BlockSpec((tm, LANES), lambda i: (i % tpb, 0)),
                  pl.BlockSpec((tm, LANES), lambda i: (i % tpb, 0)),
                  full(gq), full(gkv), full(wq), full(wqr), full(wk), full(wv)],
        out_specs=[pl.BlockSpec((tm, 1024), lambda i: (i, 0)),
                   pl.BlockSpec((tm, 1024), lambda i: (i, 0)),
                   pl.BlockSpec((tm, 512), lambda i: (i, 0))],
        out_shape=[jax.ShapeDtypeStruct((t, 1024), BF16),
                   jax.ShapeDtypeStruct((t, 1024), BF16),
                   jax.ShapeDtypeStruct((t, 512), BF16)],
        compiler_params=_cparams(("arbitrary",)),
        name="mla_prep",
    )(cm, cos_t, sin_t, gq, gkv, wq, wqr, wk, wv)


def _mla_attn_kernel(q_ref, k_ref, v_ref, o_ref, s_ref, m_ref, l_ref, acc_ref, *, tq, kc):
    qb = pl.program_id(2)
    q0 = qb * tq
    n_full = lax.shift_right_logical(q0, _log2(kc))
    rowpos = q0 + lax.broadcasted_iota(I32, (tq, kc), 0)
    lane = lax.broadcasted_iota(I32, (tq, kc), 1)
    c = ((QK_NOPE + QK_ROPE) ** -0.5) * LOG2E
    n_lt = kc // LANES
    m_ref[...] = jnp.full(m_ref.shape, NEG, F32)
    l_ref[...] = jnp.zeros(l_ref.shape, F32)
    acc_ref[...] = jnp.zeros(acc_ref.shape, F32)

    def max_step(j, masked):
        ks = pl.multiple_of(j * kc, kc)
        for hh in range(2):
            s = _nt_dot(q_ref[:, hh * LANES:(hh + 1) * LANES],
                        k_ref[pl.ds(ks, kc), hh * LANES:(hh + 1) * LANES]) * c
            if masked:
                s = jnp.where(ks + lane <= rowpos, s, NEG)
            s_ref[hh, j] = s
            mp = m_ref[hh]
            for t in range(n_lt):
                mp = jnp.maximum(mp, s[:, t * LANES:(t + 1) * LANES])
            m_ref[hh] = mp

    def max_body(j, carry):
        max_step(j, False)
        return carry

    _loop_pairs(n_full, max_body)
    max_step(n_full, True)
    for hh in range(2):
        m_ref[hh] = jnp.broadcast_to(jnp.max(m_ref[hh], axis=1, keepdims=True), (tq, LANES))

    def pv_body(j, carry):
        ks = pl.multiple_of(j * kc, kc)
        vch = v_ref[pl.ds(ks, kc), :]
        for hh in range(2):
            mb = m_ref[hh]
            lp = l_ref[hh]
            tiles = []
            for t in range(n_lt):
                p = jnp.exp2(s_ref[hh, j, :, t * LANES:(t + 1) * LANES] - mb)
                lp = lp + p
                tiles.append(p.astype(BF16))
            l_ref[hh] = lp
            acc_ref[hh] += jnp.dot(jnp.concatenate(tiles, axis=1), vch, preferred_element_type=F32)
        return carry

    _loop_pairs(n_full + 1, pv_body)
    outs = [acc_ref[hh] * (1.0 / jnp.sum(l_ref[hh], axis=1, keepdims=True)) for hh in range(2)]
    lane_o = lax.broadcasted_iota(I32, (tq, LANES), 1)
    o_ref[...] = jnp.where(lane_o < V_DIM, outs[0], outs[1]).astype(o_ref.dtype)


def _mla_attn_call(qm, km, vm, *, batch, s_len):
    t = qm.shape[0]
    tq = min(MLA_TQ, s_len)
    kc = min(MLA_KC, s_len)
    assert kc % tq == 0
    nqb = s_len // tq
    once = pl.Buffered(1)
    return pl.pallas_call(
        functools.partial(_mla_attn_kernel, tq=tq, kc=kc),
        grid=(batch, C_HEADS // 2, nqb),
        in_specs=[pl.BlockSpec((tq, 256), lambda b, h, i: (b * nqb + i, h)),
                  pl.BlockSpec((s_len, 256), lambda b, h, i: (b, h), pipeline_mode=once),
                  pl.BlockSpec((s_len, 128), lambda b, h, i: (b, h), pipeline_mode=once)],
        out_specs=pl.BlockSpec((tq, 128), lambda b, h, i: (b * nqb + i, h)),
        out_shape=jax.ShapeDtypeStruct((t, 512), BF16),
        scratch_shapes=[pltpu.VMEM((2, s_len // kc, tq, kc), F32)] + [pltpu.VMEM((2, tq, LANES), F32)] * 3,
        compiler_params=_cparams(("arbitrary", "arbitrary", "arbitrary")),
        name="mla_attn",
    )(qm, km, vm)


def _lru_kernel(xr_ref, xg_ref, prev_ref, cw_ref, cb_ref, wa_ref, ba_ref, wx_ref, bx_ref, lam_ref,
                o_ref, xe_ref, h_ref, *, ln):
    i = pl.program_id(1)
    xe_ref[0:8, :] = jnp.where(i > 0, prev_ref[...], 0.0)
    xe_ref[8:8 + ln, :] = xr_ref[...]
    cw = cw_ref[...]
    xc = cb_ref[...] + cw[0:1] * xe_ref[pl.ds(5, ln), :]
    for j in range(1, CONV_WIDTH):
        xc = xc + cw[j:j + 1] * xe_ref[pl.ds(5 + j, ln), :]
    xcb = xc.astype(BF16)
    r = jax.nn.sigmoid(jnp.dot(xcb, wa_ref[...], preferred_element_type=F32) + ba_ref[...])
    gi = jax.nn.sigmoid(jnp.dot(xcb, wx_ref[...], preferred_element_type=F32) + bx_ref[...])
    z = -lam_ref[...]
    softplus = jnp.maximum(z, 0.0) + jnp.log1p(jnp.exp(-jnp.abs(z)))
    log_a = -LRU_C * r * softplus
    a = jnp.exp(log_a)
    b = jnp.sqrt(-_expm1(2.0 * log_a)) * (gi * xc)
    row = lax.broadcasted_iota(I32, (ln, LRU_WIDTH), 0)
    d = 1
    while d < ln:
        keep = row >= d
        b = jnp.where(keep, a * pltpu.roll(b, d, axis=0) + b, b)
        a = jnp.where(keep, a * pltpu.roll(a, d, axis=0), a)
        d *= 2
    h_prev = jnp.where(i > 0, h_ref[0:1, :], 0.0)
    h = a * h_prev + b
    h_ref[0:1, :] = h[ln - 1:ln, :]
    xg = xg_ref[...]
    gelu = 0.5 * xg * (1.0 + jnp.tanh(0.7978845608028654 * (xg + 0.044715 * (xg * xg * xg))))
    o_ref[...] = (h * gelu).astype(o_ref.dtype)


def _lru_call(rm, cw, cb, wa, ba, wx, bx, lam, *, batch, s_len):
    t = rm.shape[0]
    ln = min(LRU_L, s_len)
    nt = s_len // ln
    w = LRU_WIDTH
    vec = lambda a: pl.BlockSpec(a.shape, lambda b, i: (0, 0))
    return pl.pallas_call(
        functools.partial(_lru_kernel, ln=ln),
        grid=(batch, nt),
        in_specs=[pl.BlockSpec((ln, w), lambda b, i: (b * nt + i, 0)),
                  pl.BlockSpec((ln, w), lambda b, i: (b * nt + i, 1)),
                  pl.BlockSpec((8, w), lambda b, i: (jnp.maximum((b * nt + i) * (ln // 8) - 1, 0), 0)),
                  vec(cw), vec(cb), vec(wa), vec(ba), vec(wx), vec(bx), vec(lam)],
        out_specs=pl.BlockSpec((ln, w), lambda b, i: (b * nt + i, 0)),
        out_shape=jax.ShapeDtypeStruct((t, w), BF16),
        scratch_shapes=[pltpu.VMEM((ln + 8, w), F32), pltpu.VMEM((8, w), F32)],
        compiler_params=_cparams(("arbitrary", "arbitrary")),
        name="rglru",
    )(rm, rm, rm, cw, cb, wa, ba, wx, bx, lam)


def _merge_kernel(x_ref, mod_ref, g_ref, ya_ref, yb_ref, yc_ref, yd_ref, wg_ref, wb_ref, wo_ref, o_ref):
    m = mod_ref[0]
    x = x_ref[...]
    h = (_rms(x, g_ref[2:3]) * (1.0 + m[4:5]) + m[3:4]).astype(BF16)
    merged = None
    for n, y_ref in enumerate((ya_ref, yb_ref, yc_ref, yd_ref)):
        gate = jax.nn.sigmoid(jnp.dot(h, wg_ref[:, n * D_MODEL:(n + 1) * D_MODEL], preferred_element_type=F32))
        term = gate * jnp.dot(y_ref[...], wb_ref[n], preferred_element_type=F32)
        merged = term if merged is None else merged + term
    y = jnp.dot(merged.astype(BF16), wo_ref[...], preferred_element_type=F32)
    o_ref[...] = x + m[5:6] * _rms(y, g_ref[3:4])


def _merge_call(x2, mod_l, g_l, ys, w_gate, w_branch, w_out, *, s_len):
    t, d = x2.shape
    tm = min(MERGE_TM, s_len)
    tpb = s_len // tm
    return pl.pallas_call(
        _merge_kernel,
        grid=(t // tm,),
        in_specs=[pl.BlockSpec((tm, d), lambda i: (i, 0)),
                  pl.BlockSpec((1, 9, d), lambda i: (i // tpb, 0, 0)),
                  pl.BlockSpec((6, d), lambda i: (0, 0))]
                 + [pl.BlockSpec((tm, BRANCH_WIDTH), lambda i: (i, 0))] * N_BRANCH
                 + [pl.BlockSpec(w_gate.shape, lambda i: (0, 0)),
                    pl.BlockSpec(w_branch.shape, lambda i: (0, 0, 0)),
                    pl.BlockSpec(w_out.shape, lambda i: (0, 0))],
        out_specs=pl.BlockSpec((tm, d), lambda i: (i, 0)),
        out_shape=jax.ShapeDtypeStruct((t, d), F32),
        compiler_params=_cparams(("arbitrary",)),
        name="merge",
    )(x2, mod_l, g_l, *ys, w_gate, w_branch, w_out)


def _mixer_weights(w_in_l):
    o = IN_OFFSETS
    col = lambda n: w_in_l[:, o[n]:o[n + 1]]
    z = lambda n: jnp.zeros((D_MODEL, n), w_in_l.dtype)

    def pad_heads(k):
        return jnp.concatenate([k[:, :64], z(64), k[:, 64:], z(64)], axis=1)

    a_q, a_k, a_v, i_q, i_k, i_w, r_x, r_g, c_q, c_kv, d_q, d_k, d_v = (col(n) for n in range(13))
    k_rope = c_kv[:, KV_LORA:]
    half = QK_ROPE // 2
    k_rope_rot = jnp.concatenate([-k_rope[:, half:], k_rope[:, :half]], axis=1)
    groups = [a_q, pad_heads(a_k), a_v,
              i_q, i_k, i_k,
              i_w, z(LANES - IDX_HEADS),
              r_x, r_g,
              c_q, c_kv[:, :KV_LORA], z(64), k_rope, z(32), z(64), k_rope_rot, z(32),
              d_q, pad_heads(d_k), d_v]
    w_all = jnp.concatenate(groups, axis=1).astype(BF16)
    w_gate = col(13).astype(BF16)
    return w_all, w_gate


def _kpos_table(s_len):
    pos = np.arange(s_len)
    tile = np.zeros((s_len, LANES), np.float32)
    for i in range(3):
        tile[:, HEAD_DIM + 2 * i] = pos // POS_SPLIT
        tile[:, HEAD_DIM + 2 * i + 1] = pos % POS_SPLIT
    return jnp.asarray(np.concatenate([tile, tile], axis=1))


def _mla_weights(w_uq, w_ukv):
    dq = QK_NOPE + QK_ROPE
    half = QK_ROPE // 2
    zq = lambda n: jnp.zeros((Q_LORA, n), w_uq.dtype)
    wq, wqr, wk, wv = [], [], [], []
    for h in range(C_HEADS):
        nope = w_uq[:, h * dq:h * dq + QK_NOPE]
        r1 = w_uq[:, h * dq + QK_NOPE:h * dq + QK_NOPE + half]
        r2 = w_uq[:, h * dq + QK_NOPE + half:(h + 1) * dq]
        wq += [nope, r1, r2, zq(32)]
        wqr += [zq(64), -r2, r1, zq(32)]
        wk += [w_ukv[:, h * 128:h * 128 + QK_NOPE], jnp.zeros((KV_LORA, 64), w_ukv.dtype)]
        wv += [w_ukv[:, h * 128 + QK_NOPE:(h + 1) * 128]]
    cat = lambda xs: jnp.concatenate(xs, axis=1).astype(BF16)
    return cat(wq), cat(wqr), cat(wk), cat(wv)


def _rope_tables(s_len):
    half = QK_ROPE // 2
    inv = ROPE_THETA ** (-jnp.arange(half, dtype=F32) / half)
    ang = jnp.arange(s_len, dtype=F32)[:, None] * inv[None, :]
    cos, sin = jnp.cos(ang), jnp.sin(ang)
    ones = lambda n: jnp.ones((s_len, n), F32)
    zeros = lambda n: jnp.zeros((s_len, n), F32)
    cos_t = jnp.concatenate([ones(64), cos, cos, ones(32)], axis=1)
    sin_t = jnp.concatenate([zeros(64), sin, sin, zeros(32)], axis=1)
    return cos_t, sin_t


def _block_diag(w):
    n, bw, _ = w.shape
    out = jnp.zeros((n * bw, n * bw), w.dtype)
    for k in range(n):
        out = lax.dynamic_update_slice(out, w[k], (k * bw, k * bw))
    return out.astype(BF16)


def kernel(x, c, w_ada, b_ada, norm_g, ffn1_w_in, ffn1_w_out, w_in, conv_w, conv_b, lru_wa, lru_ba, lru_wx,
           lru_bx, lru_lambda, mla_g_q, mla_g_kv, mla_w_uq, mla_w_ukv, swa_sinks, w_branch, w_out, ffn2_w_in,
           ffn2_w_out):
    batch, s_len, d = x.shape
    depth = w_ada.shape[0]
    t = batch * s_len
    assert s_len // POS_SPLIT <= 256
    x2 = x.reshape(t, d)
    c_pad = jnp.zeros((8, d), F32).at[:batch].set(c)
    mod_all = _ada_call(c_pad, w_ada, b_ada)
    cos_t, sin_t = _rope_tables(s_len)
    kpos = _kpos_table(s_len)
    row = lambda v: v.reshape(1, -1)
    for l in range(depth):
        mod_l = mod_all[l, :batch].reshape(batch, 9, d)
        g_l = norm_g[l]
        x2 = _ffn_call(x2, mod_l, g_l, ffn1_w_in, ffn1_w_out, l, sub=0, resid_w=0.5, s_len=s_len)
        w_all, w_gate = _mixer_weights(w_in[l])
        qkv_a, idx, iw, rm, cm, qkv_d = _proj_call(x2, mod_l, g_l, w_all, kpos, s_len=s_len)
        y_a = _dsa_call(qkv_a, idx, iw, batch=batch, s_len=s_len)
        y_b = _lru_call(rm, conv_w[l], row(conv_b[l]), _block_diag(lru_wa[l]), row(lru_ba[l]),
                        _block_diag(lru_wx[l]), row(lru_bx[l]), row(lru_lambda[l]), batch=batch, s_len=s_len)
        qm, km, vm = _mla_prep_call(cm, cos_t, sin_t, row(mla_g_q[l]), row(mla_g_kv[l]),
                                    *_mla_weights(mla_w_uq[l], mla_w_ukv[l]), s_len=s_len)
        y_c = _mla_attn_call(qm, km, vm, batch=batch, s_len=s_len)
        sinks_pad = jnp.zeros((1, LANES), F32).at[0, :D_HEADS].set(swa_sinks[l])
        y_d = _swa_call(qkv_d, sinks_pad, batch=batch, s_len=s_len)
        x2 = _merge_call(x2, mod_l, g_l, (y_a, y_b, y_c, y_d), w_gate, w_branch[l].astype(BF16),
                         w_out[l].astype(BF16), s_len=s_len)
        x2 = _ffn_call(x2, mod_l, g_l, ffn2_w_in, ffn2_w_out, l, sub=2, resid_w=0.5, s_len=s_len)
    return x2.reshape(batch, s_len, d)
```

```python
import functools

import numpy as np
import jax
import jax.numpy as jnp
from jax import lax
from jax.experimental import pallas as pl
from jax.experimental.pallas import tpu as pltpu

F32 = jnp.float32
BF16 = jnp.bfloat16
I32 = jnp.int32

D_MODEL = 1024
HEAD_DIM = 64
BLOCK = 128
EPS = 1e-6
NEG = -1e30
A_HEADS = 8
A_KV_HEADS = 2
IDX_HEADS = 8
IDX_DIM = 64
TOPK_MAX = 256
LRU_WIDTH = 512
LRU_BLOCKS = 8
CONV_WIDTH = 4
LRU_C = 8.0
C_HEADS = 8
Q_LORA = 256
KV_LORA = 128
QK_NOPE = 64
QK_ROPE = 32
V_DIM = 64
ROPE_THETA = 10000.0
D_HEADS = 8
D_KV_HEADS = 2
WINDOW = 128
N_BRANCH = 4
BRANCH_WIDTH = 512
N_ALIBI = A_HEADS + D_HEADS
D_FF = 2816
IN_SPLITS = (A_HEADS * HEAD_DIM, A_KV_HEADS * HEAD_DIM, A_KV_HEADS * HEAD_DIM,
             IDX_HEADS * IDX_DIM, IDX_DIM, IDX_HEADS,
             LRU_WIDTH, LRU_WIDTH,
             Q_LORA, KV_LORA + QK_ROPE,
             D_HEADS * HEAD_DIM, D_KV_HEADS * HEAD_DIM, D_KV_HEADS * HEAD_DIM,
             N_BRANCH * D_MODEL)
IN_OFFSETS = tuple(int(v) for v in np.concatenate([[0], np.cumsum(IN_SPLITS)]))

LANES = 128
INT_MIN = -2 ** 31
INT_MAX = 2 ** 31 - 1
VMEM_LIMIT = 56 * 1024 * 1024
DSA_VMEM_LIMIT = 62 * 1024 * 1024
LOG2E = 1.4426950408889634
POS_SPLIT = 64
SEARCH_GROUP = 4
LOOP_GROUP = 4
CNT_ROWS = 64

QKV_W = 512 + 256 + 128
IDX_W = 512 + 128
MLA_W = 640

FFN_TM = 1024
FFN_TF = 256
PROJ_TM = 512
DSA_TQ = 128
DSA_KC = 512
MLA_TQ = 512
MLA_KC = 512
SWA_BLOCKS = 4
LRU_L = 256
MERGE_TM = 512


def _alibi(i):
    return float(2.0 ** (-8.0 * i / N_ALIBI))


SLOPES_D = tuple(_alibi(i) for i in range(1, D_HEADS + 1))
SLOPES_A = tuple(_alibi(i) for i in range(D_HEADS + 1, N_ALIBI + 1))


def _bf16_parts(x):
    parts = []
    rem = np.float32(x)
    for _ in range(3):
        p = np.float32(np.asarray(rem, np.float32).astype(jnp.bfloat16).astype(np.float32))
        parts.append(float(p))
        rem = np.float32(rem - p)
    return parts


def _cparams(sem):
    return pltpu.CompilerParams(dimension_semantics=sem, vmem_limit_bytes=VMEM_LIMIT)


def _rms(x, g):
    return x * lax.rsqrt(jnp.mean(x * x, axis=-1, keepdims=True) + EPS) * g


def _nt_dot(a, b):
    return lax.dot_general(a, b, (((1,), (1,)), ((), ())), preferred_element_type=F32)


def _expm1(y):
    u = jnp.exp(y)
    safe = (u != 1.0) & (y > -1.0)
    ratio = y / jnp.log(jnp.where(safe, u, 2.0))
    return jnp.where(u == 1.0, y, jnp.where(safe, (u - 1.0) * ratio, u - 1.0))


def _loop_pairs(n, body):
    return _loop_groups(n, body, 0)


def _loop_groups(n, body, init):
    carry, start, width = init, 0, LOOP_GROUP
    while width >= 1:
        def group(i, carry, start=start, width=width):
            for u in range(width):
                carry = body(start + width * i + u, carry)
            return carry

        trips = lax.shift_right_logical(n - start, _log2(width))
        carry = lax.fori_loop(0, trips, group, carry)
        start = start + trips * width
        width //= 2
    return carry


def _log2(n):
    l = int(n).bit_length() - 1
    assert (1 << l) == n
    return l


def _ada_kernel(c_ref, w_ref, b_ref, o_ref):
    c = c_ref[...]
    sc = (c * jax.nn.sigmoid(c)).astype(BF16)
    o_ref[0] = jnp.dot(sc, w_ref[0].astype(BF16), preferred_element_type=F32) + b_ref[0]


def _ada_call(c_pad, w_ada, b_ada):
    depth, d, n = w_ada.shape
    tn = 1152
    return pl.pallas_call(
        _ada_kernel,
        grid=(depth, n // tn),
        in_specs=[pl.BlockSpec((c_pad.shape[0], d), lambda l, j: (0, 0)),
                  pl.BlockSpec((1, d, tn), lambda l, j: (l, 0, j)),
                  pl.BlockSpec((1, 1, tn), lambda l, j: (l, 0, j))],
        out_specs=pl.BlockSpec((1, c_pad.shape[0], tn), lambda l, j: (l, 0, j)),
        out_shape=jax.ShapeDtypeStruct((depth, c_pad.shape[0], n), F32),
        compiler_params=_cparams(("arbitrary", "arbitrary")),
        name="adaln",
    )(c_pad, w_ada, b_ada.reshape(depth, 1, n))


def _ffn_kernel(x_ref, mod_ref, g_ref, wg_ref, wu_ref, wo_ref, o_ref, h_ref, acc_ref, *, sub, resid_w):
    k = pl.program_id(1)
    m = mod_ref[0]

    @pl.when(k == 0)
    def _():
        x = x_ref[...]
        xn = _rms(x, g_ref[2 * sub:2 * sub + 1])
        h_ref[...] = (xn * (1.0 + m[3 * sub + 1:3 * sub + 2]) + m[3 * sub:3 * sub + 1]).astype(BF16)
        acc_ref[...] = jnp.zeros_like(acc_ref)

    h = h_ref[...]
    gate = jnp.dot(h, wg_ref[0].astype(BF16), preferred_element_type=F32)
    up = jnp.dot(h, wu_ref[0].astype(BF16), preferred_element_type=F32)
    act = (gate * jax.nn.sigmoid(gate) * up).astype(BF16)
    acc_ref[...] += jnp.dot(act, wo_ref[0].astype(BF16), preferred_element_type=F32)

    @pl.when(k == pl.num_programs(1) - 1)
    def _():
        yn = _rms(acc_ref[...], g_ref[2 * sub + 1:2 * sub + 2])
        o_ref[...] = x_ref[...] + resid_w * m[3 * sub + 2:3 * sub + 3] * yn


def _ffn_call(x2, mod_l, g_l, w_in, w_out, layer, *, sub, resid_w, s_len):
    t, d = x2.shape
    f = w_out.shape[1]
    tm = min(FFN_TM, s_len)
    tf = FFN_TF
    nf = f // tf
    tpb = s_len // tm
    return pl.pallas_call(
        functools.partial(_ffn_kernel, sub=sub, resid_w=resid_w),
        grid=(t // tm, nf),
        in_specs=[pl.BlockSpec((tm, d), lambda i, k: (i, 0)),
                  pl.BlockSpec((1, 9, d), lambda i, k: (i // tpb, 0, 0)),
                  pl.BlockSpec((6, d), lambda i, k: (0, 0)),
                  pl.BlockSpec((1, d, tf), lambda i, k: (layer, 0, k)),
                  pl.BlockSpec((1, d, tf), lambda i, k: (layer, 0, k + nf)),
                  pl.BlockSpec((1, tf, d), lambda i, k: (layer, k, 0))],
        out_specs=pl.BlockSpec((tm, d), lambda i, k: (i, 0)),
        out_shape=jax.ShapeDtypeStruct((t, d), F32),
        scratch_shapes=[pltpu.VMEM((tm, d), BF16), pltpu.VMEM((tm, d), F32)],
        compiler_params=_cparams(("arbitrary", "arbitrary")),
        name="ffn",
    )(x2, mod_l, g_l, w_in, w_in, w_out)


def _proj_kernel(x_ref, mod_ref, g_ref, w_ref, kpos_ref, oa_ref, oi_ref, ow_ref, or_ref, oc_ref, od_ref):
    m = mod_ref[0]
    xn = _rms(x_ref[...], g_ref[2:3])
    h = (xn * (1.0 + m[4:5]) + m[3:4]).astype(BF16)
    off = 0
    for ref in (oa_ref, oi_ref, ow_ref, or_ref, oc_ref, od_ref):
        w = ref.shape[1]
        z = jnp.dot(h, w_ref[:, off:off + w], preferred_element_type=F32)
        if ref is oa_ref or ref is od_ref:
            qscale = (HEAD_DIM ** -0.5) * (LOG2E if ref is oa_ref else 1.0)
            z = jnp.concatenate([z[:, :512] * qscale, z[:, 512:768] + kpos_ref[...], z[:, 768:]], axis=1)
        ref[...] = z.astype(ref.dtype)
        off += w


def _proj_call(x2, mod_l, g_l, w_all, kpos, *, s_len):
    t, d = x2.shape
    tm = min(PROJ_TM, s_len)
    tpb = s_len // tm
    widths = (QKV_W, IDX_W, LANES, 2 * LRU_WIDTH, MLA_W, QKV_W)
    dtypes = (BF16, BF16, F32, F32, F32, BF16)
    assert sum(widths) == w_all.shape[1]
    return pl.pallas_call(
        _proj_kernel,
        grid=(t // tm,),
        in_specs=[pl.BlockSpec((tm, d), lambda i: (i, 0)),
                  pl.BlockSpec((1, 9, d), lambda i: (i // tpb, 0, 0)),
                  pl.BlockSpec((6, d), lambda i: (0, 0)),
                  pl.BlockSpec(w_all.shape, lambda i: (0, 0)),
                  pl.BlockSpec((tm, 256), lambda i: (i % tpb, 0))],
        out_specs=[pl.BlockSpec((tm, w), lambda i: (i, 0)) for w in widths],
        out_shape=[jax.ShapeDtypeStruct((t, w), dt) for w, dt in zip(widths, dtypes)],
        compiler_params=_cparams(("arbitrary",)),
        name="mixer_proj",
    )(x2, mod_l, g_l, w_all, kpos)


def _head_rows(q, n_heads, tq):
    lane = lax.broadcasted_iota(I32, (tq, LANES), 1)
    rows = []
    for h in range(n_heads):
        tile = q[:, LANES * (h // 2):LANES * (h // 2 + 1)].astype(F32)
        keep = (lane >= HEAD_DIM) if (h % 2) else (lane < HEAD_DIM)
        rows.append(jnp.where(keep, tile, 0.0).astype(BF16))
    return jnp.concatenate(rows, axis=0)


def _alibi_lanes(slopes, tq):
    lane = lax.broadcasted_iota(I32, (tq, LANES), 1)
    out = []
    for slope in slopes:
        tile = jnp.zeros((tq, LANES), F32)
        for i, part in enumerate(_bf16_parts(slope)):
            tile = jnp.where(lane == HEAD_DIM + 2 * i, POS_SPLIT * part, tile)
            tile = jnp.where(lane == HEAD_DIM + 2 * i + 1, part, tile)
        out.append(tile)
    return out


def _alibi_q_tiles(q, alibi_lanes, tq):
    lane = lax.broadcasted_iota(I32, (tq, LANES), 1)
    rows = []
    for h, al in enumerate(alibi_lanes):
        tile = q[:, LANES * (h // 2):LANES * (h // 2 + 1)].astype(F32)
        if h % 2:
            tile = pltpu.roll(tile, HEAD_DIM, axis=1)
        rows.append(jnp.where(lane < HEAD_DIM, tile, al).astype(BF16))
    return jnp.concatenate(rows, axis=0)


def _gqa_out(accs, inv_ls, tq):
    lane = lax.broadcasted_iota(I32, (tq, LANES), 1)
    tiles = []
    for j in range(4):
        g = j // 2
        halves = []
        for h in (2 * j, 2 * j + 1):
            o = accs[g][h % 4] * inv_ls[g][h % 4]
            src_hi = (g == 1)
            dst_hi = (h % 2 == 1)
            if src_hi != dst_hi:
                o = pltpu.roll(o, HEAD_DIM, axis=1)
            halves.append(o)
        tiles.append(jnp.where(lane < HEAD_DIM, halves[0], halves[1]))
    return jnp.concatenate(tiles, axis=1)


def _dsa_kernel(q_ref, k_ref, v_ref, iq_ref, ik_ref, iw_ref, o_ref,
                key_ref, hi_ref, s_ref, tau_ref, iqs_ref, qs_ref, m_ref, l_ref, acc_ref, *, tq, kc, topk, pos_bits):
    qb = pl.program_id(1)
    q0 = qb * tq
    nk = lax.shift_right_logical(q0 + tq + kc - 1, _log2(kc))
    n_lt = kc // LANES
    n_rt = kc // CNT_ROWS
    qpos_l = q0 + lax.broadcasted_iota(I32, (kc, tq), 1)
    krow = lax.broadcasted_iota(I32, (kc, tq), 0)
    krow_t = lax.broadcasted_iota(I32, (CNT_ROWS, tq), 0)

    iqs_ref[...] = _head_rows(iq_ref[...], IDX_HEADS, tq)
    iw_t = iw_ref[...].T

    def score_body(j, carry):
        ks = pl.multiple_of(j * kc, kc)
        d = _nt_dot(ik_ref[pl.ds(ks, kc), :], iqs_ref[...])
        acc = jnp.zeros((kc, tq), F32)
        for h in range(IDX_HEADS):
            acc = acc + iw_t[h:h + 1, :] * jnp.maximum(d[:, h * tq:(h + 1) * tq], 0.0)
        bits = pltpu.bitcast(acc, I32)
        key = jnp.where(bits < 0, bits ^ INT_MAX, bits)
        key = jnp.where(ks + krow <= qpos_l, key, INT_MIN)
        key_ref[j] = key
        hi_ref[j] = lax.shift_right_arithmetic(key, 16).astype(jnp.int16)
        return carry

    _loop_pairs(nk, score_body)

    def count(pred):
        def body(j, cnt):
            for r in range(n_rt):
                kp = (j * kc + r * CNT_ROWS) + krow_t
                cnt = cnt + jnp.where(pred(key_ref[j, r * CNT_ROWS:(r + 1) * CNT_ROWS, :], kp), 1.0, 0.0)
            return cnt
        cnt = _loop_groups(nk, body, jnp.zeros((CNT_ROWS, tq), F32))
        return jnp.sum(cnt, axis=0, keepdims=True)

    kf = float(topk)
    n_valid = (q0 + 1 + lax.broadcasted_iota(I32, (1, tq), 1)).astype(F32)
    settled0 = n_valid <= kf

    def count_hi(cand):
        cb = jnp.broadcast_to(lax.shift_right_arithmetic(cand, 16), (CNT_ROWS, tq)).astype(jnp.int16)

        def body(j, cnt):
            for r in range(n_rt):
                hj = hi_ref[j, r * CNT_ROWS:(r + 1) * CNT_ROWS, :]
                cnt = cnt + jnp.where(hj >= cb, jnp.int16(1), jnp.int16(0))
            return cnt
        cnt = _loop_groups(nk, body, jnp.zeros((CNT_ROWS, tq), jnp.int16))
        return jnp.sum(cnt.astype(F32), axis=0, keepdims=True)

    def count_full(cand):
        cb = jnp.broadcast_to(cand, (CNT_ROWS, tq))
        return count(lambda kv, kp: kv >= cb)

    def bit_step(i, tau, cnt, count_fn):
        cand = tau + lax.shift_left(jnp.int32(1), 31 - i)
        c = count_fn(cand)
        ok = c >= kf
        return jnp.where(ok, cand, tau), jnp.where(ok, c, cnt)

    tau, n_ge = lax.fori_loop(0, 16, lambda i, st: bit_step(i, st[0], st[1], count_hi),
                              (jnp.full((1, tq), INT_MIN, I32), jnp.broadcast_to((nk * kc).astype(F32), (1, tq))))

    def all_settled(cnt):
        return jnp.min(jnp.where(settled0 | (cnt == kf), 1, 0))

    def low_cond(state):
        i, _, _, done = state
        return (i < 32) & (done == 0)

    def low_body(state):
        i, tau, cnt, _ = state
        for b in range(SEARCH_GROUP):
            tau, cnt = bit_step(i + b, tau, cnt, count_full)
        return i + SEARCH_GROUP, tau, cnt, all_settled(cnt)

    _, tau, n_ge, _ = lax.while_loop(low_cond, low_body, (jnp.int32(16), tau, n_ge, all_settled(n_ge)))

    tie_f = jnp.where((n_ge > kf) & (tau > INT_MIN), 1.0, 0.0)
    any_tie = jnp.max(tie_f) > 0.0
    tau_c = jnp.maximum(tau, INT_MIN + 1)
    tau_ref[...] = jnp.broadcast_to(tau_c, tau_ref.shape)

    @pl.when(any_tie)
    def _():
        tb = jnp.broadcast_to(tau_c, (CNT_ROWS, tq))
        need = kf - count(lambda kv, kp: kv > tb)

        def pos_body(i, p):
            cand = p | lax.shift_left(jnp.int32(1), pos_bits - 1 - i)
            cb = jnp.broadcast_to(cand, (CNT_ROWS, tq))
            below = count(lambda kv, kp: (kv == tb) & (kp < cb))
            return jnp.where(below < need, cand, p)

        p = lax.fori_loop(0, pos_bits, pos_body, jnp.zeros((1, tq), I32))
        p = jnp.where(tie_f > 0.0, p, INT_MAX)

        def rewrite_body(j, carry):
            kj = key_ref[j]
            kp = j * kc + krow
            sel = (kj > tau_c) | ((kj == tau_c) & (kp <= p))
            key_ref[j] = jnp.where(sel, 1, -1)
            return carry

        lax.fori_loop(0, nk, rewrite_body, 0)
        tau_ref[...] = jnp.zeros(tau_ref.shape, I32)

    q = q_ref[...]
    for g in range(A_KV_HEADS):
        qs_ref[g] = _alibi_q_tiles(q[:, 4 * HEAD_DIM * g:4 * HEAD_DIM * (g + 1)],
                                   _alibi_lanes([s * LOG2E for s in SLOPES_A[4 * g:4 * g + 4]], tq), tq)
    m_ref[...] = jnp.full(m_ref.shape, NEG, F32)
    l_ref[...] = jnp.zeros(l_ref.shape, F32)
    acc_ref[...] = jnp.zeros(acc_ref.shape, F32)

    def max_body(j, carry):
        ks = pl.multiple_of(j * kc, kc)
        sel = jnp.where(key_ref[j] >= tau_ref[0:1, :], 1.0, 0.0).T > 0.5
        for g in range(A_KV_HEADS):
            s = _nt_dot(qs_ref[g], k_ref[pl.ds(ks, kc), g * LANES:(g + 1) * LANES])
            for h in range(4):
                sh = jnp.where(sel, s[h * tq:(h + 1) * tq], NEG)
                s_ref[g, h, j] = sh
                mp = m_ref[g, h]
                for c in range(n_lt):
                    mp = jnp.maximum(mp, sh[:, c * LANES:(c + 1) * LANES])
                m_ref[g, h] = mp
        return carry

    _loop_pairs(nk, max_body)
    for g in range(A_KV_HEADS):
        for h in range(4):
            m_ref[g, h] = jnp.broadcast_to(jnp.max(m_ref[g, h], axis=1, keepdims=True), (tq, LANES))

    def pv_body(j, carry):
        ks = pl.multiple_of(j * kc, kc)
        vch = v_ref[pl.ds(ks, kc), :]
        for g in range(A_KV_HEADS):
            ps = []
            for h in range(4):
                mb = m_ref[g, h]
                lp = l_ref[g, h]
                tiles = []
                for c in range(n_lt):
                    p = jnp.exp2(s_ref[g, h, j, :, c * LANES:(c + 1) * LANES] - mb)
                    lp = lp + p
                    tiles.append(p.astype(BF16))
                l_ref[g, h] = lp
                ps.append(jnp.concatenate(tiles, axis=1))
            pv = jnp.dot(jnp.concatenate(ps, axis=0), vch, preferred_element_type=F32)
            for h in range(4):
                acc_ref[g, h] += pv[h * tq:(h + 1) * tq]
        return carry

    _loop_pairs(nk, pv_body)
    accs = [[acc_ref[g, h] for h in range(4)] for g in range(A_KV_HEADS)]
    inv_ls = [[1.0 / jnp.sum(l_ref[g, h], axis=1, keepdims=True) for h in range(4)] for g in range(A_KV_HEADS)]
    o_ref[...] = _gqa_out(accs, inv_ls, tq).astype(o_ref.dtype)


def _dsa_call(qkv, idx, iw, *, batch, s_len):
    t = qkv.shape[0]
    tq = min(DSA_TQ, s_len)
    kc = min(DSA_KC, s_len)
    topk = min(TOPK_MAX, s_len // 4)
    assert kc >= topk and kc % tq == 0
    nqb = s_len // tq
    once = pl.Buffered(1)
    return pl.pallas_call(
        functools.partial(_dsa_kernel, tq=tq, kc=kc, topk=topk, pos_bits=_log2(s_len)),
        grid=(batch, nqb),
        in_specs=[pl.BlockSpec((tq, 512), lambda b, i: (b * nqb + i, 0)),
                  pl.BlockSpec((s_len, 256), lambda b, i: (b, 2), pipeline_mode=once),
                  pl.BlockSpec((s_len, 128), lambda b, i: (b, 6), pipeline_mode=once),
                  pl.BlockSpec((tq, 512), lambda b, i: (b * nqb + i, 0)),
                  pl.BlockSpec((s_len, 128), lambda b, i: (b, 4), pipeline_mode=once),
                  pl.BlockSpec((tq, LANES), lambda b, i: (b * nqb + i, 0))],
        out_specs=pl.BlockSpec((tq, 512), lambda b, i: (b * nqb + i, 0)),
        out_shape=jax.ShapeDtypeStruct((t, 512), BF16),
        scratch_shapes=[pltpu.VMEM((s_len // kc, kc, tq), I32),
                        pltpu.VMEM((s_len // kc, kc, tq), jnp.int16),
                        pltpu.VMEM((A_KV_HEADS, 4, s_len // kc, tq, kc), F32),
                        pltpu.VMEM((8, tq), I32),
                        pltpu.VMEM((IDX_HEADS * tq, LANES), BF16),
                        pltpu.VMEM((A_KV_HEADS, 4 * tq, LANES), BF16),
                        pltpu.VMEM((A_KV_HEADS, 4, tq, LANES), F32),
                        pltpu.VMEM((A_KV_HEADS, 4, tq, LANES), F32),
                        pltpu.VMEM((A_KV_HEADS, 4, tq, LANES), F32)],
        compiler_params=pltpu.CompilerParams(dimension_semantics=("arbitrary", "arbitrary"),
                                             vmem_limit_bytes=DSA_VMEM_LIMIT),
        name="dsa",
    )(qkv, qkv, qkv, idx, idx, iw)


def _swa_kernel(q_ref, kp_ref, kc_ref, vp_ref, vc_ref, sink_ref, o_ref, *, tq, nblk):
    i = pl.program_id(1)
    krow = lax.broadcasted_iota(I32, (2 * tq, tq), 0)
    qcol = lax.broadcasted_iota(I32, (2 * tq, tq), 1)
    dist = qcol + tq - krow
    in_band = (dist >= 0) & (dist < WINDOW)
    sinks = sink_ref[...]
    sink_vec = [jnp.broadcast_to(sinks[:, h:h + 1], (1, tq)) for h in range(D_HEADS)]
    alibi = _alibi_lanes(SLOPES_D, tq)
    for n in range(nblk):
        blk = i * nblk + n
        rows = slice(n * tq, (n + 1) * tq)
        q = q_ref[rows, :]
        k_prev = kp_ref[...] if n == 0 else kc_ref[(n - 1) * tq:n * tq, :]
        v_prev = vp_ref[...] if n == 0 else vc_ref[(n - 1) * tq:n * tq, :]
        kk = jnp.concatenate([k_prev, kc_ref[rows, :]], axis=0)
        vv_t = jnp.concatenate([v_prev, vc_ref[rows, :]], axis=0).T
        valid = in_band & ((blk * tq - tq + krow) >= 0)
        qpos = (blk * tq + lax.broadcasted_iota(I32, (1, tq), 1)).astype(F32)
        tiles = []
        for g in range(D_KV_HEADS):
            qs = _alibi_q_tiles(q[:, 4 * HEAD_DIM * g:4 * HEAD_DIM * (g + 1)], alibi[4 * g:4 * g + 4], tq)
            s_t = _nt_dot(kk[:, g * LANES:(g + 1) * LANES], qs)
            ps, ils = [], []
            for h in range(4):
                sh = jnp.where(valid, s_t[:, h * tq:(h + 1) * tq], NEG)
                sink = sink_vec[4 * g + h] + SLOPES_D[4 * g + h] * qpos
                m = jnp.maximum(jnp.max(sh, axis=0, keepdims=True), sink)
                p = jnp.exp(sh - m)
                ils.append(1.0 / (jnp.sum(p, axis=0, keepdims=True) + jnp.exp(sink - m)))
                ps.append(p.astype(BF16))
            o_t = jnp.dot(vv_t, jnp.concatenate(ps, axis=1), preferred_element_type=F32)
            o_t = o_t[g * HEAD_DIM:(g + 1) * HEAD_DIM, :] * jnp.concatenate(ils, axis=1)
            for j in range(2):
                pair = jnp.concatenate([o_t[:, (2 * j) * tq:(2 * j + 1) * tq],
                                        o_t[:, (2 * j + 1) * tq:(2 * j + 2) * tq]], axis=0)
                tiles.append(pair.T)
        o_ref[rows, :] = jnp.concatenate(tiles, axis=1).astype(o_ref.dtype)


def _swa_call(qkv, sinks_pad, *, batch, s_len):
    t = qkv.shape[0]
    tq = BLOCK
    nblk = min(SWA_BLOCKS, s_len // tq)
    ts = nblk * tq
    nst = s_len // ts
    cur = lambda b, i: b * nst + i
    prev = lambda b, i: jnp.maximum((b * nst + i) * nblk - 1, 0)
    return pl.pallas_call(
        functools.partial(_swa_kernel, tq=tq, nblk=nblk),
        grid=(batch, nst),
        in_specs=[pl.BlockSpec((ts, 512), lambda b, i: (cur(b, i), 0)),
                  pl.BlockSpec((tq, 256), lambda b, i: (prev(b, i), 2)),
                  pl.BlockSpec((ts, 256), lambda b, i: (cur(b, i), 2)),
                  pl.BlockSpec((tq, 128), lambda b, i: (prev(b, i), 6)),
                  pl.BlockSpec((ts, 128), lambda b, i: (cur(b, i), 6)),
                  pl.BlockSpec((1, LANES), lambda b, i: (0, 0))],
        out_specs=pl.BlockSpec((ts, 512), lambda b, i: (cur(b, i), 0)),
        out_shape=jax.ShapeDtypeStruct((t, 512), BF16),
        compiler_params=_cparams(("arbitrary", "arbitrary")),
        name="swa",
    )(qkv, qkv, qkv, qkv, qkv, sinks_pad)


def _mla_prep_kernel(c_ref, cos_ref, sin_ref, gq_ref, gkv_ref, wq_ref, wqr_ref, wk_ref, wv_ref,
                     q_ref, k_ref, v_ref):
    c = c_ref[...]
    cos = cos_ref[...]
    sin = sin_ref[...]
    cos8 = jnp.concatenate([cos] * C_HEADS, axis=1)
    sin8 = jnp.concatenate([sin] * C_HEADS, axis=1)
    cqn = _rms(c[:, :Q_LORA], gq_ref[...]).astype(BF16)
    q = (jnp.dot(cqn, wq_ref[...], preferred_element_type=F32) * cos8
         + jnp.dot(cqn, wqr_ref[...], preferred_element_type=F32) * sin8)
    q_ref[...] = q.astype(BF16)
    ckvn = _rms(c[:, Q_LORA:Q_LORA + KV_LORA], gkv_ref[...]).astype(BF16)
    kr = c[:, 384:512] * cos + c[:, 512:640] * sin
    k = jnp.dot(ckvn, wk_ref[...], preferred_element_type=F32) + jnp.concatenate([kr] * C_HEADS, axis=1)
    k_ref[...] = k.astype(BF16)
    v_ref[...] = jnp.dot(ckvn, wv_ref[...], preferred_element_type=F32).astype(BF16)


def _mla_prep_call(cm, cos_t, sin_t, gq, gkv, wq, wqr, wk, wv, *, s_len):
    t = cm.shape[0]
    tm = min(PROJ_TM, s_len)
    tpb = s_len // tm
    full = lambda a: pl.BlockSpec(a.shape, lambda i: (0, 0))
    return pl.pallas_call(
        _mla_prep_kernel,
        grid=(t // tm,),
        in_specs=[pl.BlockSpec((tm, MLA_W), lambda i: (i, 0)),
                  pl.BlockSpec((tm, LANES), lambda i: (i % tpb, 0)),
                  pl.BlockSpec((tm, LANES), lambda i: (i % tpb, 0)),
                  full(gq), full(gkv), full(wq), full(wqr), full(wk), full(wv)],
        out_specs=[pl.BlockSpec((tm, 1024), lambda i: (i, 0)),
                   pl.BlockSpec((tm, 1024), lambda i: (i, 0)),
                   pl.BlockSpec((tm, 512), lambda i: (i, 0))],
        out_shape=[jax.ShapeDtypeStruct((t, 1024), BF16),
                   jax.ShapeDtypeStruct((t, 1024), BF16),
                   jax.ShapeDtypeStruct((t, 512), BF16)],
        compiler_params=_cparams(("arbitrary",)),
        name="mla_prep",
    )(cm, cos_t, sin_t, gq, gkv, wq, wqr, wk, wv)


def _mla_attn_kernel(q_ref, k_ref, v_ref, o_ref, s_ref, m_ref, l_ref, acc_ref, *, tq, kc):
    qb = pl.program_id(2)
    q0 = qb * tq
    n_full = lax.shift_right_logical(q0, _log2(kc))
    rowpos = q0 + lax.broadcasted_iota(I32, (tq, kc), 0)
    lane = lax.broadcasted_iota(I32, (tq, kc), 1)
    c = ((QK_NOPE + QK_ROPE) ** -0.5) * LOG2E
    n_lt = kc // LANES
    m_ref[...] = jnp.full(m_ref.shape, NEG, F32)
    l_ref[...] = jnp.zeros(l_ref.shape, F32)
    acc_ref[...] = jnp.zeros(acc_ref.shape, F32)

    def max_step(j, masked):
        ks = pl.multiple_of(j * kc, kc)
        for hh in range(2):
            s = _nt_dot(q_ref[:, hh * LANES:(hh + 1) * LANES],
                        k_ref[pl.ds(ks, kc), hh * LANES:(hh + 1) * LANES]) * c
            if masked:
                s = jnp.where(ks + lane <= rowpos, s, NEG)
            s_ref[hh, j] = s
            mp = m_ref[hh]
            for t in range(n_lt):
                mp = jnp.maximum(mp, s[:, t * LANES:(t + 1) * LANES])
            m_ref[hh] = mp

    def max_body(j, carry):
        max_step(j, False)
        return carry

    _loop_pairs(n_full, max_body)
    max_step(n_full, True)
    for hh in range(2):
        m_ref[hh] = jnp.broadcast_to(jnp.max(m_ref[hh], axis=1, keepdims=True), (tq, LANES))

    def pv_body(j, carry):
        ks = pl.multiple_of(j * kc, kc)
        vch = v_ref[pl.ds(ks, kc), :]
        for hh in range(2):
            mb = m_ref[hh]
            lp = l_ref[hh]
            tiles = []
            for t in range(n_lt):
                p = jnp.exp2(s_ref[hh, j, :, t * LANES:(t + 1) * LANES] - mb)
                lp = lp + p
                tiles.append(p.astype(BF16))
            l_ref[hh] = lp
            acc_ref[hh] += jnp.dot(jnp.concatenate(tiles, axis=1), vch, preferred_element_type=F32)
        return carry

    _loop_pairs(n_full + 1, pv_body)
    outs = [acc_ref[hh] * (1.0 / jnp.sum(l_ref[hh], axis=1, keepdims=True)) for hh in range(2)]
    lane_o = lax.broadcasted_iota(I32, (tq, LANES), 1)
    o_ref[...] = jnp.where(lane_o < V_DIM, outs[0], outs[1]).astype(o_ref.dtype)


def _mla_attn_call(qm, km, vm, *, batch, s_len):
    t = qm.shape[0]
    tq = min(MLA_TQ, s_len)
    kc = min(MLA_KC, s_len)
    assert kc % tq == 0
    nqb = s_len // tq
    once = pl.Buffered(1)
    return pl.pallas_call(
        functools.partial(_mla_attn_kernel, tq=tq, kc=kc),
        grid=(batch, C_HEADS // 2, nqb),
        in_specs=[pl.BlockSpec((tq, 256), lambda b, h, i: (b * nqb + i, h)),
                  pl.BlockSpec((s_len, 256), lambda b, h, i: (b, h), pipeline_mode=once),
                  pl.BlockSpec((s_len, 128), lambda b, h, i: (b, h), pipeline_mode=once)],
        out_specs=pl.BlockSpec((tq, 128), lambda b, h, i: (b * nqb + i, h)),
        out_shape=jax.ShapeDtypeStruct((t, 512), BF16),
        scratch_shapes=[pltpu.VMEM((2, s_len // kc, tq, kc), F32)] + [pltpu.VMEM((2, tq, LANES), F32)] * 3,
        compiler_params=_cparams(("arbitrary", "arbitrary", "arbitrary")),
        name="mla_attn",
    )(qm, km, vm)


def _lru_kernel(xr_ref, xg_ref, prev_ref, cw_ref, cb_ref, wa_ref, ba_ref, wx_ref, bx_ref, lam_ref,
                o_ref, xe_ref, h_ref, *, ln):
    i = pl.program_id(1)
    xe_ref[0:8, :] = jnp.where(i > 0, prev_ref[...], 0.0)
    xe_ref[8:8 + ln, :] = xr_ref[...]
    cw = cw_ref[...]
    xc = cb_ref[...] + cw[0:1] * xe_ref[pl.ds(5, ln), :]
    for j in range(1, CONV_WIDTH):
        xc = xc + cw[j:j + 1] * xe_ref[pl.ds(5 + j, ln), :]
    xcb = xc.astype(BF16)
    r = jax.nn.sigmoid(jnp.dot(xcb, wa_ref[...], preferred_element_type=F32) + ba_ref[...])
    gi = jax.nn.sigmoid(jnp.dot(xcb, wx_ref[...], preferred_element_type=F32) + bx_ref[...])
    z = -lam_ref[...]
    softplus = jnp.maximum(z, 0.0) + jnp.log1p(jnp.exp(-jnp.abs(z)))
    log_a = -LRU_C * r * softplus
    a = jnp.exp(log_a)
    b = jnp.sqrt(-_expm1(2.0 * log_a)) * (gi * xc)
    row = lax.broadcasted_iota(I32, (ln, LRU_WIDTH), 0)
    d = 1
    while d < ln:
        keep = row >= d
        b = jnp.where(keep, a * pltpu.roll(b, d, axis=0) + b, b)
        a = jnp.where(keep, a * pltpu.roll(a, d, axis=0), a)
        d *= 2
    h_prev = jnp.where(i > 0, h_ref[0:1, :], 0.0)
    h = a * h_prev + b
    h_ref[0:1, :] = h[ln - 1:ln, :]
    xg = xg_ref[...]
    gelu = 0.5 * xg * (1.0 + jnp.tanh(0.7978845608028654 * (xg + 0.044715 * (xg * xg * xg))))
    o_ref[...] = (h * gelu).astype(o_ref.dtype)


def _lru_call(rm, cw, cb, wa, ba, wx, bx, lam, *, batch, s_len):
    t = rm.shape[0]
    ln = min(LRU_L, s_len)
    nt = s_len // ln
    w = LRU_WIDTH
    vec = lambda a: pl.BlockSpec(a.shape, lambda b, i: (0, 0))
    return pl.pallas_call(
        functools.partial(_lru_kernel, ln=ln),
        grid=(batch, nt),
        in_specs=[pl.BlockSpec((ln, w), lambda b, i: (b * nt + i, 0)),
                  pl.BlockSpec((ln, w), lambda b, i: (b * nt + i, 1)),
                  pl.BlockSpec((8, w), lambda b, i: (jnp.maximum((b * nt + i) * (ln // 8) - 1, 0), 0)),
                  vec(cw), vec(cb), vec(wa), vec(ba), vec(wx), vec(bx), vec(lam)],
        out_specs=pl.BlockSpec((ln, w), lambda b, i: (b * nt + i, 0)),
        out_shape=jax.ShapeDtypeStruct((t, w), BF16),
        scratch_shapes=[pltpu.VMEM((ln + 8, w), F32), pltpu.VMEM((8, w), F32)],
        compiler_params=_cparams(("arbitrary", "arbitrary")),
        name="rglru",
    )(rm, rm, rm, cw, cb, wa, ba, wx, bx, lam)


def _merge_kernel(x_ref, mod_ref, g_ref, ya_ref, yb_ref, yc_ref, yd_ref, wg_ref, wb_ref, wo_ref, o_ref):
    m = mod_ref[0]
    x = x_ref[...]
    h = (_rms(x, g_ref[2:3]) * (1.0 + m[4:5]) + m[3:4]).astype(BF16)
    merged = None
    for n, y_ref in enumerate((ya_ref, yb_ref, yc_ref, yd_ref)):
        gate = jax.nn.sigmoid(jnp.dot(h, wg_ref[:, n * D_MODEL:(n + 1) * D_MODEL], preferred_element_type=F32))
        term = gate * jnp.dot(y_ref[...], wb_ref[n], preferred_element_type=F32)
        merged = term if merged is None else merged + term
    y = jnp.dot(merged.astype(BF16), wo_ref[...], preferred_element_type=F32)
    o_ref[...] = x + m[5:6] * _rms(y, g_ref[3:4])


def _merge_call(x2, mod_l, g_l, ys, w_gate, w_branch, w_out, *, s_len):
    t, d = x2.shape
    tm = min(MERGE_TM, s_len)
    tpb = s_len // tm
    return pl.pallas_call(
        _merge_kernel,
        grid=(t // tm,),
        in_specs=[pl.BlockSpec((tm, d), lambda i: (i, 0)),
                  pl.BlockSpec((1, 9, d), lambda i: (i // tpb, 0, 0)),
                  pl.BlockSpec((6, d), lambda i: (0, 0))]
                 + [pl.BlockSpec((tm, BRANCH_WIDTH), lambda i: (i, 0))] * N_BRANCH
                 + [pl.BlockSpec(w_gate.shape, lambda i: (0, 0)),
                    pl.BlockSpec(w_branch.shape, lambda i: (0, 0, 0)),
                    pl.BlockSpec(w_out.shape, lambda i: (0, 0))],
        out_specs=pl.BlockSpec((tm, d), lambda i: (i, 0)),
        out_shape=jax.ShapeDtypeStruct((t, d), F32),
        compiler_params=_cparams(("arbitrary",)),
        name="merge",
    )(x2, mod_l, g_l, *ys, w_gate, w_branch, w_out)


def _mixer_weights(w_in_l):
    o = IN_OFFSETS
    col = lambda n: w_in_l[:, o[n]:o[n + 1]]
    z = lambda n: jnp.zeros((D_MODEL, n), w_in_l.dtype)

    def pad_heads(k):
        return jnp.concatenate([k[:, :64], z(64), k[:, 64:], z(64)], axis=1)

    a_q, a_k, a_v, i_q, i_k, i_w, r_x, r_g, c_q, c_kv, d_q, d_k, d_v = (col(n) for n in range(13))
    k_rope = c_kv[:, KV_LORA:]
    half = QK_ROPE // 2
    k_rope_rot = jnp.concatenate([-k_rope[:, half:], k_rope[:, :half]], axis=1)
    groups = [a_q, pad_heads(a_k), a_v,
              i_q, i_k, i_k,
              i_w, z(LANES - IDX_HEADS),
              r_x, r_g,
              c_q, c_kv[:, :KV_LORA], z(64), k_rope, z(32), z(64), k_rope_rot, z(32),
              d_q, pad_heads(d_k), d_v]
    w_all = jnp.concatenate(groups, axis=1).astype(BF16)
    w_gate = col(13).astype(BF16)
    return w_all, w_gate


def _kpos_table(s_len):
    pos = np.arange(s_len)
    tile = np.zeros((s_len, LANES), np.float32)
    for i in range(3):
        tile[:, HEAD_DIM + 2 * i] = pos // POS_SPLIT
        tile[:, HEAD_DIM + 2 * i + 1] = pos % POS_SPLIT
    return jnp.asarray(np.concatenate([tile, tile], axis=1))


def _mla_weights(w_uq, w_ukv):
    dq = QK_NOPE + QK_ROPE
    half = QK_ROPE // 2
    zq = lambda n: jnp.zeros((Q_LORA, n), w_uq.dtype)
    wq, wqr, wk, wv = [], [], [], []
    for h in range(C_HEADS):
        nope = w_uq[:, h * dq:h * dq + QK_NOPE]
        r1 = w_uq[:, h * dq + QK_NOPE:h * dq + QK_NOPE + half]
        r2 = w_uq[:, h * dq + QK_NOPE + half:(h + 1) * dq]
        wq += [nope, r1, r2, zq(32)]
        wqr += [zq(64), -r2, r1, zq(32)]
        wk += [w_ukv[:, h * 128:h * 128 + QK_NOPE], jnp.zeros((KV_LORA, 64), w_ukv.dtype)]
        wv += [w_ukv[:, h * 128 + QK_NOPE:(h + 1) * 128]]
    cat = lambda xs: jnp.concatenate(xs, axis=1).astype(BF16)
    return cat(wq), cat(wqr), cat(wk), cat(wv)


def _rope_tables(s_len):
    half = QK_ROPE // 2
    inv = (np.float32(ROPE_THETA) ** (-np.arange(half, dtype=np.float32) / np.float32(half))).astype(np.float32)
    ang = (np.arange(s_len, dtype=np.float32)[:, None] * inv[None, :]).astype(np.float32)
    cos, sin = np.cos(ang.astype(np.float64)).astype(np.float32), np.sin(ang.astype(np.float64)).astype(np.float32)
    ones = lambda n: np.ones((s_len, n), np.float32)
    zeros = lambda n: np.zeros((s_len, n), np.float32)
    cos_t = np.concatenate([ones(64), cos, cos, ones(32)], axis=1)
    sin_t = np.concatenate([zeros(64), sin, sin, zeros(32)], axis=1)
    return jnp.asarray(cos_t), jnp.asarray(sin_t)


def _block_diag(w):
    n, bw, _ = w.shape
    eye = jnp.eye(n, dtype=w.dtype)
    return (w[:, :, None, :] * eye[:, None, :, None]).reshape(n * bw, n * bw).astype(BF16)


def kernel(x, c, w_ada, b_ada, norm_g, ffn1_w_in, ffn1_w_out, w_in, conv_w, conv_b, lru_wa, lru_ba, lru_wx,
           lru_bx, lru_lambda, mla_g_q, mla_g_kv, mla_w_uq, mla_w_ukv, swa_sinks, w_branch, w_out, ffn2_w_in,
           ffn2_w_out):
    batch, s_len, d = x.shape
    depth = w_ada.shape[0]
    t = batch * s_len
    assert s_len // POS_SPLIT <= 256
    x2 = x.reshape(t, d)
    c_pad = jnp.zeros((8, d), F32).at[:batch].set(c)
    mod_all = _ada_call(c_pad, w_ada, b_ada)
    cos_t, sin_t = _rope_tables(s_len)
    kpos = _kpos_table(s_len)
    row = lambda v: v.reshape(1, -1)
    for l in range(depth):
        mod_l = mod_all[l, :batch].reshape(batch, 9, d)
        g_l = norm_g[l]
        x2 = _ffn_call(x2, mod_l, g_l, ffn1_w_in, ffn1_w_out, l, sub=0, resid_w=0.5, s_len=s_len)
        w_all, w_gate = _mixer_weights(w_in[l])
        qkv_a, idx, iw, rm, cm, qkv_d = _proj_call(x2, mod_l, g_l, w_all, kpos, s_len=s_len)
        y_a = _dsa_call(qkv_a, idx, iw, batch=batch, s_len=s_len)
        y_b = _lru_call(rm, conv_w[l], row(conv_b[l]), _block_diag(lru_wa[l]), row(lru_ba[l]),
                        _block_diag(lru_wx[l]), row(lru_bx[l]), row(lru_lambda[l]), batch=batch, s_len=s_len)
        qm, km, vm = _mla_prep_call(cm, cos_t, sin_t, row(mla_g_q[l]), row(mla_g_kv[l]),
                                    *_mla_weights(mla_w_uq[l], mla_w_ukv[l]), s_len=s_len)
        y_c = _mla_attn_call(qm, km, vm, batch=batch, s_len=s_len)
        sinks_pad = jnp.zeros((1, LANES), F32).at[0, :D_HEADS].set(swa_sinks[l])
        y_d = _swa_call(qkv_d, sinks_pad, batch=batch, s_len=s_len)
        x2 = _merge_call(x2, mod_l, g_l, (y_a, y_b, y_c, y_d), w_gate, w_branch[l].astype(BF16),
                         w_out[l].astype(BF16), s_len=s_len)
        x2 = _ffn_call(x2, mod_l, g_l, ffn2_w_in, ffn2_w_out, l, sub=2, resid_w=0.5, s_len=s_len)
    return x2.reshape(batch, s_len, d)
```

```python
import functools

import numpy as np
import jax
import jax.numpy as jnp
from jax import lax
from jax.experimental import pallas as pl
from jax.experimental.pallas import tpu as pltpu

F32 = jnp.float32
BF16 = jnp.bfloat16
I32 = jnp.int32

D_MODEL = 1024
HEAD_DIM = 64
BLOCK = 128
EPS = 1e-6
NEG = -1e30
A_HEADS = 8
A_KV_HEADS = 2
IDX_HEADS = 8
IDX_DIM = 64
TOPK_MAX = 256
LRU_WIDTH = 512
LRU_BLOCKS = 8
CONV_WIDTH = 4
LRU_C = 8.0
C_HEADS = 8
Q_LORA = 256
KV_LORA = 128
QK_NOPE = 64
QK_ROPE = 32
V_DIM = 64
ROPE_THETA = 10000.0
D_HEADS = 8
D_KV_HEADS = 2
WINDOW = 128
N_BRANCH = 4
BRANCH_WIDTH = 512
N_ALIBI = A_HEADS + D_HEADS
D_FF = 2816
IN_SPLITS = (A_HEADS * HEAD_DIM, A_KV_HEADS * HEAD_DIM, A_KV_HEADS * HEAD_DIM,
             IDX_HEADS * IDX_DIM, IDX_DIM, IDX_HEADS,
             LRU_WIDTH, LRU_WIDTH,
             Q_LORA, KV_LORA + QK_ROPE,
             D_HEADS * HEAD_DIM, D_KV_HEADS * HEAD_DIM, D_KV_HEADS * HEAD_DIM,
             N_BRANCH * D_MODEL)
IN_OFFSETS = tuple(int(v) for v in np.concatenate([[0], np.cumsum(IN_SPLITS)]))

LANES = 128
INT_MIN = -2 ** 31
INT_MAX = 2 ** 31 - 1
VMEM_LIMIT = 56 * 1024 * 1024
DSA_VMEM_LIMIT = 62 * 1024 * 1024
LOG2E = 1.4426950408889634
POS_SPLIT = 64
LOOP_GROUP = 4
CNT_ROWS = 64

QKV_W = 512 + 256 + 128
IDX_W = 512 + 128
MLA_W = 640

FFN_TM = 1024
FFN_TF = 256
PROJ_TM = 512
DSA_TQ = 128
DSA_KC = 512
MLA_TQ = 512
MLA_KC = 512
SWA_BLOCKS = 4
LRU_L = 256
MERGE_TM = 512


def _alibi(i):
    return float(2.0 ** (-8.0 * i / N_ALIBI))


SLOPES_D = tuple(_alibi(i) for i in range(1, D_HEADS + 1))
SLOPES_A = tuple(_alibi(i) for i in range(D_HEADS + 1, N_ALIBI + 1))


def _bf16_parts(x):
    parts = []
    rem = np.float32(x)
    for _ in range(3):
        p = np.float32(np.asarray(rem, np.float32).astype(jnp.bfloat16).astype(np.float32))
        parts.append(float(p))
        rem = np.float32(rem - p)
    return parts


def _cparams(sem):
    return pltpu.CompilerParams(dimension_semantics=sem, vmem_limit_bytes=VMEM_LIMIT)


def _rms(x, g):
    return x * lax.rsqrt(jnp.mean(x * x, axis=-1, keepdims=True) + EPS) * g


def _nt_dot(a, b):
    return lax.dot_general(a, b, (((1,), (1,)), ((), ())), preferred_element_type=F32)


def _expm1(y):
    u = jnp.exp(y)
    safe = (u != 1.0) & (y > -1.0)
    ratio = y / jnp.log(jnp.where(safe, u, 2.0))
    return jnp.where(u == 1.0, y, jnp.where(safe, (u - 1.0) * ratio, u - 1.0))


def _loop_pairs(n, body):
    return _loop_groups(n, body, 0)


def _loop_groups(n, body, init):
    carry, start, width = init, 0, LOOP_GROUP
    while width >= 1:
        def group(i, carry, start=start, width=width):
            for u in range(width):
                carry = body(start + width * i + u, carry)
            return carry

        trips = lax.shift_right_logical(n - start, _log2(width))
        carry = lax.fori_loop(0, trips, group, carry)
        start = start + trips * width
        width //= 2
    return carry


def _log2(n):
    l = int(n).bit_length() - 1
    assert (1 << l) == n
    return l


def _ada_kernel(c_ref, w_ref, b_ref, o_ref):
    c = c_ref[...]
    sc = (c * jax.nn.sigmoid(c)).astype(BF16)
    o_ref[0] = jnp.dot(sc, w_ref[0].astype(BF16), preferred_element_type=F32) + b_ref[0]


def _ada_call(c_pad, w_ada, b_ada):
    depth, d, n = w_ada.shape
    tn = 1152
    return pl.pallas_call(
        _ada_kernel,
        grid=(depth, n // tn),
        in_specs=[pl.BlockSpec((c_pad.shape[0], d), lambda l, j: (0, 0)),
                  pl.BlockSpec((1, d, tn), lambda l, j: (l, 0, j)),
                  pl.BlockSpec((1, 1, tn), lambda l, j: (l, 0, j))],
        out_specs=pl.BlockSpec((1, c_pad.shape[0], tn), lambda l, j: (l, 0, j)),
        out_shape=jax.ShapeDtypeStruct((depth, c_pad.shape[0], n), F32),
        compiler_params=_cparams(("arbitrary", "arbitrary")),
        name="adaln",
    )(c_pad, w_ada, b_ada.reshape(depth, 1, n))


def _ffn_kernel(x_ref, mod_ref, g_ref, wg_ref, wu_ref, wo_ref, o_ref, h_ref, acc_ref, *, sub, resid_w):
    k = pl.program_id(1)
    m = mod_ref[0]

    @pl.when(k == 0)
    def _():
        x = x_ref[...]
        xn = _rms(x, g_ref[2 * sub:2 * sub + 1])
        h_ref[...] = (xn * (1.0 + m[3 * sub + 1:3 * sub + 2]) + m[3 * sub:3 * sub + 1]).astype(BF16)
        acc_ref[...] = jnp.zeros_like(acc_ref)

    h = h_ref[...]
    gate = jnp.dot(h, wg_ref[0].astype(BF16), preferred_element_type=F32)
    up = jnp.dot(h, wu_ref[0].astype(BF16), preferred_element_type=F32)
    act = (gate * jax.nn.sigmoid(gate) * up).astype(BF16)
    acc_ref[...] += jnp.dot(act, wo_ref[0].astype(BF16), preferred_element_type=F32)

    @pl.when(k == pl.num_programs(1) - 1)
    def _():
        yn = _rms(acc_ref[...], g_ref[2 * sub + 1:2 * sub + 2])
        o_ref[...] = x_ref[...] + resid_w * m[3 * sub + 2:3 * sub + 3] * yn


def _ffn_call(x2, mod_l, g_l, w_in, w_out, layer, *, sub, resid_w, s_len):
    t, d = x2.shape
    f = w_out.shape[1]
    tm = min(FFN_TM, s_len)
    tf = FFN_TF
    nf = f // tf
    tpb = s_len // tm
    return pl.pallas_call(
        functools.partial(_ffn_kernel, sub=sub, resid_w=resid_w),
        grid=(t // tm, nf),
        in_specs=[pl.BlockSpec((tm, d), lambda i, k: (i, 0)),
                  pl.BlockSpec((1, 9, d), lambda i, k: (i // tpb, 0, 0)),
                  pl.BlockSpec((6, d), lambda i, k: (0, 0)),
                  pl.BlockSpec((1, d, tf), lambda i, k: (layer, 0, k)),
                  pl.BlockSpec((1, d, tf), lambda i, k: (layer, 0, k + nf)),
                  pl.BlockSpec((1, tf, d), lambda i, k: (layer, k, 0))],
        out_specs=pl.BlockSpec((tm, d), lambda i, k: (i, 0)),
        out_shape=jax.ShapeDtypeStruct((t, d), F32),
        scratch_shapes=[pltpu.VMEM((tm, d), BF16), pltpu.VMEM((tm, d), F32)],
        compiler_params=_cparams(("arbitrary", "arbitrary")),
        name="ffn",
    )(x2, mod_l, g_l, w_in, w_in, w_out)


def _proj_kernel(x_ref, mod_ref, g_ref, w_ref, kpos_ref, oa_ref, oi_ref, ow_ref, or_ref, oc_ref, od_ref):
    m = mod_ref[0]
    xn = _rms(x_ref[...], g_ref[2:3])
    h = (xn * (1.0 + m[4:5]) + m[3:4]).astype(BF16)
    off = 0
    for ref in (oa_ref, oi_ref, ow_ref, or_ref, oc_ref, od_ref):
        w = ref.shape[1]
        z = jnp.dot(h, w_ref[:, off:off + w], preferred_element_type=F32)
        if ref is oa_ref or ref is od_ref:
            qscale = (HEAD_DIM ** -0.5) * (LOG2E if ref is oa_ref else 1.0)
            z = jnp.concatenate([z[:, :512] * qscale, z[:, 512:768] + kpos_ref[...], z[:, 768:]], axis=1)
        ref[...] = z.astype(ref.dtype)
        off += w


def _proj_call(x2, mod_l, g_l, w_all, kpos, *, s_len):
    t, d = x2.shape
    tm = min(PROJ_TM, s_len)
    tpb = s_len // tm
    widths = (QKV_W, IDX_W, LANES, 2 * LRU_WIDTH, MLA_W, QKV_W)
    dtypes = (BF16, BF16, F32, F32, F32, BF16)
    assert sum(widths) == w_all.shape[1]
    return pl.pallas_call(
        _proj_kernel,
        grid=(t // tm,),
        in_specs=[pl.BlockSpec((tm, d), lambda i: (i, 0)),
                  pl.BlockSpec((1, 9, d), lambda i: (i // tpb, 0, 0)),
                  pl.BlockSpec((6, d), lambda i: (0, 0)),
                  pl.BlockSpec(w_all.shape, lambda i: (0, 0)),
                  pl.BlockSpec((tm, 256), lambda i: (i % tpb, 0))],
        out_specs=[pl.BlockSpec((tm, w), lambda i: (i, 0)) for w in widths],
        out_shape=[jax.ShapeDtypeStruct((t, w), dt) for w, dt in zip(widths, dtypes)],
        compiler_params=_cparams(("arbitrary",)),
        name="mixer_proj",
    )(x2, mod_l, g_l, w_all, kpos)


def _head_rows(q, n_heads, tq):
    lane = lax.broadcasted_iota(I32, (tq, LANES), 1)
    rows = []
    for h in range(n_heads):
        tile = q[:, LANES * (h // 2):LANES * (h // 2 + 1)].astype(F32)
        keep = (lane >= HEAD_DIM) if (h % 2) else (lane < HEAD_DIM)
        rows.append(jnp.where(keep, tile, 0.0).astype(BF16))
    return jnp.concatenate(rows, axis=0)


def _alibi_lanes(slopes, tq):
    lane = lax.broadcasted_iota(I32, (tq, LANES), 1)
    out = []
    for slope in slopes:
        tile = jnp.zeros((tq, LANES), F32)
        for i, part in enumerate(_bf16_parts(slope)):
            tile = jnp.where(lane == HEAD_DIM + 2 * i, POS_SPLIT * part, tile)
            tile = jnp.where(lane == HEAD_DIM + 2 * i + 1, part, tile)
        out.append(tile)
    return out


def _alibi_q_tiles(q, alibi_lanes, tq):
    lane = lax.broadcasted_iota(I32, (tq, LANES), 1)
    rows = []
    for h, al in enumerate(alibi_lanes):
        tile = q[:, LANES * (h // 2):LANES * (h // 2 + 1)].astype(F32)
        if h % 2:
            tile = pltpu.roll(tile, HEAD_DIM, axis=1)
        rows.append(jnp.where(lane < HEAD_DIM, tile, al).astype(BF16))
    return jnp.concatenate(rows, axis=0)


def _gqa_out(accs, inv_ls, tq):
    lane = lax.broadcasted_iota(I32, (tq, LANES), 1)
    tiles = []
    for j in range(4):
        g = j // 2
        halves = []
        for h in (2 * j, 2 * j + 1):
            o = accs[g][h % 4] * inv_ls[g][h % 4]
            src_hi = (g == 1)
            dst_hi = (h % 2 == 1)
            if src_hi != dst_hi:
                o = pltpu.roll(o, HEAD_DIM, axis=1)
            halves.append(o)
        tiles.append(jnp.where(lane < HEAD_DIM, halves[0], halves[1]))
    return jnp.concatenate(tiles, axis=1)


def _dsa_kernel(q_ref, k_ref, v_ref, iq_ref, ik_ref, iw_ref, o_ref,
                key_ref, hi_ref, s_ref, tau_ref, iqs_ref, qs_ref, m_ref, l_ref, acc_ref, *, tq, kc, topk, pos_bits):
    qb = pl.program_id(1)
    q0 = qb * tq
    nk = lax.shift_right_logical(q0 + tq + kc - 1, _log2(kc))
    n_lt = kc // LANES
    n_rt = kc // CNT_ROWS
    qpos_l = q0 + lax.broadcasted_iota(I32, (kc, tq), 1)
    krow = lax.broadcasted_iota(I32, (kc, tq), 0)
    krow_t = lax.broadcasted_iota(I32, (CNT_ROWS, tq), 0)

    iqs_ref[...] = _head_rows(iq_ref[...], IDX_HEADS, tq)
    iw_t = iw_ref[...].T

    def score_body(j, carry):
        ks = pl.multiple_of(j * kc, kc)
        d = _nt_dot(ik_ref[pl.ds(ks, kc), :], iqs_ref[...])
        acc = jnp.zeros((kc, tq), F32)
        for h in range(IDX_HEADS):
            acc = acc + iw_t[h:h + 1, :] * jnp.maximum(d[:, h * tq:(h + 1) * tq], 0.0)
        bits = pltpu.bitcast(acc, I32)
        key = jnp.where(bits < 0, bits ^ INT_MAX, bits)
        key = jnp.where(ks + krow <= qpos_l, key, INT_MIN)
        key_ref[j] = key
        hi_ref[j] = lax.shift_right_arithmetic(key, 16).astype(jnp.int16)
        return carry

    _loop_pairs(nk, score_body)

    def count(pred):
        def body(j, cnt):
            for r in range(n_rt):
                kp = (j * kc + r * CNT_ROWS) + krow_t
                cnt = cnt + jnp.where(pred(key_ref[j, r * CNT_ROWS:(r + 1) * CNT_ROWS, :], kp), 1.0, 0.0)
            return cnt
        cnt = _loop_groups(nk, body, jnp.zeros((CNT_ROWS, tq), F32))
        return jnp.sum(cnt, axis=0, keepdims=True)

    kf = float(topk)

    def count16(pred16):
        def body(j, cnt):
            for r in range(n_rt):
                hj = hi_ref[j, r * CNT_ROWS:(r + 1) * CNT_ROWS, :]
                cnt = cnt + jnp.where(pred16(hj), jnp.int16(1), jnp.int16(0))
            return cnt
        cnt = _loop_groups(nk, body, jnp.zeros((CNT_ROWS, tq), jnp.int16))
        return jnp.sum(cnt.astype(F32), axis=0, keepdims=True)

    def rows16(v):
        return jnp.broadcast_to(v, (CNT_ROWS, tq)).astype(jnp.int16)

    def bit_step(i, tau, cnt, count_fn):
        cand = tau + lax.shift_left(jnp.int32(1), 31 - i)
        c = count_fn(cand)
        ok = c >= kf
        return jnp.where(ok, cand, tau), jnp.where(ok, c, cnt)

    def count_hi(cand):
        cb = rows16(lax.shift_right_arithmetic(cand, 16))
        return count16(lambda hj: hj >= cb)

    tau, n_ge = lax.fori_loop(0, 16, lambda i, st: bit_step(i, st[0], st[1], count_hi),
                              (jnp.full((1, tq), INT_MIN, I32), jnp.broadcast_to((nk * kc).astype(F32), (1, tq))))

    tau_hi = lax.shift_right_arithmetic(tau, 16)
    th16 = rows16(tau_hi)
    n_above = count16(lambda hj: hj > th16)
    th32 = jnp.broadcast_to(tau_hi, (kc, tq))

    def low_half_body(j, carry):
        kj = key_ref[j]
        lo = (kj & 0xFFFF) - 0x8000
        in_bucket = lax.shift_right_arithmetic(kj, 16) == th32
        hi_ref[j] = jnp.where(in_bucket, lo, -0x8000).astype(jnp.int16)
        return carry

    _loop_pairs(nk, low_half_body)

    def count_lo(cand):
        cb = rows16((cand & 0xFFFF) - 0x8000)
        return n_above + count16(lambda hj: hj >= cb)

    tau, n_ge = lax.fori_loop(16, 32, lambda i, st: bit_step(i, st[0], st[1], count_lo), (tau, n_ge))

    tie_f = jnp.where((n_ge > kf) & (tau > INT_MIN), 1.0, 0.0)
    any_tie = jnp.max(tie_f) > 0.0
    tau_c = jnp.maximum(tau, INT_MIN + 1)
    tau_ref[...] = jnp.broadcast_to(tau_c, tau_ref.shape)

    @pl.when(any_tie)
    def _():
        tb = jnp.broadcast_to(tau_c, (CNT_ROWS, tq))
        need = kf - count(lambda kv, kp: kv > tb)

        def pos_body(i, p):
            cand = p | lax.shift_left(jnp.int32(1), pos_bits - 1 - i)
            cb = jnp.broadcast_to(cand, (CNT_ROWS, tq))
            below = count(lambda kv, kp: (kv == tb) & (kp < cb))
            return jnp.where(below < need, cand, p)

        p = lax.fori_loop(0, pos_bits, pos_body, jnp.zeros((1, tq), I32))
        p = jnp.where(tie_f > 0.0, p, INT_MAX)

        def rewrite_body(j, carry):
            kj = key_ref[j]
            kp = j * kc + krow
            sel = (kj > tau_c) | ((kj == tau_c) & (kp <= p))
            key_ref[j] = jnp.where(sel, 1, -1)
            return carry

        lax.fori_loop(0, nk, rewrite_body, 0)
        tau_ref[...] = jnp.zeros(tau_ref.shape, I32)

    q = q_ref[...]
    for g in range(A_KV_HEADS):
        qs_ref[g] = _alibi_q_tiles(q[:, 4 * HEAD_DIM * g:4 * HEAD_DIM * (g + 1)],
                                   _alibi_lanes([s * LOG2E for s in SLOPES_A[4 * g:4 * g + 4]], tq), tq)
    m_ref[...] = jnp.full(m_ref.shape, NEG, F32)
    l_ref[...] = jnp.zeros(l_ref.shape, F32)
    acc_ref[...] = jnp.zeros(acc_ref.shape, F32)

    def max_body(j, carry):
        ks = pl.multiple_of(j * kc, kc)
        sel = jnp.where(key_ref[j] >= tau_ref[0:1, :], 1.0, 0.0).T > 0.5
        for g in range(A_KV_HEADS):
            s = _nt_dot(qs_ref[g], k_ref[pl.ds(ks, kc), g * LANES:(g + 1) * LANES])
            for h in range(4):
                sh = jnp.where(sel, s[h * tq:(h + 1) * tq], NEG)
                s_ref[g, h, j] = sh
                mp = m_ref[g, h]
                for c in range(n_lt):
                    mp = jnp.maximum(mp, sh[:, c * LANES:(c + 1) * LANES])
                m_ref[g, h] = mp
        return carry

    _loop_pairs(nk, max_body)
    for g in range(A_KV_HEADS):
        for h in range(4):
            m_ref[g, h] = jnp.broadcast_to(jnp.max(m_ref[g, h], axis=1, keepdims=True), (tq, LANES))

    def pv_body(j, carry):
        ks = pl.multiple_of(j * kc, kc)
        vch = v_ref[pl.ds(ks, kc), :]
        for g in range(A_KV_HEADS):
            ps = []
            for h in range(4):
                mb = m_ref[g, h]
                lp = l_ref[g, h]
                tiles = []
                for c in range(n_lt):
                    p = jnp.exp2(s_ref[g, h, j, :, c * LANES:(c + 1) * LANES] - mb)
                    lp = lp + p
                    tiles.append(p.astype(BF16))
                l_ref[g, h] = lp
                ps.append(jnp.concatenate(tiles, axis=1))
            pv = jnp.dot(jnp.concatenate(ps, axis=0), vch, preferred_element_type=F32)
            for h in range(4):
                acc_ref[g, h] += pv[h * tq:(h + 1) * tq]
        return carry

    _loop_pairs(nk, pv_body)
    accs = [[acc_ref[g, h] for h in range(4)] for g in range(A_KV_HEADS)]
    inv_ls = [[1.0 / jnp.sum(l_ref[g, h], axis=1, keepdims=True) for h in range(4)] for g in range(A_KV_HEADS)]
    o_ref[...] = _gqa_out(accs, inv_ls, tq).astype(o_ref.dtype)


def _dsa_call(qkv, idx, iw, *, batch, s_len):
    t = qkv.shape[0]
    tq = min(DSA_TQ, s_len)
    kc = min(DSA_KC, s_len)
    topk = min(TOPK_MAX, s_len // 4)
    assert kc >= topk and kc % tq == 0
    nqb = s_len // tq
    once = pl.Buffered(1)
    return pl.pallas_call(
        functools.partial(_dsa_kernel, tq=tq, kc=kc, topk=topk, pos_bits=_log2(s_len)),
        grid=(batch, nqb),
        in_specs=[pl.BlockSpec((tq, 512), lambda b, i: (b * nqb + i, 0)),
                  pl.BlockSpec((s_len, 256), lambda b, i: (b, 2), pipeline_mode=once),
                  pl.BlockSpec((s_len, 128), lambda b, i: (b, 6), pipeline_mode=once),
                  pl.BlockSpec((tq, 512), lambda b, i: (b * nqb + i, 0)),
                  pl.BlockSpec((s_len, 128), lambda b, i: (b, 4), pipeline_mode=once),
                  pl.BlockSpec((tq, LANES), lambda b, i: (b * nqb + i, 0))],
        out_specs=pl.BlockSpec((tq, 512), lambda b, i: (b * nqb + i, 0)),
        out_shape=jax.ShapeDtypeStruct((t, 512), BF16),
        scratch_shapes=[pltpu.VMEM((s_len // kc, kc, tq), I32),
                        pltpu.VMEM((s_len // kc, kc, tq), jnp.int16),
                        pltpu.VMEM((A_KV_HEADS, 4, s_len // kc, tq, kc), F32),
                        pltpu.VMEM((8, tq), I32),
                        pltpu.VMEM((IDX_HEADS * tq, LANES), BF16),
                        pltpu.VMEM((A_KV_HEADS, 4 * tq, LANES), BF16),
                        pltpu.VMEM((A_KV_HEADS, 4, tq, LANES), F32),
                        pltpu.VMEM((A_KV_HEADS, 4, tq, LANES), F32),
                        pltpu.VMEM((A_KV_HEADS, 4, tq, LANES), F32)],
        compiler_params=pltpu.CompilerParams(dimension_semantics=("arbitrary", "arbitrary"),
                                             vmem_limit_bytes=DSA_VMEM_LIMIT),
        name="dsa",
    )(qkv, qkv, qkv, idx, idx, iw)


def _swa_kernel(q_ref, kp_ref, kc_ref, vp_ref, vc_ref, sink_ref, o_ref, *, tq, nblk):
    i = pl.program_id(1)
    krow = lax.broadcasted_iota(I32, (2 * tq, tq), 0)
    qcol = lax.broadcasted_iota(I32, (2 * tq, tq), 1)
    dist = qcol + tq - krow
    in_band = (dist >= 0) & (dist < WINDOW)
    sinks = sink_ref[...]
    sink_vec = [jnp.broadcast_to(sinks[:, h:h + 1], (1, tq)) for h in range(D_HEADS)]
    alibi = _alibi_lanes(SLOPES_D, tq)
    for n in range(nblk):
        blk = i * nblk + n
        rows = slice(n * tq, (n + 1) * tq)
        q = q_ref[rows, :]
        k_prev = kp_ref[...] if n == 0 else kc_ref[(n - 1) * tq:n * tq, :]
        v_prev = vp_ref[...] if n == 0 else vc_ref[(n - 1) * tq:n * tq, :]
        kk = jnp.concatenate([k_prev, kc_ref[rows, :]], axis=0)
        vv_t = jnp.concatenate([v_prev, vc_ref[rows, :]], axis=0).T
        valid = in_band & ((blk * tq - tq + krow) >= 0)
        qpos = (blk * tq + lax.broadcasted_iota(I32, (1, tq), 1)).astype(F32)
        tiles = []
        for g in range(D_KV_HEADS):
            qs = _alibi_q_tiles(q[:, 4 * HEAD_DIM * g:4 * HEAD_DIM * (g + 1)], alibi[4 * g:4 * g + 4], tq)
            s_t = _nt_dot(kk[:, g * LANES:(g + 1) * LANES], qs)
            ps, ils = [], []
            for h in range(4):
                sh = jnp.where(valid, s_t[:, h * tq:(h + 1) * tq], NEG)
                sink = sink_vec[4 * g + h] + SLOPES_D[4 * g + h] * qpos
                m = jnp.maximum(jnp.max(sh, axis=0, keepdims=True), sink)
                p = jnp.exp(sh - m)
                ils.append(1.0 / (jnp.sum(p, axis=0, keepdims=True) + jnp.exp(sink - m)))
                ps.append(p.astype(BF16))
            o_t = jnp.dot(vv_t, jnp.concatenate(ps, axis=1), preferred_element_type=F32)
            o_t = o_t[g * HEAD_DIM:(g + 1) * HEAD_DIM, :] * jnp.concatenate(ils, axis=1)
            for j in range(2):
                pair = jnp.concatenate([o_t[:, (2 * j) * tq:(2 * j + 1) * tq],
                                        o_t[:, (2 * j + 1) * tq:(2 * j + 2) * tq]], axis=0)
                tiles.append(pair.T)
        o_ref[rows, :] = jnp.concatenate(tiles, axis=1).astype(o_ref.dtype)


def _swa_call(qkv, sinks_pad, *, batch, s_len):
    t = qkv.shape[0]
    tq = BLOCK
    nblk = min(SWA_BLOCKS, s_len // tq)
    ts = nblk * tq
    nst = s_len // ts
    cur = lambda b, i: b * nst + i
    prev = lambda b, i: jnp.maximum((b * nst + i) * nblk - 1, 0)
    return pl.pallas_call(
        functools.partial(_swa_kernel, tq=tq, nblk=nblk),
        grid=(batch, nst),
        in_specs=[pl.BlockSpec((ts, 512), lambda b, i: (cur(b, i), 0)),
                  pl.BlockSpec((tq, 256), lambda b, i: (prev(b, i), 2)),
                  pl.BlockSpec((ts, 256), lambda b, i: (cur(b, i), 2)),
                  pl.BlockSpec((tq, 128), lambda b, i: (prev(b, i), 6)),
                  pl.BlockSpec((ts, 128), lambda b, i: (cur(b, i), 6)),
                  pl.BlockSpec((1, LANES), lambda b, i: (0, 0))],
        out_specs=pl.BlockSpec((ts, 512), lambda b, i: (cur(b, i), 0)),
        out_shape=jax.ShapeDtypeStruct((t, 512), BF16),
        compiler_params=_cparams(("arbitrary", "arbitrary")),
        name="swa",
    )(qkv, qkv, qkv, qkv, qkv, sinks_pad)


def _mla_prep_kernel(c_ref, cos_ref, sin_ref, gq_ref, gkv_ref, wq_ref, wqr_ref, wk_ref, wv_ref,
                     q_ref, k_ref, v_ref):
    c = c_ref[...]
    cos = cos_ref[...]
    sin = sin_ref[...]
    cos8 = jnp.concatenate([cos] * C_HEADS, axis=1)
    sin8 = jnp.concatenate([sin] * C_HEADS, axis=1)
    cqn = _rms(c[:, :Q_LORA], gq_ref[...]).astype(BF16)
    q = (jnp.dot(cqn, wq_ref[...], preferred_element_type=F32) * cos8
         + jnp.dot(cqn, wqr_ref[...], preferred_element_type=F32) * sin8)
    q_ref[...] = q.astype(BF16)
    ckvn = _rms(c[:, Q_LORA:Q_LORA + KV_LORA], gkv_ref[...]).astype(BF16)
    kr = c[:, 384:512] * cos + c[:, 512:640] * sin
    k = jnp.dot(ckvn, wk_ref[...], preferred_element_type=F32) + jnp.concatenate([kr] * C_HEADS, axis=1)
    k_ref[...] = k.astype(BF16)
    v_ref[...] = jnp.dot(ckvn, wv_ref[...], preferred_element_type=F32).astype(BF16)


def _mla_prep_call(cm, cos_t, sin_t, gq, gkv, wq, wqr, wk, wv, *, s_len):
    t = cm.shape[0]
    tm = min(PROJ_TM, s_len)
    tpb = s_len // tm
    full = lambda a: pl.BlockSpec(a.shape, lambda i: (0, 0))
    return pl.pallas_call(
        _mla_prep_kernel,
        grid=(t // tm,),
        in_specs=[pl.BlockSpec((tm, MLA_W), lambda i: (i, 0)),
                  pl.BlockSpec((tm, LANES), lambda i: (i % tpb, 0)),
                  pl.BlockSpec((tm, LANES), lambda i: (i % tpb, 0)),
                  full(gq), full(gkv), full(wq), full(wqr), full(wk), full(wv)],
        out_specs=[pl.BlockSpec((tm, 1024), lambda i: (i, 0)),
                   pl.BlockSpec((tm, 1024), lambda i: (i, 0)),
                   pl.BlockSpec((tm, 512), lambda i: (i, 0))],
        out_shape=[jax.ShapeDtypeStruct((t, 1024), BF16),
                   jax.ShapeDtypeStruct((t, 1024), BF16),
                   jax.ShapeDtypeStruct((t, 512), BF16)],
        compiler_params=_cparams(("arbitrary",)),
        name="mla_prep",
    )(cm, cos_t, sin_t, gq, gkv, wq, wqr, wk, wv)


def _mla_attn_kernel(q_ref, k_ref, v_ref, o_ref, s_ref, m_ref, l_ref, acc_ref, *, tq, kc):
    qb = pl.program_id(2)
    q0 = qb * tq
    n_full = lax.shift_right_logical(q0, _log2(kc))
    rowpos = q0 + lax.broadcasted_iota(I32, (tq, kc), 0)
    lane = lax.broadcasted_iota(I32, (tq, kc), 1)
    c = ((QK_NOPE + QK_ROPE) ** -0.5) * LOG2E
    n_lt = kc // LANES
    m_ref[...] = jnp.full(m_ref.shape, NEG, F32)
    l_ref[...] = jnp.zeros(l_ref.shape, F32)
    acc_ref[...] = jnp.zeros(acc_ref.shape, F32)

    def max_step(j, masked):
        ks = pl.multiple_of(j * kc, kc)
        for hh in range(2):
            s = _nt_dot(q_ref[:, hh * LANES:(hh + 1) * LANES],
                        k_ref[pl.ds(ks, kc), hh * LANES:(hh + 1) * LANES]) * c
            if masked:
                s = jnp.where(ks + lane <= rowpos, s, NEG)
            s_ref[hh, j] = s
            mp = m_ref[hh]
            for t in range(n_lt):
                mp = jnp.maximum(mp, s[:, t * LANES:(t + 1) * LANES])
            m_ref[hh] = mp

    def max_body(j, carry):
        max_step(j, False)
        return carry

    _loop_pairs(n_full, max_body)
    max_step(n_full, True)
    for hh in range(2):
        m_ref[hh] = jnp.broadcast_to(jnp.max(m_ref[hh], axis=1, keepdims=True), (tq, LANES))

    def pv_body(j, carry):
        ks = pl.multiple_of(j * kc, kc)
        vch = v_ref[pl.ds(ks, kc), :]
        for hh in range(2):
            mb = m_ref[hh]
            lp = l_ref[hh]
            tiles = []
            for t in range(n_lt):
                p = jnp.exp2(s_ref[hh, j, :, t * LANES:(t + 1) * LANES] - mb)
                lp = lp + p
                tiles.append(p.astype(BF16))
            l_ref[hh] = lp
            acc_ref[hh] += jnp.dot(jnp.concatenate(tiles, axis=1), vch, preferred_element_type=F32)
        return carry

    _loop_pairs(n_full + 1, pv_body)
    outs = [acc_ref[hh] * (1.0 / jnp.sum(l_ref[hh], axis=1, keepdims=True)) for hh in range(2)]
    lane_o = lax.broadcasted_iota(I32, (tq, LANES), 1)
    o_ref[...] = jnp.where(lane_o < V_DIM, outs[0], outs[1]).astype(o_ref.dtype)


def _mla_attn_call(qm, km, vm, *, batch, s_len):
    t = qm.shape[0]
    tq = min(MLA_TQ, s_len)
    kc = min(MLA_KC, s_len)
    assert kc % tq == 0
    nqb = s_len // tq
    once = pl.Buffered(1)
    return pl.pallas_call(
        functools.partial(_mla_attn_kernel, tq=tq, kc=kc),
        grid=(batch, C_HEADS // 2, nqb),
        in_specs=[pl.BlockSpec((tq, 256), lambda b, h, i: (b * nqb + i, h)),
                  pl.BlockSpec((s_len, 256), lambda b, h, i: (b, h), pipeline_mode=once),
                  pl.BlockSpec((s_len, 128), lambda b, h, i: (b, h), pipeline_mode=once)],
        out_specs=pl.BlockSpec((tq, 128), lambda b, h, i: (b * nqb + i, h)),
        out_shape=jax.ShapeDtypeStruct((t, 512), BF16),
        scratch_shapes=[pltpu.VMEM((2, s_len // kc, tq, kc), F32)] + [pltpu.VMEM((2, tq, LANES), F32)] * 3,
        compiler_params=_cparams(("arbitrary", "arbitrary", "arbitrary")),
        name="mla_attn",
    )(qm, km, vm)


def _lru_kernel(xr_ref, xg_ref, prev_ref, cw_ref, cb_ref, wa_ref, ba_ref, wx_ref, bx_ref, lam_ref,
                o_ref, xe_ref, h_ref, *, ln):
    i = pl.program_id(1)
    xe_ref[0:8, :] = jnp.where(i > 0, prev_ref[...], 0.0)
    xe_ref[8:8 + ln, :] = xr_ref[...]
    cw = cw_ref[...]
    xc = cb_ref[...] + cw[0:1] * xe_ref[pl.ds(5, ln), :]
    for j in range(1, CONV_WIDTH):
        xc = xc + cw[j:j + 1] * xe_ref[pl.ds(5 + j, ln), :]
    xcb = xc.astype(BF16)
    r = jax.nn.sigmoid(jnp.dot(xcb, wa_ref[...], preferred_element_type=F32) + ba_ref[...])
    gi = jax.nn.sigmoid(jnp.dot(xcb, wx_ref[...], preferred_element_type=F32) + bx_ref[...])
    z = -lam_ref[...]
    softplus = jnp.maximum(z, 0.0) + jnp.log1p(jnp.exp(-jnp.abs(z)))
    log_a = -LRU_C * r * softplus
    a = jnp.exp(log_a)
    b = jnp.sqrt(-_expm1(2.0 * log_a)) * (gi * xc)
    row = lax.broadcasted_iota(I32, (ln, LRU_WIDTH), 0)
    d = 1
    while d < ln:
        keep = row >= d
        b = jnp.where(keep, a * pltpu.roll(b, d, axis=0) + b, b)
        a = jnp.where(keep, a * pltpu.roll(a, d, axis=0), a)
        d *= 2
    h_prev = jnp.where(i > 0, h_ref[0:1, :], 0.0)
    h = a * h_prev + b
    h_ref[0:1, :] = h[ln - 1:ln, :]
    xg = xg_ref[...]
    gelu = 0.5 * xg * (1.0 + jnp.tanh(0.7978845608028654 * (xg + 0.044715 * (xg * xg * xg))))
    o_ref[...] = (h * gelu).astype(o_ref.dtype)


def _lru_call(rm, cw, cb, wa, ba, wx, bx, lam, *, batch, s_len):
    t = rm.shape[0]
    ln = min(LRU_L, s_len)
    nt = s_len // ln
    w = LRU_WIDTH
    vec = lambda a: pl.BlockSpec(a.shape, lambda b, i: (0, 0))
    return pl.pallas_call(
        functools.partial(_lru_kernel, ln=ln),
        grid=(batch, nt),
        in_specs=[pl.BlockSpec((ln, w), lambda b, i: (b * nt + i, 0)),
                  pl.BlockSpec((ln, w), lambda b, i: (b * nt + i, 1)),
                  pl.BlockSpec((8, w), lambda b, i: (jnp.maximum((b * nt + i) * (ln // 8) - 1, 0), 0)),
                  vec(cw), vec(cb), vec(wa), vec(ba), vec(wx), vec(bx), vec(lam)],
        out_specs=pl.BlockSpec((ln, w), lambda b, i: (b * nt + i, 0)),
        out_shape=jax.ShapeDtypeStruct((t, w), BF16),
        scratch_shapes=[pltpu.VMEM((ln + 8, w), F32), pltpu.VMEM((8, w), F32)],
        compiler_params=_cparams(("arbitrary", "arbitrary")),
        name="rglru",
    )(rm, rm, rm, cw, cb, wa, ba, wx, bx, lam)


def _merge_kernel(x_ref, mod_ref, g_ref, ya_ref, yb_ref, yc_ref, yd_ref, wg_ref, wb_ref, wo_ref, o_ref):
    m = mod_ref[0]
    x = x_ref[...]
    h = (_rms(x, g_ref[2:3]) * (1.0 + m[4:5]) + m[3:4]).astype(BF16)
    merged = None
    for n, y_ref in enumerate((ya_ref, yb_ref, yc_ref, yd_ref)):
        gate = jax.nn.sigmoid(jnp.dot(h, wg_ref[:, n * D_MODEL:(n + 1) * D_MODEL], preferred_element_type=F32))
        term = gate * jnp.dot(y_ref[...], wb_ref[n], preferred_element_type=F32)
        merged = term if merged is None else merged + term
    y = jnp.dot(merged.astype(BF16), wo_ref[...], preferred_element_type=F32)
    o_ref[...] = x + m[5:6] * _rms(y, g_ref[3:4])


def _merge_call(x2, mod_l, g_l, ys, w_gate, w_branch, w_out, *, s_len):
    t, d = x2.shape
    tm = min(MERGE_TM, s_len)
    tpb = s_len // tm
    return pl.pallas_call(
        _merge_kernel,
        grid=(t // tm,),
        in_specs=[pl.BlockSpec((tm, d), lambda i: (i, 0)),
                  pl.BlockSpec((1, 9, d), lambda i: (i // tpb, 0, 0)),
                  pl.BlockSpec((6, d), lambda i: (0, 0))]
                 + [pl.BlockSpec((tm, BRANCH_WIDTH), lambda i: (i, 0))] * N_BRANCH
                 + [pl.BlockSpec(w_gate.shape, lambda i: (0, 0)),
                    pl.BlockSpec(w_branch.shape, lambda i: (0, 0, 0)),
                    pl.BlockSpec(w_out.shape, lambda i: (0, 0))],
        out_specs=pl.BlockSpec((tm, d), lambda i: (i, 0)),
        out_shape=jax.ShapeDtypeStruct((t, d), F32),
        compiler_params=_cparams(("arbitrary",)),
        name="merge",
    )(x2, mod_l, g_l, *ys, w_gate, w_branch, w_out)


def _mixer_weights(w_in_l):
    o = IN_OFFSETS
    col = lambda n: w_in_l[:, o[n]:o[n + 1]]
    z = lambda n: jnp.zeros((D_MODEL, n), w_in_l.dtype)

    def pad_heads(k):
        return jnp.concatenate([k[:, :64], z(64), k[:, 64:], z(64)], axis=1)

    a_q, a_k, a_v, i_q, i_k, i_w, r_x, r_g, c_q, c_kv, d_q, d_k, d_v = (col(n) for n in range(13))
    k_rope = c_kv[:, KV_LORA:]
    half = QK_ROPE // 2
    k_rope_rot = jnp.concatenate([-k_rope[:, half:], k_rope[:, :half]], axis=1)
    groups = [a_q, pad_heads(a_k), a_v,
              i_q, i_k, i_k,
              i_w, z(LANES - IDX_HEADS),
              r_x, r_g,
              c_q, c_kv[:, :KV_LORA], z(64), k_rope, z(32), z(64), k_rope_rot, z(32),
              d_q, pad_heads(d_k), d_v]
    w_all = jnp.concatenate(groups, axis=1).astype(BF16)
    w_gate = col(13).astype(BF16)
    return w_all, w_gate


def _kpos_table(s_len):
    pos = np.arange(s_len)
    tile = np.zeros((s_len, LANES), np.float32)
    for i in range(3):
        tile[:, HEAD_DIM + 2 * i] = pos // POS_SPLIT
        tile[:, HEAD_DIM + 2 * i + 1] = pos % POS_SPLIT
    return jnp.asarray(np.concatenate([tile, tile], axis=1))


def _mla_weights(w_uq, w_ukv):
    dq = QK_NOPE + QK_ROPE
    half = QK_ROPE // 2
    zq = lambda n: jnp.zeros((Q_LORA, n), w_uq.dtype)
    wq, wqr, wk, wv = [], [], [], []
    for h in range(C_HEADS):
        nope = w_uq[:, h * dq:h * dq + QK_NOPE]
        r1 = w_uq[:, h * dq + QK_NOPE:h * dq + QK_NOPE + half]
        r2 = w_uq[:, h * dq + QK_NOPE + half:(h + 1) * dq]
        wq += [nope, r1, r2, zq(32)]
        wqr += [zq(64), -r2, r1, zq(32)]
        wk += [w_ukv[:, h * 128:h * 128 + QK_NOPE], jnp.zeros((KV_LORA, 64), w_ukv.dtype)]
        wv += [w_ukv[:, h * 128 + QK_NOPE:(h + 1) * 128]]
    cat = lambda xs: jnp.concatenate(xs, axis=1).astype(BF16)
    return cat(wq), cat(wqr), cat(wk), cat(wv)


def _rope_tables(s_len):
    half = QK_ROPE // 2
    inv = (np.float32(ROPE_THETA) ** (-np.arange(half, dtype=np.float32) / np.float32(half))).astype(np.float32)
    ang = (np.arange(s_len, dtype=np.float32)[:, None] * inv[None, :]).astype(np.float32)
    cos, sin = np.cos(ang.astype(np.float64)).astype(np.float32), np.sin(ang.astype(np.float64)).astype(np.float32)
    ones = lambda n: np.ones((s_len, n), np.float32)
    zeros = lambda n: np.zeros((s_len, n), np.float32)
    cos_t = np.concatenate([ones(64), cos, cos, ones(32)], axis=1)
    sin_t = np.concatenate([zeros(64), sin, sin, zeros(32)], axis=1)
    return jnp.asarray(cos_t), jnp.asarray(sin_t)


def _block_diag(w):
    n, bw, _ = w.shape
    eye = jnp.eye(n, dtype=w.dtype)
    return (w[:, :, None, :] * eye[:, None, :, None]).reshape(n * bw, n * bw).astype(BF16)


def kernel(x, c, w_ada, b_ada, norm_g, ffn1_w_in, ffn1_w_out, w_in, conv_w, conv_b, lru_wa, lru_ba, lru_wx,
           lru_bx, lru_lambda, mla_g_q, mla_g_kv, mla_w_uq, mla_w_ukv, swa_sinks, w_branch, w_out, ffn2_w_in,
           ffn2_w_out):
    batch, s_len, d = x.shape
    depth = w_ada.shape[0]
    t = batch * s_len
    assert s_len // POS_SPLIT <= 256
    x2 = x.reshape(t, d)
    c_pad = jnp.zeros((8, d), F32).at[:batch].set(c)
    mod_all = _ada_call(c_pad, w_ada, b_ada)
    cos_t, sin_t = _rope_tables(s_len)
    kpos = _kpos_table(s_len)
    row = lambda v: v.reshape(1, -1)
    for l in range(depth):
        mod_l = mod_all[l, :batch].reshape(batch, 9, d)
        g_l = norm_g[l]
        x2 = _ffn_call(x2, mod_l, g_l, ffn1_w_in, ffn1_w_out, l, sub=0, resid_w=0.5, s_len=s_len)
        w_all, w_gate = _mixer_weights(w_in[l])
        qkv_a, idx, iw, rm, cm, qkv_d = _proj_call(x2, mod_l, g_l, w_all, kpos, s_len=s_len)
        y_a = _dsa_call(qkv_a, idx, iw, batch=batch, s_len=s_len)
        y_b = _lru_call(rm, conv_w[l], row(conv_b[l]), _block_diag(lru_wa[l]), row(lru_ba[l]),
                        _block_diag(lru_wx[l]), row(lru_bx[l]), row(lru_lambda[l]), batch=batch, s_len=s_len)
        qm, km, vm = _mla_prep_call(cm, cos_t, sin_t, row(mla_g_q[l]), row(mla_g_kv[l]),
                                    *_mla_weights(mla_w_uq[l], mla_w_ukv[l]), s_len=s_len)
        y_c = _mla_attn_call(qm, km, vm, batch=batch, s_len=s_len)
        sinks_pad = jnp.zeros((1, LANES), F32).at[0, :D_HEADS].set(swa_sinks[l])
        y_d = _swa_call(qkv_d, sinks_pad, batch=batch, s_len=s_len)
        x2 = _merge_call(x2, mod_l, g_l, (y_a, y_b, y_c, y_d), w_gate, w_branch[l].astype(BF16),
                         w_out[l].astype(BF16), s_len=s_len)
        x2 = _ffn_call(x2, mod_l, g_l, ffn2_w_in, ffn2_w_out, l, sub=2, resid_w=0.5, s_len=s_len)
    return x2.reshape(batch, s_len, d)
```

```python
import functools

import numpy as np
import jax
import jax.numpy as jnp
from jax import lax
from jax.experimental import pallas as pl
from jax.experimental.pallas import tpu as pltpu

F32 = jnp.float32
BF16 = jnp.bfloat16
I32 = jnp.int32

D_MODEL = 1024
HEAD_DIM = 64
BLOCK = 128
EPS = 1e-6
NEG = -1e30
A_HEADS = 8
A_KV_HEADS = 2
IDX_HEADS = 8
IDX_DIM = 64
TOPK_MAX = 256
LRU_WIDTH = 512
LRU_BLOCKS = 8
CONV_WIDTH = 4
LRU_C = 8.0
C_HEADS = 8
Q_LORA = 256
KV_LORA = 128
QK_NOPE = 64
QK_ROPE = 32
V_DIM = 64
ROPE_THETA = 10000.0
D_HEADS = 8
D_KV_HEADS = 2
WINDOW = 128
N_BRANCH = 4
BRANCH_WIDTH = 512
N_ALIBI = A_HEADS + D_HEADS
D_FF = 2816
IN_SPLITS = (A_HEADS * HEAD_DIM, A_KV_HEADS * HEAD_DIM, A_KV_HEADS * HEAD_DIM,
             IDX_HEADS * IDX_DIM, IDX_DIM, IDX_HEADS,
             LRU_WIDTH, LRU_WIDTH,
             Q_LORA, KV_LORA + QK_ROPE,
             D_HEADS * HEAD_DIM, D_KV_HEADS * HEAD_DIM, D_KV_HEADS * HEAD_DIM,
             N_BRANCH * D_MODEL)
IN_OFFSETS = tuple(int(v) for v in np.concatenate([[0], np.cumsum(IN_SPLITS)]))

LANES = 128
INT_MIN = -2 ** 31
INT_MAX = 2 ** 31 - 1
VMEM_LIMIT = 56 * 1024 * 1024
DSA_VMEM_LIMIT = 62 * 1024 * 1024
LOG2E = 1.4426950408889634
POS_SPLIT = 64
LOOP_GROUP = 8
CNT_ROWS = 64

QKV_W = 512 + 256 + 128
IDX_W = 512 + 128
MLA_W = 640

FFN_TM = 1024
FFN_TF = 256
PROJ_TM = 512
DSA_TQ = 128
DSA_KC = 512
MLA_TQ = 512
MLA_KC = 512
SWA_BLOCKS = 4
LRU_L = 256
MERGE_TM = 512


def _alibi(i):
    return float(2.0 ** (-8.0 * i / N_ALIBI))


SLOPES_D = tuple(_alibi(i) for i in range(1, D_HEADS + 1))
SLOPES_A = tuple(_alibi(i) for i in range(D_HEADS + 1, N_ALIBI + 1))


def _bf16_parts(x):
    parts = []
    rem = np.float32(x)
    for _ in range(3):
        p = np.float32(np.asarray(rem, np.float32).astype(jnp.bfloat16).astype(np.float32))
        parts.append(float(p))
        rem = np.float32(rem - p)
    return parts


def _cparams(sem):
    return pltpu.CompilerParams(dimension_semantics=sem, vmem_limit_bytes=VMEM_LIMIT)


def _rms(x, g):
    return x * lax.rsqrt(jnp.mean(x * x, axis=-1, keepdims=True) + EPS) * g


def _nt_dot(a, b):
    return lax.dot_general(a, b, (((1,), (1,)), ((), ())), preferred_element_type=F32)


def _expm1(y):
    u = jnp.exp(y)
    safe = (u != 1.0) & (y > -1.0)
    ratio = y / jnp.log(jnp.where(safe, u, 2.0))
    return jnp.where(u == 1.0, y, jnp.where(safe, (u - 1.0) * ratio, u - 1.0))


def _loop_groups(n, body, init=0):
    carry, start, width = init, 0, LOOP_GROUP
    while width >= 1:
        def group(i, carry, start=start, width=width):
            for u in range(width):
                carry = body(start + width * i + u, carry)
            return carry

        trips = lax.shift_right_logical(n - start, _log2(width))
        carry = lax.fori_loop(0, trips, group, carry)
        start = start + trips * width
        width //= 2
    return carry


def _log2(n):
    l = int(n).bit_length() - 1
    assert (1 << l) == n
    return l


def _ada_kernel(c_ref, w_ref, b_ref, o_ref):
    c = c_ref[...]
    sc = (c * jax.nn.sigmoid(c)).astype(BF16)
    o_ref[0] = jnp.dot(sc, w_ref[0].astype(BF16), preferred_element_type=F32) + b_ref[0]


def _ada_call(c_pad, w_ada, b_ada):
    depth, d, n = w_ada.shape
    tn = 1152
    return pl.pallas_call(
        _ada_kernel,
        grid=(depth, n // tn),
        in_specs=[pl.BlockSpec((c_pad.shape[0], d), lambda l, j: (0, 0)),
                  pl.BlockSpec((1, d, tn), lambda l, j: (l, 0, j)),
                  pl.BlockSpec((1, 1, tn), lambda l, j: (l, 0, j))],
        out_specs=pl.BlockSpec((1, c_pad.shape[0], tn), lambda l, j: (l, 0, j)),
        out_shape=jax.ShapeDtypeStruct((depth, c_pad.shape[0], n), F32),
        compiler_params=_cparams(("arbitrary", "arbitrary")),
        name="adaln",
    )(c_pad, w_ada, b_ada.reshape(depth, 1, n))


def _ffn_kernel(x_ref, mod_ref, g_ref, wg_ref, wu_ref, wo_ref, o_ref, h_ref, acc_ref, *, sub, resid_w):
    k = pl.program_id(1)
    m = mod_ref[0]

    @pl.when(k == 0)
    def _():
        x = x_ref[...]
        xn = _rms(x, g_ref[2 * sub:2 * sub + 1])
        h_ref[...] = (xn * (1.0 + m[3 * sub + 1:3 * sub + 2]) + m[3 * sub:3 * sub + 1]).astype(BF16)
        acc_ref[...] = jnp.zeros_like(acc_ref)

    h = h_ref[...]
    gate = jnp.dot(h, wg_ref[0].astype(BF16), preferred_element_type=F32)
    up = jnp.dot(h, wu_ref[0].astype(BF16), preferred_element_type=F32)
    act = (gate * jax.nn.sigmoid(gate) * up).astype(BF16)
    acc_ref[...] += jnp.dot(act, wo_ref[0].astype(BF16), preferred_element_type=F32)

    @pl.when(k == pl.num_programs(1) - 1)
    def _():
        yn = _rms(acc_ref[...], g_ref[2 * sub + 1:2 * sub + 2])
        o_ref[...] = x_ref[...] + resid_w * m[3 * sub + 2:3 * sub + 3] * yn


def _ffn_call(x2, mod_l, g_l, w_in, w_out, layer, *, sub, resid_w, s_len):
    t, d = x2.shape
    f = w_out.shape[1]
    tm = min(FFN_TM, s_len)
    tf = FFN_TF
    nf = f // tf
    tpb = s_len // tm
    return pl.pallas_call(
        functools.partial(_ffn_kernel, sub=sub, resid_w=resid_w),
        grid=(t // tm, nf),
        in_specs=[pl.BlockSpec((tm, d), lambda i, k: (i, 0)),
                  pl.BlockSpec((1, 9, d), lambda i, k: (i // tpb, 0, 0)),
                  pl.BlockSpec((6, d), lambda i, k: (0, 0)),
                  pl.BlockSpec((1, d, tf), lambda i, k: (layer, 0, k)),
                  pl.BlockSpec((1, d, tf), lambda i, k: (layer, 0, k + nf)),
                  pl.BlockSpec((1, tf, d), lambda i, k: (layer, k, 0))],
        out_specs=pl.BlockSpec((tm, d), lambda i, k: (i, 0)),
        out_shape=jax.ShapeDtypeStruct((t, d), F32),
        scratch_shapes=[pltpu.VMEM((tm, d), BF16), pltpu.VMEM((tm, d), F32)],
        compiler_params=_cparams(("arbitrary", "arbitrary")),
        name="ffn",
    )(x2, mod_l, g_l, w_in, w_in, w_out)


def _proj_kernel(x_ref, mod_ref, g_ref, w_ref, kpos_ref, oa_ref, oi_ref, ow_ref, or_ref, oc_ref, od_ref):
    m = mod_ref[0]
    xn = _rms(x_ref[...], g_ref[2:3])
    h = (xn * (1.0 + m[4:5]) + m[3:4]).astype(BF16)
    off = 0
    for ref in (oa_ref, oi_ref, ow_ref, or_ref, oc_ref, od_ref):
        w = ref.shape[1]
        z = jnp.dot(h, w_ref[:, off:off + w], preferred_element_type=F32)
        if ref is oa_ref or ref is od_ref:
            qscale = (HEAD_DIM ** -0.5) * (LOG2E if ref is oa_ref else 1.0)
            z = jnp.concatenate([z[:, :512] * qscale, z[:, 512:768] + kpos_ref[...], z[:, 768:]], axis=1)
        ref[...] = z.astype(ref.dtype)
        off += w


def _proj_call(x2, mod_l, g_l, w_all, kpos, *, s_len):
    t, d = x2.shape
    tm = min(PROJ_TM, s_len)
    tpb = s_len // tm
    widths = (QKV_W, IDX_W, LANES, 2 * LRU_WIDTH, MLA_W, QKV_W)
    dtypes = (BF16, BF16, F32, F32, F32, BF16)
    assert sum(widths) == w_all.shape[1]
    return pl.pallas_call(
        _proj_kernel,
        grid=(t // tm,),
        in_specs=[pl.BlockSpec((tm, d), lambda i: (i, 0)),
                  pl.BlockSpec((1, 9, d), lambda i: (i // tpb, 0, 0)),
                  pl.BlockSpec((6, d), lambda i: (0, 0)),
                  pl.BlockSpec(w_all.shape, lambda i: (0, 0)),
                  pl.BlockSpec((tm, 256), lambda i: (i % tpb, 0))],
        out_specs=[pl.BlockSpec((tm, w), lambda i: (i, 0)) for w in widths],
        out_shape=[jax.ShapeDtypeStruct((t, w), dt) for w, dt in zip(widths, dtypes)],
        compiler_params=_cparams(("arbitrary",)),
        name="mixer_proj",
    )(x2, mod_l, g_l, w_all, kpos)


def _head_rows(q, n_heads, tq):
    lane = lax.broadcasted_iota(I32, (tq, LANES), 1)
    rows = []
    for h in range(n_heads):
        tile = q[:, LANES * (h // 2):LANES * (h // 2 + 1)].astype(F32)
        keep = (lane >= HEAD_DIM) if (h % 2) else (lane < HEAD_DIM)
        rows.append(jnp.where(keep, tile, 0.0).astype(BF16))
    return jnp.concatenate(rows, axis=0)


def _alibi_lanes(slopes, tq):
    lane = lax.broadcasted_iota(I32, (tq, LANES), 1)
    out = []
    for slope in slopes:
        tile = jnp.zeros((tq, LANES), F32)
        for i, part in enumerate(_bf16_parts(slope)):
            tile = jnp.where(lane == HEAD_DIM + 2 * i, POS_SPLIT * part, tile)
            tile = jnp.where(lane == HEAD_DIM + 2 * i + 1, part, tile)
        out.append(tile)
    return out


def _alibi_q_tiles(q, alibi_lanes, tq):
    lane = lax.broadcasted_iota(I32, (tq, LANES), 1)
    rows = []
    for h, al in enumerate(alibi_lanes):
        tile = q[:, LANES * (h // 2):LANES * (h // 2 + 1)].astype(F32)
        if h % 2:
            tile = pltpu.roll(tile, HEAD_DIM, axis=1)
        rows.append(jnp.where(lane < HEAD_DIM, tile, al).astype(BF16))
    return jnp.concatenate(rows, axis=0)


def _gqa_out(accs, inv_ls, tq):
    lane = lax.broadcasted_iota(I32, (tq, LANES), 1)
    tiles = []
    for j in range(4):
        g = j // 2
        halves = []
        for h in (2 * j, 2 * j + 1):
            o = accs[g][h % 4] * inv_ls[g][h % 4]
            src_hi = (g == 1)
            dst_hi = (h % 2 == 1)
            if src_hi != dst_hi:
                o = pltpu.roll(o, HEAD_DIM, axis=1)
            halves.append(o)
        tiles.append(jnp.where(lane < HEAD_DIM, halves[0], halves[1]))
    return jnp.concatenate(tiles, axis=1)


def _dsa_kernel(q_ref, k_ref, v_ref, iq_ref, ik_ref, iw_ref, o_ref,
                key_ref, s_ref, tau_ref, iqs_ref, qs_ref, m_ref, l_ref, acc_ref, *, tq, kc, topk, pos_bits):
    qb = pl.program_id(1)
    q0 = qb * tq
    nk = lax.shift_right_logical(q0 + tq + kc - 1, _log2(kc))
    n_lt = kc // LANES
    n_rt = kc // CNT_ROWS
    qpos_l = q0 + lax.broadcasted_iota(I32, (kc, tq), 1)
    krow = lax.broadcasted_iota(I32, (kc, tq), 0)
    krow_t = lax.broadcasted_iota(I32, (CNT_ROWS, tq), 0)

    iqs_ref[...] = _head_rows(iq_ref[...], IDX_HEADS, tq)
    iw_t = iw_ref[...].T

    def score_body(j, carry):
        ks = pl.multiple_of(j * kc, kc)
        d = _nt_dot(ik_ref[pl.ds(ks, kc), :], iqs_ref[...])
        acc = jnp.zeros((kc, tq), F32)
        for h in range(IDX_HEADS):
            acc = acc + iw_t[h:h + 1, :] * jnp.maximum(d[:, h * tq:(h + 1) * tq], 0.0)
        bits = pltpu.bitcast(acc, I32)
        key = jnp.where(bits < 0, bits ^ INT_MAX, bits)
        key_ref[j] = jnp.where(ks + krow <= qpos_l, key, INT_MIN)
        return carry

    _loop_groups(nk, score_body)

    def count(pred):
        def body(j, cnt):
            for r in range(n_rt):
                kp = (j * kc + r * CNT_ROWS) + krow_t
                cnt = cnt + jnp.where(pred(key_ref[j, r * CNT_ROWS:(r + 1) * CNT_ROWS, :], kp), 1.0, 0.0)
            return cnt
        cnt = _loop_groups(nk, body, jnp.zeros((CNT_ROWS, tq), F32))
        return jnp.sum(cnt, axis=0, keepdims=True)

    kf = float(topk)

    def bit_step(i, state):
        tau, cnt = state
        cand = tau + lax.shift_left(jnp.int32(1), 31 - i)
        cb = jnp.broadcast_to(cand, (CNT_ROWS, tq))
        c = count(lambda kv, kp: kv >= cb)
        ok = c >= kf
        return jnp.where(ok, cand, tau), jnp.where(ok, c, cnt)

    tau, n_ge = lax.fori_loop(0, 32, bit_step, (jnp.full((1, tq), INT_MIN, I32),
                                                 jnp.broadcast_to((nk * kc).astype(F32), (1, tq))))

    tie_f = jnp.where((n_ge > kf) & (tau > INT_MIN), 1.0, 0.0)
    any_tie = jnp.max(tie_f) > 0.0
    tau_c = jnp.maximum(tau, INT_MIN + 1)
    tau_ref[...] = jnp.broadcast_to(tau_c, tau_ref.shape)

    @pl.when(any_tie)
    def _():
        tb = jnp.broadcast_to(tau_c, (CNT_ROWS, tq))
        need = kf - count(lambda kv, kp: kv > tb)

        def pos_body(i, p):
            cand = p | lax.shift_left(jnp.int32(1), pos_bits - 1 - i)
            cb = jnp.broadcast_to(cand, (CNT_ROWS, tq))
            below = count(lambda kv, kp: (kv == tb) & (kp < cb))
            return jnp.where(below < need, cand, p)

        p = lax.fori_loop(0, pos_bits, pos_body, jnp.zeros((1, tq), I32))
        p = jnp.where(tie_f > 0.0, p, INT_MAX)

        def rewrite_body(j, carry):
            kj = key_ref[j]
            kp = j * kc + krow
            sel = (kj > tau_c) | ((kj == tau_c) & (kp <= p))
            key_ref[j] = jnp.where(sel, 1, -1)
            return carry

        lax.fori_loop(0, nk, rewrite_body, 0)
        tau_ref[...] = jnp.zeros(tau_ref.shape, I32)

    q = q_ref[...]
    for g in range(A_KV_HEADS):
        qs_ref[g] = _alibi_q_tiles(q[:, 4 * HEAD_DIM * g:4 * HEAD_DIM * (g + 1)],
                                   _alibi_lanes([s * LOG2E for s in SLOPES_A[4 * g:4 * g + 4]], tq), tq)
    m_ref[...] = jnp.full(m_ref.shape, NEG, F32)
    l_ref[...] = jnp.zeros(l_ref.shape, F32)
    acc_ref[...] = jnp.zeros(acc_ref.shape, F32)

    def max_body(j, carry):
        ks = pl.multiple_of(j * kc, kc)
        sel = jnp.where(key_ref[j] >= tau_ref[0:1, :], 1.0, 0.0).T > 0.5
        for g in range(A_KV_HEADS):
            s = _nt_dot(qs_ref[g], k_ref[pl.ds(ks, kc), g * LANES:(g + 1) * LANES])
            for h in range(4):
                sh = jnp.where(sel, s[h * tq:(h + 1) * tq], NEG)
                s_ref[g, h, j] = sh
                mp = m_ref[g, h]
                for c in range(n_lt):
                    mp = jnp.maximum(mp, sh[:, c * LANES:(c + 1) * LANES])
                m_ref[g, h] = mp
        return carry

    _loop_groups(nk, max_body)
    for g in range(A_KV_HEADS):
        for h in range(4):
            m_ref[g, h] = jnp.broadcast_to(jnp.max(m_ref[g, h], axis=1, keepdims=True), (tq, LANES))

    def pv_body(j, carry):
        ks = pl.multiple_of(j * kc, kc)
        vch = v_ref[pl.ds(ks, kc), :]
        for g in range(A_KV_HEADS):
            ps = []
            for h in range(4):
                mb = m_ref[g, h]
                lp = l_ref[g, h]
                tiles = []
                for c in range(n_lt):
                    p = jnp.exp2(s_ref[g, h, j, :, c * LANES:(c + 1) * LANES] - mb)
                    lp = lp + p
                    tiles.append(p.astype(BF16))
                l_ref[g, h] = lp
                ps.append(jnp.concatenate(tiles, axis=1))
            pv = jnp.dot(jnp.concatenate(ps, axis=0), vch, preferred_element_type=F32)
            for h in range(4):
                acc_ref[g, h] += pv[h * tq:(h + 1) * tq]
        return carry

    _loop_groups(nk, pv_body)
    accs = [[acc_ref[g, h] for h in range(4)] for g in range(A_KV_HEADS)]
    inv_ls = [[1.0 / jnp.sum(l_ref[g, h], axis=1, keepdims=True) for h in range(4)] for g in range(A_KV_HEADS)]
    o_ref[...] = _gqa_out(accs, inv_ls, tq).astype(o_ref.dtype)


def _dsa_call(qkv, idx, iw, *, batch, s_len):
    t = qkv.shape[0]
    tq = min(DSA_TQ, s_len)
    kc = min(DSA_KC, s_len)
    topk = min(TOPK_MAX, s_len // 4)
    assert kc >= topk and kc % tq == 0
    nqb = s_len // tq
    once = pl.Buffered(1)
    return pl.pallas_call(
        functools.partial(_dsa_kernel, tq=tq, kc=kc, topk=topk, pos_bits=_log2(s_len)),
        grid=(batch, nqb),
        in_specs=[pl.BlockSpec((tq, 512), lambda b, i: (b * nqb + i, 0)),
                  pl.BlockSpec((s_len, 256), lambda b, i: (b, 2), pipeline_mode=once),
                  pl.BlockSpec((s_len, 128), lambda b, i: (b, 6), pipeline_mode=once),
                  pl.BlockSpec((tq, 512), lambda b, i: (b * nqb + i, 0)),
                  pl.BlockSpec((s_len, 128), lambda b, i: (b, 4), pipeline_mode=once),
                  pl.BlockSpec((tq, LANES), lambda b, i: (b * nqb + i, 0))],
        out_specs=pl.BlockSpec((tq, 512), lambda b, i: (b * nqb + i, 0)),
        out_shape=jax.ShapeDtypeStruct((t, 512), BF16),
        scratch_shapes=[pltpu.VMEM((s_len // kc, kc, tq), I32),
                        pltpu.VMEM((A_KV_HEADS, 4, s_len // kc, tq, kc), F32),
                        pltpu.VMEM((8, tq), I32),
                        pltpu.VMEM((IDX_HEADS * tq, LANES), BF16),
                        pltpu.VMEM((A_KV_HEADS, 4 * tq, LANES), BF16),
                        pltpu.VMEM((A_KV_HEADS, 4, tq, LANES), F32),
                        pltpu.VMEM((A_KV_HEADS, 4, tq, LANES), F32),
                        pltpu.VMEM((A_KV_HEADS, 4, tq, LANES), F32)],
        compiler_params=pltpu.CompilerParams(dimension_semantics=("arbitrary", "arbitrary"),
                                             vmem_limit_bytes=DSA_VMEM_LIMIT),
        name="dsa",
    )(qkv, qkv, qkv, idx, idx, iw)


def _swa_kernel(q_ref, kp_ref, kc_ref, vp_ref, vc_ref, sink_ref, o_ref, *, tq, nblk):
    i = pl.program_id(1)
    krow = lax.broadcasted_iota(I32, (2 * tq, tq), 0)
    qcol = lax.broadcasted_iota(I32, (2 * tq, tq), 1)
    dist = qcol + tq - krow
    in_band = (dist >= 0) & (dist < WINDOW)
    sinks = sink_ref[...]
    sink_vec = [jnp.broadcast_to(sinks[:, h:h + 1], (1, tq)) for h in range(D_HEADS)]
    alibi = _alibi_lanes(SLOPES_D, tq)
    for n in range(nblk):
        blk = i * nblk + n
        rows = slice(n * tq, (n + 1) * tq)
        q = q_ref[rows, :]
        k_prev = kp_ref[...] if n == 0 else kc_ref[(n - 1) * tq:n * tq, :]
        v_prev = vp_ref[...] if n == 0 else vc_ref[(n - 1) * tq:n * tq, :]
        kk = jnp.concatenate([k_prev, kc_ref[rows, :]], axis=0)
        vv_t = jnp.concatenate([v_prev, vc_ref[rows, :]], axis=0).T
        valid = in_band & ((blk * tq - tq + krow) >= 0)
        qpos = (blk * tq + lax.broadcasted_iota(I32, (1, tq), 1)).astype(F32)
        tiles = []
        for g in range(D_KV_HEADS):
            qs = _alibi_q_tiles(q[:, 4 * HEAD_DIM * g:4 * HEAD_DIM * (g + 1)], alibi[4 * g:4 * g + 4], tq)
            s_t = _nt_dot(kk[:, g * LANES:(g + 1) * LANES], qs)
            ps, ils = [], []
            for h in range(4):
                sh = jnp.where(valid, s_t[:, h * tq:(h + 1) * tq], NEG)
                sink = sink_vec[4 * g + h] + SLOPES_D[4 * g + h] * qpos
                m = jnp.maximum(jnp.max(sh, axis=0, keepdims=True), sink)
                p = jnp.exp(sh - m)
                ils.append(1.0 / (jnp.sum(p, axis=0, keepdims=True) + jnp.exp(sink - m)))
                ps.append(p.astype(BF16))
            o_t = jnp.dot(vv_t, jnp.concatenate(ps, axis=1), preferred_element_type=F32)
            o_t = o_t[g * HEAD_DIM:(g + 1) * HEAD_DIM, :] * jnp.concatenate(ils, axis=1)
            for j in range(2):
                pair = jnp.concatenate([o_t[:, (2 * j) * tq:(2 * j + 1) * tq],
                                        o_t[:, (2 * j + 1) * tq:(2 * j + 2) * tq]], axis=0)
                tiles.append(pair.T)
        o_ref[rows, :] = jnp.concatenate(tiles, axis=1).astype(o_ref.dtype)


def _swa_call(qkv, sinks_pad, *, batch, s_len):
    t = qkv.shape[0]
    tq = BLOCK
    nblk = min(SWA_BLOCKS, s_len // tq)
    ts = nblk * tq
    nst = s_len // ts
    cur = lambda b, i: b * nst + i
    prev = lambda b, i: jnp.maximum((b * nst + i) * nblk - 1, 0)
    return pl.pallas_call(
        functools.partial(_swa_kernel, tq=tq, nblk=nblk),
        grid=(batch, nst),
        in_specs=[pl.BlockSpec((ts, 512), lambda b, i: (cur(b, i), 0)),
                  pl.BlockSpec((tq, 256), lambda b, i: (prev(b, i), 2)),
                  pl.BlockSpec((ts, 256), lambda b, i: (cur(b, i), 2)),
                  pl.BlockSpec((tq, 128), lambda b, i: (prev(b, i), 6)),
                  pl.BlockSpec((ts, 128), lambda b, i: (cur(b, i), 6)),
                  pl.BlockSpec((1, LANES), lambda b, i: (0, 0))],
        out_specs=pl.BlockSpec((ts, 512), lambda b, i: (cur(b, i), 0)),
        out_shape=jax.ShapeDtypeStruct((t, 512), BF16),
        compiler_params=_cparams(("arbitrary", "arbitrary")),
        name="swa",
    )(qkv, qkv, qkv, qkv, qkv, sinks_pad)


def _mla_prep_kernel(c_ref, cos_ref, sin_ref, gq_ref, gkv_ref, wq_ref, wqr_ref, wk_ref, wv_ref,
                     q_ref, k_ref, v_ref):
    c = c_ref[...]
    cos = cos_ref[...]
    sin = sin_ref[...]
    cos8 = jnp.concatenate([cos] * C_HEADS, axis=1)
    sin8 = jnp.concatenate([sin] * C_HEADS, axis=1)
    cqn = _rms(c[:, :Q_LORA], gq_ref[...]).astype(BF16)
    q = (jnp.dot(cqn, wq_ref[...], preferred_element_type=F32) * cos8
         + jnp.dot(cqn, wqr_ref[...], preferred_element_type=F32) * sin8)
    q_ref[...] = q.astype(BF16)
    ckvn = _rms(c[:, Q_LORA:Q_LORA + KV_LORA], gkv_ref[...]).astype(BF16)
    kr = c[:, 384:512] * cos + c[:, 512:640] * sin
    k = jnp.dot(ckvn, wk_ref[...], preferred_element_type=F32) + jnp.concatenate([kr] * C_HEADS, axis=1)
    k_ref[...] = k.astype(BF16)
    v_ref[...] = jnp.dot(ckvn, wv_ref[...], preferred_element_type=F32).astype(BF16)


def _mla_prep_call(cm, cos_t, sin_t, gq, gkv, wq, wqr, wk, wv, *, s_len):
    t = cm.shape[0]
    tm = min(PROJ_TM, s_len)
    tpb = s_len // tm
    full = lambda a: pl.BlockSpec(a.shape, lambda i: (0, 0))
    return pl.pallas_call(
        _mla_prep_kernel,
        grid=(t // tm,),
        in_specs=[pl.BlockSpec((tm, MLA_W), lambda i: (i, 0)),
                  pl.BlockSpec((tm, LANES), lambda i: (i % tpb, 0)),
                  pl.BlockSpec((tm, LANES), lambda i: (i % tpb, 0)),
                  full(gq), full(gkv), full(wq), full(wqr), full(wk), full(wv)],
        out_specs=[pl.BlockSpec((tm, 1024), lambda i: (i, 0)),
                   pl.BlockSpec((tm, 1024), lambda i: (i, 0)),
                   pl.BlockSpec((tm, 512), lambda i: (i, 0))],
        out_shape=[jax.ShapeDtypeStruct((t, 1024), BF16),
                   jax.ShapeDtypeStruct((t, 1024), BF16),
                   jax.ShapeDtypeStruct((t, 512), BF16)],
        compiler_params=_cparams(("arbitrary",)),
        name="mla_prep",
    )(cm, cos_t, sin_t, gq, gkv, wq, wqr, wk, wv)


def _mla_attn_kernel(q_ref, k_ref, v_ref, o_ref, s_ref, m_ref, l_ref, acc_ref, *, tq, kc):
    qb = pl.program_id(2)
    q0 = qb * tq
    n_full = lax.shift_right_logical(q0, _log2(kc))
    rowpos = q0 + lax.broadcasted_iota(I32, (tq, kc), 0)
    lane = lax.broadcasted_iota(I32, (tq, kc), 1)
    c = ((QK_NOPE + QK_ROPE) ** -0.5) * LOG2E
    n_lt = kc // LANES
    m_ref[...] = jnp.full(m_ref.shape, NEG, F32)
    l_ref[...] = jnp.zeros(l_ref.shape, F32)
    acc_ref[...] = jnp.zeros(acc_ref.shape, F32)

    def max_step(j, masked):
        ks = pl.multiple_of(j * kc, kc)
        for hh in range(2):
            s = _nt_dot(q_ref[:, hh * LANES:(hh + 1) * LANES],
                        k_ref[pl.ds(ks, kc), hh * LANES:(hh + 1) * LANES]) * c
            if masked:
                s = jnp.where(ks + lane <= rowpos, s, NEG)
            s_ref[hh, j] = s
            mp = m_ref[hh]
            for t in range(n_lt):
                mp = jnp.maximum(mp, s[:, t * LANES:(t + 1) * LANES])
            m_ref[hh] = mp

    def max_body(j, carry):
        max_step(j, False)
        return carry

    _loop_groups(n_full, max_body)
    max_step(n_full, True)
    for hh in range(2):
        m_ref[hh] = jnp.broadcast_to(jnp.max(m_ref[hh], axis=1, keepdims=True), (tq, LANES))

    def pv_body(j, carry):
        ks = pl.multiple_of(j * kc, kc)
        vch = v_ref[pl.ds(ks, kc), :]
        for hh in range(2):
            mb = m_ref[hh]
            lp = l_ref[hh]
            tiles = []
            for t in range(n_lt):
                p = jnp.exp2(s_ref[hh, j, :, t * LANES:(t + 1) * LANES] - mb)
                lp = lp + p
                tiles.append(p.astype(BF16))
            l_ref[hh] = lp
            acc_ref[hh] += jnp.dot(jnp.concatenate(tiles, axis=1), vch, preferred_element_type=F32)
        return carry

    _loop_groups(n_full + 1, pv_body)
    outs = [acc_ref[hh] * (1.0 / jnp.sum(l_ref[hh], axis=1, keepdims=True)) for hh in range(2)]
    lane_o = lax.broadcasted_iota(I32, (tq, LANES), 1)
    o_ref[...] = jnp.where(lane_o < V_DIM, outs[0], outs[1]).astype(o_ref.dtype)


def _mla_attn_call(qm, km, vm, *, batch, s_len):
    t = qm.shape[0]
    tq = min(MLA_TQ, s_len)
    kc = min(MLA_KC, s_len)
    assert kc % tq == 0
    nqb = s_len // tq
    once = pl.Buffered(1)
    return pl.pallas_call(
        functools.partial(_mla_attn_kernel, tq=tq, kc=kc),
        grid=(batch, C_HEADS // 2, nqb),
        in_specs=[pl.BlockSpec((tq, 256), lambda b, h, i: (b * nqb + i, h)),
                  pl.BlockSpec((s_len, 256), lambda b, h, i: (b, h), pipeline_mode=once),
                  pl.BlockSpec((s_len, 128), lambda b, h, i: (b, h), pipeline_mode=once)],
        out_specs=pl.BlockSpec((tq, 128), lambda b, h, i: (b * nqb + i, h)),
        out_shape=jax.ShapeDtypeStruct((t, 512), BF16),
        scratch_shapes=[pltpu.VMEM((2, s_len // kc, tq, kc), F32)] + [pltpu.VMEM((2, tq, LANES), F32)] * 3,
        compiler_params=_cparams(("arbitrary", "arbitrary", "arbitrary")),
        name="mla_attn",
    )(qm, km, vm)


def _lru_kernel(xr_ref, xg_ref, prev_ref, cw_ref, cb_ref, wa_ref, ba_ref, wx_ref, bx_ref, lam_ref,
                o_ref, xe_ref, h_ref, *, ln):
    i = pl.program_id(1)
    xe_ref[0:8, :] = jnp.where(i > 0, prev_ref[...], 0.0)
    xe_ref[8:8 + ln, :] = xr_ref[...]
    cw = cw_ref[...]
    xc = cb_ref[...] + cw[0:1] * xe_ref[pl.ds(5, ln), :]
    for j in range(1, CONV_WIDTH):
        xc = xc + cw[j:j + 1] * xe_ref[pl.ds(5 + j, ln), :]
    xcb = xc.astype(BF16)
    r = jax.nn.sigmoid(jnp.dot(xcb, wa_ref[...], preferred_element_type=F32) + ba_ref[...])
    gi = jax.nn.sigmoid(jnp.dot(xcb, wx_ref[...], preferred_element_type=F32) + bx_ref[...])
    z = -lam_ref[...]
    softplus = jnp.maximum(z, 0.0) + jnp.log1p(jnp.exp(-jnp.abs(z)))
    log_a = -LRU_C * r * softplus
    a = jnp.exp(log_a)
    b = jnp.sqrt(-_expm1(2.0 * log_a)) * (gi * xc)
    row = lax.broadcasted_iota(I32, (ln, LRU_WIDTH), 0)
    d = 1
    while d < ln:
        keep = row >= d
        b = jnp.where(keep, a * pltpu.roll(b, d, axis=0) + b, b)
        a = jnp.where(keep, a * pltpu.roll(a, d, axis=0), a)
        d *= 2
    h_prev = jnp.where(i > 0, h_ref[0:1, :], 0.0)
    h = a * h_prev + b
    h_ref[0:1, :] = h[ln - 1:ln, :]
    xg = xg_ref[...]
    gelu = 0.5 * xg * (1.0 + jnp.tanh(0.7978845608028654 * (xg + 0.044715 * (xg * xg * xg))))
    o_ref[...] = (h * gelu).astype(o_ref.dtype)


def _lru_call(rm, cw, cb, wa, ba, wx, bx, lam, *, batch, s_len):
    t = rm.shape[0]
    ln = min(LRU_L, s_len)
    nt = s_len // ln
    w = LRU_WIDTH
    vec = lambda a: pl.BlockSpec(a.shape, lambda b, i: (0, 0))
    return pl.pallas_call(
        functools.partial(_lru_kernel, ln=ln),
        grid=(batch, nt),
        in_specs=[pl.BlockSpec((ln, w), lambda b, i: (b * nt + i, 0)),
                  pl.BlockSpec((ln, w), lambda b, i: (b * nt + i, 1)),
                  pl.BlockSpec((8, w), lambda b, i: (jnp.maximum((b * nt + i) * (ln // 8) - 1, 0), 0)),
                  vec(cw), vec(cb), vec(wa), vec(ba), vec(wx), vec(bx), vec(lam)],
        out_specs=pl.BlockSpec((ln, w), lambda b, i: (b * nt + i, 0)),
        out_shape=jax.ShapeDtypeStruct((t, w), BF16),
        scratch_shapes=[pltpu.VMEM((ln + 8, w), F32), pltpu.VMEM((8, w), F32)],
        compiler_params=_cparams(("arbitrary", "arbitrary")),
        name="rglru",
    )(rm, rm, rm, cw, cb, wa, ba, wx, bx, lam)


def _merge_kernel(x_ref, mod_ref, g_ref, ya_ref, yb_ref, yc_ref, yd_ref, wg_ref, wb_ref, wo_ref, o_ref):
    m = mod_ref[0]
    x = x_ref[...]
    h = (_rms(x, g_ref[2:3]) * (1.0 + m[4:5]) + m[3:4]).astype(BF16)
    merged = None
    for n, y_ref in enumerate((ya_ref, yb_ref, yc_ref, yd_ref)):
        gate = jax.nn.sigmoid(jnp.dot(h, wg_ref[:, n * D_MODEL:(n + 1) * D_MODEL], preferred_element_type=F32))
        term = gate * jnp.dot(y_ref[...], wb_ref[n], preferred_element_type=F32)
        merged = term if merged is None else merged + term
    y = jnp.dot(merged.astype(BF16), wo_ref[...], preferred_element_type=F32)
    o_ref[...] = x + m[5:6] * _rms(y, g_ref[3:4])


def _merge_call(x2, mod_l, g_l, ys, w_gate, w_branch, w_out, *, s_len):
    t, d = x2.shape
    tm = min(MERGE_TM, s_len)
    tpb = s_len // tm
    return pl.pallas_call(
        _merge_kernel,
        grid=(t // tm,),
        in_specs=[pl.BlockSpec((tm, d), lambda i: (i, 0)),
                  pl.BlockSpec((1, 9, d), lambda i: (i // tpb, 0, 0)),
                  pl.BlockSpec((6, d), lambda i: (0, 0))]
                 + [pl.BlockSpec((tm, BRANCH_WIDTH), lambda i: (i, 0))] * N_BRANCH
                 + [pl.BlockSpec(w_gate.shape, lambda i: (0, 0)),
                    pl.BlockSpec(w_branch.shape, lambda i: (0, 0, 0)),
                    pl.BlockSpec(w_out.shape, lambda i: (0, 0))],
        out_specs=pl.BlockSpec((tm, d), lambda i: (i, 0)),
        out_shape=jax.ShapeDtypeStruct((t, d), F32),
        compiler_params=_cparams(("arbitrary",)),
        name="merge",
    )(x2, mod_l, g_l, *ys, w_gate, w_branch, w_out)


def _mixer_weights(w_in_l):
    o = IN_OFFSETS
    col = lambda n: w_in_l[:, o[n]:o[n + 1]]
    z = lambda n: jnp.zeros((D_MODEL, n), w_in_l.dtype)

    def pad_heads(k):
        return jnp.concatenate([k[:, :64], z(64), k[:, 64:], z(64)], axis=1)

    a_q, a_k, a_v, i_q, i_k, i_w, r_x, r_g, c_q, c_kv, d_q, d_k, d_v = (col(n) for n in range(13))
    k_rope = c_kv[:, KV_LORA:]
    half = QK_ROPE // 2
    k_rope_rot = jnp.concatenate([-k_rope[:, half:], k_rope[:, :half]], axis=1)
    groups = [a_q, pad_heads(a_k), a_v,
              i_q, i_k, i_k,
              i_w, z(LANES - IDX_HEADS),
              r_x, r_g,
              c_q, c_kv[:, :KV_LORA], z(64), k_rope, z(32), z(64), k_rope_rot, z(32),
              d_q, pad_heads(d_k), d_v]
    w_all = jnp.concatenate(groups, axis=1).astype(BF16)
    w_gate = col(13).astype(BF16)
    return w_all, w_gate


def _kpos_table(s_len):
    pos = np.arange(s_len)
    tile = np.zeros((s_len, LANES), np.float32)
    for i in range(3):
        tile[:, HEAD_DIM + 2 * i] = pos // POS_SPLIT
        tile[:, HEAD_DIM + 2 * i + 1] = pos % POS_SPLIT
    return jnp.asarray(np.concatenate([tile, tile], axis=1))


def _mla_weights(w_uq, w_ukv):
    dq = QK_NOPE + QK_ROPE
    half = QK_ROPE // 2
    zq = lambda n: jnp.zeros((Q_LORA, n), w_uq.dtype)
    wq, wqr, wk, wv = [], [], [], []
    for h in range(C_HEADS):
        nope = w_uq[:, h * dq:h * dq + QK_NOPE]
        r1 = w_uq[:, h * dq + QK_NOPE:h * dq + QK_NOPE + half]
        r2 = w_uq[:, h * dq + QK_NOPE + half:(h + 1) * dq]
        wq += [nope, r1, r2, zq(32)]
        wqr += [zq(64), -r2, r1, zq(32)]
        wk += [w_ukv[:, h * 128:h * 128 + QK_NOPE], jnp.zeros((KV_LORA, 64), w_ukv.dtype)]
        wv += [w_ukv[:, h * 128 + QK_NOPE:(h + 1) * 128]]
    cat = lambda xs: jnp.concatenate(xs, axis=1).astype(BF16)
    return cat(wq), cat(wqr), cat(wk), cat(wv)


def _rope_tables(s_len):
    half = QK_ROPE // 2
    inv = (np.float32(ROPE_THETA) ** (-np.arange(half, dtype=np.float32) / np.float32(half))).astype(np.float32)
    ang = (np.arange(s_len, dtype=np.float32)[:, None] * inv[None, :]).astype(np.float32)
    cos, sin = np.cos(ang.astype(np.float64)).astype(np.float32), np.sin(ang.astype(np.float64)).astype(np.float32)
    ones = lambda n: np.ones((s_len, n), np.float32)
    zeros = lambda n: np.zeros((s_len, n), np.float32)
    cos_t = np.concatenate([ones(64), cos, cos, ones(32)], axis=1)
    sin_t = np.concatenate([zeros(64), sin, sin, zeros(32)], axis=1)
    return jnp.asarray(cos_t), jnp.asarray(sin_t)


def _block_diag(w):
    n, bw, _ = w.shape
    eye = jnp.eye(n, dtype=w.dtype)
    return (w[:, :, None, :] * eye[:, None, :, None]).reshape(n * bw, n * bw).astype(BF16)


def kernel(x, c, w_ada, b_ada, norm_g, ffn1_w_in, ffn1_w_out, w_in, conv_w, conv_b, lru_wa, lru_ba, lru_wx,
           lru_bx, lru_lambda, mla_g_q, mla_g_kv, mla_w_uq, mla_w_ukv, swa_sinks, w_branch, w_out, ffn2_w_in,
           ffn2_w_out):
    batch, s_len, d = x.shape
    depth = w_ada.shape[0]
    t = batch * s_len
    assert s_len // POS_SPLIT <= 256
    x2 = x.reshape(t, d)
    c_pad = jnp.zeros((8, d), F32).at[:batch].set(c)
    mod_all = _ada_call(c_pad, w_ada, b_ada)
    cos_t, sin_t = _rope_tables(s_len)
    kpos = _kpos_table(s_len)
    row = lambda v: v.reshape(1, -1)
    for l in range(depth):
        mod_l = mod_all[l, :batch].reshape(batch, 9, d)
        g_l = norm_g[l]
        x2 = _ffn_call(x2, mod_l, g_l, ffn1_w_in, ffn1_w_out, l, sub=0, resid_w=0.5, s_len=s_len)
        w_all, w_gate = _mixer_weights(w_in[l])
        qkv_a, idx, iw, rm, cm, qkv_d = _proj_call(x2, mod_l, g_l, w_all, kpos, s_len=s_len)
        y_a = _dsa_call(qkv_a, idx, iw, batch=batch, s_len=s_len)
        y_b = _lru_call(rm, conv_w[l], row(conv_b[l]), _block_diag(lru_wa[l]), row(lru_ba[l]),
                        _block_diag(lru_wx[l]), row(lru_bx[l]), row(lru_lambda[l]), batch=batch, s_len=s_len)
        qm, km, vm = _mla_prep_call(cm, cos_t, sin_t, row(mla_g_q[l]), row(mla_g_kv[l]),
                                    *_mla_weights(mla_w_uq[l], mla_w_ukv[l]), s_len=s_len)
        y_c = _mla_attn_call(qm, km, vm, batch=batch, s_len=s_len)
        sinks_pad = jnp.zeros((1, LANES), F32).at[0, :D_HEADS].set(swa_sinks[l])
        y_d = _swa_call(qkv_d, sinks_pad, batch=batch, s_len=s_len)
        x2 = _merge_call(x2, mod_l, g_l, (y_a, y_b, y_c, y_d), w_gate, w_branch[l].astype(BF16),
                         w_out[l].astype(BF16), s_len=s_len)
        x2 = _ffn_call(x2, mod_l, g_l, ffn2_w_in, ffn2_w_out, l, sub=2, resid_w=0.5, s_len=s_len)
    return x2.reshape(batch, s_len, d)
```

```python
import functools

import numpy as np
import jax
import jax.numpy as jnp
from jax import lax
from jax.experimental import pallas as pl
from jax.experimental.pallas import tpu as pltpu

F32 = jnp.float32
BF16 = jnp.bfloat16
I32 = jnp.int32

D_MODEL = 1024
HEAD_DIM = 64
BLOCK = 128
EPS = 1e-6
NEG = -1e30
A_HEADS = 8
A_KV_HEADS = 2
IDX_HEADS = 8
IDX_DIM = 64
TOPK_MAX = 256
LRU_WIDTH = 512
LRU_BLOCKS = 8
CONV_WIDTH = 4
LRU_C = 8.0
C_HEADS = 8
Q_LORA = 256
KV_LORA = 128
QK_NOPE = 64
QK_ROPE = 32
V_DIM = 64
ROPE_THETA = 10000.0
D_HEADS = 8
D_KV_HEADS = 2
WINDOW = 128
N_BRANCH = 4
BRANCH_WIDTH = 512
N_ALIBI = A_HEADS + D_HEADS
D_FF = 2816
IN_SPLITS = (A_HEADS * HEAD_DIM, A_KV_HEADS * HEAD_DIM, A_KV_HEADS * HEAD_DIM,
             IDX_HEADS * IDX_DIM, IDX_DIM, IDX_HEADS,
             LRU_WIDTH, LRU_WIDTH,
             Q_LORA, KV_LORA + QK_ROPE,
             D_HEADS * HEAD_DIM, D_KV_HEADS * HEAD_DIM, D_KV_HEADS * HEAD_DIM,
             N_BRANCH * D_MODEL)
IN_OFFSETS = tuple(int(v) for v in np.concatenate([[0], np.cumsum(IN_SPLITS)]))

LANES = 128
INT_MIN = -2 ** 31
INT_MAX = 2 ** 31 - 1
VMEM_LIMIT = 56 * 1024 * 1024
DSA_VMEM_LIMIT = 62 * 1024 * 1024
LOG2E = 1.4426950408889634
POS_SPLIT = 64
LOOP_GROUP = 16
CNT_ROWS = 64

QKV_W = 512 + 256 + 128
IDX_W = 512 + 128
MLA_W = 640

FFN_TM = 1024
FFN_TF = 256
PROJ_TM = 512
DSA_TQ = 128
DSA_KC = 256
MLA_TQ = 512
MLA_KC = 512
SWA_BLOCKS = 4
LRU_L = 256
MERGE_TM = 512


def _alibi(i):
    return float(2.0 ** (-8.0 * i / N_ALIBI))


SLOPES_D = tuple(_alibi(i) for i in range(1, D_HEADS + 1))
SLOPES_A = tuple(_alibi(i) for i in range(D_HEADS + 1, N_ALIBI + 1))


def _bf16_parts(x):
    parts = []
    rem = np.float32(x)
    for _ in range(3):
        p = np.float32(np.asarray(rem, np.float32).astype(jnp.bfloat16).astype(np.float32))
        parts.append(float(p))
        rem = np.float32(rem - p)
    return parts


def _cparams(sem):
    return pltpu.CompilerParams(dimension_semantics=sem, vmem_limit_bytes=VMEM_LIMIT)


def _rms(x, g):
    return x * lax.rsqrt(jnp.mean(x * x, axis=-1, keepdims=True) + EPS) * g


def _nt_dot(a, b):
    return lax.dot_general(a, b, (((1,), (1,)), ((), ())), preferred_element_type=F32)


def _expm1(y):
    u = jnp.exp(y)
    safe = (u != 1.0) & (y > -1.0)
    ratio = y / jnp.log(jnp.where(safe, u, 2.0))
    return jnp.where(u == 1.0, y, jnp.where(safe, (u - 1.0) * ratio, u - 1.0))


def _loop_groups(n, body, init=0):
    carry, start, width = init, 0, LOOP_GROUP
    while width >= 1:
        def group(i, carry, start=start, width=width):
            for u in range(width):
                carry = body(start + width * i + u, carry)
            return carry

        trips = lax.shift_right_logical(n - start, _log2(width))
        carry = lax.fori_loop(0, trips, group, carry)
        start = start + trips * width
        width //= 2
    return carry


def _log2(n):
    l = int(n).bit_length() - 1
    assert (1 << l) == n
    return l


def _ada_kernel(c_ref, w_ref, b_ref, o_ref):
    c = c_ref[...]
    sc = (c * jax.nn.sigmoid(c)).astype(BF16)
    o_ref[0] = jnp.dot(sc, w_ref[0].astype(BF16), preferred_element_type=F32) + b_ref[0]


def _ada_call(c_pad, w_ada, b_ada):
    depth, d, n = w_ada.shape
    tn = 1152
    return pl.pallas_call(
        _ada_kernel,
        grid=(depth, n // tn),
        in_specs=[pl.BlockSpec((c_pad.shape[0], d), lambda l, j: (0, 0)),
                  pl.BlockSpec((1, d, tn), lambda l, j: (l, 0, j)),
                  pl.BlockSpec((1, 1, tn), lambda l, j: (l, 0, j))],
        out_specs=pl.BlockSpec((1, c_pad.shape[0], tn), lambda l, j: (l, 0, j)),
        out_shape=jax.ShapeDtypeStruct((depth, c_pad.shape[0], n), F32),
        compiler_params=_cparams(("arbitrary", "arbitrary")),
        name="adaln",
    )(c_pad, w_ada, b_ada.reshape(depth, 1, n))


def _ffn_kernel(x_ref, mod_ref, g_ref, wg_ref, wu_ref, wo_ref, o_ref, h_ref, acc_ref, *, sub, resid_w):
    k = pl.program_id(1)
    m = mod_ref[0]

    @pl.when(k == 0)
    def _():
        x = x_ref[...]
        xn = _rms(x, g_ref[2 * sub:2 * sub + 1])
        h_ref[...] = (xn * (1.0 + m[3 * sub + 1:3 * sub + 2]) + m[3 * sub:3 * sub + 1]).astype(BF16)
        acc_ref[...] = jnp.zeros_like(acc_ref)

    h = h_ref[...]
    gate = jnp.dot(h, wg_ref[0].astype(BF16), preferred_element_type=F32)
    up = jnp.dot(h, wu_ref[0].astype(BF16), preferred_element_type=F32)
    act = (gate * jax.nn.sigmoid(gate) * up).astype(BF16)
    acc_ref[...] += jnp.dot(act, wo_ref[0].astype(BF16), preferred_element_type=F32)

    @pl.when(k == pl.num_programs(1) - 1)
    def _():
        yn = _rms(acc_ref[...], g_ref[2 * sub + 1:2 * sub + 2])
        o_ref[...] = x_ref[...] + resid_w * m[3 * sub + 2:3 * sub + 3] * yn


def _ffn_call(x2, mod_l, g_l, w_in, w_out, layer, *, sub, resid_w, s_len):
    t, d = x2.shape
    f = w_out.shape[1]
    tm = min(FFN_TM, s_len)
    tf = FFN_TF
    nf = f // tf
    tpb = s_len // tm
    return pl.pallas_call(
        functools.partial(_ffn_kernel, sub=sub, resid_w=resid_w),
        grid=(t // tm, nf),
        in_specs=[pl.BlockSpec((tm, d), lambda i, k: (i, 0)),
                  pl.BlockSpec((1, 9, d), lambda i, k: (i // tpb, 0, 0)),
                  pl.BlockSpec((6, d), lambda i, k: (0, 0)),
                  pl.BlockSpec((1, d, tf), lambda i, k: (layer, 0, k)),
                  pl.BlockSpec((1, d, tf), lambda i, k: (layer, 0, k + nf)),
                  pl.BlockSpec((1, tf, d), lambda i, k: (layer, k, 0))],
        out_specs=pl.BlockSpec((tm, d), lambda i, k: (i, 0)),
        out_shape=jax.ShapeDtypeStruct((t, d), F32),
        scratch_shapes=[pltpu.VMEM((tm, d), BF16), pltpu.VMEM((tm, d), F32)],
        compiler_params=_cparams(("arbitrary", "arbitrary")),
        name="ffn",
    )(x2, mod_l, g_l, w_in, w_in, w_out)


def _proj_kernel(x_ref, mod_ref, g_ref, w_ref, kpos_ref, oa_ref, oi_ref, ow_ref, or_ref, oc_ref, od_ref):
    m = mod_ref[0]
    xn = _rms(x_ref[...], g_ref[2:3])
    h = (xn * (1.0 + m[4:5]) + m[3:4]).astype(BF16)
    off = 0
    for ref in (oa_ref, oi_ref, ow_ref, or_ref, oc_ref, od_ref):
        w = ref.shape[1]
        z = jnp.dot(h, w_ref[:, off:off + w], preferred_element_type=F32)
        if ref is oa_ref or ref is od_ref:
            qscale = (HEAD_DIM ** -0.5) * (LOG2E if ref is oa_ref else 1.0)
            z = jnp.concatenate([z[:, :512] * qscale, z[:, 512:768] + kpos_ref[...], z[:, 768:]], axis=1)
        ref[...] = z.astype(ref.dtype)
        off += w


def _proj_call(x2, mod_l, g_l, w_all, kpos, *, s_len):
    t, d = x2.shape
    tm = min(PROJ_TM, s_len)
    tpb = s_len // tm
    widths = (QKV_W, IDX_W, LANES, 2 * LRU_WIDTH, MLA_W, QKV_W)
    dtypes = (BF16, BF16, F32, F32, F32, BF16)
    assert sum(widths) == w_all.shape[1]
    return pl.pallas_call(
        _proj_kernel,
        grid=(t // tm,),
        in_specs=[pl.BlockSpec((tm, d), lambda i: (i, 0)),
                  pl.BlockSpec((1, 9, d), lambda i: (i // tpb, 0, 0)),
                  pl.BlockSpec((6, d), lambda i: (0, 0)),
                  pl.BlockSpec(w_all.shape, lambda i: (0, 0)),
                  pl.BlockSpec((tm, 256), lambda i: (i % tpb, 0))],
        out_specs=[pl.BlockSpec((tm, w), lambda i: (i, 0)) for w in widths],
        out_shape=[jax.ShapeDtypeStruct((t, w), dt) for w, dt in zip(widths, dtypes)],
        compiler_params=_cparams(("arbitrary",)),
        name="mixer_proj",
    )(x2, mod_l, g_l, w_all, kpos)


def _head_rows(q, n_heads, tq):
    lane = lax.broadcasted_iota(I32, (tq, LANES), 1)
    rows = []
    for h in range(n_heads):
        tile = q[:, LANES * (h // 2):LANES * (h // 2 + 1)].astype(F32)
        keep = (lane >= HEAD_DIM) if (h % 2) else (lane < HEAD_DIM)
        rows.append(jnp.where(keep, tile, 0.0).astype(BF16))
    return jnp.concatenate(rows, axis=0)


def _alibi_lanes(slopes, tq):
    lane = lax.broadcasted_iota(I32, (tq, LANES), 1)
    out = []
    for slope in slopes:
        tile = jnp.zeros((tq, LANES), F32)
        for i, part in enumerate(_bf16_parts(slope)):
            tile = jnp.where(lane == HEAD_DIM + 2 * i, POS_SPLIT * part, tile)
            tile = jnp.where(lane == HEAD_DIM + 2 * i + 1, part, tile)
        out.append(tile)
    return out


def _alibi_q_tiles(q, alibi_lanes, tq):
    lane = lax.broadcasted_iota(I32, (tq, LANES), 1)
    rows = []
    for h, al in enumerate(alibi_lanes):
        tile = q[:, LANES * (h // 2):LANES * (h // 2 + 1)].astype(F32)
        if h % 2:
            tile = pltpu.roll(tile, HEAD_DIM, axis=1)
        rows.append(jnp.where(lane < HEAD_DIM, tile, al).astype(BF16))
    return jnp.concatenate(rows, axis=0)


def _gqa_out(accs, inv_ls, tq):
    lane = lax.broadcasted_iota(I32, (tq, LANES), 1)
    tiles = []
    for j in range(4):
        g = j // 2
        halves = []
        for h in (2 * j, 2 * j + 1):
            o = accs[g][h % 4] * inv_ls[g][h % 4]
            src_hi = (g == 1)
            dst_hi = (h % 2 == 1)
            if src_hi != dst_hi:
                o = pltpu.roll(o, HEAD_DIM, axis=1)
            halves.append(o)
        tiles.append(jnp.where(lane < HEAD_DIM, halves[0], halves[1]))
    return jnp.concatenate(tiles, axis=1)


def _dsa_kernel(q_ref, k_ref, v_ref, iq_ref, ik_ref, iw_ref, o_ref,
                key_ref, s_ref, tau_ref, iqs_ref, qs_ref, m_ref, l_ref, acc_ref, *, tq, kc, topk, pos_bits):
    qb = pl.program_id(1)
    q0 = qb * tq
    nk = lax.shift_right_logical(q0 + tq + kc - 1, _log2(kc))
    n_lt = kc // LANES
    n_rt = kc // CNT_ROWS
    qpos_l = q0 + lax.broadcasted_iota(I32, (kc, tq), 1)
    krow = lax.broadcasted_iota(I32, (kc, tq), 0)
    krow_t = lax.broadcasted_iota(I32, (CNT_ROWS, tq), 0)

    iqs_ref[...] = _head_rows(iq_ref[...], IDX_HEADS, tq)
    iw_t = iw_ref[...].T

    def score_body(j, carry):
        ks = pl.multiple_of(j * kc, kc)
        d = _nt_dot(ik_ref[pl.ds(ks, kc), :], iqs_ref[...])
        acc = jnp.zeros((kc, tq), F32)
        for h in range(IDX_HEADS):
            acc = acc + iw_t[h:h + 1, :] * jnp.maximum(d[:, h * tq:(h + 1) * tq], 0.0)
        bits = pltpu.bitcast(acc, I32)
        key = jnp.where(bits < 0, bits ^ INT_MAX, bits)
        key_ref[j] = jnp.where(ks + krow <= qpos_l, key, INT_MIN)
        return carry

    _loop_groups(nk, score_body)

    def count(pred):
        def body(j, cnt):
            for r in range(n_rt):
                kp = (j * kc + r * CNT_ROWS) + krow_t
                cnt = cnt + jnp.where(pred(key_ref[j, r * CNT_ROWS:(r + 1) * CNT_ROWS, :], kp), 1.0, 0.0)
            return cnt
        cnt = _loop_groups(nk, body, jnp.zeros((CNT_ROWS, tq), F32))
        return jnp.sum(cnt, axis=0, keepdims=True)

    kf = float(topk)

    def bit_step(i, state):
        tau, cnt = state
        cand = tau + lax.shift_left(jnp.int32(1), 31 - i)
        cb = jnp.broadcast_to(cand, (CNT_ROWS, tq))
        c = count(lambda kv, kp: kv >= cb)
        ok = c >= kf
        return jnp.where(ok, cand, tau), jnp.where(ok, c, cnt)

    tau, n_ge = lax.fori_loop(0, 32, bit_step, (jnp.full((1, tq), INT_MIN, I32),
                                                 jnp.broadcast_to((nk * kc).astype(F32), (1, tq))))

    tie_f = jnp.where((n_ge > kf) & (tau > INT_MIN), 1.0, 0.0)
    any_tie = jnp.max(tie_f) > 0.0
    tau_c = jnp.maximum(tau, INT_MIN + 1)
    tau_ref[...] = jnp.broadcast_to(tau_c, tau_ref.shape)

    @pl.when(any_tie)
    def _():
        tb = jnp.broadcast_to(tau_c, (CNT_ROWS, tq))
        need = kf - count(lambda kv, kp: kv > tb)

        def pos_body(i, p):
            cand = p | lax.shift_left(jnp.int32(1), pos_bits - 1 - i)
            cb = jnp.broadcast_to(cand, (CNT_ROWS, tq))
            below = count(lambda kv, kp: (kv == tb) & (kp < cb))
            return jnp.where(below < need, cand, p)

        p = lax.fori_loop(0, pos_bits, pos_body, jnp.zeros((1, tq), I32))
        p = jnp.where(tie_f > 0.0, p, INT_MAX)

        def rewrite_body(j, carry):
            kj = key_ref[j]
            kp = j * kc + krow
            sel = (kj > tau_c) | ((kj == tau_c) & (kp <= p))
            key_ref[j] = jnp.where(sel, 1, -1)
            return carry

        lax.fori_loop(0, nk, rewrite_body, 0)
        tau_ref[...] = jnp.zeros(tau_ref.shape, I32)

    q = q_ref[...]
    for g in range(A_KV_HEADS):
        qs_ref[g] = _alibi_q_tiles(q[:, 4 * HEAD_DIM * g:4 * HEAD_DIM * (g + 1)],
                                   _alibi_lanes([s * LOG2E for s in SLOPES_A[4 * g:4 * g + 4]], tq), tq)
    m_ref[...] = jnp.full(m_ref.shape, NEG, F32)
    l_ref[...] = jnp.zeros(l_ref.shape, F32)
    acc_ref[...] = jnp.zeros(acc_ref.shape, F32)

    def max_body(j, carry):
        ks = pl.multiple_of(j * kc, kc)
        sel = jnp.where(key_ref[j] >= tau_ref[0:1, :], 1.0, 0.0).T > 0.5
        for g in range(A_KV_HEADS):
            s = _nt_dot(qs_ref[g], k_ref[pl.ds(ks, kc), g * LANES:(g + 1) * LANES])
            for h in range(4):
                sh = jnp.where(sel, s[h * tq:(h + 1) * tq], NEG)
                s_ref[g, h, j] = sh
                mp = m_ref[g, h]
                for c in range(n_lt):
                    mp = jnp.maximum(mp, sh[:, c * LANES:(c + 1) * LANES])
                m_ref[g, h] = mp
        return carry

    _loop_groups(nk, max_body)
    for g in range(A_KV_HEADS):
        for h in range(4):
            m_ref[g, h] = jnp.broadcast_to(jnp.max(m_ref[g, h], axis=1, keepdims=True), (tq, LANES))

    def pv_body(j, carry):
        ks = pl.multiple_of(j * kc, kc)
        vch = v_ref[pl.ds(ks, kc), :]
        for g in range(A_KV_HEADS):
            ps = []
            for h in range(4):
                mb = m_ref[g, h]
                lp = l_ref[g, h]
                tiles = []
                for c in range(n_lt):
                    p = jnp.exp2(s_ref[g, h, j, :, c * LANES:(c + 1) * LANES] - mb)
                    lp = lp + p
                    tiles.append(p.astype(BF16))
                l_ref[g, h] = lp
                ps.append(jnp.concatenate(tiles, axis=1))
            pv = jnp.dot(jnp.concatenate(ps, axis=0), vch, preferred_element_type=F32)
            for h in range(4):
                acc_ref[g, h] += pv[h * tq:(h + 1) * tq]
        return carry

    _loop_groups(nk, pv_body)
    accs = [[acc_ref[g, h] for h in range(4)] for g in range(A_KV_HEADS)]
    inv_ls = [[1.0 / jnp.sum(l_ref[g, h], axis=1, keepdims=True) for h in range(4)] for g in range(A_KV_HEADS)]
    o_ref[...] = _gqa_out(accs, inv_ls, tq).astype(o_ref.dtype)


def _dsa_call(qkv, idx, iw, *, batch, s_len):
    t = qkv.shape[0]
    tq = min(DSA_TQ, s_len)
    kc = min(DSA_KC, s_len)
    topk = min(TOPK_MAX, s_len // 4)
    assert kc >= topk and kc % tq == 0
    nqb = s_len // tq
    once = pl.Buffered(1)
    return pl.pallas_call(
        functools.partial(_dsa_kernel, tq=tq, kc=kc, topk=topk, pos_bits=_log2(s_len)),
        grid=(batch, nqb),
        in_specs=[pl.BlockSpec((tq, 512), lambda b, i: (b * nqb + i, 0)),
                  pl.BlockSpec((s_len, 256), lambda b, i: (b, 2), pipeline_mode=once),
                  pl.BlockSpec((s_len, 128), lambda b, i: (b, 6), pipeline_mode=once),
                  pl.BlockSpec((tq, 512), lambda b, i: (b * nqb + i, 0)),
                  pl.BlockSpec((s_len, 128), lambda b, i: (b, 4), pipeline_mode=once),
                  pl.BlockSpec((tq, LANES), lambda b, i: (b * nqb + i, 0))],
        out_specs=pl.BlockSpec((tq, 512), lambda b, i: (b * nqb + i, 0)),
        out_shape=jax.ShapeDtypeStruct((t, 512), BF16),
        scratch_shapes=[pltpu.VMEM((s_len // kc, kc, tq), I32),
                        pltpu.VMEM((A_KV_HEADS, 4, s_len // kc, tq, kc), F32),
                        pltpu.VMEM((8, tq), I32),
                        pltpu.VMEM((IDX_HEADS * tq, LANES), BF16),
                        pltpu.VMEM((A_KV_HEADS, 4 * tq, LANES), BF16),
                        pltpu.VMEM((A_KV_HEADS, 4, tq, LANES), F32),
                        pltpu.VMEM((A_KV_HEADS, 4, tq, LANES), F32),
                        pltpu.VMEM((A_KV_HEADS, 4, tq, LANES), F32)],
        compiler_params=pltpu.CompilerParams(dimension_semantics=("arbitrary", "arbitrary"),
                                             vmem_limit_bytes=DSA_VMEM_LIMIT),
        name="dsa",
    )(qkv, qkv, qkv, idx, idx, iw)


def _swa_kernel(q_ref, kp_ref, kc_ref, vp_ref, vc_ref, sink_ref, o_ref, *, tq, nblk):
    i = pl.program_id(1)
    krow = lax.broadcasted_iota(I32, (2 * tq, tq), 0)
    qcol = lax.broadcasted_iota(I32, (2 * tq, tq), 1)
    dist = qcol + tq - krow
    in_band = (dist >= 0) & (dist < WINDOW)
    sinks = sink_ref[...]
    sink_vec = [jnp.broadcast_to(sinks[:, h:h + 1], (1, tq)) for h in range(D_HEADS)]
    alibi = _alibi_lanes(SLOPES_D, tq)
    for n in range(nblk):
        blk = i * nblk + n
        rows = slice(n * tq, (n + 1) * tq)
        q = q_ref[rows, :]
        k_prev = kp_ref[...] if n == 0 else kc_ref[(n - 1) * tq:n * tq, :]
        v_prev = vp_ref[...] if n == 0 else vc_ref[(n - 1) * tq:n * tq, :]
        kk = jnp.concatenate([k_prev, kc_ref[rows, :]], axis=0)
        vv_t = jnp.concatenate([v_prev, vc_ref[rows, :]], axis=0).T
        valid = in_band & ((blk * tq - tq + krow) >= 0)
        qpos = (blk * tq + lax.broadcasted_iota(I32, (1, tq), 1)).astype(F32)
        tiles = []
        for g in range(D_KV_HEADS):
            qs = _alibi_q_tiles(q[:, 4 * HEAD_DIM * g:4 * HEAD_DIM * (g + 1)], alibi[4 * g:4 * g + 4], tq)
            s_t = _nt_dot(kk[:, g * LANES:(g + 1) * LANES], qs)
            ps, ils = [], []
            for h in range(4):
                sh = jnp.where(valid, s_t[:, h * tq:(h + 1) * tq], NEG)
                sink = sink_vec[4 * g + h] + SLOPES_D[4 * g + h] * qpos
                m = jnp.maximum(jnp.max(sh, axis=0, keepdims=True), sink)
                p = jnp.exp(sh - m)
                ils.append(1.0 / (jnp.sum(p, axis=0, keepdims=True) + jnp.exp(sink - m)))
                ps.append(p.astype(BF16))
            o_t = jnp.dot(vv_t, jnp.concatenate(ps, axis=1), preferred_element_type=F32)
            o_t = o_t[g * HEAD_DIM:(g + 1) * HEAD_DIM, :] * jnp.concatenate(ils, axis=1)
            for j in range(2):
                pair = jnp.concatenate([o_t[:, (2 * j) * tq:(2 * j + 1) * tq],
                                        o_t[:, (2 * j + 1) * tq:(2 * j + 2) * tq]], axis=0)
                tiles.append(pair.T)
        o_ref[rows, :] = jnp.concatenate(tiles, axis=1).astype(o_ref.dtype)


def _swa_call(qkv, sinks_pad, *, batch, s_len):
    t = qkv.shape[0]
    tq = BLOCK
    nblk = min(SWA_BLOCKS, s_len // tq)
    ts = nblk * tq
    nst = s_len // ts
    cur = lambda b, i: b * nst + i
    prev = lambda b, i: jnp.maximum((b * nst + i) * nblk - 1, 0)
    return pl.pallas_call(
        functools.partial(_swa_kernel, tq=tq, nblk=nblk),
        grid=(batch, nst),
        in_specs=[pl.BlockSpec((ts, 512), lambda b, i: (cur(b, i), 0)),
                  pl.BlockSpec((tq, 256), lambda b, i: (prev(b, i), 2)),
                  pl.BlockSpec((ts, 256), lambda b, i: (cur(b, i), 2)),
                  pl.BlockSpec((tq, 128), lambda b, i: (prev(b, i), 6)),
                  pl.BlockSpec((ts, 128), lambda b, i: (cur(b, i), 6)),
                  pl.BlockSpec((1, LANES), lambda b, i: (0, 0))],
        out_specs=pl.BlockSpec((ts, 512), lambda b, i: (cur(b, i), 0)),
        out_shape=jax.ShapeDtypeStruct((t, 512), BF16),
        compiler_params=_cparams(("arbitrary", "arbitrary")),
        name="swa",
    )(qkv, qkv, qkv, qkv, qkv, sinks_pad)


def _mla_prep_kernel(c_ref, cos_ref, sin_ref, gq_ref, gkv_ref, wq_ref, wqr_ref, wk_ref, wv_ref,
                     q_ref, k_ref, v_ref):
    c = c_ref[...]
    cos = cos_ref[...]
    sin = sin_ref[...]
    cos8 = jnp.concatenate([cos] * C_HEADS, axis=1)
    sin8 = jnp.concatenate([sin] * C_HEADS, axis=1)
    cqn = _rms(c[:, :Q_LORA], gq_ref[...]).astype(BF16)
    q = (jnp.dot(cqn, wq_ref[...], preferred_element_type=F32) * cos8
         + jnp.dot(cqn, wqr_ref[...], preferred_element_type=F32) * sin8)
    q_ref[...] = q.astype(BF16)
    ckvn = _rms(c[:, Q_LORA:Q_LORA + KV_LORA], gkv_ref[...]).astype(BF16)
    kr = c[:, 384:512] * cos + c[:, 512:640] * sin
    k = jnp.dot(ckvn, wk_ref[...], preferred_element_type=F32) + jnp.concatenate([kr] * C_HEADS, axis=1)
    k_ref[...] = k.astype(BF16)
    v_ref[...] = jnp.dot(ckvn, wv_ref[...], preferred_element_type=F32).astype(BF16)


def _mla_prep_call(cm, cos_t, sin_t, gq, gkv, wq, wqr, wk, wv, *, s_len):
    t = cm.shape[0]
    tm = min(PROJ_TM, s_len)
    tpb = s_len // tm
    full = lambda a: pl.BlockSpec(a.shape, lambda i: (0, 0))
    return pl.pallas_call(
        _mla_prep_kernel,
        grid=(t // tm,),
        in_specs=[pl.BlockSpec((tm, MLA_W), lambda i: (i, 0)),
                  pl.BlockSpec((tm, LANES), lambda i: (i % tpb, 0)),
                  pl.BlockSpec((tm, LANES), lambda i: (i % tpb, 0)),
                  full(gq), full(gkv), full(wq), full(wqr), full(wk), full(wv)],
        out_specs=[pl.BlockSpec((tm, 1024), lambda i: (i, 0)),
                   pl.BlockSpec((tm, 1024), lambda i: (i, 0)),
                   pl.BlockSpec((tm, 512), lambda i: (i, 0))],
        out_shape=[jax.ShapeDtypeStruct((t, 1024), BF16),
                   jax.ShapeDtypeStruct((t, 1024), BF16),
                   jax.ShapeDtypeStruct((t, 512), BF16)],
        compiler_params=_cparams(("arbitrary",)),
        name="mla_prep",
    )(cm, cos_t, sin_t, gq, gkv, wq, wqr, wk, wv)


def _mla_attn_kernel(q_ref, k_ref, v_ref, o_ref, s_ref, m_ref, l_ref, acc_ref, *, tq, kc):
    qb = pl.program_id(2)
    q0 = qb * tq
    n_full = lax.shift_right_logical(q0, _log2(kc))
    rowpos = q0 + lax.broadcasted_iota(I32, (tq, kc), 0)
    lane = lax.broadcasted_iota(I32, (tq, kc), 1)
    c = ((QK_NOPE + QK_ROPE) ** -0.5) * LOG2E
    n_lt = kc // LANES
    m_ref[...] = jnp.full(m_ref.shape, NEG, F32)
    l_ref[...] = jnp.zeros(l_ref.shape, F32)
    acc_ref[...] = jnp.zeros(acc_ref.shape, F32)

    def max_step(j, masked):
        ks = pl.multiple_of(j * kc, kc)
        for hh in range(2):
            s = _nt_dot(q_ref[:, hh * LANES:(hh + 1) * LANES],
                        k_ref[pl.ds(ks, kc), hh * LANES:(hh + 1) * LANES]) * c
            if masked:
                s = jnp.where(ks + lane <= rowpos, s, NEG)
            s_ref[hh, j] = s
            mp = m_ref[hh]
            for t in range(n_lt):
                mp = jnp.maximum(mp, s[:, t * LANES:(t + 1) * LANES])
            m_ref[hh] = mp

    def max_body(j, carry):
        max_step(j, False)
        return carry

    _loop_groups(n_full, max_body)
    max_step(n_full, True)
    for hh in range(2):
        m_ref[hh] = jnp.broadcast_to(jnp.max(m_ref[hh], axis=1, keepdims=True), (tq, LANES))

    def pv_body(j, carry):
        ks = pl.multiple_of(j * kc, kc)
        vch = v_ref[pl.ds(ks, kc), :]
        for hh in range(2):
            mb = m_ref[hh]
            lp = l_ref[hh]
            tiles = []
            for t in range(n_lt):
                p = jnp.exp2(s_ref[hh, j, :, t * LANES:(t + 1) * LANES] - mb)
                lp = lp + p
                tiles.append(p.astype(BF16))
            l_ref[hh] = lp
            acc_ref[hh] += jnp.dot(jnp.concatenate(tiles, axis=1), vch, preferred_element_type=F32)
        return carry

    _loop_groups(n_full + 1, pv_body)
    outs = [acc_ref[hh] * (1.0 / jnp.sum(l_ref[hh], axis=1, keepdims=True)) for hh in range(2)]
    lane_o = lax.broadcasted_iota(I32, (tq, LANES), 1)
    o_ref[...] = jnp.where(lane_o < V_DIM, outs[0], outs[1]).astype(o_ref.dtype)


def _mla_attn_call(qm, km, vm, *, batch, s_len):
    t = qm.shape[0]
    tq = min(MLA_TQ, s_len)
    kc = min(MLA_KC, s_len)
    assert kc % tq == 0
    nqb = s_len // tq
    once = pl.Buffered(1)
    return pl.pallas_call(
        functools.partial(_mla_attn_kernel, tq=tq, kc=kc),
        grid=(batch, C_HEADS // 2, nqb),
        in_specs=[pl.BlockSpec((tq, 256), lambda b, h, i: (b * nqb + i, h)),
                  pl.BlockSpec((s_len, 256), lambda b, h, i: (b, h), pipeline_mode=once),
                  pl.BlockSpec((s_len, 128), lambda b, h, i: (b, h), pipeline_mode=once)],
        out_specs=pl.BlockSpec((tq, 128), lambda b, h, i: (b * nqb + i, h)),
        out_shape=jax.ShapeDtypeStruct((t, 512), BF16),
        scratch_shapes=[pltpu.VMEM((2, s_len // kc, tq, kc), F32)] + [pltpu.VMEM((2, tq, LANES), F32)] * 3,
        compiler_params=_cparams(("arbitrary", "arbitrary", "arbitrary")),
        name="mla_attn",
    )(qm, km, vm)


def _lru_kernel(xr_ref, xg_ref, prev_ref, cw_ref, cb_ref, wa_ref, ba_ref, wx_ref, bx_ref, lam_ref,
                o_ref, xe_ref, h_ref, *, ln):
    i = pl.program_id(1)
    xe_ref[0:8, :] = jnp.where(i > 0, prev_ref[...], 0.0)
    xe_ref[8:8 + ln, :] = xr_ref[...]
    cw = cw_ref[...]
    xc = cb_ref[...] + cw[0:1] * xe_ref[pl.ds(5, ln), :]
    for j in range(1, CONV_WIDTH):
        xc = xc + cw[j:j + 1] * xe_ref[pl.ds(5 + j, ln), :]
    xcb = xc.astype(BF16)
    r = jax.nn.sigmoid(jnp.dot(xcb, wa_ref[...], preferred_element_type=F32) + ba_ref[...])
    gi = jax.nn.sigmoid(jnp.dot(xcb, wx_ref[...], preferred_element_type=F32) + bx_ref[...])
    z = -lam_ref[...]
    softplus = jnp.maximum(z, 0.0) + jnp.log1p(jnp.exp(-jnp.abs(z)))
    log_a = -LRU_C * r * softplus
    a = jnp.exp(log_a)
    b = jnp.sqrt(-_expm1(2.0 * log_a)) * (gi * xc)
    row = lax.broadcasted_iota(I32, (ln, LRU_WIDTH), 0)
    d = 1
    while d < ln:
        keep = row >= d
        b = jnp.where(keep, a * pltpu.roll(b, d, axis=0) + b, b)
        a = jnp.where(keep, a * pltpu.roll(a, d, axis=0), a)
        d *= 2
    h_prev = jnp.where(i > 0, h_ref[0:1, :], 0.0)
    h = a * h_prev + b
    h_ref[0:1, :] = h[ln - 1:ln, :]
    xg = xg_ref[...]
    gelu = 0.5 * xg * (1.0 + jnp.tanh(0.7978845608028654 * (xg + 0.044715 * (xg * xg * xg))))
    o_ref[...] = (h * gelu).astype(o_ref.dtype)


def _lru_call(rm, cw, cb, wa, ba, wx, bx, lam, *, batch, s_len):
    t = rm.shape[0]
    ln = min(LRU_L, s_len)
    nt = s_len // ln
    w = LRU_WIDTH
    vec = lambda a: pl.BlockSpec(a.shape, lambda b, i: (0, 0))
    return pl.pallas_call(
        functools.partial(_lru_kernel, ln=ln),
        grid=(batch, nt),
        in_specs=[pl.BlockSpec((ln, w), lambda b, i: (b * nt + i, 0)),
                  pl.BlockSpec((ln, w), lambda b, i: (b * nt + i, 1)),
                  pl.BlockSpec((8, w), lambda b, i: (jnp.maximum((b * nt + i) * (ln // 8) - 1, 0), 0)),
                  vec(cw), vec(cb), vec(wa), vec(ba), vec(wx), vec(bx), vec(lam)],
        out_specs=pl.BlockSpec((ln, w), lambda b, i: (b * nt + i, 0)),
        out_shape=jax.ShapeDtypeStruct((t, w), BF16),
        scratch_shapes=[pltpu.VMEM((ln + 8, w), F32), pltpu.VMEM((8, w), F32)],
        compiler_params=_cparams(("arbitrary", "arbitrary")),
        name="rglru",
    )(rm, rm, rm, cw, cb, wa, ba, wx, bx, lam)


def _merge_kernel(x_ref, mod_ref, g_ref, ya_ref, yb_ref, yc_ref, yd_ref, wg_ref, wb_ref, wo_ref, o_ref):
    m = mod_ref[0]
    x = x_ref[...]
    h = (_rms(x, g_ref[2:3]) * (1.0 + m[4:5]) + m[3:4]).astype(BF16)
    merged = None
    for n, y_ref in enumerate((ya_ref, yb_ref, yc_ref, yd_ref)):
        gate = jax.nn.sigmoid(jnp.dot(h, wg_ref[:, n * D_MODEL:(n + 1) * D_MODEL], preferred_element_type=F32))
        term = gate * jnp.dot(y_ref[...], wb_ref[n], preferred_element_type=F32)
        merged = term if merged is None else merged + term
    y = jnp.dot(merged.astype(BF16), wo_ref[...], preferred_element_type=F32)
    o_ref[...] = x + m[5:6] * _rms(y, g_ref[3:4])


def _merge_call(x2, mod_l, g_l, ys, w_gate, w_branch, w_out, *, s_len):
    t, d = x2.shape
    tm = min(MERGE_TM, s_len)
    tpb = s_len // tm
    return pl.pallas_call(
        _merge_kernel,
        grid=(t // tm,),
        in_specs=[pl.BlockSpec((tm, d), lambda i: (i, 0)),
                  pl.BlockSpec((1, 9, d), lambda i: (i // tpb, 0, 0)),
                  pl.BlockSpec((6, d), lambda i: (0, 0))]
                 + [pl.BlockSpec((tm, BRANCH_WIDTH), lambda i: (i, 0))] * N_BRANCH
                 + [pl.BlockSpec(w_gate.shape, lambda i: (0, 0)),
                    pl.BlockSpec(w_branch.shape, lambda i: (0, 0, 0)),
                    pl.BlockSpec(w_out.shape, lambda i: (0, 0))],
        out_specs=pl.BlockSpec((tm, d), lambda i: (i, 0)),
        out_shape=jax.ShapeDtypeStruct((t, d), F32),
        compiler_params=_cparams(("arbitrary",)),
        name="merge",
    )(x2, mod_l, g_l, *ys, w_gate, w_branch, w_out)


def _mixer_weights(w_in_l):
    o = IN_OFFSETS
    col = lambda n: w_in_l[:, o[n]:o[n + 1]]
    z = lambda n: jnp.zeros((D_MODEL, n), w_in_l.dtype)

    def pad_heads(k):
        return jnp.concatenate([k[:, :64], z(64), k[:, 64:], z(64)], axis=1)

    a_q, a_k, a_v, i_q, i_k, i_w, r_x, r_g, c_q, c_kv, d_q, d_k, d_v = (col(n) for n in range(13))
    k_rope = c_kv[:, KV_LORA:]
    half = QK_ROPE // 2
    k_rope_rot = jnp.concatenate([-k_rope[:, half:], k_rope[:, :half]], axis=1)
    groups = [a_q, pad_heads(a_k), a_v,
              i_q, i_k, i_k,
              i_w, z(LANES - IDX_HEADS),
              r_x, r_g,
              c_q, c_kv[:, :KV_LORA], z(64), k_rope, z(32), z(64), k_rope_rot, z(32),
              d_q, pad_heads(d_k), d_v]
    w_all = jnp.concatenate(groups, axis=1).astype(BF16)
    w_gate = col(13).astype(BF16)
    return w_all, w_gate


def _kpos_table(s_len):
    pos = np.arange(s_len)
    tile = np.zeros((s_len, LANES), np.float32)
    for i in range(3):
        tile[:, HEAD_DIM + 2 * i] = pos // POS_SPLIT
        tile[:, HEAD_DIM + 2 * i + 1] = pos % POS_SPLIT
    return jnp.asarray(np.concatenate([tile, tile], axis=1))


def _mla_weights(w_uq, w_ukv):
    dq = QK_NOPE + QK_ROPE
    half = QK_ROPE // 2
    zq = lambda n: jnp.zeros((Q_LORA, n), w_uq.dtype)
    wq, wqr, wk, wv = [], [], [], []
    for h in range(C_HEADS):
        nope = w_uq[:, h * dq:h * dq + QK_NOPE]
        r1 = w_uq[:, h * dq + QK_NOPE:h * dq + QK_NOPE + half]
        r2 = w_uq[:, h * dq + QK_NOPE + half:(h + 1) * dq]
        wq += [nope, r1, r2, zq(32)]
        wqr += [zq(64), -r2, r1, zq(32)]
        wk += [w_ukv[:, h * 128:h * 128 + QK_NOPE], jnp.zeros((KV_LORA, 64), w_ukv.dtype)]
        wv += [w_ukv[:, h * 128 + QK_NOPE:(h + 1) * 128]]
    cat = lambda xs: jnp.concatenate(xs, axis=1).astype(BF16)
    return cat(wq), cat(wqr), cat(wk), cat(wv)


def _rope_tables(s_len):
    half = QK_ROPE // 2
    inv = (np.float32(ROPE_THETA) ** (-np.arange(half, dtype=np.float32) / np.float32(half))).astype(np.float32)
    ang = (np.arange(s_len, dtype=np.float32)[:, None] * inv[None, :]).astype(np.float32)
    cos, sin = np.cos(ang.astype(np.float64)).astype(np.float32), np.sin(ang.astype(np.float64)).astype(np.float32)
    ones = lambda n: np.ones((s_len, n), np.float32)
    zeros = lambda n: np.zeros((s_len, n), np.float32)
    cos_t = np.concatenate([ones(64), cos, cos, ones(32)], axis=1)
    sin_t = np.concatenate([zeros(64), sin, sin, zeros(32)], axis=1)
    return jnp.asarray(cos_t), jnp.asarray(sin_t)


def _block_diag(w):
    n, bw, _ = w.shape
    eye = jnp.eye(n, dtype=w.dtype)
    return (w[:, :, None, :] * eye[:, None, :, None]).reshape(n * bw, n * bw).astype(BF16)


def kernel(x, c, w_ada, b_ada, norm_g, ffn1_w_in, ffn1_w_out, w_in, conv_w, conv_b, lru_wa, lru_ba, lru_wx,
           lru_bx, lru_lambda, mla_g_q, mla_g_kv, mla_w_uq, mla_w_ukv, swa_sinks, w_branch, w_out, ffn2_w_in,
           ffn2_w_out):
    batch, s_len, d = x.shape
    depth = w_ada.shape[0]
    t = batch * s_len
    assert s_len // POS_SPLIT <= 256
    x2 = x.reshape(t, d)
    c_pad = jnp.zeros((8, d), F32).at[:batch].set(c)
    mod_all = _ada_call(c_pad, w_ada, b_ada)
    cos_t, sin_t = _rope_tables(s_len)
    kpos = _kpos_table(s_len)
    row = lambda v: v.reshape(1, -1)
    for l in range(depth):
        mod_l = mod_all[l, :batch].reshape(batch, 9, d)
        g_l = norm_g[l]
        x2 = _ffn_call(x2, mod_l, g_l, ffn1_w_in, ffn1_w_out, l, sub=0, resid_w=0.5, s_len=s_len)
        w_all, w_gate = _mixer_weights(w_in[l])
        qkv_a, idx, iw, rm, cm, qkv_d = _proj_call(x2, mod_l, g_l, w_all, kpos, s_len=s_len)
        y_a = _dsa_call(qkv_a, idx, iw, batch=batch, s_len=s_len)
        y_b = _lru_call(rm, conv_w[l], row(conv_b[l]), _block_diag(lru_wa[l]), row(lru_ba[l]),
                        _block_diag(lru_wx[l]), row(lru_bx[l]), row(lru_lambda[l]), batch=batch, s_len=s_len)
        qm, km, vm = _mla_prep_call(cm, cos_t, sin_t, row(mla_g_q[l]), row(mla_g_kv[l]),
                                    *_mla_weights(mla_w_uq[l], mla_w_ukv[l]), s_len=s_len)
        y_c = _mla_attn_call(qm, km, vm, batch=batch, s_len=s_len)
        sinks_pad = jnp.zeros((1, LANES), F32).at[0, :D_HEADS].set(swa_sinks[l])
        y_d = _swa_call(qkv_d, sinks_pad, batch=batch, s_len=s_len)
        x2 = _merge_call(x2, mod_l, g_l, (y_a, y_b, y_c, y_d), w_gate, w_branch[l].astype(BF16),
                         w_out[l].astype(BF16), s_len=s_len)
        x2 = _ffn_call(x2, mod_l, g_l, ffn2_w_in, ffn2_w_out, l, sub=2, resid_w=0.5, s_len=s_len)
    return x2.reshape(batch, s_len, d)
```

```python
import functools

import numpy as np
import jax
import jax.numpy as jnp
from jax import lax
from jax.experimental import pallas as pl
from jax.experimental.pallas import tpu as pltpu

F32 = jnp.float32
BF16 = jnp.bfloat16
I32 = jnp.int32

D_MODEL = 1024
HEAD_DIM = 64
BLOCK = 128
EPS = 1e-6
NEG = -1e30
A_HEADS = 8
A_KV_HEADS = 2
IDX_HEADS = 8
IDX_DIM = 64
TOPK_MAX = 256
LRU_WIDTH = 512
LRU_BLOCKS = 8
CONV_WIDTH = 4
LRU_C = 8.0
C_HEADS = 8
Q_LORA = 256
KV_LORA = 128
QK_NOPE = 64
QK_ROPE = 32
V_DIM = 64
ROPE_THETA = 10000.0
D_HEADS = 8
D_KV_HEADS = 2
WINDOW = 128
N_BRANCH = 4
BRANCH_WIDTH = 512
N_ALIBI = A_HEADS + D_HEADS
D_FF = 2816
IN_SPLITS = (A_HEADS * HEAD_DIM, A_KV_HEADS * HEAD_DIM, A_KV_HEADS * HEAD_DIM,
             IDX_HEADS * IDX_DIM, IDX_DIM, IDX_HEADS,
             LRU_WIDTH, LRU_WIDTH,
             Q_LORA, KV_LORA + QK_ROPE,
             D_HEADS * HEAD_DIM, D_KV_HEADS * HEAD_DIM, D_KV_HEADS * HEAD_DIM,
             N_BRANCH * D_MODEL)
IN_OFFSETS = tuple(int(v) for v in np.concatenate([[0], np.cumsum(IN_SPLITS)]))

LANES = 128
INT_MIN = -2 ** 31
INT_MAX = 2 ** 31 - 1
VMEM_LIMIT = 56 * 1024 * 1024
DSA_VMEM_LIMIT = 62 * 1024 * 1024
LOG2E = 1.4426950408889634
POS_SPLIT = 64
DSA_GROUP = 16
MLA_GROUP = 8
CNT_GROUP = 16
CNT_ROWS = 64

QKV_W = 512 + 256 + 128
IDX_W = 512 + 128
MLA_W = 640

FFN_TM = 1024
FFN_TF = 256
PROJ_TM = 512
DSA_TQ = 128
DSA_KC = 256
MLA_TQ = 512
MLA_KC = 512
SWA_BLOCKS = 4
LRU_L = 256
MERGE_TM = 512


def _alibi(i):
    return float(2.0 ** (-8.0 * i / N_ALIBI))


SLOPES_D = tuple(_alibi(i) for i in range(1, D_HEADS + 1))
SLOPES_A = tuple(_alibi(i) for i in range(D_HEADS + 1, N_ALIBI + 1))


def _bf16_parts(x):
    parts = []
    rem = np.float32(x)
    for _ in range(3):
        p = np.float32(np.asarray(rem, np.float32).astype(jnp.bfloat16).astype(np.float32))
        parts.append(float(p))
        rem = np.float32(rem - p)
    return parts


def _cparams(sem):
    return pltpu.CompilerParams(dimension_semantics=sem, vmem_limit_bytes=VMEM_LIMIT)


def _rms(x, g):
    return x * lax.rsqrt(jnp.mean(x * x, axis=-1, keepdims=True) + EPS) * g


def _nt_dot(a, b):
    return lax.dot_general(a, b, (((1,), (1,)), ((), ())), preferred_element_type=F32)


def _expm1(y):
    u = jnp.exp(y)
    safe = (u != 1.0) & (y > -1.0)
    ratio = y / jnp.log(jnp.where(safe, u, 2.0))
    return jnp.where(u == 1.0, y, jnp.where(safe, (u - 1.0) * ratio, u - 1.0))


def _loop_groups(n, body, init=0, *, width):
    carry, start = init, 0
    while width >= 1:
        def group(i, carry, start=start, width=width):
            for u in range(width):
                carry = body(start + width * i + u, carry)
            return carry

        trips = lax.shift_right_logical(n - start, _log2(width))
        carry = lax.fori_loop(0, trips, group, carry)
        start = start + trips * width
        width //= 2
    return carry


def _log2(n):
    l = int(n).bit_length() - 1
    assert (1 << l) == n
    return l


def _ada_kernel(c_ref, w_ref, b_ref, o_ref):
    c = c_ref[...]
    sc = (c * jax.nn.sigmoid(c)).astype(BF16)
    o_ref[0] = jnp.dot(sc, w_ref[0].astype(BF16), preferred_element_type=F32) + b_ref[0]


def _ada_call(c_pad, w_ada, b_ada):
    depth, d, n = w_ada.shape
    tn = 1152
    return pl.pallas_call(
        _ada_kernel,
        grid=(depth, n // tn),
        in_specs=[pl.BlockSpec((c_pad.shape[0], d), lambda l, j: (0, 0)),
                  pl.BlockSpec((1, d, tn), lambda l, j: (l, 0, j)),
                  pl.BlockSpec((1, 1, tn), lambda l, j: (l, 0, j))],
        out_specs=pl.BlockSpec((1, c_pad.shape[0], tn), lambda l, j: (l, 0, j)),
        out_shape=jax.ShapeDtypeStruct((depth, c_pad.shape[0], n), F32),
        compiler_params=_cparams(("arbitrary", "arbitrary")),
        name="adaln",
    )(c_pad, w_ada, b_ada.reshape(depth, 1, n))


def _ffn_kernel(x_ref, mod_ref, g_ref, wg_ref, wu_ref, wo_ref, o_ref, h_ref, acc_ref, *, sub, resid_w):
    k = pl.program_id(1)
    m = mod_ref[0]

    @pl.when(k == 0)
    def _():
        x = x_ref[...]
        xn = _rms(x, g_ref[2 * sub:2 * sub + 1])
        h_ref[...] = (xn * (1.0 + m[3 * sub + 1:3 * sub + 2]) + m[3 * sub:3 * sub + 1]).astype(BF16)
        acc_ref[...] = jnp.zeros_like(acc_ref)

    h = h_ref[...]
    gate = jnp.dot(h, wg_ref[0].astype(BF16), preferred_element_type=F32)
    up = jnp.dot(h, wu_ref[0].astype(BF16), preferred_element_type=F32)
    act = (gate * jax.nn.sigmoid(gate) * up).astype(BF16)
    acc_ref[...] += jnp.dot(act, wo_ref[0].astype(BF16), preferred_element_type=F32)

    @pl.when(k == pl.num_programs(1) - 1)
    def _():
        yn = _rms(acc_ref[...], g_ref[2 * sub + 1:2 * sub + 2])
        o_ref[...] = x_ref[...] + resid_w * m[3 * sub + 2:3 * sub + 3] * yn


def _ffn_call(x2, mod_l, g_l, w_in, w_out, layer, *, sub, resid_w, s_len):
    t, d = x2.shape
    f = w_out.shape[1]
    tm = min(FFN_TM, s_len)
    tf = FFN_TF
    nf = f // tf
    tpb = s_len // tm
    return pl.pallas_call(
        functools.partial(_ffn_kernel, sub=sub, resid_w=resid_w),
        grid=(t // tm, nf),
        in_specs=[pl.BlockSpec((tm, d), lambda i, k: (i, 0)),
                  pl.BlockSpec((1, 9, d), lambda i, k: (i // tpb, 0, 0)),
                  pl.BlockSpec((6, d), lambda i, k: (0, 0)),
                  pl.BlockSpec((1, d, tf), lambda i, k: (layer, 0, k)),
                  pl.BlockSpec((1, d, tf), lambda i, k: (layer, 0, k + nf)),
                  pl.BlockSpec((1, tf, d), lambda i, k: (layer, k, 0))],
        out_specs=pl.BlockSpec((tm, d), lambda i, k: (i, 0)),
        out_shape=jax.ShapeDtypeStruct((t, d), F32),
        scratch_shapes=[pltpu.VMEM((tm, d), BF16), pltpu.VMEM((tm, d), F32)],
        compiler_params=_cparams(("arbitrary", "arbitrary")),
        name="ffn",
    )(x2, mod_l, g_l, w_in, w_in, w_out)


def _proj_kernel(x_ref, mod_ref, g_ref, w_ref, kpos_ref, oa_ref, oi_ref, ow_ref, or_ref, oc_ref, od_ref):
    m = mod_ref[0]
    xn = _rms(x_ref[...], g_ref[2:3])
    h = (xn * (1.0 + m[4:5]) + m[3:4]).astype(BF16)
    off = 0
    for ref in (oa_ref, oi_ref, ow_ref, or_ref, oc_ref, od_ref):
        w = ref.shape[1]
        z = jnp.dot(h, w_ref[:, off:off + w], preferred_element_type=F32)
        if ref is oa_ref or ref is od_ref:
            qscale = (HEAD_DIM ** -0.5) * (LOG2E if ref is oa_ref else 1.0)
            z = jnp.concatenate([z[:, :512] * qscale, z[:, 512:768] + kpos_ref[...], z[:, 768:]], axis=1)
        ref[...] = z.astype(ref.dtype)
        off += w


def _proj_call(x2, mod_l, g_l, w_all, kpos, *, s_len):
    t, d = x2.shape
    tm = min(PROJ_TM, s_len)
    tpb = s_len // tm
    widths = (QKV_W, IDX_W, LANES, 2 * LRU_WIDTH, MLA_W, QKV_W)
    dtypes = (BF16, BF16, F32, F32, F32, BF16)
    assert sum(widths) == w_all.shape[1]
    return pl.pallas_call(
        _proj_kernel,
        grid=(t // tm,),
        in_specs=[pl.BlockSpec((tm, d), lambda i: (i, 0)),
                  pl.BlockSpec((1, 9, d), lambda i: (i // tpb, 0, 0)),
                  pl.BlockSpec((6, d), lambda i: (0, 0)),
                  pl.BlockSpec(w_all.shape, lambda i: (0, 0)),
                  pl.BlockSpec((tm, 256), lambda i: (i % tpb, 0))],
        out_specs=[pl.BlockSpec((tm, w), lambda i: (i, 0)) for w in widths],
        out_shape=[jax.ShapeDtypeStruct((t, w), dt) for w, dt in zip(widths, dtypes)],
        compiler_params=_cparams(("arbitrary",)),
        name="mixer_proj",
    )(x2, mod_l, g_l, w_all, kpos)


def _head_rows(q, n_heads, tq):
    lane = lax.broadcasted_iota(I32, (tq, LANES), 1)
    rows = []
    for h in range(n_heads):
        tile = q[:, LANES * (h // 2):LANES * (h // 2 + 1)].astype(F32)
        keep = (lane >= HEAD_DIM) if (h % 2) else (lane < HEAD_DIM)
        rows.append(jnp.where(keep, tile, 0.0).astype(BF16))
    return jnp.concatenate(rows, axis=0)


def _alibi_lanes(slopes, tq):
    lane = lax.broadcasted_iota(I32, (tq, LANES), 1)
    out = []
    for slope in slopes:
        tile = jnp.zeros((tq, LANES), F32)
        for i, part in enumerate(_bf16_parts(slope)):
            tile = jnp.where(lane == HEAD_DIM + 2 * i, POS_SPLIT * part, tile)
            tile = jnp.where(lane == HEAD_DIM + 2 * i + 1, part, tile)
        out.append(tile)
    return out


def _alibi_q_tiles(q, alibi_lanes, tq):
    lane = lax.broadcasted_iota(I32, (tq, LANES), 1)
    rows = []
    for h, al in enumerate(alibi_lanes):
        tile = q[:, LANES * (h // 2):LANES * (h // 2 + 1)].astype(F32)
        if h % 2:
            tile = pltpu.roll(tile, HEAD_DIM, axis=1)
        rows.append(jnp.where(lane < HEAD_DIM, tile, al).astype(BF16))
    return jnp.concatenate(rows, axis=0)


def _gqa_out(accs, inv_ls, tq):
    lane = lax.broadcasted_iota(I32, (tq, LANES), 1)
    tiles = []
    for j in range(4):
        g = j // 2
        halves = []
        for h in (2 * j, 2 * j + 1):
            o = accs[g][h % 4] * inv_ls[g][h % 4]
            src_hi = (g == 1)
            dst_hi = (h % 2 == 1)
            if src_hi != dst_hi:
                o = pltpu.roll(o, HEAD_DIM, axis=1)
            halves.append(o)
        tiles.append(jnp.where(lane < HEAD_DIM, halves[0], halves[1]))
    return jnp.concatenate(tiles, axis=1)


def _dsa_kernel(q_ref, k_ref, v_ref, iq_ref, ik_ref, iw_ref, o_ref,
                key_ref, s_ref, tau_ref, iqs_ref, qs_ref, m_ref, l_ref, acc_ref, *, tq, kc, topk, pos_bits):
    qb = pl.program_id(1)
    q0 = qb * tq
    nk = lax.shift_right_logical(q0 + tq + kc - 1, _log2(kc))
    n_lt = kc // LANES
    n_rt = kc // CNT_ROWS
    qpos_l = q0 + lax.broadcasted_iota(I32, (kc, tq), 1)
    krow = lax.broadcasted_iota(I32, (kc, tq), 0)
    krow_t = lax.broadcasted_iota(I32, (CNT_ROWS, tq), 0)

    iqs_ref[...] = _head_rows(iq_ref[...], IDX_HEADS, tq)
    iw_t = iw_ref[...].T

    def score_body(j, carry):
        ks = pl.multiple_of(j * kc, kc)
        d = _nt_dot(ik_ref[pl.ds(ks, kc), :], iqs_ref[...])
        acc = jnp.zeros((kc, tq), F32)
        for h in range(IDX_HEADS):
            acc = acc + iw_t[h:h + 1, :] * jnp.maximum(d[:, h * tq:(h + 1) * tq], 0.0)
        bits = pltpu.bitcast(acc, I32)
        key = jnp.where(bits < 0, bits ^ INT_MAX, bits)
        key_ref[j] = jnp.where(ks + krow <= qpos_l, key, INT_MIN)
        return carry

    _loop_groups(nk, score_body, width=DSA_GROUP)

    def count(pred):
        def body(j, cnt):
            for r in range(n_rt):
                kp = (j * kc + r * CNT_ROWS) + krow_t
                cnt = cnt + jnp.where(pred(key_ref[j, r * CNT_ROWS:(r + 1) * CNT_ROWS, :], kp), 1.0, 0.0)
            return cnt
        cnt = _loop_groups(nk, body, jnp.zeros((CNT_ROWS, tq), F32), width=CNT_GROUP)
        return jnp.sum(cnt, axis=0, keepdims=True)

    kf = float(topk)

    def bit_step(i, state):
        tau, cnt = state
        cand = tau + lax.shift_left(jnp.int32(1), 31 - i)
        cb = jnp.broadcast_to(cand, (CNT_ROWS, tq))
        c = count(lambda kv, kp: kv >= cb)
        ok = c >= kf
        return jnp.where(ok, cand, tau), jnp.where(ok, c, cnt)

    tau, n_ge = lax.fori_loop(0, 32, bit_step, (jnp.full((1, tq), INT_MIN, I32),
                                                 jnp.broadcast_to((nk * kc).astype(F32), (1, tq))))

    tie_f = jnp.where((n_ge > kf) & (tau > INT_MIN), 1.0, 0.0)
    any_tie = jnp.max(tie_f) > 0.0
    tau_c = jnp.maximum(tau, INT_MIN + 1)
    tau_ref[...] = jnp.broadcast_to(tau_c, tau_ref.shape)

    @pl.when(any_tie)
    def _():
        tb = jnp.broadcast_to(tau_c, (CNT_ROWS, tq))
        need = kf - count(lambda kv, kp: kv > tb)

        def pos_body(i, p):
            cand = p | lax.shift_left(jnp.int32(1), pos_bits - 1 - i)
            cb = jnp.broadcast_to(cand, (CNT_ROWS, tq))
            below = count(lambda kv, kp: (kv == tb) & (kp < cb))
            return jnp.where(below < need, cand, p)

        p = lax.fori_loop(0, pos_bits, pos_body, jnp.zeros((1, tq), I32))
        p = jnp.where(tie_f > 0.0, p, INT_MAX)

        def rewrite_body(j, carry):
            kj = key_ref[j]
            kp = j * kc + krow
            sel = (kj > tau_c) | ((kj == tau_c) & (kp <= p))
            key_ref[j] = jnp.where(sel, 1, -1)
            return carry

        lax.fori_loop(0, nk, rewrite_body, 0)
        tau_ref[...] = jnp.zeros(tau_ref.shape, I32)

    q = q_ref[...]
    for g in range(A_KV_HEADS):
        qs_ref[g] = _alibi_q_tiles(q[:, 4 * HEAD_DIM * g:4 * HEAD_DIM * (g + 1)],
                                   _alibi_lanes([s * LOG2E for s in SLOPES_A[4 * g:4 * g + 4]], tq), tq)
    m_ref[...] = jnp.full(m_ref.shape, NEG, F32)
    l_ref[...] = jnp.zeros(l_ref.shape, F32)
    acc_ref[...] = jnp.zeros(acc_ref.shape, F32)

    def max_body(j, carry):
        ks = pl.multiple_of(j * kc, kc)
        sel = jnp.where(key_ref[j] >= tau_ref[0:1, :], 1.0, 0.0).T > 0.5
        for g in range(A_KV_HEADS):
            s = _nt_dot(qs_ref[g], k_ref[pl.ds(ks, kc), g * LANES:(g + 1) * LANES])
            for h in range(4):
                sh = jnp.where(sel, s[h * tq:(h + 1) * tq], NEG)
                s_ref[g, h, j] = sh
                mp = m_ref[g, h]
                for c in range(n_lt):
                    mp = jnp.maximum(mp, sh[:, c * LANES:(c + 1) * LANES])
                m_ref[g, h] = mp
        return carry

    _loop_groups(nk, max_body, width=DSA_GROUP)
    for g in range(A_KV_HEADS):
        for h in range(4):
            m_ref[g, h] = jnp.broadcast_to(jnp.max(m_ref[g, h], axis=1, keepdims=True), (tq, LANES))

    def pv_body(j, carry):
        ks = pl.multiple_of(j * kc, kc)
        vch = v_ref[pl.ds(ks, kc), :]
        for g in range(A_KV_HEADS):
            ps = []
            for h in range(4):
                mb = m_ref[g, h]
                lp = l_ref[g, h]
                tiles = []
                for c in range(n_lt):
                    p = jnp.exp2(s_ref[g, h, j, :, c * LANES:(c + 1) * LANES] - mb)
                    lp = lp + p
                    tiles.append(p.astype(BF16))
                l_ref[g, h] = lp
                ps.append(jnp.concatenate(tiles, axis=1))
            pv = jnp.dot(jnp.concatenate(ps, axis=0), vch, preferred_element_type=F32)
            for h in range(4):
                acc_ref[g, h] += pv[h * tq:(h + 1) * tq]
        return carry

    _loop_groups(nk, pv_body, width=DSA_GROUP)
    accs = [[acc_ref[g, h] for h in range(4)] for g in range(A_KV_HEADS)]
    inv_ls = [[1.0 / jnp.sum(l_ref[g, h], axis=1, keepdims=True) for h in range(4)] for g in range(A_KV_HEADS)]
    o_ref[...] = _gqa_out(accs, inv_ls, tq).astype(o_ref.dtype)


def _dsa_call(qkv, idx, iw, *, batch, s_len):
    t = qkv.shape[0]
    tq = min(DSA_TQ, s_len)
    kc = min(DSA_KC, s_len)
    topk = min(TOPK_MAX, s_len // 4)
    assert kc >= topk and kc % tq == 0
    nqb = s_len // tq
    once = pl.Buffered(1)
    return pl.pallas_call(
        functools.partial(_dsa_kernel, tq=tq, kc=kc, topk=topk, pos_bits=_log2(s_len)),
        grid=(batch, nqb),
        in_specs=[pl.BlockSpec((tq, 512), lambda b, i: (b * nqb + i, 0)),
                  pl.BlockSpec((s_len, 256), lambda b, i: (b, 2), pipeline_mode=once),
                  pl.BlockSpec((s_len, 128), lambda b, i: (b, 6), pipeline_mode=once),
                  pl.BlockSpec((tq, 512), lambda b, i: (b * nqb + i, 0)),
                  pl.BlockSpec((s_len, 128), lambda b, i: (b, 4), pipeline_mode=once),
                  pl.BlockSpec((tq, LANES), lambda b, i: (b * nqb + i, 0))],
        out_specs=pl.BlockSpec((tq, 512), lambda b, i: (b * nqb + i, 0)),
        out_shape=jax.ShapeDtypeStruct((t, 512), BF16),
        scratch_shapes=[pltpu.VMEM((s_len // kc, kc, tq), I32),
                        pltpu.VMEM((A_KV_HEADS, 4, s_len // kc, tq, kc), F32),
                        pltpu.VMEM((8, tq), I32),
                        pltpu.VMEM((IDX_HEADS * tq, LANES), BF16),
                        pltpu.VMEM((A_KV_HEADS, 4 * tq, LANES), BF16),
                        pltpu.VMEM((A_KV_HEADS, 4, tq, LANES), F32),
                        pltpu.VMEM((A_KV_HEADS, 4, tq, LANES), F32),
                        pltpu.VMEM((A_KV_HEADS, 4, tq, LANES), F32)],
        compiler_params=pltpu.CompilerParams(dimension_semantics=("arbitrary", "arbitrary"),
                                             vmem_limit_bytes=DSA_VMEM_LIMIT),
        name="dsa",
    )(qkv, qkv, qkv, idx, idx, iw)


def _swa_kernel(q_ref, kp_ref, kc_ref, vp_ref, vc_ref, sink_ref, o_ref, *, tq, nblk):
    i = pl.program_id(1)
    krow = lax.broadcasted_iota(I32, (2 * tq, tq), 0)
    qcol = lax.broadcasted_iota(I32, (2 * tq, tq), 1)
    dist = qcol + tq - krow
    in_band = (dist >= 0) & (dist < WINDOW)
    sinks = sink_ref[...]
    sink_vec = [jnp.broadcast_to(sinks[:, h:h + 1], (1, tq)) for h in range(D_HEADS)]
    alibi = _alibi_lanes(SLOPES_D, tq)
    for n in range(nblk):
        blk = i * nblk + n
        rows = slice(n * tq, (n + 1) * tq)
        q = q_ref[rows, :]
        k_prev = kp_ref[...] if n == 0 else kc_ref[(n - 1) * tq:n * tq, :]
        v_prev = vp_ref[...] if n == 0 else vc_ref[(n - 1) * tq:n * tq, :]
        kk = jnp.concatenate([k_prev, kc_ref[rows, :]], axis=0)
        vv_t = jnp.concatenate([v_prev, vc_ref[rows, :]], axis=0).T
        valid = in_band & ((blk * tq - tq + krow) >= 0)
        qpos = (blk * tq + lax.broadcasted_iota(I32, (1, tq), 1)).astype(F32)
        tiles = []
        for g in range(D_KV_HEADS):
            qs = _alibi_q_tiles(q[:, 4 * HEAD_DIM * g:4 * HEAD_DIM * (g + 1)], alibi[4 * g:4 * g + 4], tq)
            s_t = _nt_dot(kk[:, g * LANES:(g + 1) * LANES], qs)
            ps, ils = [], []
            for h in range(4):
                sh = jnp.where(valid, s_t[:, h * tq:(h + 1) * tq], NEG)
                sink = sink_vec[4 * g + h] + SLOPES_D[4 * g + h] * qpos
                m = jnp.maximum(jnp.max(sh, axis=0, keepdims=True), sink)
                p = jnp.exp(sh - m)
                ils.append(1.0 / (jnp.sum(p, axis=0, keepdims=True) + jnp.exp(sink - m)))
                ps.append(p.astype(BF16))
            o_t = jnp.dot(vv_t, jnp.concatenate(ps, axis=1), preferred_element_type=F32)
            o_t = o_t[g * HEAD_DIM:(g + 1) * HEAD_DIM, :] * jnp.concatenate(ils, axis=1)
            for j in range(2):
                pair = jnp.concatenate([o_t[:, (2 * j) * tq:(2 * j + 1) * tq],
                                        o_t[:, (2 * j + 1) * tq:(2 * j + 2) * tq]], axis=0)
                tiles.append(pair.T)
        o_ref[rows, :] = jnp.concatenate(tiles, axis=1).astype(o_ref.dtype)


def _swa_call(qkv, sinks_pad, *, batch, s_len):
    t = qkv.shape[0]
    tq = BLOCK
    nblk = min(SWA_BLOCKS, s_len // tq)
    ts = nblk * tq
    nst = s_len // ts
    cur = lambda b, i: b * nst + i
    prev = lambda b, i: jnp.maximum((b * nst + i) * nblk - 1, 0)
    return pl.pallas_call(
        functools.partial(_swa_kernel, tq=tq, nblk=nblk),
        grid=(batch, nst),
        in_specs=[pl.BlockSpec((ts, 512), lambda b, i: (cur(b, i), 0)),
                  pl.BlockSpec((tq, 256), lambda b, i: (prev(b, i), 2)),
                  pl.BlockSpec((ts, 256), lambda b, i: (cur(b, i), 2)),
                  pl.BlockSpec((tq, 128), lambda b, i: (prev(b, i), 6)),
                  pl.BlockSpec((ts, 128), lambda b, i: (cur(b, i), 6)),
                  pl.BlockSpec((1, LANES), lambda b, i: (0, 0))],
        out_specs=pl.BlockSpec((ts, 512), lambda b, i: (cur(b, i), 0)),
        out_shape=jax.ShapeDtypeStruct((t, 512), BF16),
        compiler_params=_cparams(("arbitrary", "arbitrary")),
        name="swa",
    )(qkv, qkv, qkv, qkv, qkv, sinks_pad)


def _mla_prep_kernel(c_ref, cos_ref, sin_ref, gq_ref, gkv_ref, wq_ref, wqr_ref, wk_ref, wv_ref,
                     q_ref, k_ref, v_ref):
    c = c_ref[...]
    cos = cos_ref[...]
    sin = sin_ref[...]
    cos8 = jnp.concatenate([cos] * C_HEADS, axis=1)
    sin8 = jnp.concatenate([sin] * C_HEADS, axis=1)
    cqn = _rms(c[:, :Q_LORA], gq_ref[...]).astype(BF16)
    q = (jnp.dot(cqn, wq_ref[...], preferred_element_type=F32) * cos8
         + jnp.dot(cqn, wqr_ref[...], preferred_element_type=F32) * sin8)
    q_ref[...] = q.astype(BF16)
    ckvn = _rms(c[:, Q_LORA:Q_LORA + KV_LORA], gkv_ref[...]).astype(BF16)
    kr = c[:, 384:512] * cos + c[:, 512:640] * sin
    k = jnp.dot(ckvn, wk_ref[...], preferred_element_type=F32) + jnp.concatenate([kr] * C_HEADS, axis=1)
    k_ref[...] = k.astype(BF16)
    v_ref[...] = jnp.dot(ckvn, wv_ref[...], preferred_element_type=F32).astype(BF16)


def _mla_prep_call(cm, cos_t, sin_t, gq, gkv, wq, wqr, wk, wv, *, s_len):
    t = cm.shape[0]
    tm = min(PROJ_TM, s_len)
    tpb = s_len // tm
    full = lambda a: pl.BlockSpec(a.shape, lambda i: (0, 0))
    return pl.pallas_call(
        _mla_prep_kernel,
        grid=(t // tm,),
        in_specs=[pl.BlockSpec((tm, MLA_W), lambda i: (i, 0)),
                  pl.BlockSpec((tm, LANES), lambda i: (i % tpb, 0)),
                  pl.BlockSpec((tm, LANES), lambda i: (i % tpb, 0)),
                  full(gq), full(gkv), full(wq), full(wqr), full(wk), full(wv)],
        out_specs=[pl.BlockSpec((tm, 1024), lambda i: (i, 0)),
                   pl.BlockSpec((tm, 1024), lambda i: (i, 0)),
                   pl.BlockSpec((tm, 512), lambda i: (i, 0))],
        out_shape=[jax.ShapeDtypeStruct((t, 1024), BF16),
                   jax.ShapeDtypeStruct((t, 1024), BF16),
                   jax.ShapeDtypeStruct((t, 512), BF16)],
        compiler_params=_cparams(("arbitrary",)),
        name="mla_prep",
    )(cm, cos_t, sin_t, gq, gkv, wq, wqr, wk, wv)


def _mla_attn_kernel(q_ref, k_ref, v_ref, o_ref, s_ref, m_ref, l_ref, acc_ref, *, tq, kc):
    qb = pl.program_id(2)
    q0 = qb * tq
    n_full = lax.shift_right_logical(q0, _log2(kc))
    rowpos = q0 + lax.broadcasted_iota(I32, (tq, kc), 0)
    lane = lax.broadcasted_iota(I32, (tq, kc), 1)
    c = ((QK_NOPE + QK_ROPE) ** -0.5) * LOG2E
    n_lt = kc // LANES
    m_ref[...] = jnp.full(m_ref.shape, NEG, F32)
    l_ref[...] = jnp.zeros(l_ref.shape, F32)
    acc_ref[...] = jnp.zeros(acc_ref.shape, F32)

    def max_step(j, masked):
        ks = pl.multiple_of(j * kc, kc)
        for hh in range(2):
            s = _nt_dot(q_ref[:, hh * LANES:(hh + 1) * LANES],
                        k_ref[pl.ds(ks, kc), hh * LANES:(hh + 1) * LANES]) * c
            if masked:
                s = jnp.where(ks + lane <= rowpos, s, NEG)
            s_ref[hh, j] = s
            mp = m_ref[hh]
            for t in range(n_lt):
                mp = jnp.maximum(mp, s[:, t * LANES:(t + 1) * LANES])
            m_ref[hh] = mp

    def max_body(j, carry):
        max_step(j, False)
        return carry

    _loop_groups(n_full, max_body, width=MLA_GROUP)
    max_step(n_full, True)
    for hh in range(2):
        m_ref[hh] = jnp.broadcast_to(jnp.max(m_ref[hh], axis=1, keepdims=True), (tq, LANES))

    def pv_body(j, carry):
        ks = pl.multiple_of(j * kc, kc)
        vch = v_ref[pl.ds(ks, kc), :]
        for hh in range(2):
            mb = m_ref[hh]
            lp = l_ref[hh]
            tiles = []
            for t in range(n_lt):
                p = jnp.exp2(s_ref[hh, j, :, t * LANES:(t + 1) * LANES] - mb)
                lp = lp + p
                tiles.append(p.astype(BF16))
            l_ref[hh] = lp
            acc_ref[hh] += jnp.dot(jnp.concatenate(tiles, axis=1), vch, preferred_element_type=F32)
        return carry

    _loop_groups(n_full + 1, pv_body, width=MLA_GROUP)
    outs = [acc_ref[hh] * (1.0 / jnp.sum(l_ref[hh], axis=1, keepdims=True)) for hh in range(2)]
    lane_o = lax.broadcasted_iota(I32, (tq, LANES), 1)
    o_ref[...] = jnp.where(lane_o < V_DIM, outs[0], outs[1]).astype(o_ref.dtype)


def _mla_attn_call(qm, km, vm, *, batch, s_len):
    t = qm.shape[0]
    tq = min(MLA_TQ, s_len)
    kc = min(MLA_KC, s_len)
    assert kc % tq == 0
    nqb = s_len // tq
    once = pl.Buffered(1)
    return pl.pallas_call(
        functools.partial(_mla_attn_kernel, tq=tq, kc=kc),
        grid=(batch, C_HEADS // 2, nqb),
        in_specs=[pl.BlockSpec((tq, 256), lambda b, h, i: (b * nqb + i, h)),
                  pl.BlockSpec((s_len, 256), lambda b, h, i: (b, h), pipeline_mode=once),
                  pl.BlockSpec((s_len, 128), lambda b, h, i: (b, h), pipeline_mode=once)],
        out_specs=pl.BlockSpec((tq, 128), lambda b, h, i: (b * nqb + i, h)),
        out_shape=jax.ShapeDtypeStruct((t, 512), BF16),
        scratch_shapes=[pltpu.VMEM((2, s_len // kc, tq, kc), F32)] + [pltpu.VMEM((2, tq, LANES), F32)] * 3,
        compiler_params=_cparams(("arbitrary", "arbitrary", "arbitrary")),
        name="mla_attn",
    )(qm, km, vm)


def _lru_kernel(xr_ref, xg_ref, prev_ref, cw_ref, cb_ref, wa_ref, ba_ref, wx_ref, bx_ref, lam_ref,
                o_ref, xe_ref, h_ref, *, ln):
    i = pl.program_id(1)
    xe_ref[0:8, :] = jnp.where(i > 0, prev_ref[...], 0.0)
    xe_ref[8:8 + ln, :] = xr_ref[...]
    cw = cw_ref[...]
    xc = cb_ref[...] + cw[0:1] * xe_ref[pl.ds(5, ln), :]
    for j in range(1, CONV_WIDTH):
        xc = xc + cw[j:j + 1] * xe_ref[pl.ds(5 + j, ln), :]
    xcb = xc.astype(BF16)
    r = jax.nn.sigmoid(jnp.dot(xcb, wa_ref[...], preferred_element_type=F32) + ba_ref[...])
    gi = jax.nn.sigmoid(jnp.dot(xcb, wx_ref[...], preferred_element_type=F32) + bx_ref[...])
    z = -lam_ref[...]
    softplus = jnp.maximum(z, 0.0) + jnp.log1p(jnp.exp(-jnp.abs(z)))
    log_a = -LRU_C * r * softplus
    a = jnp.exp(log_a)
    b = jnp.sqrt(-_expm1(2.0 * log_a)) * (gi * xc)
    row = lax.broadcasted_iota(I32, (ln, LRU_WIDTH), 0)
    d = 1
    while d < ln:
        keep = row >= d
        b = jnp.where(keep, a * pltpu.roll(b, d, axis=0) + b, b)
        a = jnp.where(keep, a * pltpu.roll(a, d, axis=0), a)
        d *= 2
    h_prev = jnp.where(i > 0, h_ref[0:1, :], 0.0)
    h = a * h_prev + b
    h_ref[0:1, :] = h[ln - 1:ln, :]
    xg = xg_ref[...]
    gelu = 0.5 * xg * (1.0 + jnp.tanh(0.7978845608028654 * (xg + 0.044715 * (xg * xg * xg))))
    o_ref[...] = (h * gelu).astype(o_ref.dtype)


def _lru_call(rm, cw, cb, wa, ba, wx, bx, lam, *, batch, s_len):
    t = rm.shape[0]
    ln = min(LRU_L, s_len)
    nt = s_len // ln
    w = LRU_WIDTH
    vec = lambda a: pl.BlockSpec(a.shape, lambda b, i: (0, 0))
    return pl.pallas_call(
        functools.partial(_lru_kernel, ln=ln),
        grid=(batch, nt),
        in_specs=[pl.BlockSpec((ln, w), lambda b, i: (b * nt + i, 0)),
                  pl.BlockSpec((ln, w), lambda b, i: (b * nt + i, 1)),
                  pl.BlockSpec((8, w), lambda b, i: (jnp.maximum((b * nt + i) * (ln // 8) - 1, 0), 0)),
                  vec(cw), vec(cb), vec(wa), vec(ba), vec(wx), vec(bx), vec(lam)],
        out_specs=pl.BlockSpec((ln, w), lambda b, i: (b * nt + i, 0)),
        out_shape=jax.ShapeDtypeStruct((t, w), BF16),
        scratch_shapes=[pltpu.VMEM((ln + 8, w), F32), pltpu.VMEM((8, w), F32)],
        compiler_params=_cparams(("arbitrary", "arbitrary")),
        name="rglru",
    )(rm, rm, rm, cw, cb, wa, ba, wx, bx, lam)


def _merge_kernel(x_ref, mod_ref, g_ref, ya_ref, yb_ref, yc_ref, yd_ref, wg_ref, wb_ref, wo_ref, o_ref):
    m = mod_ref[0]
    x = x_ref[...]
    h = (_rms(x, g_ref[2:3]) * (1.0 + m[4:5]) + m[3:4]).astype(BF16)
    merged = None
    for n, y_ref in enumerate((ya_ref, yb_ref, yc_ref, yd_ref)):
        gate = jax.nn.sigmoid(jnp.dot(h, wg_ref[:, n * D_MODEL:(n + 1) * D_MODEL], preferred_element_type=F32))
        term = gate * jnp.dot(y_ref[...], wb_ref[n], preferred_element_type=F32)
        merged = term if merged is None else merged + term
    y = jnp.dot(merged.astype(BF16), wo_ref[...], preferred_element_type=F32)
    o_ref[...] = x + m[5:6] * _rms(y, g_ref[3:4])


def _merge_call(x2, mod_l, g_l, ys, w_gate, w_branch, w_out, *, s_len):
    t, d = x2.shape
    tm = min(MERGE_TM, s_len)
    tpb = s_len // tm
    return pl.pallas_call(
        _merge_kernel,
        grid=(t // tm,),
        in_specs=[pl.BlockSpec((tm, d), lambda i: (i, 0)),
                  pl.BlockSpec((1, 9, d), lambda i: (i // tpb, 0, 0)),
                  pl.BlockSpec((6, d), lambda i: (0, 0))]
                 + [pl.BlockSpec((tm, BRANCH_WIDTH), lambda i: (i, 0))] * N_BRANCH
                 + [pl.BlockSpec(w_gate.shape, lambda i: (0, 0)),
                    pl.BlockSpec(w_branch.shape, lambda i: (0, 0, 0)),
                    pl.BlockSpec(w_out.shape, lambda i: (0, 0))],
        out_specs=pl.BlockSpec((tm, d), lambda i: (i, 0)),
        out_shape=jax.ShapeDtypeStruct((t, d), F32),
        compiler_params=_cparams(("arbitrary",)),
        name="merge",
    )(x2, mod_l, g_l, *ys, w_gate, w_branch, w_out)


def _mixer_weights(w_in_l):
    o = IN_OFFSETS
    col = lambda n: w_in_l[:, o[n]:o[n + 1]]
    z = lambda n: jnp.zeros((D_MODEL, n), w_in_l.dtype)

    def pad_heads(k):
        return jnp.concatenate([k[:, :64], z(64), k[:, 64:], z(64)], axis=1)

    a_q, a_k, a_v, i_q, i_k, i_w, r_x, r_g, c_q, c_kv, d_q, d_k, d_v = (col(n) for n in range(13))
    k_rope = c_kv[:, KV_LORA:]
    half = QK_ROPE // 2
    k_rope_rot = jnp.concatenate([-k_rope[:, half:], k_rope[:, :half]], axis=1)
    groups = [a_q, pad_heads(a_k), a_v,
              i_q, i_k, i_k,
              i_w, z(LANES - IDX_HEADS),
              r_x, r_g,
              c_q, c_kv[:, :KV_LORA], z(64), k_rope, z(32), z(64), k_rope_rot, z(32),
              d_q, pad_heads(d_k), d_v]
    w_all = jnp.concatenate(groups, axis=1).astype(BF16)
    w_gate = col(13).astype(BF16)
    return w_all, w_gate


def _kpos_table(s_len):
    pos = np.arange(s_len)
    tile = np.zeros((s_len, LANES), np.float32)
    for i in range(3):
        tile[:, HEAD_DIM + 2 * i] = pos // POS_SPLIT
        tile[:, HEAD_DIM + 2 * i + 1] = pos % POS_SPLIT
    return jnp.asarray(np.concatenate([tile, tile], axis=1))


def _mla_weights(w_uq, w_ukv):
    dq = QK_NOPE + QK_ROPE
    half = QK_ROPE // 2
    zq = lambda n: jnp.zeros((Q_LORA, n), w_uq.dtype)
    wq, wqr, wk, wv = [], [], [], []
    for h in range(C_HEADS):
        nope = w_uq[:, h * dq:h * dq + QK_NOPE]
        r1 = w_uq[:, h * dq + QK_NOPE:h * dq + QK_NOPE + half]
        r2 = w_uq[:, h * dq + QK_NOPE + half:(h + 1) * dq]
        wq += [nope, r1, r2, zq(32)]
        wqr += [zq(64), -r2, r1, zq(32)]
        wk += [w_ukv[:, h * 128:h * 128 + QK_NOPE], jnp.zeros((KV_LORA, 64), w_ukv.dtype)]
        wv += [w_ukv[:, h * 128 + QK_NOPE:(h + 1) * 128]]
    cat = lambda xs: jnp.concatenate(xs, axis=1).astype(BF16)
    return cat(wq), cat(wqr), cat(wk), cat(wv)


def _rope_tables(s_len):
    half = QK_ROPE // 2
    inv = (np.float32(ROPE_THETA) ** (-np.arange(half, dtype=np.float32) / np.float32(half))).astype(np.float32)
    ang = (np.arange(s_len, dtype=np.float32)[:, None] * inv[None, :]).astype(np.float32)
    cos, sin = np.cos(ang.astype(np.float64)).astype(np.float32), np.sin(ang.astype(np.float64)).astype(np.float32)
    ones = lambda n: np.ones((s_len, n), np.float32)
    zeros = lambda n: np.zeros((s_len, n), np.float32)
    cos_t = np.concatenate([ones(64), cos, cos, ones(32)], axis=1)
    sin_t = np.concatenate([zeros(64), sin, sin, zeros(32)], axis=1)
    return jnp.asarray(cos_t), jnp.asarray(sin_t)


def _block_diag(w):
    n, bw, _ = w.shape
    eye = jnp.eye(n, dtype=w.dtype)
    return (w[:, :, None, :] * eye[:, None, :, None]).reshape(n * bw, n * bw).astype(BF16)


def kernel(x, c, w_ada, b_ada, norm_g, ffn1_w_in, ffn1_w_out, w_in, conv_w, conv_b, lru_wa, lru_ba, lru_wx,
           lru_bx, lru_lambda, mla_g_q, mla_g_kv, mla_w_uq, mla_w_ukv, swa_sinks, w_branch, w_out, ffn2_w_in,
           ffn2_w_out):
    batch, s_len, d = x.shape
    depth = w_ada.shape[0]
    t = batch * s_len
    assert s_len // POS_SPLIT <= 256
    x2 = x.reshape(t, d)
    c_pad = jnp.zeros((8, d), F32).at[:batch].set(c)
    mod_all = _ada_call(c_pad, w_ada, b_ada)
    cos_t, sin_t = _rope_tables(s_len)
    kpos = _kpos_table(s_len)
    row = lambda v: v.reshape(1, -1)
    for l in range(depth):
        mod_l = mod_all[l, :batch].reshape(batch, 9, d)
        g_l = norm_g[l]
        x2 = _ffn_call(x2, mod_l, g_l, ffn1_w_in, ffn1_w_out, l, sub=0, resid_w=0.5, s_len=s_len)
        w_all, w_gate = _mixer_weights(w_in[l])
        qkv_a, idx, iw, rm, cm, qkv_d = _proj_call(x2, mod_l, g_l, w_all, kpos, s_len=s_len)
        y_a = _dsa_call(qkv_a, idx, iw, batch=batch, s_len=s_len)
        y_b = _lru_call(rm, conv_w[l], row(conv_b[l]), _block_diag(lru_wa[l]), row(lru_ba[l]),
                        _block_diag(lru_wx[l]), row(lru_bx[l]), row(lru_lambda[l]), batch=batch, s_len=s_len)
        qm, km, vm = _mla_prep_call(cm, cos_t, sin_t, row(mla_g_q[l]), row(mla_g_kv[l]),
                                    *_mla_weights(mla_w_uq[l], mla_w_ukv[l]), s_len=s_len)
        y_c = _mla_attn_call(qm, km, vm, batch=batch, s_len=s_len)
        sinks_pad = jnp.zeros((1, LANES), F32).at[0, :D_HEADS].set(swa_sinks[l])
        y_d = _swa_call(qkv_d, sinks_pad, batch=batch, s_len=s_len)
        x2 = _merge_call(x2, mod_l, g_l, (y_a, y_b, y_c, y_d), w_gate, w_branch[l].astype(BF16),
                         w_out[l].astype(BF16), s_len=s_len)
        x2 = _ffn_call(x2, mod_l, g_l, ffn2_w_in, ffn2_w_out, l, sub=2, resid_w=0.5, s_len=s_len)
    return x2.reshape(batch, s_len, d)
```

```python
import functools

import numpy as np
import jax
import jax.numpy as jnp
from jax import lax
from jax.experimental import pallas as pl
from jax.experimental.pallas import tpu as pltpu

F32 = jnp.float32
BF16 = jnp.bfloat16
I32 = jnp.int32

D_MODEL = 1024
HEAD_DIM = 64
BLOCK = 128
EPS = 1e-6
NEG = -1e30
A_HEADS = 8
A_KV_HEADS = 2
IDX_HEADS = 8
IDX_DIM = 64
TOPK_MAX = 256
LRU_WIDTH = 512
LRU_BLOCKS = 8
CONV_WIDTH = 4
LRU_C = 8.0
C_HEADS = 8
Q_LORA = 256
KV_LORA = 128
QK_NOPE = 64
QK_ROPE = 32
V_DIM = 64
ROPE_THETA = 10000.0
D_HEADS = 8
D_KV_HEADS = 2
WINDOW = 128
N_BRANCH = 4
BRANCH_WIDTH = 512
N_ALIBI = A_HEADS + D_HEADS
D_FF = 2816
IN_SPLITS = (A_HEADS * HEAD_DIM, A_KV_HEADS * HEAD_DIM, A_KV_HEADS * HEAD_DIM,
             IDX_HEADS * IDX_DIM, IDX_DIM, IDX_HEADS,
             LRU_WIDTH, LRU_WIDTH,
             Q_LORA, KV_LORA + QK_ROPE,
             D_HEADS * HEAD_DIM, D_KV_HEADS * HEAD_DIM, D_KV_HEADS * HEAD_DIM,
             N_BRANCH * D_MODEL)
IN_OFFSETS = tuple(int(v) for v in np.concatenate([[0], np.cumsum(IN_SPLITS)]))

LANES = 128
INT_MIN = -2 ** 31
INT_MAX = 2 ** 31 - 1
VMEM_LIMIT = 56 * 1024 * 1024
DSA_VMEM_LIMIT = 62 * 1024 * 1024
LOG2E = 1.4426950408889634
POS_SPLIT = 64
DSA_GROUP = 16
MLA_GROUP = 8
CNT_GROUP = 16
CNT_ROWS = 128

QKV_W = 512 + 256 + 128
IDX_W = 512 + 128
MLA_W = 640

FFN_TM = 1024
FFN_TF = 256
PROJ_TM = 512
DSA_TQ = 128
DSA_KC = 256
MLA_TQ = 512
MLA_KC = 512
SWA_BLOCKS = 4
LRU_L = 256
MERGE_TM = 512


def _alibi(i):
    return float(2.0 ** (-8.0 * i / N_ALIBI))


SLOPES_D = tuple(_alibi(i) for i in range(1, D_HEADS + 1))
SLOPES_A = tuple(_alibi(i) for i in range(D_HEADS + 1, N_ALIBI + 1))


def _bf16_parts(x):
    parts = []
    rem = np.float32(x)
    for _ in range(3):
        p = np.float32(np.asarray(rem, np.float32).astype(jnp.bfloat16).astype(np.float32))
        parts.append(float(p))
        rem = np.float32(rem - p)
    return parts


def _cparams(sem):
    return pltpu.CompilerParams(dimension_semantics=sem, vmem_limit_bytes=VMEM_LIMIT)


def _rms(x, g):
    return x * lax.rsqrt(jnp.mean(x * x, axis=-1, keepdims=True) + EPS) * g


def _nt_dot(a, b):
    return lax.dot_general(a, b, (((1,), (1,)), ((), ())), preferred_element_type=F32)


def _expm1(y):
    u = jnp.exp(y)
    safe = (u != 1.0) & (y > -1.0)
    ratio = y / jnp.log(jnp.where(safe, u, 2.0))
    return jnp.where(u == 1.0, y, jnp.where(safe, (u - 1.0) * ratio, u - 1.0))


def _loop_groups(n, body, init=0, *, width):
    carry, start = init, 0
    while width >= 1:
        def group(i, carry, start=start, width=width):
            for u in range(width):
                carry = body(start + width * i + u, carry)
            return carry

        trips = lax.shift_right_logical(n - start, _log2(width))
        carry = lax.fori_loop(0, trips, group, carry)
        start = start + trips * width
        width //= 2
    return carry


def _log2(n):
    l = int(n).bit_length() - 1
    assert (1 << l) == n
    return l


def _ada_kernel(c_ref, w_ref, b_ref, o_ref):
    c = c_ref[...]
    sc = (c * jax.nn.sigmoid(c)).astype(BF16)
    o_ref[0] = jnp.dot(sc, w_ref[0].astype(BF16), preferred_element_type=F32) + b_ref[0]


def _ada_call(c_pad, w_ada, b_ada):
    depth, d, n = w_ada.shape
    tn = 1152
    return pl.pallas_call(
        _ada_kernel,
        grid=(depth, n // tn),
        in_specs=[pl.BlockSpec((c_pad.shape[0], d), lambda l, j: (0, 0)),
                  pl.BlockSpec((1, d, tn), lambda l, j: (l, 0, j)),
                  pl.BlockSpec((1, 1, tn), lambda l, j: (l, 0, j))],
        out_specs=pl.BlockSpec((1, c_pad.shape[0], tn), lambda l, j: (l, 0, j)),
        out_shape=jax.ShapeDtypeStruct((depth, c_pad.shape[0], n), F32),
        compiler_params=_cparams(("arbitrary", "arbitrary")),
        name="adaln",
    )(c_pad, w_ada, b_ada.reshape(depth, 1, n))


def _ffn_kernel(x_ref, mod_ref, g_ref, wg_ref, wu_ref, wo_ref, o_ref, h_ref, acc_ref, *, sub, resid_w):
    k = pl.program_id(1)
    m = mod_ref[0]

    @pl.when(k == 0)
    def _():
        x = x_ref[...]
        xn = _rms(x, g_ref[2 * sub:2 * sub + 1])
        h_ref[...] = (xn * (1.0 + m[3 * sub + 1:3 * sub + 2]) + m[3 * sub:3 * sub + 1]).astype(BF16)
        acc_ref[...] = jnp.zeros_like(acc_ref)

    h = h_ref[...]
    gate = jnp.dot(h, wg_ref[0].astype(BF16), preferred_element_type=F32)
    up = jnp.dot(h, wu_ref[0].astype(BF16), preferred_element_type=F32)
    act = (gate * jax.nn.sigmoid(gate) * up).astype(BF16)
    acc_ref[...] += jnp.dot(act, wo_ref[0].astype(BF16), preferred_element_type=F32)

    @pl.when(k == pl.num_programs(1) - 1)
    def _():
        yn = _rms(acc_ref[...], g_ref[2 * sub + 1:2 * sub + 2])
        o_ref[...] = x_ref[...] + resid_w * m[3 * sub + 2:3 * sub + 3] * yn


def _ffn_call(x2, mod_l, g_l, w_in, w_out, layer, *, sub, resid_w, s_len):
    t, d = x2.shape
    f = w_out.shape[1]
    tm = min(FFN_TM, s_len)
    tf = FFN_TF
    nf = f // tf
    tpb = s_len // tm
    return pl.pallas_call(
        functools.partial(_ffn_kernel, sub=sub, resid_w=resid_w),
        grid=(t // tm, nf),
        in_specs=[pl.BlockSpec((tm, d), lambda i, k: (i, 0)),
                  pl.BlockSpec((1, 9, d), lambda i, k: (i // tpb, 0, 0)),
                  pl.BlockSpec((6, d), lambda i, k: (0, 0)),
                  pl.BlockSpec((1, d, tf), lambda i, k: (layer, 0, k)),
                  pl.BlockSpec((1, d, tf), lambda i, k: (layer, 0, k + nf)),
                  pl.BlockSpec((1, tf, d), lambda i, k: (layer, k, 0))],
        out_specs=pl.BlockSpec((tm, d), lambda i, k: (i, 0)),
        out_shape=jax.ShapeDtypeStruct((t, d), F32),
        scratch_shapes=[pltpu.VMEM((tm, d), BF16), pltpu.VMEM((tm, d), F32)],
        compiler_params=_cparams(("arbitrary", "arbitrary")),
        name="ffn",
    )(x2, mod_l, g_l, w_in, w_in, w_out)


def _proj_kernel(x_ref, mod_ref, g_ref, w_ref, kpos_ref, oa_ref, oi_ref, ow_ref, or_ref, oc_ref, od_ref):
    m = mod_ref[0]
    xn = _rms(x_ref[...], g_ref[2:3])
    h = (xn * (1.0 + m[4:5]) + m[3:4]).astype(BF16)
    off = 0
    for ref in (oa_ref, oi_ref, ow_ref, or_ref, oc_ref, od_ref):
        w = ref.shape[1]
        z = jnp.dot(h, w_ref[:, off:off + w], preferred_element_type=F32)
        if ref is oa_ref or ref is od_ref:
            qscale = (HEAD_DIM ** -0.5) * (LOG2E if ref is oa_ref else 1.0)
            z = jnp.concatenate([z[:, :512] * qscale, z[:, 512:768] + kpos_ref[...], z[:, 768:]], axis=1)
        ref[...] = z.astype(ref.dtype)
        off += w


def _proj_call(x2, mod_l, g_l, w_all, kpos, *, s_len):
    t, d = x2.shape
    tm = min(PROJ_TM, s_len)
    tpb = s_len // tm
    widths = (QKV_W, IDX_W, LANES, 2 * LRU_WIDTH, MLA_W, QKV_W)
    dtypes = (BF16, BF16, F32, F32, F32, BF16)
    assert sum(widths) == w_all.shape[1]
    return pl.pallas_call(
        _proj_kernel,
        grid=(t // tm,),
        in_specs=[pl.BlockSpec((tm, d), lambda i: (i, 0)),
                  pl.BlockSpec((1, 9, d), lambda i: (i // tpb, 0, 0)),
                  pl.BlockSpec((6, d), lambda i: (0, 0)),
                  pl.BlockSpec(w_all.shape, lambda i: (0, 0)),
                  pl.BlockSpec((tm, 256), lambda i: (i % tpb, 0))],
        out_specs=[pl.BlockSpec((tm, w), lambda i: (i, 0)) for w in widths],
        out_shape=[jax.ShapeDtypeStruct((t, w), dt) for w, dt in zip(widths, dtypes)],
        compiler_params=_cparams(("arbitrary",)),
        name="mixer_proj",
    )(x2, mod_l, g_l, w_all, kpos)


def _head_rows(q, n_heads, tq):
    lane = lax.broadcasted_iota(I32, (tq, LANES), 1)
    rows = []
    for h in range(n_heads):
        tile = q[:, LANES * (h // 2):LANES * (h // 2 + 1)].astype(F32)
        keep = (lane >= HEAD_DIM) if (h % 2) else (lane < HEAD_DIM)
        rows.append(jnp.where(keep, tile, 0.0).astype(BF16))
    return jnp.concatenate(rows, axis=0)


def _alibi_lanes(slopes, tq):
    lane = lax.broadcasted_iota(I32, (tq, LANES), 1)
    out = []
    for slope in slopes:
        tile = jnp.zeros((tq, LANES), F32)
        for i, part in enumerate(_bf16_parts(slope)):
            tile = jnp.where(lane == HEAD_DIM + 2 * i, POS_SPLIT * part, tile)
            tile = jnp.where(lane == HEAD_DIM + 2 * i + 1, part, tile)
        out.append(tile)
    return out


def _alibi_q_tiles(q, alibi_lanes, tq):
    lane = lax.broadcasted_iota(I32, (tq, LANES), 1)
    rows = []
    for h, al in enumerate(alibi_lanes):
        tile = q[:, LANES * (h // 2):LANES * (h // 2 + 1)].astype(F32)
        if h % 2:
            tile = pltpu.roll(tile, HEAD_DIM, axis=1)
        rows.append(jnp.where(lane < HEAD_DIM, tile, al).astype(BF16))
    return jnp.concatenate(rows, axis=0)


def _gqa_out(accs, inv_ls, tq):
    lane = lax.broadcasted_iota(I32, (tq, LANES), 1)
    tiles = []
    for j in range(4):
        g = j // 2
        halves = []
        for h in (2 * j, 2 * j + 1):
            o = accs[g][h % 4] * inv_ls[g][h % 4]
            src_hi = (g == 1)
            dst_hi = (h % 2 == 1)
            if src_hi != dst_hi:
                o = pltpu.roll(o, HEAD_DIM, axis=1)
            halves.append(o)
        tiles.append(jnp.where(lane < HEAD_DIM, halves[0], halves[1]))
    return jnp.concatenate(tiles, axis=1)


def _dsa_kernel(q_ref, k_ref, v_ref, iq_ref, ik_ref, iw_ref, o_ref,
                key_ref, s_ref, tau_ref, iqs_ref, qs_ref, m_ref, l_ref, acc_ref, *, tq, kc, topk, pos_bits):
    qb = pl.program_id(1)
    q0 = qb * tq
    nk = lax.shift_right_logical(q0 + tq + kc - 1, _log2(kc))
    n_lt = kc // LANES
    n_rt = kc // CNT_ROWS
    qpos_l = q0 + lax.broadcasted_iota(I32, (kc, tq), 1)
    krow = lax.broadcasted_iota(I32, (kc, tq), 0)
    krow_t = lax.broadcasted_iota(I32, (CNT_ROWS, tq), 0)

    iqs_ref[...] = _head_rows(iq_ref[...], IDX_HEADS, tq)
    iw_t = iw_ref[...].T

    def score_body(j, carry):
        ks = pl.multiple_of(j * kc, kc)
        d = _nt_dot(ik_ref[pl.ds(ks, kc), :], iqs_ref[...])
        acc = jnp.zeros((kc, tq), F32)
        for h in range(IDX_HEADS):
            acc = acc + iw_t[h:h + 1, :] * jnp.maximum(d[:, h * tq:(h + 1) * tq], 0.0)
        bits = pltpu.bitcast(acc, I32)
        key = jnp.where(bits < 0, bits ^ INT_MAX, bits)
        key_ref[j] = jnp.where(ks + krow <= qpos_l, key, INT_MIN)
        return carry

    _loop_groups(nk, score_body, width=DSA_GROUP)

    def count(pred):
        def body(j, cnt):
            for r in range(n_rt):
                kp = (j * kc + r * CNT_ROWS) + krow_t
                cnt = cnt + jnp.where(pred(key_ref[j, r * CNT_ROWS:(r + 1) * CNT_ROWS, :], kp), 1.0, 0.0)
            return cnt
        cnt = _loop_groups(nk, body, jnp.zeros((CNT_ROWS, tq), F32), width=CNT_GROUP)
        return jnp.sum(cnt, axis=0, keepdims=True)

    kf = float(topk)

    def bit_step(i, state):
        tau, cnt = state
        cand = tau + lax.shift_left(jnp.int32(1), 31 - i)
        cb = jnp.broadcast_to(cand, (CNT_ROWS, tq))
        c = count(lambda kv, kp: kv >= cb)
        ok = c >= kf
        return jnp.where(ok, cand, tau), jnp.where(ok, c, cnt)

    tau, n_ge = lax.fori_loop(0, 32, bit_step, (jnp.full((1, tq), INT_MIN, I32),
                                                 jnp.broadcast_to((nk * kc).astype(F32), (1, tq))))

    tie_f = jnp.where((n_ge > kf) & (tau > INT_MIN), 1.0, 0.0)
    any_tie = jnp.max(tie_f) > 0.0
    tau_c = jnp.maximum(tau, INT_MIN + 1)
    tau_ref[...] = jnp.broadcast_to(tau_c, tau_ref.shape)

    @pl.when(any_tie)
    def _():
        tb = jnp.broadcast_to(tau_c, (CNT_ROWS, tq))
        need = kf - count(lambda kv, kp: kv > tb)

        def pos_body(i, p):
            cand = p | lax.shift_left(jnp.int32(1), pos_bits - 1 - i)
            cb = jnp.broadcast_to(cand, (CNT_ROWS, tq))
            below = count(lambda kv, kp: (kv == tb) & (kp < cb))
            return jnp.where(below < need, cand, p)

        p = lax.fori_loop(0, pos_bits, pos_body, jnp.zeros((1, tq), I32))
        p = jnp.where(tie_f > 0.0, p, INT_MAX)

        def rewrite_body(j, carry):
            kj = key_ref[j]
            kp = j * kc + krow
            sel = (kj > tau_c) | ((kj == tau_c) & (kp <= p))
            key_ref[j] = jnp.where(sel, 1, -1)
            return carry

        lax.fori_loop(0, nk, rewrite_body, 0)
        tau_ref[...] = jnp.zeros(tau_ref.shape, I32)

    q = q_ref[...]
    for g in range(A_KV_HEADS):
        qs_ref[g] = _alibi_q_tiles(q[:, 4 * HEAD_DIM * g:4 * HEAD_DIM * (g + 1)],
                                   _alibi_lanes([s * LOG2E for s in SLOPES_A[4 * g:4 * g + 4]], tq), tq)
    m_ref[...] = jnp.full(m_ref.shape, NEG, F32)
    l_ref[...] = jnp.zeros(l_ref.shape, F32)
    acc_ref[...] = jnp.zeros(acc_ref.shape, F32)

    def max_body(j, carry):
        ks = pl.multiple_of(j * kc, kc)
        sel = jnp.where(key_ref[j] >= tau_ref[0:1, :], 1.0, 0.0).T > 0.5
        for g in range(A_KV_HEADS):
            s = _nt_dot(qs_ref[g], k_ref[pl.ds(ks, kc), g * LANES:(g + 1) * LANES])
            for h in range(4):
                sh = jnp.where(sel, s[h * tq:(h + 1) * tq], NEG)
                s_ref[g, h, j] = sh
                mp = m_ref[g, h]
                for c in range(n_lt):
                    mp = jnp.maximum(mp, sh[:, c * LANES:(c + 1) * LANES])
                m_ref[g, h] = mp
        return carry

    _loop_groups(nk, max_body, width=DSA_GROUP)
    for g in range(A_KV_HEADS):
        for h in range(4):
            m_ref[g, h] = jnp.broadcast_to(jnp.max(m_ref[g, h], axis=1, keepdims=True), (tq, LANES))

    def pv_body(j, carry):
        ks = pl.multiple_of(j * kc, kc)
        vch = v_ref[pl.ds(ks, kc), :]
        for g in range(A_KV_HEADS):
            ps = []
            for h in range(4):
                mb = m_ref[g, h]
                lp = l_ref[g, h]
                tiles = []
                for c in range(n_lt):
                    p = jnp.exp2(s_ref[g, h, j, :, c * LANES:(c + 1) * LANES] - mb)
                    lp = lp + p
                    tiles.append(p.astype(BF16))
                l_ref[g, h] = lp
                ps.append(jnp.concatenate(tiles, axis=1))
            pv = jnp.dot(jnp.concatenate(ps, axis=0), vch, preferred_element_type=F32)
            for h in range(4):
                acc_ref[g, h] += pv[h * tq:(h + 1) * tq]
        return carry

    _loop_groups(nk, pv_body, width=DSA_GROUP)
    accs = [[acc_ref[g, h] for h in range(4)] for g in range(A_KV_HEADS)]
    inv_ls = [[1.0 / jnp.sum(l_ref[g, h], axis=1, keepdims=True) for h in range(4)] for g in range(A_KV_HEADS)]
    o_ref[...] = _gqa_out(accs, inv_ls, tq).astype(o_ref.dtype)


def _dsa_call(qkv, idx, iw, *, batch, s_len):
    t = qkv.shape[0]
    tq = min(DSA_TQ, s_len)
    kc = min(DSA_KC, s_len)
    topk = min(TOPK_MAX, s_len // 4)
    assert kc >= topk and kc % tq == 0
    nqb = s_len // tq
    once = pl.Buffered(1)
    return pl.pallas_call(
        functools.partial(_dsa_kernel, tq=tq, kc=kc, topk=topk, pos_bits=_log2(s_len)),
        grid=(batch, nqb),
        in_specs=[pl.BlockSpec((tq, 512), lambda b, i: (b * nqb + i, 0)),
                  pl.BlockSpec((s_len, 256), lambda b, i: (b, 2), pipeline_mode=once),
                  pl.BlockSpec((s_len, 128), lambda b, i: (b, 6), pipeline_mode=once),
                  pl.BlockSpec((tq, 512), lambda b, i: (b * nqb + i, 0)),
                  pl.BlockSpec((s_len, 128), lambda b, i: (b, 4), pipeline_mode=once),
                  pl.BlockSpec((tq, LANES), lambda b, i: (b * nqb + i, 0))],
        out_specs=pl.BlockSpec((tq, 512), lambda b, i: (b * nqb + i, 0)),
        out_shape=jax.ShapeDtypeStruct((t, 512), BF16),
        scratch_shapes=[pltpu.VMEM((s_len // kc, kc, tq), I32),
                        pltpu.VMEM((A_KV_HEADS, 4, s_len // kc, tq, kc), F32),
                        pltpu.VMEM((8, tq), I32),
                        pltpu.VMEM((IDX_HEADS * tq, LANES), BF16),
                        pltpu.VMEM((A_KV_HEADS, 4 * tq, LANES), BF16),
                        pltpu.VMEM((A_KV_HEADS, 4, tq, LANES), F32),
                        pltpu.VMEM((A_KV_HEADS, 4, tq, LANES), F32),
                        pltpu.VMEM((A_KV_HEADS, 4, tq, LANES), F32)],
        compiler_params=pltpu.CompilerParams(dimension_semantics=("arbitrary", "arbitrary"),
                                             vmem_limit_bytes=DSA_VMEM_LIMIT),
        name="dsa",
    )(qkv, qkv, qkv, idx, idx, iw)


def _swa_kernel(q_ref, kp_ref, kc_ref, vp_ref, vc_ref, sink_ref, o_ref, *, tq, nblk):
    i = pl.program_id(1)
    krow = lax.broadcasted_iota(I32, (2 * tq, tq), 0)
    qcol = lax.broadcasted_iota(I32, (2 * tq, tq), 1)
    dist = qcol + tq - krow
    in_band = (dist >= 0) & (dist < WINDOW)
    sinks = sink_ref[...]
    sink_vec = [jnp.broadcast_to(sinks[:, h:h + 1], (1, tq)) for h in range(D_HEADS)]
    alibi = _alibi_lanes(SLOPES_D, tq)
    for n in range(nblk):
        blk = i * nblk + n
        rows = slice(n * tq, (n + 1) * tq)
        q = q_ref[rows, :]
        k_prev = kp_ref[...] if n == 0 else kc_ref[(n - 1) * tq:n * tq, :]
        v_prev = vp_ref[...] if n == 0 else vc_ref[(n - 1) * tq:n * tq, :]
        kk = jnp.concatenate([k_prev, kc_ref[rows, :]], axis=0)
        vv_t = jnp.concatenate([v_prev, vc_ref[rows, :]], axis=0).T
        valid = in_band & ((blk * tq - tq + krow) >= 0)
        qpos = (blk * tq + lax.broadcasted_iota(I32, (1, tq), 1)).astype(F32)
        tiles = []
        for g in range(D_KV_HEADS):
            qs = _alibi_q_tiles(q[:, 4 * HEAD_DIM * g:4 * HEAD_DIM * (g + 1)], alibi[4 * g:4 * g + 4], tq)
            s_t = _nt_dot(kk[:, g * LANES:(g + 1) * LANES], qs)
            ps, ils = [], []
            for h in range(4):
                sh = jnp.where(valid, s_t[:, h * tq:(h + 1) * tq], NEG)
                sink = sink_vec[4 * g + h] + SLOPES_D[4 * g + h] * qpos
                m = jnp.maximum(jnp.max(sh, axis=0, keepdims=True), sink)
                p = jnp.exp(sh - m)
                ils.append(1.0 / (jnp.sum(p, axis=0, keepdims=True) + jnp.exp(sink - m)))
                ps.append(p.astype(BF16))
            o_t = jnp.dot(vv_t, jnp.concatenate(ps, axis=1), preferred_element_type=F32)
            o_t = o_t[g * HEAD_DIM:(g + 1) * HEAD_DIM, :] * jnp.concatenate(ils, axis=1)
            for j in range(2):
                pair = jnp.concatenate([o_t[:, (2 * j) * tq:(2 * j + 1) * tq],
                                        o_t[:, (2 * j + 1) * tq:(2 * j + 2) * tq]], axis=0)
                tiles.append(pair.T)
        o_ref[rows, :] = jnp.concatenate(tiles, axis=1).astype(o_ref.dtype)


def _swa_call(qkv, sinks_pad, *, batch, s_len):
    t = qkv.shape[0]
    tq = BLOCK
    nblk = min(SWA_BLOCKS, s_len // tq)
    ts = nblk * tq
    nst = s_len // ts
    cur = lambda b, i: b * nst + i
    prev = lambda b, i: jnp.maximum((b * nst + i) * nblk - 1, 0)
    return pl.pallas_call(
        functools.partial(_swa_kernel, tq=tq, nblk=nblk),
        grid=(batch, nst),
        in_specs=[pl.BlockSpec((ts, 512), lambda b, i: (cur(b, i), 0)),
                  pl.BlockSpec((tq, 256), lambda b, i: (prev(b, i), 2)),
                  pl.BlockSpec((ts, 256), lambda b, i: (cur(b, i), 2)),
                  pl.BlockSpec((tq, 128), lambda b, i: (prev(b, i), 6)),
                  pl.BlockSpec((ts, 128), lambda b, i: (cur(b, i), 6)),
                  pl.BlockSpec((1, LANES), lambda b, i: (0, 0))],
        out_specs=pl.BlockSpec((ts, 512), lambda b, i: (cur(b, i), 0)),
        out_shape=jax.ShapeDtypeStruct((t, 512), BF16),
        compiler_params=_cparams(("arbitrary", "arbitrary")),
        name="swa",
    )(qkv, qkv, qkv, qkv, qkv, sinks_pad)


def _mla_prep_kernel(c_ref, cos_ref, sin_ref, gq_ref, gkv_ref, wq_ref, wqr_ref, wk_ref, wv_ref,
                     q_ref, k_ref, v_ref):
    c = c_ref[...]
    cos = cos_ref[...]
    sin = sin_ref[...]
    cos8 = jnp.concatenate([cos] * C_HEADS, axis=1)
    sin8 = jnp.concatenate([sin] * C_HEADS, axis=1)
    cqn = _rms(c[:, :Q_LORA], gq_ref[...]).astype(BF16)
    q = (jnp.dot(cqn, wq_ref[...], preferred_element_type=F32) * cos8
         + jnp.dot(cqn, wqr_ref[...], preferred_element_type=F32) * sin8)
    q_ref[...] = q.astype(BF16)
    ckvn = _rms(c[:, Q_LORA:Q_LORA + KV_LORA], gkv_ref[...]).astype(BF16)
    kr = c[:, 384:512] * cos + c[:, 512:640] * sin
    k = jnp.dot(ckvn, wk_ref[...], preferred_element_type=F32) + jnp.concatenate([kr] * C_HEADS, axis=1)
    k_ref[...] = k.astype(BF16)
    v_ref[...] = jnp.dot(ckvn, wv_ref[...], preferred_element_type=F32).astype(BF16)


def _mla_prep_call(cm, cos_t, sin_t, gq, gkv, wq, wqr, wk, wv, *, s_len):
    t = cm.shape[0]
    tm = min(PROJ_TM, s_len)
    tpb = s_len // tm
    full = lambda a: pl.BlockSpec(a.shape, lambda i: (0, 0))
    return pl.pallas_call(
        _mla_prep_kernel,
        grid=(t // tm,),
        in_specs=[pl.BlockSpec((tm, MLA_W), lambda i: (i, 0)),
                  pl.BlockSpec((tm, LANES), lambda i: (i % tpb, 0)),
                  pl.BlockSpec((tm, LANES), lambda i: (i % tpb, 0)),
                  full(gq), full(gkv), full(wq), full(wqr), full(wk), full(wv)],
        out_specs=[pl.BlockSpec((tm, 1024), lambda i: (i, 0)),
                   pl.BlockSpec((tm, 1024), lambda i: (i, 0)),
                   pl.BlockSpec((tm, 512), lambda i: (i, 0))],
        out_shape=[jax.ShapeDtypeStruct((t, 1024), BF16),
                   jax.ShapeDtypeStruct((t, 1024), BF16),
                   jax.ShapeDtypeStruct((t, 512), BF16)],
        compiler_params=_cparams(("arbitrary",)),
        name="mla_prep",
    )(cm, cos_t, sin_t, gq, gkv, wq, wqr, wk, wv)


def _mla_attn_kernel(q_ref, k_ref, v_ref, o_ref, s_ref, m_ref, l_ref, acc_ref, *, tq, kc):
    qb = pl.program_id(2)
    q0 = qb * tq
    n_full = lax.shift_right_logical(q0, _log2(kc))
    rowpos = q0 + lax.broadcasted_iota(I32, (tq, kc), 0)
    lane = lax.broadcasted_iota(I32, (tq, kc), 1)
    c = ((QK_NOPE + QK_ROPE) ** -0.5) * LOG2E
    n_lt = kc // LANES
    m_ref[...] = jnp.full(m_ref.shape, NEG, F32)
    l_ref[...] = jnp.zeros(l_ref.shape, F32)
    acc_ref[...] = jnp.zeros(acc_ref.shape, F32)

    def max_step(j, masked):
        ks = pl.multiple_of(j * kc, kc)
        for hh in range(2):
            s = _nt_dot(q_ref[:, hh * LANES:(hh + 1) * LANES],
                        k_ref[pl.ds(ks, kc), hh * LANES:(hh + 1) * LANES]) * c
            if masked:
                s = jnp.where(ks + lane <= rowpos, s, NEG)
            s_ref[hh, j] = s
            mp = m_ref[hh]
            for t in range(n_lt):
                mp = jnp.maximum(mp, s[:, t * LANES:(t + 1) * LANES])
            m_ref[hh] = mp

    def max_body(j, carry):
        max_step(j, False)
        return carry

    _loop_groups(n_full, max_body, width=MLA_GROUP)
    max_step(n_full, True)
    for hh in range(2):
        m_ref[hh] = jnp.broadcast_to(jnp.max(m_ref[hh], axis=1, keepdims=True), (tq, LANES))

    def pv_body(j, carry):
        ks = pl.multiple_of(j * kc, kc)
        vch = v_ref[pl.ds(ks, kc), :]
        for hh in range(2):
            mb = m_ref[hh]
            lp = l_ref[hh]
            tiles = []
            for t in range(n_lt):
                p = jnp.exp2(s_ref[hh, j, :, t * LANES:(t + 1) * LANES] - mb)
                lp = lp + p
                tiles.append(p.astype(BF16))
            l_ref[hh] = lp
            acc_ref[hh] += jnp.dot(jnp.concatenate(tiles, axis=1), vch, preferred_element_type=F32)
        return carry

    _loop_groups(n_full + 1, pv_body, width=MLA_GROUP)
    outs = [acc_ref[hh] * (1.0 / jnp.sum(l_ref[hh], axis=1, keepdims=True)) for hh in range(2)]
    lane_o = lax.broadcasted_iota(I32, (tq, LANES), 1)
    o_ref[...] = jnp.where(lane_o < V_DIM, outs[0], outs[1]).astype(o_ref.dtype)


def _mla_attn_call(qm, km, vm, *, batch, s_len):
    t = qm.shape[0]
    tq = min(MLA_TQ, s_len)
    kc = min(MLA_KC, s_len)
    assert kc % tq == 0
    nqb = s_len // tq
    once = pl.Buffered(1)
    return pl.pallas_call(
        functools.partial(_mla_attn_kernel, tq=tq, kc=kc),
        grid=(batch, C_HEADS // 2, nqb),
        in_specs=[pl.BlockSpec((tq, 256), lambda b, h, i: (b * nqb + i, h)),
                  pl.BlockSpec((s_len, 256), lambda b, h, i: (b, h), pipeline_mode=once),
                  pl.BlockSpec((s_len, 128), lambda b, h, i: (b, h), pipeline_mode=once)],
        out_specs=pl.BlockSpec((tq, 128), lambda b, h, i: (b * nqb + i, h)),
        out_shape=jax.ShapeDtypeStruct((t, 512), BF16),
        scratch_shapes=[pltpu.VMEM((2, s_len // kc, tq, kc), F32)] + [pltpu.VMEM((2, tq, LANES), F32)] * 3,
        compiler_params=_cparams(("arbitrary", "arbitrary", "arbitrary")),
        name="mla_attn",
    )(qm, km, vm)


def _lru_kernel(xr_ref, xg_ref, prev_ref, cw_ref, cb_ref, wa_ref, ba_ref, wx_ref, bx_ref, lam_ref,
                o_ref, xe_ref, h_ref, *, ln):
    i = pl.program_id(1)
    xe_ref[0:8, :] = jnp.where(i > 0, prev_ref[...], 0.0)
    xe_ref[8:8 + ln, :] = xr_ref[...]
    cw = cw_ref[...]
    xc = cb_ref[...] + cw[0:1] * xe_ref[pl.ds(5, ln), :]
    for j in range(1, CONV_WIDTH):
        xc = xc + cw[j:j + 1] * xe_ref[pl.ds(5 + j, ln), :]
    xcb = xc.astype(BF16)
    r = jax.nn.sigmoid(jnp.dot(xcb, wa_ref[...], preferred_element_type=F32) + ba_ref[...])
    gi = jax.nn.sigmoid(jnp.dot(xcb, wx_ref[...], preferred_element_type=F32) + bx_ref[...])
    z = -lam_ref[...]
    softplus = jnp.maximum(z, 0.0) + jnp.log1p(jnp.exp(-jnp.abs(z)))
    log_a = -LRU_C * r * softplus
    a = jnp.exp(log_a)
    b = jnp.sqrt(-_expm1(2.0 * log_a)) * (gi * xc)
    row = lax.broadcasted_iota(I32, (ln, LRU_WIDTH), 0)
    d = 1
    while d < ln:
        keep = row >= d
        b = jnp.where(keep, a * pltpu.roll(b, d, axis=0) + b, b)
        a = jnp.where(keep, a * pltpu.roll(a, d, axis=0), a)
        d *= 2
    h_prev = jnp.where(i > 0, h_ref[0:1, :], 0.0)
    h = a * h_prev + b
    h_ref[0:1, :] = h[ln - 1:ln, :]
    xg = xg_ref[...]
    gelu = 0.5 * xg * (1.0 + jnp.tanh(0.7978845608028654 * (xg + 0.044715 * (xg * xg * xg))))
    o_ref[...] = (h * gelu).astype(o_ref.dtype)


def _lru_call(rm, cw, cb, wa, ba, wx, bx, lam, *, batch, s_len):
    t = rm.shape[0]
    ln = min(LRU_L, s_len)
    nt = s_len // ln
    w = LRU_WIDTH
    vec = lambda a: pl.BlockSpec(a.shape, lambda b, i: (0, 0))
    return pl.pallas_call(
        functools.partial(_lru_kernel, ln=ln),
        grid=(batch, nt),
        in_specs=[pl.BlockSpec((ln, w), lambda b, i: (b * nt + i, 0)),
                  pl.BlockSpec((ln, w), lambda b, i: (b * nt + i, 1)),
                  pl.BlockSpec((8, w), lambda b, i: (jnp.maximum((b * nt + i) * (ln // 8) - 1, 0), 0)),
                  vec(cw), vec(cb), vec(wa), vec(ba), vec(wx), vec(bx), vec(lam)],
        out_specs=pl.BlockSpec((ln, w), lambda b, i: (b * nt + i, 0)),
        out_shape=jax.ShapeDtypeStruct((t, w), BF16),
        scratch_shapes=[pltpu.VMEM((ln + 8, w), F32), pltpu.VMEM((8, w), F32)],
        compiler_params=_cparams(("arbitrary", "arbitrary")),
        name="rglru",
    )(rm, rm, rm, cw, cb, wa, ba, wx, bx, lam)


def _merge_kernel(x_ref, mod_ref, g_ref, ya_ref, yb_ref, yc_ref, yd_ref, wg_ref, wb_ref, wo_ref, o_ref):
    m = mod_ref[0]
    x = x_ref[...]
    h = (_rms(x, g_ref[2:3]) * (1.0 + m[4:5]) + m[3:4]).astype(BF16)
    merged = None
    for n, y_ref in enumerate((ya_ref, yb_ref, yc_ref, yd_ref)):
        gate = jax.nn.sigmoid(jnp.dot(h, wg_ref[:, n * D_MODEL:(n + 1) * D_MODEL], preferred_element_type=F32))
        term = gate * jnp.dot(y_ref[...], wb_ref[n], preferred_element_type=F32)
        merged = term if merged is None else merged + term
    y = jnp.dot(merged.astype(BF16), wo_ref[...], preferred_element_type=F32)
    o_ref[...] = x + m[5:6] * _rms(y, g_ref[3:4])


def _merge_call(x2, mod_l, g_l, ys, w_gate, w_branch, w_out, *, s_len):
    t, d = x2.shape
    tm = min(MERGE_TM, s_len)
    tpb = s_len // tm
    return pl.pallas_call(
        _merge_kernel,
        grid=(t // tm,),
        in_specs=[pl.BlockSpec((tm, d), lambda i: (i, 0)),
                  pl.BlockSpec((1, 9, d), lambda i: (i // tpb, 0, 0)),
                  pl.BlockSpec((6, d), lambda i: (0, 0))]
                 + [pl.BlockSpec((tm, BRANCH_WIDTH), lambda i: (i, 0))] * N_BRANCH
                 + [pl.BlockSpec(w_gate.shape, lambda i: (0, 0)),
                    pl.BlockSpec(w_branch.shape, lambda i: (0, 0, 0)),
                    pl.BlockSpec(w_out.shape, lambda i: (0, 0))],
        out_specs=pl.BlockSpec((tm, d), lambda i: (i, 0)),
        out_shape=jax.ShapeDtypeStruct((t, d), F32),
        compiler_params=_cparams(("arbitrary",)),
        name="merge",
    )(x2, mod_l, g_l, *ys, w_gate, w_branch, w_out)


def _mixer_weights(w_in_l):
    o = IN_OFFSETS
    col = lambda n: w_in_l[:, o[n]:o[n + 1]]
    z = lambda n: jnp.zeros((D_MODEL, n), w_in_l.dtype)

    def pad_heads(k):
        return jnp.concatenate([k[:, :64], z(64), k[:, 64:], z(64)], axis=1)

    a_q, a_k, a_v, i_q, i_k, i_w, r_x, r_g, c_q, c_kv, d_q, d_k, d_v = (col(n) for n in range(13))
    k_rope = c_kv[:, KV_LORA:]
    half = QK_ROPE // 2
    k_rope_rot = jnp.concatenate([-k_rope[:, half:], k_rope[:, :half]], axis=1)
    groups = [a_q, pad_heads(a_k), a_v,
              i_q, i_k, i_k,
              i_w, z(LANES - IDX_HEADS),
              r_x, r_g,
              c_q, c_kv[:, :KV_LORA], z(64), k_rope, z(32), z(64), k_rope_rot, z(32),
              d_q, pad_heads(d_k), d_v]
    w_all = jnp.concatenate(groups, axis=1).astype(BF16)
    w_gate = col(13).astype(BF16)
    return w_all, w_gate


def _kpos_table(s_len):
    pos = np.arange(s_len)
    tile = np.zeros((s_len, LANES), np.float32)
    for i in range(3):
        tile[:, HEAD_DIM + 2 * i] = pos // POS_SPLIT
        tile[:, HEAD_DIM + 2 * i + 1] = pos % POS_SPLIT
    return jnp.asarray(np.concatenate([tile, tile], axis=1))


def _mla_weights(w_uq, w_ukv):
    dq = QK_NOPE + QK_ROPE
    half = QK_ROPE // 2
    zq = lambda n: jnp.zeros((Q_LORA, n), w_uq.dtype)
    wq, wqr, wk, wv = [], [], [], []
    for h in range(C_HEADS):
        nope = w_uq[:, h * dq:h * dq + QK_NOPE]
        r1 = w_uq[:, h * dq + QK_NOPE:h * dq + QK_NOPE + half]
        r2 = w_uq[:, h * dq + QK_NOPE + half:(h + 1) * dq]
        wq += [nope, r1, r2, zq(32)]
        wqr += [zq(64), -r2, r1, zq(32)]
        wk += [w_ukv[:, h * 128:h * 128 + QK_NOPE], jnp.zeros((KV_LORA, 64), w_ukv.dtype)]
        wv += [w_ukv[:, h * 128 + QK_NOPE:(h + 1) * 128]]
    cat = lambda xs: jnp.concatenate(xs, axis=1).astype(BF16)
    return cat(wq), cat(wqr), cat(wk), cat(wv)


def _rope_tables(s_len):
    half = QK_ROPE // 2
    inv = (np.float32(ROPE_THETA) ** (-np.arange(half, dtype=np.float32) / np.float32(half))).astype(np.float32)
    ang = (np.arange(s_len, dtype=np.float32)[:, None] * inv[None, :]).astype(np.float32)
    cos, sin = np.cos(ang.astype(np.float64)).astype(np.float32), np.sin(ang.astype(np.float64)).astype(np.float32)
    ones = lambda n: np.ones((s_len, n), np.float32)
    zeros = lambda n: np.zeros((s_len, n), np.float32)
    cos_t = np.concatenate([ones(64), cos, cos, ones(32)], axis=1)
    sin_t = np.concatenate([zeros(64), sin, sin, zeros(32)], axis=1)
    return jnp.asarray(cos_t), jnp.asarray(sin_t)


def _block_diag(w):
    n, bw, _ = w.shape
    eye = jnp.eye(n, dtype=w.dtype)
    return (w[:, :, None, :] * eye[:, None, :, None]).reshape(n * bw, n * bw).astype(BF16)


def kernel(x, c, w_ada, b_ada, norm_g, ffn1_w_in, ffn1_w_out, w_in, conv_w, conv_b, lru_wa, lru_ba, lru_wx,
           lru_bx, lru_lambda, mla_g_q, mla_g_kv, mla_w_uq, mla_w_ukv, swa_sinks, w_branch, w_out, ffn2_w_in,
           ffn2_w_out):
    batch, s_len, d = x.shape
    depth = w_ada.shape[0]
    t = batch * s_len
    assert s_len // POS_SPLIT <= 256
    x2 = x.reshape(t, d)
    c_pad = jnp.zeros((8, d), F32).at[:batch].set(c)
    mod_all = _ada_call(c_pad, w_ada, b_ada)
    cos_t, sin_t = _rope_tables(s_len)
    kpos = _kpos_table(s_len)
    row = lambda v: v.reshape(1, -1)
    for l in range(depth):
        mod_l = mod_all[l, :batch].reshape(batch, 9, d)
        g_l = norm_g[l]
        x2 = _ffn_call(x2, mod_l, g_l, ffn1_w_in, ffn1_w_out, l, sub=0, resid_w=0.5, s_len=s_len)
        w_all, w_gate = _mixer_weights(w_in[l])
        qkv_a, idx, iw, rm, cm, qkv_d = _proj_call(x2, mod_l, g_l, w_all, kpos, s_len=s_len)
        y_a = _dsa_call(qkv_a, idx, iw, batch=batch, s_len=s_len)
        y_b = _lru_call(rm, conv_w[l], row(conv_b[l]), _block_diag(lru_wa[l]), row(lru_ba[l]),
                        _block_diag(lru_wx[l]), row(lru_bx[l]), row(lru_lambda[l]), batch=batch, s_len=s_len)
        qm, km, vm = _mla_prep_call(cm, cos_t, sin_t, row(mla_g_q[l]), row(mla_g_kv[l]),
                                    *_mla_weights(mla_w_uq[l], mla_w_ukv[l]), s_len=s_len)
        y_c = _mla_attn_call(qm, km, vm, batch=batch, s_len=s_len)
        sinks_pad = jnp.zeros((1, LANES), F32).at[0, :D_HEADS].set(swa_sinks[l])
        y_d = _swa_call(qkv_d, sinks_pad, batch=batch, s_len=s_len)
        x2 = _merge_call(x2, mod_l, g_l, (y_a, y_b, y_c, y_d), w_gate, w_branch[l].astype(BF16),
                         w_out[l].astype(BF16), s_len=s_len)
        x2 = _ffn_call(x2, mod_l, g_l, ffn2_w_in, ffn2_w_out, l, sub=2, resid_w=0.5, s_len=s_len)
    return x2.reshape(batch, s_len, d)
```

```python
import functools

import numpy as np
import jax
import jax.numpy as jnp
from jax import lax
from jax.experimental import pallas as pl
from jax.experimental.pallas import tpu as pltpu

F32 = jnp.float32
BF16 = jnp.bfloat16
I32 = jnp.int32

D_MODEL = 1024
HEAD_DIM = 64
BLOCK = 128
EPS = 1e-6
NEG = -1e30
A_HEADS = 8
A_KV_HEADS = 2
IDX_HEADS = 8
IDX_DIM = 64
TOPK_MAX = 256
LRU_WIDTH = 512
LRU_BLOCKS = 8
CONV_WIDTH = 4
LRU_C = 8.0
C_HEADS = 8
Q_LORA = 256
KV_LORA = 128
QK_NOPE = 64
QK_ROPE = 32
V_DIM = 64
ROPE_THETA = 10000.0
D_HEADS = 8
D_KV_HEADS = 2
WINDOW = 128
N_BRANCH = 4
BRANCH_WIDTH = 512
N_ALIBI = A_HEADS + D_HEADS
D_FF = 2816
IN_SPLITS = (A_HEADS * HEAD_DIM, A_KV_HEADS * HEAD_DIM, A_KV_HEADS * HEAD_DIM,
             IDX_HEADS * IDX_DIM, IDX_DIM, IDX_HEADS,
             LRU_WIDTH, LRU_WIDTH,
             Q_LORA, KV_LORA + QK_ROPE,
             D_HEADS * HEAD_DIM, D_KV_HEADS * HEAD_DIM, D_KV_HEADS * HEAD_DIM,
             N_BRANCH * D_MODEL)
IN_OFFSETS = tuple(int(v) for v in np.concatenate([[0], np.cumsum(IN_SPLITS)]))

LANES = 128
INT_MIN = -2 ** 31
INT_MAX = 2 ** 31 - 1
VMEM_LIMIT = 56 * 1024 * 1024
DSA_VMEM_LIMIT = 62 * 1024 * 1024
LOG2E = 1.4426950408889634
POS_SPLIT = 64
DSA_GROUP = 16
MLA_GROUP = 8
CNT_GROUP = 16
CNT_ROWS = 32

QKV_W = 512 + 256 + 128
IDX_W = 512 + 128
MLA_W = 640

FFN_TM = 1024
FFN_TF = 256
PROJ_TM = 512
DSA_TQ = 128
DSA_KC = 256
MLA_TQ = 512
MLA_KC = 512
SWA_BLOCKS = 4
LRU_L = 256
MERGE_TM = 512


def _alibi(i):
    return float(2.0 ** (-8.0 * i / N_ALIBI))


SLOPES_D = tuple(_alibi(i) for i in range(1, D_HEADS + 1))
SLOPES_A = tuple(_alibi(i) for i in range(D_HEADS + 1, N_ALIBI + 1))


def _bf16_parts(x):
    parts = []
    rem = np.float32(x)
    for _ in range(3):
        p = np.float32(np.asarray(rem, np.float32).astype(jnp.bfloat16).astype(np.float32))
        parts.append(float(p))
        rem = np.float32(rem - p)
    return parts


def _cparams(sem):
    return pltpu.CompilerParams(dimension_semantics=sem, vmem_limit_bytes=VMEM_LIMIT)


def _rms(x, g):
    return x * lax.rsqrt(jnp.mean(x * x, axis=-1, keepdims=True) + EPS) * g


def _nt_dot(a, b):
    return lax.dot_general(a, b, (((1,), (1,)), ((), ())), preferred_element_type=F32)


def _expm1(y):
    u = jnp.exp(y)
    safe = (u != 1.0) & (y > -1.0)
    ratio = y / jnp.log(jnp.where(safe, u, 2.0))
    return jnp.where(u == 1.0, y, jnp.where(safe, (u - 1.0) * ratio, u - 1.0))


def _loop_groups(n, body, init=0, *, width):
    carry, start = init, 0
    while width >= 1:
        def group(i, carry, start=start, width=width):
            for u in range(width):
                carry = body(start + width * i + u, carry)
            return carry

        trips = lax.shift_right_logical(n - start, _log2(width))
        carry = lax.fori_loop(0, trips, group, carry)
        start = start + trips * width
        width //= 2
    return carry


def _log2(n):
    l = int(n).bit_length() - 1
    assert (1 << l) == n
    return l


def _ada_kernel(c_ref, w_ref, b_ref, o_ref):
    c = c_ref[...]
    sc = (c * jax.nn.sigmoid(c)).astype(BF16)
    o_ref[0] = jnp.dot(sc, w_ref[0].astype(BF16), preferred_element_type=F32) + b_ref[0]


def _ada_call(c_pad, w_ada, b_ada):
    depth, d, n = w_ada.shape
    tn = 1152
    return pl.pallas_call(
        _ada_kernel,
        grid=(depth, n // tn),
        in_specs=[pl.BlockSpec((c_pad.shape[0], d), lambda l, j: (0, 0)),
                  pl.BlockSpec((1, d, tn), lambda l, j: (l, 0, j)),
                  pl.BlockSpec((1, 1, tn), lambda l, j: (l, 0, j))],
        out_specs=pl.BlockSpec((1, c_pad.shape[0], tn), lambda l, j: (l, 0, j)),
        out_shape=jax.ShapeDtypeStruct((depth, c_pad.shape[0], n), F32),
        compiler_params=_cparams(("arbitrary", "arbitrary")),
        name="adaln",
    )(c_pad, w_ada, b_ada.reshape(depth, 1, n))


def _ffn_kernel(x_ref, mod_ref, g_ref, wg_ref, wu_ref, wo_ref, o_ref, h_ref, acc_ref, *, sub, resid_w):
    k = pl.program_id(1)
    m = mod_ref[0]

    @pl.when(k == 0)
    def _():
        x = x_ref[...]
        xn = _rms(x, g_ref[2 * sub:2 * sub + 1])
        h_ref[...] = (xn * (1.0 + m[3 * sub + 1:3 * sub + 2]) + m[3 * sub:3 * sub + 1]).astype(BF16)
        acc_ref[...] = jnp.zeros_like(acc_ref)

    h = h_ref[...]
    gate = jnp.dot(h, wg_ref[0].astype(BF16), preferred_element_type=F32)
    up = jnp.dot(h, wu_ref[0].astype(BF16), preferred_element_type=F32)
    act = (gate * jax.nn.sigmoid(gate) * up).astype(BF16)
    acc_ref[...] += jnp.dot(act, wo_ref[0].astype(BF16), preferred_element_type=F32)

    @pl.when(k == pl.num_programs(1) - 1)
    def _():
        yn = _rms(acc_ref[...], g_ref[2 * sub + 1:2 * sub + 2])
        o_ref[...] = x_ref[...] + resid_w * m[3 * sub + 2:3 * sub + 3] * yn


def _ffn_call(x2, mod_l, g_l, w_in, w_out, layer, *, sub, resid_w, s_len):
    t, d = x2.shape
    f = w_out.shape[1]
    tm = min(FFN_TM, s_len)
    tf = FFN_TF
    nf = f // tf
    tpb = s_len // tm
    return pl.pallas_call(
        functools.partial(_ffn_kernel, sub=sub, resid_w=resid_w),
        grid=(t // tm, nf),
        in_specs=[pl.BlockSpec((tm, d), lambda i, k: (i, 0)),
                  pl.BlockSpec((1, 9, d), lambda i, k: (i // tpb, 0, 0)),
                  pl.BlockSpec((6, d), lambda i, k: (0, 0)),
                  pl.BlockSpec((1, d, tf), lambda i, k: (layer, 0, k)),
                  pl.BlockSpec((1, d, tf), lambda i, k: (layer, 0, k + nf)),
                  pl.BlockSpec((1, tf, d), lambda i, k: (layer, k, 0))],
        out_specs=pl.BlockSpec((tm, d), lambda i, k: (i, 0)),
        out_shape=jax.ShapeDtypeStruct((t, d), F32),
        scratch_shapes=[pltpu.VMEM((tm, d), BF16), pltpu.VMEM((tm, d), F32)],
        compiler_params=_cparams(("arbitrary", "arbitrary")),
        name="ffn",
    )(x2, mod_l, g_l, w_in, w_in, w_out)


def _proj_kernel(x_ref, mod_ref, g_ref, w_ref, kpos_ref, oa_ref, oi_ref, ow_ref, or_ref, oc_ref, od_ref):
    m = mod_ref[0]
    xn = _rms(x_ref[...], g_ref[2:3])
    h = (xn * (1.0 + m[4:5]) + m[3:4]).astype(BF16)
    off = 0
    for ref in (oa_ref, oi_ref, ow_ref, or_ref, oc_ref, od_ref):
        w = ref.shape[1]
        z = jnp.dot(h, w_ref[:, off:off + w], preferred_element_type=F32)
        if ref is oa_ref or ref is od_ref:
            qscale = (HEAD_DIM ** -0.5) * (LOG2E if ref is oa_ref else 1.0)
            z = jnp.concatenate([z[:, :512] * qscale, z[:, 512:768] + kpos_ref[...], z[:, 768:]], axis=1)
        ref[...] = z.astype(ref.dtype)
        off += w


def _proj_call(x2, mod_l, g_l, w_all, kpos, *, s_len):
    t, d = x2.shape
    tm = min(PROJ_TM, s_len)
    tpb = s_len // tm
    widths = (QKV_W, IDX_W, LANES, 2 * LRU_WIDTH, MLA_W, QKV_W)
    dtypes = (BF16, BF16, F32, F32, F32, BF16)
    assert sum(widths) == w_all.shape[1]
    return pl.pallas_call(
        _proj_kernel,
        grid=(t // tm,),
        in_specs=[pl.BlockSpec((tm, d), lambda i: (i, 0)),
                  pl.BlockSpec((1, 9, d), lambda i: (i // tpb, 0, 0)),
                  pl.BlockSpec((6, d), lambda i: (0, 0)),
                  pl.BlockSpec(w_all.shape, lambda i: (0, 0)),
                  pl.BlockSpec((tm, 256), lambda i: (i % tpb, 0))],
        out_specs=[pl.BlockSpec((tm, w), lambda i: (i, 0)) for w in widths],
        out_shape=[jax.ShapeDtypeStruct((t, w), dt) for w, dt in zip(widths, dtypes)],
        compiler_params=_cparams(("arbitrary",)),
        name="mixer_proj",
    )(x2, mod_l, g_l, w_all, kpos)


def _head_rows(q, n_heads, tq):
    lane = lax.broadcasted_iota(I32, (tq, LANES), 1)
    rows = []
    for h in range(n_heads):
        tile = q[:, LANES * (h // 2):LANES * (h // 2 + 1)].astype(F32)
        keep = (lane >= HEAD_DIM) if (h % 2) else (lane < HEAD_DIM)
        rows.append(jnp.where(keep, tile, 0.0).astype(BF16))
    return jnp.concatenate(rows, axis=0)


def _alibi_lanes(slopes, tq):
    lane = lax.broadcasted_iota(I32, (tq, LANES), 1)
    out = []
    for slope in slopes:
        tile = jnp.zeros((tq, LANES), F32)
        for i, part in enumerate(_bf16_parts(slope)):
            tile = jnp.where(lane == HEAD_DIM + 2 * i, POS_SPLIT * part, tile)
            tile = jnp.where(lane == HEAD_DIM + 2 * i + 1, part, tile)
        out.append(tile)
    return out


def _alibi_q_tiles(q, alibi_lanes, tq):
    lane = lax.broadcasted_iota(I32, (tq, LANES), 1)
    rows = []
    for h, al in enumerate(alibi_lanes):
        tile = q[:, LANES * (h // 2):LANES * (h // 2 + 1)].astype(F32)
        if h % 2:
            tile = pltpu.roll(tile, HEAD_DIM, axis=1)
        rows.append(jnp.where(lane < HEAD_DIM, tile, al).astype(BF16))
    return jnp.concatenate(rows, axis=0)


def _gqa_out(accs, inv_ls, tq):
    lane = lax.broadcasted_iota(I32, (tq, LANES), 1)
    tiles = []
    for j in range(4):
        g = j // 2
        halves = []
        for h in (2 * j, 2 * j + 1):
            o = accs[g][h % 4] * inv_ls[g][h % 4]
            src_hi = (g == 1)
            dst_hi = (h % 2 == 1)
            if src_hi != dst_hi:
                o = pltpu.roll(o, HEAD_DIM, axis=1)
            halves.append(o)
        tiles.append(jnp.where(lane < HEAD_DIM, halves[0], halves[1]))
    return jnp.concatenate(tiles, axis=1)


def _dsa_kernel(q_ref, k_ref, v_ref, iq_ref, ik_ref, iw_ref, o_ref,
                key_ref, s_ref, tau_ref, iqs_ref, qs_ref, m_ref, l_ref, acc_ref, *, tq, kc, topk, pos_bits):
    qb = pl.program_id(1)
    q0 = qb * tq
    nk = lax.shift_right_logical(q0 + tq + kc - 1, _log2(kc))
    n_lt = kc // LANES
    n_rt = kc // CNT_ROWS
    qpos_l = q0 + lax.broadcasted_iota(I32, (kc, tq), 1)
    krow = lax.broadcasted_iota(I32, (kc, tq), 0)
    krow_t = lax.broadcasted_iota(I32, (CNT_ROWS, tq), 0)

    iqs_ref[...] = _head_rows(iq_ref[...], IDX_HEADS, tq)
    iw_t = iw_ref[...].T

    def score_body(j, carry):
        ks = pl.multiple_of(j * kc, kc)
        d = _nt_dot(ik_ref[pl.ds(ks, kc), :], iqs_ref[...])
        acc = jnp.zeros((kc, tq), F32)
        for h in range(IDX_HEADS):
            acc = acc + iw_t[h:h + 1, :] * jnp.maximum(d[:, h * tq:(h + 1) * tq], 0.0)
        bits = pltpu.bitcast(acc, I32)
        key = jnp.where(bits < 0, bits ^ INT_MAX, bits)
        key_ref[j] = jnp.where(ks + krow <= qpos_l, key, INT_MIN)
        return carry

    _loop_groups(nk, score_body, width=DSA_GROUP)

    def count(pred):
        def body(j, cnt):
            for r in range(n_rt):
                kp = (j * kc + r * CNT_ROWS) + krow_t
                cnt = cnt + jnp.where(pred(key_ref[j, r * CNT_ROWS:(r + 1) * CNT_ROWS, :], kp), 1.0, 0.0)
            return cnt
        cnt = _loop_groups(nk, body, jnp.zeros((CNT_ROWS, tq), F32), width=CNT_GROUP)
        return jnp.sum(cnt, axis=0, keepdims=True)

    kf = float(topk)

    def bit_step(i, state):
        tau, cnt = state
        cand = tau + lax.shift_left(jnp.int32(1), 31 - i)
        cb = jnp.broadcast_to(cand, (CNT_ROWS, tq))
        c = count(lambda kv, kp: kv >= cb)
        ok = c >= kf
        return jnp.where(ok, cand, tau), jnp.where(ok, c, cnt)

    tau, n_ge = lax.fori_loop(0, 32, bit_step, (jnp.full((1, tq), INT_MIN, I32),
                                                 jnp.broadcast_to((nk * kc).astype(F32), (1, tq))))

    tie_f = jnp.where((n_ge > kf) & (tau > INT_MIN), 1.0, 0.0)
    any_tie = jnp.max(tie_f) > 0.0
    tau_c = jnp.maximum(tau, INT_MIN + 1)
    tau_ref[...] = jnp.broadcast_to(tau_c, tau_ref.shape)

    @pl.when(any_tie)
    def _():
        tb = jnp.broadcast_to(tau_c, (CNT_ROWS, tq))
        need = kf - count(lambda kv, kp: kv > tb)

        def pos_body(i, p):
            cand = p | lax.shift_left(jnp.int32(1), pos_bits - 1 - i)
            cb = jnp.broadcast_to(cand, (CNT_ROWS, tq))
            below = count(lambda kv, kp: (kv == tb) & (kp < cb))
            return jnp.where(below < need, cand, p)

        p = lax.fori_loop(0, pos_bits, pos_body, jnp.zeros((1, tq), I32))
        p = jnp.where(tie_f > 0.0, p, INT_MAX)

        def rewrite_body(j, carry):
            kj = key_ref[j]
            kp = j * kc + krow
            sel = (kj > tau_c) | ((kj == tau_c) & (kp <= p))
            key_ref[j] = jnp.where(sel, 1, -1)
            return carry

        lax.fori_loop(0, nk, rewrite_body, 0)
        tau_ref[...] = jnp.zeros(tau_ref.shape, I32)

    q = q_ref[...]
    for g in range(A_KV_HEADS):
        qs_ref[g] = _alibi_q_tiles(q[:, 4 * HEAD_DIM * g:4 * HEAD_DIM * (g + 1)],
                                   _alibi_lanes([s * LOG2E for s in SLOPES_A[4 * g:4 * g + 4]], tq), tq)
    m_ref[...] = jnp.full(m_ref.shape, NEG, F32)
    l_ref[...] = jnp.zeros(l_ref.shape, F32)
    acc_ref[...] = jnp.zeros(acc_ref.shape, F32)

    def max_body(j, carry):
        ks = pl.multiple_of(j * kc, kc)
        sel = jnp.where(key_ref[j] >= tau_ref[0:1, :], 1.0, 0.0).T > 0.5
        for g in range(A_KV_HEADS):
            s = _nt_dot(qs_ref[g], k_ref[pl.ds(ks, kc), g * LANES:(g + 1) * LANES])
            for h in range(4):
                sh = jnp.where(sel, s[h * tq:(h + 1) * tq], NEG)
                s_ref[g, h, j] = sh
                mp = m_ref[g, h]
                for c in range(n_lt):
                    mp = jnp.maximum(mp, sh[:, c * LANES:(c + 1) * LANES])
                m_ref[g, h] = mp
        return carry

    _loop_groups(nk, max_body, width=DSA_GROUP)
    for g in range(A_KV_HEADS):
        for h in range(4):
            m_ref[g, h] = jnp.broadcast_to(jnp.max(m_ref[g, h], axis=1, keepdims=True), (tq, LANES))

    def pv_body(j, carry):
        ks = pl.multiple_of(j * kc, kc)
        vch = v_ref[pl.ds(ks, kc), :]
        for g in range(A_KV_HEADS):
            ps = []
            for h in range(4):
                mb = m_ref[g, h]
                lp = l_ref[g, h]
                tiles = []
                for c in range(n_lt):
                    p = jnp.exp2(s_ref[g, h, j, :, c * LANES:(c + 1) * LANES] - mb)
                    lp = lp + p
                    tiles.append(p.astype(BF16))
                l_ref[g, h] = lp
                ps.append(jnp.concatenate(tiles, axis=1))
            pv = jnp.dot(jnp.concatenate(ps, axis=0), vch, preferred_element_type=F32)
            for h in range(4):
                acc_ref[g, h] += pv[h * tq:(h + 1) * tq]
        return carry

    _loop_groups(nk, pv_body, width=DSA_GROUP)
    accs = [[acc_ref[g, h] for h in range(4)] for g in range(A_KV_HEADS)]
    inv_ls = [[1.0 / jnp.sum(l_ref[g, h], axis=1, keepdims=True) for h in range(4)] for g in range(A_KV_HEADS)]
    o_ref[...] = _gqa_out(accs, inv_ls, tq).astype(o_ref.dtype)


def _dsa_call(qkv, idx, iw, *, batch, s_len):
    t = qkv.shape[0]
    tq = min(DSA_TQ, s_len)
    kc = min(DSA_KC, s_len)
    topk = min(TOPK_MAX, s_len // 4)
    assert kc >= topk and kc % tq == 0
    nqb = s_len // tq
    once = pl.Buffered(1)
    return pl.pallas_call(
        functools.partial(_dsa_kernel, tq=tq, kc=kc, topk=topk, pos_bits=_log2(s_len)),
        grid=(batch, nqb),
        in_specs=[pl.BlockSpec((tq, 512), lambda b, i: (b * nqb + i, 0)),
                  pl.BlockSpec((s_len, 256), lambda b, i: (b, 2), pipeline_mode=once),
                  pl.BlockSpec((s_len, 128), lambda b, i: (b, 6), pipeline_mode=once),
                  pl.BlockSpec((tq, 512), lambda b, i: (b * nqb + i, 0)),
                  pl.BlockSpec((s_len, 128), lambda b, i: (b, 4), pipeline_mode=once),
                  pl.BlockSpec((tq, LANES), lambda b, i: (b * nqb + i, 0))],
        out_specs=pl.BlockSpec((tq, 512), lambda b, i: (b * nqb + i, 0)),
        out_shape=jax.ShapeDtypeStruct((t, 512), BF16),
        scratch_shapes=[pltpu.VMEM((s_len // kc, kc, tq), I32),
                        pltpu.VMEM((A_KV_HEADS, 4, s_len // kc, tq, kc), F32),
                        pltpu.VMEM((8, tq), I32),
                        pltpu.VMEM((IDX_HEADS * tq, LANES), BF16),
                        pltpu.VMEM((A_KV_HEADS, 4 * tq, LANES), BF16),
                        pltpu.VMEM((A_KV_HEADS, 4, tq, LANES), F32),
                        pltpu.VMEM((A_KV_HEADS, 4, tq, LANES), F32),
                        pltpu.VMEM((A_KV_HEADS, 4, tq, LANES), F32)],
        compiler_params=pltpu.CompilerParams(dimension_semantics=("arbitrary", "arbitrary"),
                                             vmem_limit_bytes=DSA_VMEM_LIMIT),
        name="dsa",
    )(qkv, qkv, qkv, idx, idx, iw)


def _swa_kernel(q_ref, kp_ref, kc_ref, vp_ref, vc_ref, sink_ref, o_ref, *, tq, nblk):
    i = pl.program_id(1)
    krow = lax.broadcasted_iota(I32, (2 * tq, tq), 0)
    qcol = lax.broadcasted_iota(I32, (2 * tq, tq), 1)
    dist = qcol + tq - krow
    in_band = (dist >= 0) & (dist < WINDOW)
    sinks = sink_ref[...]
    sink_vec = [jnp.broadcast_to(sinks[:, h:h + 1], (1, tq)) for h in range(D_HEADS)]
    alibi = _alibi_lanes(SLOPES_D, tq)
    for n in range(nblk):
        blk = i * nblk + n
        rows = slice(n * tq, (n + 1) * tq)
        q = q_ref[rows, :]
        k_prev = kp_ref[...] if n == 0 else kc_ref[(n - 1) * tq:n * tq, :]
        v_prev = vp_ref[...] if n == 0 else vc_ref[(n - 1) * tq:n * tq, :]
        kk = jnp.concatenate([k_prev, kc_ref[rows, :]], axis=0)
        vv_t = jnp.concatenate([v_prev, vc_ref[rows, :]], axis=0).T
        valid = in_band & ((blk * tq - tq + krow) >= 0)
        qpos = (blk * tq + lax.broadcasted_iota(I32, (1, tq), 1)).astype(F32)
        tiles = []
        for g in range(D_KV_HEADS):
            qs = _alibi_q_tiles(q[:, 4 * HEAD_DIM * g:4 * HEAD_DIM * (g + 1)], alibi[4 * g:4 * g + 4], tq)
            s_t = _nt_dot(kk[:, g * LANES:(g + 1) * LANES], qs)
            ps, ils = [], []
            for h in range(4):
                sh = jnp.where(valid, s_t[:, h * tq:(h + 1) * tq], NEG)
                sink = sink_vec[4 * g + h] + SLOPES_D[4 * g + h] * qpos
                m = jnp.maximum(jnp.max(sh, axis=0, keepdims=True), sink)
                p = jnp.exp(sh - m)
                ils.append(1.0 / (jnp.sum(p, axis=0, keepdims=True) + jnp.exp(sink - m)))
                ps.append(p.astype(BF16))
            o_t = jnp.dot(vv_t, jnp.concatenate(ps, axis=1), preferred_element_type=F32)
            o_t = o_t[g * HEAD_DIM:(g + 1) * HEAD_DIM, :] * jnp.concatenate(ils, axis=1)
            for j in range(2):
                pair = jnp.concatenate([o_t[:, (2 * j) * tq:(2 * j + 1) * tq],
                                        o_t[:, (2 * j + 1) * tq:(2 * j + 2) * tq]], axis=0)
                tiles.append(pair.T)
        o_ref[rows, :] = jnp.concatenate(tiles, axis=1).astype(o_ref.dtype)


def _swa_call(qkv, sinks_pad, *, batch, s_len):
    t = qkv.shape[0]
    tq = BLOCK
    nblk = min(SWA_BLOCKS, s_len // tq)
    ts = nblk * tq
    nst = s_len // ts
    cur = lambda b, i: b * nst + i
    prev = lambda b, i: jnp.maximum((b * nst + i) * nblk - 1, 0)
    return pl.pallas_call(
        functools.partial(_swa_kernel, tq=tq, nblk=nblk),
        grid=(batch, nst),
        in_specs=[pl.BlockSpec((ts, 512), lambda b, i: (cur(b, i), 0)),
                  pl.BlockSpec((tq, 256), lambda b, i: (prev(b, i), 2)),
                  pl.BlockSpec((ts, 256), lambda b, i: (cur(b, i), 2)),
                  pl.BlockSpec((tq, 128), lambda b, i: (prev(b, i), 6)),
                  pl.BlockSpec((ts, 128), lambda b, i: (cur(b, i), 6)),
                  pl.BlockSpec((1, LANES), lambda b, i: (0, 0))],
        out_specs=pl.BlockSpec((ts, 512), lambda b, i: (cur(b, i), 0)),
        out_shape=jax.ShapeDtypeStruct((t, 512), BF16),
        compiler_params=_cparams(("arbitrary", "arbitrary")),
        name="swa",
    )(qkv, qkv, qkv, qkv, qkv, sinks_pad)


def _mla_prep_kernel(c_ref, cos_ref, sin_ref, gq_ref, gkv_ref, wq_ref, wqr_ref, wk_ref, wv_ref,
                     q_ref, k_ref, v_ref):
    c = c_ref[...]
    cos = cos_ref[...]
    sin = sin_ref[...]
    cos8 = jnp.concatenate([cos] * C_HEADS, axis=1)
    sin8 = jnp.concatenate([sin] * C_HEADS, axis=1)
    cqn = _rms(c[:, :Q_LORA], gq_ref[...]).astype(BF16)
    q = (jnp.dot(cqn, wq_ref[...], preferred_element_type=F32) * cos8
         + jnp.dot(cqn, wqr_ref[...], preferred_element_type=F32) * sin8)
    q_ref[...] = q.astype(BF16)
    ckvn = _rms(c[:, Q_LORA:Q_LORA + KV_LORA], gkv_ref[...]).astype(BF16)
    kr = c[:, 384:512] * cos + c[:, 512:640] * sin
    k = jnp.dot(ckvn, wk_ref[...], preferred_element_type=F32) + jnp.concatenate([kr] * C_HEADS, axis=1)
    k_ref[...] = k.astype(BF16)
    v_ref[...] = jnp.dot(ckvn, wv_ref[...], preferred_element_type=F32).astype(BF16)


def _mla_prep_call(cm, cos_t, sin_t, gq, gkv, wq, wqr, wk, wv, *, s_len):
    t = cm.shape[0]
    tm = min(PROJ_TM, s_len)
    tpb = s_len // tm
    full = lambda a: pl.BlockSpec(a.shape, lambda i: (0, 0))
    return pl.pallas_call(
        _mla_prep_kernel,
        grid=(t // tm,),
        in_specs=[pl.BlockSpec((tm, MLA_W), lambda i: (i, 0)),
                  pl.BlockSpec((tm, LANES), lambda i: (i % tpb, 0)),
                  pl.BlockSpec((tm, LANES), lambda i: (i % tpb, 0)),
                  full(gq), full(gkv), full(wq), full(wqr), full(wk), full(wv)],
        out_specs=[pl.BlockSpec((tm, 1024), lambda i: (i, 0)),
                   pl.BlockSpec((tm, 1024), lambda i: (i, 0)),
                   pl.BlockSpec((tm, 512), lambda i: (i, 0))],
        out_shape=[jax.ShapeDtypeStruct((t, 1024), BF16),
                   jax.ShapeDtypeStruct((t, 1024), BF16),
                   jax.ShapeDtypeStruct((t, 512), BF16)],
        compiler_params=_cparams(("arbitrary",)),
        name="mla_prep",
    )(cm, cos_t, sin_t, gq, gkv, wq, wqr, wk, wv)


def _mla_attn_kernel(q_ref, k_ref, v_ref, o_ref, s_ref, m_ref, l_ref, acc_ref, *, tq, kc):
    qb = pl.program_id(2)
    q0 = qb * tq
    n_full = lax.shift_right_logical(q0, _log2(kc))
    rowpos = q0 + lax.broadcasted_iota(I32, (tq, kc), 0)
    lane = lax.broadcasted_iota(I32, (tq, kc), 1)
    c = ((QK_NOPE + QK_ROPE) ** -0.5) * LOG2E
    n_lt = kc // LANES
    m_ref[...] = jnp.full(m_ref.shape, NEG, F32)
    l_ref[...] = jnp.zeros(l_ref.shape, F32)
    acc_ref[...] = jnp.zeros(acc_ref.shape, F32)

    def max_step(j, masked):
        ks = pl.multiple_of(j * kc, kc)
        for hh in range(2):
            s = _nt_dot(q_ref[:, hh * LANES:(hh + 1) * LANES],
                        k_ref[pl.ds(ks, kc), hh * LANES:(hh + 1) * LANES]) * c
            if masked:
                s = jnp.where(ks + lane <= rowpos, s, NEG)
            s_ref[hh, j] = s
            mp = m_ref[hh]
            for t in range(n_lt):
                mp = jnp.maximum(mp, s[:, t * LANES:(t + 1) * LANES])
            m_ref[hh] = mp

    def max_body(j, carry):
        max_step(j, False)
        return carry

    _loop_groups(n_full, max_body, width=MLA_GROUP)
    max_step(n_full, True)
    for hh in range(2):
        m_ref[hh] = jnp.broadcast_to(jnp.max(m_ref[hh], axis=1, keepdims=True), (tq, LANES))

    def pv_body(j, carry):
        ks = pl.multiple_of(j * kc, kc)
        vch = v_ref[pl.ds(ks, kc), :]
        for hh in range(2):
            mb = m_ref[hh]
            lp = l_ref[hh]
            tiles = []
            for t in range(n_lt):
                p = jnp.exp2(s_ref[hh, j, :, t * LANES:(t + 1) * LANES] - mb)
                lp = lp + p
                tiles.append(p.astype(BF16))
            l_ref[hh] = lp
            acc_ref[hh] += jnp.dot(jnp.concatenate(tiles, axis=1), vch, preferred_element_type=F32)
        return carry

    _loop_groups(n_full + 1, pv_body, width=MLA_GROUP)
    outs = [acc_ref[hh] * (1.0 / jnp.sum(l_ref[hh], axis=1, keepdims=True)) for hh in range(2)]
    lane_o = lax.broadcasted_iota(I32, (tq, LANES), 1)
    o_ref[...] = jnp.where(lane_o < V_DIM, outs[0], outs[1]).astype(o_ref.dtype)


def _mla_attn_call(qm, km, vm, *, batch, s_len):
    t = qm.shape[0]
    tq = min(MLA_TQ, s_len)
    kc = min(MLA_KC, s_len)
    assert kc % tq == 0
    nqb = s_len // tq
    once = pl.Buffered(1)
    return pl.pallas_call(
        functools.partial(_mla_attn_kernel, tq=tq, kc=kc),
        grid=(batch, C_HEADS // 2, nqb),
        in_specs=[pl.BlockSpec((tq, 256), lambda b, h, i: (b * nqb + i, h)),
                  pl.BlockSpec((s_len, 256), lambda b, h, i: (b, h), pipeline_mode=once),
                  pl.BlockSpec((s_len, 128), lambda b, h, i: (b, h), pipeline_mode=once)],
        out_specs=pl.BlockSpec((tq, 128), lambda b, h, i: (b * nqb + i, h)),
        out_shape=jax.ShapeDtypeStruct((t, 512), BF16),
        scratch_shapes=[pltpu.VMEM((2, s_len // kc, tq, kc), F32)] + [pltpu.VMEM((2, tq, LANES), F32)] * 3,
        compiler_params=_cparams(("arbitrary", "arbitrary", "arbitrary")),
        name="mla_attn",
    )(qm, km, vm)


def _lru_kernel(xr_ref, xg_ref, prev_ref, cw_ref, cb_ref, wa_ref, ba_ref, wx_ref, bx_ref, lam_ref,
                o_ref, xe_ref, h_ref, *, ln):
    i = pl.program_id(1)
    xe_ref[0:8, :] = jnp.where(i > 0, prev_ref[...], 0.0)
    xe_ref[8:8 + ln, :] = xr_ref[...]
    cw = cw_ref[...]
    xc = cb_ref[...] + cw[0:1] * xe_ref[pl.ds(5, ln), :]
    for j in range(1, CONV_WIDTH):
        xc = xc + cw[j:j + 1] * xe_ref[pl.ds(5 + j, ln), :]
    xcb = xc.astype(BF16)
    r = jax.nn.sigmoid(jnp.dot(xcb, wa_ref[...], preferred_element_type=F32) + ba_ref[...])
    gi = jax.nn.sigmoid(jnp.dot(xcb, wx_ref[...], preferred_element_type=F32) + bx_ref[...])
    z = -lam_ref[...]
    softplus = jnp.maximum(z, 0.0) + jnp.log1p(jnp.exp(-jnp.abs(z)))
    log_a = -LRU_C * r * softplus
    a = jnp.exp(log_a)
    b = jnp.sqrt(-_expm1(2.0 * log_a)) * (gi * xc)
    row = lax.broadcasted_iota(I32, (ln, LRU_WIDTH), 0)
    d = 1
    while d < ln:
        keep = row >= d
        b = jnp.where(keep, a * pltpu.roll(b, d, axis=0) + b, b)
        a = jnp.where(keep, a * pltpu.roll(a, d, axis=0), a)
        d *= 2
    h_prev = jnp.where(i > 0, h_ref[0:1, :], 0.0)
    h = a * h_prev + b
    h_ref[0:1, :] = h[ln - 1:ln, :]
    xg = xg_ref[...]
    gelu = 0.5 * xg * (1.0 + jnp.tanh(0.7978845608028654 * (xg + 0.044715 * (xg * xg * xg))))
    o_ref[...] = (h * gelu).astype(o_ref.dtype)


def _lru_call(rm, cw, cb, wa, ba, wx, bx, lam, *, batch, s_len):
    t = rm.shape[0]
    ln = min(LRU_L, s_len)
    nt = s_len // ln
    w = LRU_WIDTH
    vec = lambda a: pl.BlockSpec(a.shape, lambda b, i: (0, 0))
    return pl.pallas_call(
        functools.partial(_lru_kernel, ln=ln),
        grid=(batch, nt),
        in_specs=[pl.BlockSpec((ln, w), lambda b, i: (b * nt + i, 0)),
                  pl.BlockSpec((ln, w), lambda b, i: (b * nt + i, 1)),
                  pl.BlockSpec((8, w), lambda b, i: (jnp.maximum((b * nt + i) * (ln // 8) - 1, 0), 0)),
                  vec(cw), vec(cb), vec(wa), vec(ba), vec(wx), vec(bx), vec(lam)],
        out_specs=pl.BlockSpec((ln, w), lambda b, i: (b * nt + i, 0)),
        out_shape=jax.ShapeDtypeStruct((t, w), BF16),
        scratch_shapes=[pltpu.VMEM((ln + 8, w), F32), pltpu.VMEM((8, w), F32)],
        compiler_params=_cparams(("arbitrary", "arbitrary")),
        name="rglru",
    )(rm, rm, rm, cw, cb, wa, ba, wx, bx, lam)


def _merge_kernel(x_ref, mod_ref, g_ref, ya_ref, yb_ref, yc_ref, yd_ref, wg_ref, wb_ref, wo_ref, o_ref):
    m = mod_ref[0]
    x = x_ref[...]
    h = (_rms(x, g_ref[2:3]) * (1.0 + m[4:5]) + m[3:4]).astype(BF16)
    merged = None
    for n, y_ref in enumerate((ya_ref, yb_ref, yc_ref, yd_ref)):
        gate = jax.nn.sigmoid(jnp.dot(h, wg_ref[:, n * D_MODEL:(n + 1) * D_MODEL], preferred_element_type=F32))
        term = gate * jnp.dot(y_ref[...], wb_ref[n], preferred_element_type=F32)
        merged = term if merged is None else merged + term
    y = jnp.dot(merged.astype(BF16), wo_ref[...], preferred_element_type=F32)
    o_ref[...] = x + m[5:6] * _rms(y, g_ref[3:4])


def _merge_call(x2, mod_l, g_l, ys, w_gate, w_branch, w_out, *, s_len):
    t, d = x2.shape
    tm = min(MERGE_TM, s_len)
    tpb = s_len // tm
    return pl.pallas_call(
        _merge_kernel,
        grid=(t // tm,),
        in_specs=[pl.BlockSpec((tm, d), lambda i: (i, 0)),
                  pl.BlockSpec((1, 9, d), lambda i: (i // tpb, 0, 0)),
                  pl.BlockSpec((6, d), lambda i: (0, 0))]
                 + [pl.BlockSpec((tm, BRANCH_WIDTH), lambda i: (i, 0))] * N_BRANCH
                 + [pl.BlockSpec(w_gate.shape, lambda i: (0, 0)),
                    pl.BlockSpec(w_branch.shape, lambda i: (0, 0, 0)),
                    pl.BlockSpec(w_out.shape, lambda i: (0, 0))],
        out_specs=pl.BlockSpec((tm, d), lambda i: (i, 0)),
        out_shape=jax.ShapeDtypeStruct((t, d), F32),
        compiler_params=_cparams(("arbitrary",)),
        name="merge",
    )(x2, mod_l, g_l, *ys, w_gate, w_branch, w_out)


def _mixer_weights(w_in_l):
    o = IN_OFFSETS
    col = lambda n: w_in_l[:, o[n]:o[n + 1]]
    z = lambda n: jnp.zeros((D_MODEL, n), w_in_l.dtype)

    def pad_heads(k):
        return jnp.concatenate([k[:, :64], z(64), k[:, 64:], z(64)], axis=1)

    a_q, a_k, a_v, i_q, i_k, i_w, r_x, r_g, c_q, c_kv, d_q, d_k, d_v = (col(n) for n in range(13))
    k_rope = c_kv[:, KV_LORA:]
    half = QK_ROPE // 2
    k_rope_rot = jnp.concatenate([-k_rope[:, half:], k_rope[:, :half]], axis=1)
    groups = [a_q, pad_heads(a_k), a_v,
              i_q, i_k, i_k,
              i_w, z(LANES - IDX_HEADS),
              r_x, r_g,
              c_q, c_kv[:, :KV_LORA], z(64), k_rope, z(32), z(64), k_rope_rot, z(32),
              d_q, pad_heads(d_k), d_v]
    w_all = jnp.concatenate(groups, axis=1).astype(BF16)
    w_gate = col(13).astype(BF16)
    return w_all, w_gate


def _kpos_table(s_len):
    pos = np.arange(s_len)
    tile = np.zeros((s_len, LANES), np.float32)
    for i in range(3):
        tile[:, HEAD_DIM + 2 * i] = pos // POS_SPLIT
        tile[:, HEAD_DIM + 2 * i + 1] = pos % POS_SPLIT
    return jnp.asarray(np.concatenate([tile, tile], axis=1))


def _mla_weights(w_uq, w_ukv):
    dq = QK_NOPE + QK_ROPE
    half = QK_ROPE // 2
    zq = lambda n: jnp.zeros((Q_LORA, n), w_uq.dtype)
    wq, wqr, wk, wv = [], [], [], []
    for h in range(C_HEADS):
        nope = w_uq[:, h * dq:h * dq + QK_NOPE]
        r1 = w_uq[:, h * dq + QK_NOPE:h * dq + QK_NOPE + half]
        r2 = w_uq[:, h * dq + QK_NOPE + half:(h + 1) * dq]
        wq += [nope, r1, r2, zq(32)]
        wqr += [zq(64), -r2, r1, zq(32)]
        wk += [w_ukv[:, h * 128:h * 128 + QK_NOPE], jnp.zeros((KV_LORA, 64), w_ukv.dtype)]
        wv += [w_ukv[:, h * 128 + QK_NOPE:(h + 1) * 128]]
    cat = lambda xs: jnp.concatenate(xs, axis=1).astype(BF16)
    return cat(wq), cat(wqr), cat(wk), cat(wv)


def _rope_tables(s_len):
    half = QK_ROPE // 2
    inv = (np.float32(ROPE_THETA) ** (-np.arange(half, dtype=np.float32) / np.float32(half))).astype(np.float32)
    ang = (np.arange(s_len, dtype=np.float32)[:, None] * inv[None, :]).astype(np.float32)
    cos, sin = np.cos(ang.astype(np.float64)).astype(np.float32), np.sin(ang.astype(np.float64)).astype(np.float32)
    ones = lambda n: np.ones((s_len, n), np.float32)
    zeros = lambda n: np.zeros((s_len, n), np.float32)
    cos_t = np.concatenate([ones(64), cos, cos, ones(32)], axis=1)
    sin_t = np.concatenate([zeros(64), sin, sin, zeros(32)], axis=1)
    return jnp.asarray(cos_t), jnp.asarray(sin_t)


def _block_diag(w):
    n, bw, _ = w.shape
    eye = jnp.eye(n, dtype=w.dtype)
    return (w[:, :, None, :] * eye[:, None, :, None]).reshape(n * bw, n * bw).astype(BF16)


def kernel(x, c, w_ada, b_ada, norm_g, ffn1_w_in, ffn1_w_out, w_in, conv_w, conv_b, lru_wa, lru_ba, lru_wx,
           lru_bx, lru_lambda, mla_g_q, mla_g_kv, mla_w_uq, mla_w_ukv, swa_sinks, w_branch, w_out, ffn2_w_in,
           ffn2_w_out):
    batch, s_len, d = x.shape
    depth = w_ada.shape[0]
    t = batch * s_len
    assert s_len // POS_SPLIT <= 256
    x2 = x.reshape(t, d)
    c_pad = jnp.zeros((8, d), F32).at[:batch].set(c)
    mod_all = _ada_call(c_pad, w_ada, b_ada)
    cos_t, sin_t = _rope_tables(s_len)
    kpos = _kpos_table(s_len)
    row = lambda v: v.reshape(1, -1)
    for l in range(depth):
        mod_l = mod_all[l, :batch].reshape(batch, 9, d)
        g_l = norm_g[l]
        x2 = _ffn_call(x2, mod_l, g_l, ffn1_w_in, ffn1_w_out, l, sub=0, resid_w=0.5, s_len=s_len)
        w_all, w_gate = _mixer_weights(w_in[l])
        qkv_a, idx, iw, rm, cm, qkv_d = _proj_call(x2, mod_l, g_l, w_all, kpos, s_len=s_len)
        y_a = _dsa_call(qkv_a, idx, iw, batch=batch, s_len=s_len)
        y_b = _lru_call(rm, conv_w[l], row(conv_b[l]), _block_diag(lru_wa[l]), row(lru_ba[l]),
                        _block_diag(lru_wx[l]), row(lru_bx[l]), row(lru_lambda[l]), batch=batch, s_len=s_len)
        qm, km, vm = _mla_prep_call(cm, cos_t, sin_t, row(mla_g_q[l]), row(mla_g_kv[l]),
                                    *_mla_weights(mla_w_uq[l], mla_w_ukv[l]), s_len=s_len)
        y_c = _mla_attn_call(qm, km, vm, batch=batch, s_len=s_len)
        sinks_pad = jnp.zeros((1, LANES), F32).at[0, :D_HEADS].set(swa_sinks[l])
        y_d = _swa_call(qkv_d, sinks_pad, batch=batch, s_len=s_len)
        x2 = _merge_call(x2, mod_l, g_l, (y_a, y_b, y_c, y_d), w_gate, w_branch[l].astype(BF16),
                         w_out[l].astype(BF16), s_len=s_len)
        x2 = _ffn_call(x2, mod_l, g_l, ffn2_w_in, ffn2_w_out, l, sub=2, resid_w=0.5, s_len=s_len)
    return x2.reshape(batch, s_len, d)
```
